```python
import jax, jax.numpy as jnp
from jax import lax
import numpy as np

D_MODEL = 1024
BATCH = 8
SEQ = 4096
DEPTH = 4

N_GROUPS = 4
GROUP_W = D_MODEL // N_GROUPS
EPS = 1e-6

MLSTM_HEADS = 4
MLSTM_DH = GROUP_W // MLSTM_HEADS
MLSTM_CONV = 4
MLSTM_CHUNK = 64

MLA_HEADS = 4
MLA_NOPE = 64
MLA_ROPE = 32
MLA_V = GROUP_W // MLA_HEADS
MLA_Q_RANK = 192
MLA_KV_RANK = 128
ROPE_BASE = 10000.0
Q_BLOCK = 128

GLA_HEADS = 4
GLA_DK = GROUP_W // (2 * GLA_HEADS)
GLA_DV = GROUP_W // GLA_HEADS
GLA_GATE_RANK = 16
GLA_TAU = 16.0
GLA_CHUNK = 64

CONV_CH = GROUP_W
CONV_WIDTH = 31

D_FF = 256 * ((8 * D_MODEL // 3 + 255) // 256)
N_EXPERTS = 8
TOP_K = 2
MOE_BLOCK = 256
N_DENSE = (DEPTH + 1) // 2
N_MOE = DEPTH // 2

IN_SPLITS = (
    2 * MLSTM_HEADS * MLSTM_DH,
    MLSTM_HEADS * MLSTM_DH,
    MLSTM_HEADS * MLSTM_DH,
    2 * MLSTM_HEADS,
    MLA_Q_RANK,
    MLA_KV_RANK,
    MLA_ROPE,
    GLA_HEADS * GLA_DK,
    GLA_HEADS * GLA_DK,
    GLA_HEADS * GLA_DV,
    GLA_HEADS * GLA_DV,
    GLA_GATE_RANK,
    2 * CONV_CH,
)
D_IN = sum(IN_SPLITS)

kernel_name = 'hybrid_mlstm_mla_gla_conformer_moe'


def rms_norm(x, g):
    xf = x.astype(jnp.float32)
    y = xf * lax.rsqrt(jnp.mean(xf * xf, axis=-1, keepdims=True) + EPS)
    return (y * g.astype(jnp.float32)).astype(x.dtype)


def layer_norm(x, g, b):
    xf = x.astype(jnp.float32)
    mu = jnp.mean(xf, axis=-1, keepdims=True)
    var = jnp.mean(jnp.square(xf - mu), axis=-1, keepdims=True)
    y = (xf - mu) * lax.rsqrt(var + EPS)
    return (y * g.astype(jnp.float32) + b.astype(jnp.float32)).astype(x.dtype)


def group_rms_norm(y, g, n_groups):
    B, S, W = y.shape
    yf = y.reshape(B, S, n_groups, W // n_groups).astype(jnp.float32)
    yf = yf * lax.rsqrt(jnp.mean(yf * yf, axis=-1, keepdims=True) + EPS)
    return (yf.reshape(B, S, W) * g.astype(jnp.float32)).astype(y.dtype)


def causal_dwconv(x, w, b):
    k, c = w.shape
    y = lax.conv_general_dilated(x, w[:, None, :].astype(x.dtype), (1,), [(k - 1, 0)],
                                 dimension_numbers=('NWC', 'WIO', 'NWC'),
                                 feature_group_count=c)
    return y + b.astype(x.dtype)


def rotary(x, cos, sin):
    half = x.shape[-1] // 2
    x1, x2 = x[..., :half], x[..., half:]
    return jnp.concatenate([x1 * cos - x2 * sin, x2 * cos + x1 * sin], axis=-1).astype(x.dtype)


def split_cols(z, sizes):
    return jnp.split(z, np.cumsum(sizes)[:-1].tolist(), axis=-1)


def heads(a, n_heads):
    B, S, W = a.shape
    return a.reshape(B, S, n_heads, W // n_heads).transpose(0, 2, 1, 3)


def to_chunks(a, L):
    B, H, S = a.shape[:3]
    return jnp.moveaxis(a.reshape(B, H, S // L, L, *a.shape[3:]), 2, 0)


def from_chunks(a):
    nC, B, H, L = a.shape[:4]
    a = jnp.moveaxis(a, 0, 2).reshape(B, H, nC * L, *a.shape[4:])
    return jnp.swapaxes(a, 1, 2)


def mlstm_mixer(qk_pre, v_pre, o_pre, if_pre, conv_w, conv_b, gate_b):
    f32 = jnp.float32
    B, S, _ = v_pre.shape
    qk = jax.nn.silu(causal_dwconv(qk_pre, conv_w, conv_b))
    q, k = jnp.split(qk, 2, axis=-1)
    q = heads(q, MLSTM_HEADS).astype(f32) * (MLSTM_DH ** -0.5)
    k = heads(k, MLSTM_HEADS).astype(f32)
    v = heads(v_pre, MLSTM_HEADS).astype(f32)
    gates = if_pre.astype(f32) + gate_b.astype(f32)
    ig = gates[..., :MLSTM_HEADS].transpose(0, 2, 1)
    lf = jax.nn.log_sigmoid(gates[..., MLSTM_HEADS:]).transpose(0, 2, 1)
    L = MLSTM_CHUNK
    causal = jnp.tril(jnp.ones((L, L), dtype=bool))

    def step(carry, xs):
        c, n, m = carry
        qc, kc, vc, ic, fc = xs
        b = jnp.cumsum(fc, axis=-1)
        log_d = jnp.where(causal, b[..., :, None] - b[..., None, :] + ic[..., None, :], -jnp.inf)
        log_inter = b + m[..., None]
        m_t = jnp.maximum(log_inter, jnp.max(log_d, axis=-1))
        w_inter = jnp.exp(log_inter - m_t)
        s = jnp.einsum('bhtd,bhjd->bhtj', qc, kc) * jnp.exp(log_d - m_t[..., None])
        num = w_inter[..., None] * jnp.einsum('bhtd,bhdv->bhtv', qc, c) + jnp.einsum('bhtj,bhjv->bhtv', s, vc)
        den = w_inter * jnp.einsum('bhtd,bhd->bht', qc, n) + jnp.sum(s, axis=-1)
        h = num / jnp.maximum(jnp.abs(den), jnp.exp(-m_t))[..., None]
        b_end = b[..., -1]
        log_w = b_end[..., None] - b + ic
        m_new = jnp.maximum(b_end + m, jnp.max(log_w, axis=-1))
        carry_decay = jnp.exp(b_end + m - m_new)
        kw = jnp.exp(log_w - m_new[..., None])[..., None] * kc
        c = carry_decay[..., None, None] * c + jnp.einsum('bhjd,bhjv->bhdv', kw, vc)
        n = carry_decay[..., None] * n + jnp.sum(kw, axis=-2)
        return (c, n, m_new), h

    init = (jnp.zeros((B, MLSTM_HEADS, MLSTM_DH, MLSTM_DH), f32),
            jnp.zeros((B, MLSTM_HEADS, MLSTM_DH), f32),
            jnp.zeros((B, MLSTM_HEADS), f32))
    _, h = lax.scan(step, init, (to_chunks(q, L), to_chunks(k, L), to_chunks(v, L),
                                 to_chunks(ig, L), to_chunks(lf, L)))
    h = from_chunks(h).reshape(B, S, MLSTM_HEADS * MLSTM_DH)
    return (jax.nn.sigmoid(o_pre.astype(f32)) * h).astype(v_pre.dtype)


def mla_mixer(c_q, c_kv, k_rope, cos, sin, q_norm_g, w_uq, kv_norm_g, w_ukv):
    B, S, _ = c_q.shape
    q = (rms_norm(c_q, q_norm_g) @ w_uq).reshape(B, S, MLA_HEADS, MLA_NOPE + MLA_ROPE)
    q_nope = q[..., :MLA_NOPE]
    q_rot = rotary(q[..., MLA_NOPE:], cos[:, :, None], sin[:, :, None])
    kv = (rms_norm(c_kv, kv_norm_g) @ w_ukv).reshape(B, S, MLA_HEADS, MLA_NOPE + MLA_V)
    k_nope = kv[..., :MLA_NOPE].transpose(0, 2, 1, 3)
    v = kv[..., MLA_NOPE:].transpose(0, 2, 1, 3)
    k_rot = rotary(k_rope, cos, sin)
    nq = S // Q_BLOCK
    qn_b = jnp.moveaxis(q_nope.reshape(B, nq, Q_BLOCK, MLA_HEADS, MLA_NOPE), 1, 0)
    qr_b = jnp.moveaxis(q_rot.reshape(B, nq, Q_BLOCK, MLA_HEADS, MLA_ROPE), 1, 0)
    scale = (MLA_NOPE + MLA_ROPE) ** -0.5
    k_pos = jnp.arange(S)

    def attend(args):
        qn, qr, blk = args
        s = jnp.einsum('bqhd,bhkd->bhqk', qn, k_nope) + jnp.einsum('bqhd,bkd->bhqk', qr, k_rot)
        q_pos = blk * Q_BLOCK + jnp.arange(Q_BLOCK)
        mask = k_pos[None, :] <= q_pos[:, None]
        s = jnp.where(mask, s.astype(jnp.float32) * scale, -jnp.inf)
        p = jax.nn.softmax(s, axis=-1).astype(v.dtype)
        return jnp.einsum('bhqk,bhkd->bqhd', p, v)

    o = lax.map(attend, (qn_b, qr_b, jnp.arange(nq)))
    return jnp.moveaxis(o, 0, 1).reshape(B, S, MLA_HEADS * MLA_V)


def gla_mixer(q_pre, k_pre, v_pre, r_pre, a_pre, w_alpha, b_alpha):
    f32 = jnp.float32
    B, S, _ = v_pre.shape
    q = heads(q_pre, GLA_HEADS).astype(f32) * (GLA_DK ** -0.5)
    k = heads(k_pre, GLA_HEADS).astype(f32)
    v = heads(v_pre, GLA_HEADS).astype(f32)
    g = jax.nn.log_sigmoid((a_pre @ w_alpha + b_alpha).astype(f32)) / GLA_TAU
    g = heads(g, GLA_HEADS)
    L = GLA_CHUNK
    causal = jnp.tril(jnp.ones((L, L), dtype=bool))

    def step(state, xs):
        qc, kc, vc, gc = xs
        b = jnp.cumsum(gc, axis=-2)
        diff = b[:, :, :, None, :] - b[:, :, None, :, :]
        decay = jnp.exp(jnp.where(causal[:, :, None], diff, -jnp.inf))
        a = jnp.einsum('bhtd,bhjd,bhtjd->bhtj', qc, kc, decay)
        o = jnp.einsum('bhtj,bhjv->bhtv', a, vc) + jnp.einsum('bhtd,bhdv->bhtv', qc * jnp.exp(b), state)
        b_end = b[:, :, -1:, :]
        state = (jnp.exp(b_end)[:, :, 0, :, None] * state
                 + jnp.einsum('bhjd,bhjv->bhdv', kc * jnp.exp(b_end - b), vc))
        return state, o

    init = jnp.zeros((B, GLA_HEADS, GLA_DK, GLA_DV), f32)
    _, o = lax.scan(step, init, (to_chunks(q, L), to_chunks(k, L), to_chunks(v, L), to_chunks(g, L)))
    o = from_chunks(o)
    o = o * lax.rsqrt(jnp.mean(o * o, axis=-1, keepdims=True) + EPS)
    o = o.reshape(B, S, GLA_HEADS * GLA_DV) * jax.nn.silu(r_pre.astype(f32))
    return o.astype(v_pre.dtype)


def conformer_conv(glu_pre, conv_w, conv_b, ln_g, ln_b):
    val, gate = jnp.split(glu_pre, 2, axis=-1)
    u = causal_dwconv(val * jax.nn.sigmoid(gate), conv_w, conv_b)
    return jax.nn.silu(layer_norm(u, ln_g, ln_b))


def swiglu(t, w_gate, w_up, w_down):
    return (jax.nn.silu(t @ w_gate) * (t @ w_up)) @ w_down


def moe_ffn(u, router, w_gate, w_up, w_down):
    B, S, D = u.shape
    t = u.reshape(-1, D)
    n = t.shape[0]
    logits = (t @ router).astype(jnp.float32)
    top_v, top_e = lax.top_k(logits, TOP_K)
    top_w = jax.nn.softmax(top_v, axis=-1)
    a = n * TOP_K
    flat_e = top_e.reshape(-1)
    flat_tok = jnp.repeat(jnp.arange(n, dtype=jnp.int32), TOP_K)
    flat_w = top_w.reshape(-1)
    order = jnp.argsort(flat_e)
    e_sorted = flat_e[order]
    counts = jnp.bincount(flat_e, length=N_EXPERTS)
    starts = jnp.cumsum(counts) - counts
    padded = (counts + MOE_BLOCK - 1) // MOE_BLOCK * MOE_BLOCK
    padded_end = jnp.cumsum(padded)
    padded_start = padded_end - padded
    dest = padded_start[e_sorted] + jnp.arange(a) - starts[e_sorted]
    n_blocks = -(-a // MOE_BLOCK) + N_EXPERTS
    rows = n_blocks * MOE_BLOCK
    row_tok = jnp.zeros((rows,), jnp.int32).at[dest].set(flat_tok[order])
    row_w = jnp.zeros((rows,), jnp.float32).at[dest].set(flat_w[order])
    block_e = jnp.minimum(jnp.searchsorted(padded_end, jnp.arange(n_blocks) * MOE_BLOCK, side='right'),
                          N_EXPERTS - 1)
    xb = t[row_tok].reshape(n_blocks, MOE_BLOCK, D)

    def expert_block(args):
        xe, e = args
        return swiglu(xe, w_gate[e], w_up[e], w_down[e])

    yb = lax.map(expert_block, (xb, block_e)).reshape(rows, D)
    out = jnp.zeros_like(t).at[row_tok].add((yb * row_w[:, None]).astype(t.dtype))
    return out.reshape(B, S, D)


def setup_inputs(seed: int = 0) -> dict:
    key = jax.random.key(seed)
    ks = jax.random.split(key, 32)
    f32 = jnp.float32

    def nrm(k, shape, scale):
        return jax.random.normal(k, shape, f32) * scale

    def gain(k, shape):
        return 1.0 + nrm(k, shape, 0.02)

    offset = jax.random.randint(ks[1], (BATCH, 1), 0, 1024, dtype=jnp.int32)
    positions = offset + jnp.arange(SEQ, dtype=jnp.int32)[None, :]
    gate_i = nrm(ks[6], (DEPTH, MLSTM_HEADS), 0.1)
    gate_f = jnp.linspace(3.0, 6.0, MLSTM_HEADS, dtype=f32)[None, :] + nrm(ks[7], (DEPTH, MLSTM_HEADS), 0.1)
    return {
        'x': nrm(ks[0], (BATCH, SEQ, D_MODEL), 1.0),
        'positions': positions,
        'norm1_g': gain(ks[2], (DEPTH, D_MODEL)),
        'w_in': nrm(ks[3], (DEPTH, D_MODEL, D_IN), D_MODEL ** -0.5),
        'mlstm_conv_w': nrm(ks[4], (DEPTH, MLSTM_CONV, 2 * MLSTM_HEADS * MLSTM_DH), MLSTM_CONV ** -0.5),
        'mlstm_conv_b': nrm(ks[5], (DEPTH, 2 * MLSTM_HEADS * MLSTM_DH), 0.02),
        'mlstm_gate_b': jnp.concatenate([gate_i, gate_f], axis=-1),
        'mla_q_norm_g': gain(ks[8], (DEPTH, MLA_Q_RANK)),
        'mla_w_uq': nrm(ks[9], (DEPTH, MLA_Q_RANK, MLA_HEADS * (MLA_NOPE + MLA_ROPE)), MLA_Q_RANK ** -0.5),
        'mla_kv_norm_g': gain(ks[10], (DEPTH, MLA_KV_RANK)),
        'mla_w_ukv': nrm(ks[11], (DEPTH, MLA_KV_RANK, MLA_HEADS * (MLA_NOPE + MLA_V)), MLA_KV_RANK ** -0.5),
        'gla_w_alpha': nrm(ks[12], (DEPTH, GLA_GATE_RANK, GLA_HEADS * GLA_DK), GLA_GATE_RANK ** -0.5),
        'gla_b_alpha': nrm(ks[13], (DEPTH, GLA_HEADS * GLA_DK), 0.1),
        'conv_w': nrm(ks[14], (DEPTH, CONV_WIDTH, CONV_CH), CONV_WIDTH ** -0.5),
        'conv_b': nrm(ks[15], (DEPTH, CONV_CH), 0.02),
        'conv_ln_g': gain(ks[16], (DEPTH, CONV_CH)),
        'conv_ln_b': nrm(ks[17], (DEPTH, CONV_CH), 0.02),
        'fuse_g': gain(ks[18], (DEPTH, D_MODEL)),
        'w_out': nrm(ks[19], (DEPTH, D_MODEL, D_MODEL), D_MODEL ** -0.5),
        'norm2_g': gain(ks[20], (DEPTH, D_MODEL)),
        'ffn_w_gate': nrm(ks[21], (N_DENSE, D_MODEL, D_FF), D_MODEL ** -0.5),
        'ffn_w_up': nrm(ks[22], (N_DENSE, D_MODEL, D_FF), D_MODEL ** -0.5),
        'ffn_w_down': nrm(ks[23], (N_DENSE, D_FF, D_MODEL), D_FF ** -0.5),
        'moe_router': nrm(ks[24], (N_MOE, D_MODEL, N_EXPERTS), D_MODEL ** -0.5),
        'moe_w_gate': nrm(ks[25], (N_MOE, N_EXPERTS, D_MODEL, D_FF), D_MODEL ** -0.5),
        'moe_w_up': nrm(ks[26], (N_MOE, N_EXPERTS, D_MODEL, D_FF), D_MODEL ** -0.5),
        'moe_w_down': nrm(ks[27], (N_MOE, N_EXPERTS, D_FF, D_MODEL), D_FF ** -0.5),
        'final_norm_g': gain(ks[28], (D_MODEL,)),
    }


def reference(x, positions, norm1_g, w_in, mlstm_conv_w, mlstm_conv_b, mlstm_gate_b,
              mla_q_norm_g, mla_w_uq, mla_kv_norm_g, mla_w_ukv, gla_w_alpha, gla_b_alpha,
              conv_w, conv_b, conv_ln_g, conv_ln_b, fuse_g, w_out, norm2_g,
              ffn_w_gate, ffn_w_up, ffn_w_down, moe_router, moe_w_gate, moe_w_up, moe_w_down,
              final_norm_g):
    inv_freq = 1.0 / (ROPE_BASE ** (jnp.arange(0, MLA_ROPE, 2, dtype=jnp.float32) / MLA_ROPE))
    ang = positions.astype(jnp.float32)[..., None] * inv_freq
    cos, sin = jnp.cos(ang), jnp.sin(ang)
    h = x
    for l in range(DEPTH):
        z = rms_norm(h, norm1_g[l]) @ w_in[l]
        (m_qk, m_v, m_o, m_if, a_cq, a_ckv, a_kr,
         g_q, g_k, g_v, g_r, g_a, c_glu) = split_cols(z, IN_SPLITS)
        y = jnp.concatenate([
            mlstm_mixer(m_qk, m_v, m_o, m_if, mlstm_conv_w[l], mlstm_conv_b[l], mlstm_gate_b[l]),
            mla_mixer(a_cq, a_ckv, a_kr, cos, sin, mla_q_norm_g[l], mla_w_uq[l],
                      mla_kv_norm_g[l], mla_w_ukv[l]),
            gla_mixer(g_q, g_k, g_v, g_r, g_a, gla_w_alpha[l], gla_b_alpha[l]),
            conformer_conv(c_glu, conv_w[l], conv_b[l], conv_ln_g[l], conv_ln_b[l]),
        ], axis=-1)
        h = h + group_rms_norm(y, fuse_g[l], N_GROUPS) @ w_out[l]
        u = rms_norm(h, norm2_g[l])
        j = l // 2
        if l % 2 == 0:
            f = swiglu(u, ffn_w_gate[j], ffn_w_up[j], ffn_w_down[j])
        else:
            f = moe_ffn(u, moe_router[j], moe_w_gate[j], moe_w_up[j], moe_w_down[j])
        h = h + f
    return rms_norm(h, final_norm_g)
```

```python
import functools

import jax
import jax.numpy as jnp
import numpy as np
from jax import lax
from jax.experimental import pallas as pl
from jax.experimental.pallas import tpu as pltpu

F32 = jnp.float32
BF16 = jnp.bfloat16
HI = lax.Precision.HIGHEST

D_MODEL = 1024
GROUP_W = 256
EPS = 1e-6
LANES = 128
CHUNK = 64
MLSTM_HEADS = 4
MLSTM_CONV = 4
MLA_HEADS = 4
MLA_NOPE = 64
MLA_ROPE = 32
MLA_V = 64
MLA_Q_RANK = 192
MLA_KV_RANK = 128
ROPE_BASE = 10000.0
GLA_HEADS = 4
GLA_DK = 32
GLA_DV = 64
GLA_GATE_RANK = 16
GLA_TAU = 16.0
CONV_WIDTH = 31
D_FF = 2816
N_EXPERTS = 8
TOP_K = 2
MOE_BM = 256
VMEM_LIMIT = 56 * 1024 * 1024

ZM_W = 1280
ZA_W = 512
ZG_W = 896
ZC_W = 512


def _cparams(sem):
    return pltpu.CompilerParams(dimension_semantics=sem, vmem_limit_bytes=VMEM_LIMIT)


def _const_spec(shape):
    nd = len(shape)
    return pl.BlockSpec(shape, lambda *_: (0,) * nd, pipeline_mode=pl.Buffered(1))


def _sigmoid(x):
    return 1.0 / (1.0 + jnp.exp(-x))


def _log_sigmoid(x):
    return jnp.minimum(x, 0.0) - jnp.log(1.0 + jnp.exp(-jnp.abs(x)))


def _iota(shape, dim):
    return lax.broadcasted_iota(jnp.int32, shape, dim)


def _tri(n):
    return (_iota((n, n), 0) >= _iota((n, n), 1)).astype(F32)


def _inproj_kernel(h_ref, g_ref, wm_ref, wa_ref, wg_ref, wc_ref, zm_ref, za_ref, zg_ref, zc_ref):
    x = h_ref[...]
    ms = jnp.mean(x * x, axis=-1, keepdims=True)
    xn = (x * lax.rsqrt(ms + EPS) * g_ref[...]).astype(BF16)
    zm_ref[...] = jnp.dot(xn, wm_ref[...], preferred_element_type=F32)
    za_ref[...] = jnp.dot(xn, wa_ref[...], preferred_element_type=F32)
    zg_ref[...] = jnp.dot(xn, wg_ref[...], preferred_element_type=F32)
    zc_ref[...] = jnp.dot(xn, wc_ref[...], preferred_element_type=F32)


def _inproj(h, g, wm, wa, wg, wc, tm):
    n = h.shape[0]
    row = lambda w: pl.BlockSpec((tm, w), lambda i: (i, 0))
    return pl.pallas_call(
        _inproj_kernel,
        grid=(n // tm,),
        in_specs=[row(D_MODEL), _const_spec((1, D_MODEL)), _const_spec(wm.shape), _const_spec(wa.shape),
                  _const_spec(wg.shape), _const_spec(wc.shape)],
        out_specs=[row(ZM_W), row(ZA_W), row(ZG_W), row(ZC_W)],
        out_shape=[jax.ShapeDtypeStruct((n, w), F32) for w in (ZM_W, ZA_W, ZG_W, ZC_W)],
        compiler_params=_cparams(("parallel",)),
        name="inproj",
    )(h, g, wm, wa, wg, wc)


def _expand_heads(x, width):
    r = x.shape[0]
    lane = _iota((r, 4 * width), 1)
    out = jnp.broadcast_to(x[:, 3:4], (r, 4 * width))
    for h in (2, 1, 0):
        out = jnp.where(lane < (h + 1) * width, jnp.broadcast_to(x[:, h:h + 1], (r, 4 * width)), out)
    return out


def _mlstm_kernel(zm_ref, grow_ref, cw_ref, cb_ref, gbc_ref, gbr_ref, y_ref,
                  xpad_ref, q_ref, k_ref, grs_ref, cn_ref, m_ref, *, tt):
    nct = tt // CHUNK
    L = CHUNK
    W = MLSTM_HEADS * 64

    @pl.when(pl.program_id(1) == 0)
    def _():
        xpad_ref[0:8, :] = jnp.zeros((8, 2 * W), F32)
        cn_ref[...] = jnp.zeros_like(cn_ref)
        m_ref[...] = jnp.zeros_like(m_ref)

    xpad_ref[8:8 + tt, :] = zm_ref[0, :, 0:2 * W]
    acc = jnp.broadcast_to(cb_ref[...], (tt, 2 * W))
    for j in range(MLSTM_CONV):
        acc = acc + cw_ref[j:j + 1, :] * xpad_ref[pl.ds(8 - (MLSTM_CONV - 1) + j, tt), :]
    xpad_ref[0:8, :] = xpad_ref[tt:tt + 8, :]
    qk = acc * _sigmoid(acc)
    q_ref[...] = (qk[:, :W] * (64 ** -0.5)).astype(BF16)
    k_ref[...] = qk[:, W:]

    hsame = ((_iota((W, W), 0) // L) == (_iota((W, W), 1) // L))
    trit = (hsame & ((_iota((W, W), 0) % L) <= (_iota((W, W), 1) % L))).astype(F32)
    i_row = grow_ref[0, 0] + gbr_ref[0:1, :]
    lf_row = _log_sigmoid(grow_ref[0, 1] + gbr_ref[1:2, :])
    grs_ref[...] = i_row - jnp.dot(lf_row, trit, precision=HI, preferred_element_type=F32)

    tri = _tri(L)
    row_t = _iota((L, W), 0)
    lane_j = _iota((L, W), 1) % L
    causal = lane_j <= row_t
    rowblk = _iota((W, W), 0) // L
    blockmask = rowblk == (_iota((W, W), 1) // L)
    cn_mask = jnp.concatenate([blockmask, (_iota((W, LANES), 0) // L) == _iota((W, LANES), 1)], axis=1)
    ones_col = (_iota((L, LANES), 1) < MLSTM_HEADS).astype(F32)
    row_l = _iota((L, LANES), 0)

    def chunk(c, carry):
        r0 = pl.multiple_of(c * L, L)
        ic = zm_ref[0, pl.ds(r0, L), 4 * W:4 * W + LANES] + gbc_ref[0:1, :]
        lf = _log_sigmoid(zm_ref[0, pl.ds(r0, L), 4 * W + LANES:4 * W + 2 * LANES] + gbc_ref[1:2, :])
        b = jnp.dot(tri, lf, precision=HI, preferred_element_type=F32)
        g = ic - b
        cm = g
        s = 1
        while s < L:
            cm = jnp.maximum(cm, jnp.where(row_l >= s, pltpu.roll(cm, s, 0), -jnp.inf))
            s *= 2
        m_prev = m_ref[...]
        mx = jnp.maximum(m_prev, cm)
        w_inter = jnp.exp(m_prev - mx)
        b_end = b[L - 1:L, :]
        mx_end = mx[L - 1:L, :]
        m_new = b_end + mx_end
        decay = jnp.exp(m_prev - mx_end)
        kw_col = jnp.exp(g - mx_end)

        qc = q_ref[pl.ds(r0, L), :]
        kc = k_ref[pl.ds(r0, L), :]
        vc = zm_ref[0, pl.ds(r0, L), 2 * W:3 * W]
        oc = zm_ref[0, pl.ds(r0, L), 3 * W:4 * W]

        g_row = grs_ref[pl.ds(c, 1), :]
        dmat = jnp.where(causal, jnp.exp(jnp.where(causal, g_row - _expand_heads(mx, L), 0.0)), 0.0)
        kbd = jnp.where(blockmask, jnp.concatenate([kc] * 4, axis=0), 0.0).astype(BF16)
        s_qk = lax.dot_general(qc, kbd, (((1,), (1,)), ((), ())), preferred_element_type=F32)
        s_w = (s_qk * dmat).astype(BF16)
        vaug = jnp.concatenate([vc, ones_col], axis=1)
        vbd = jnp.where(cn_mask, jnp.concatenate([vaug] * 4, axis=0), 0.0).astype(BF16)
        intra = jnp.dot(s_w, vbd, preferred_element_type=F32)
        inter = jnp.dot(qc, cn_ref[...].astype(BF16), preferred_element_type=F32)
        num = _expand_heads(w_inter, L) * inter[:, :W] + intra[:, :W]
        den = w_inter * inter[:, W:] + intra[:, W:]
        hden = jnp.maximum(jnp.abs(den), jnp.exp(-(b + mx)))
        hout = num * _expand_heads(1.0 / hden, L)
        y_ref[0, pl.ds(r0, L), :] = _sigmoid(oc) * hout

        kw = (_expand_heads(kw_col, L) * kc).astype(BF16)
        upd = lax.dot_general(kw, vaug.astype(BF16), (((0,), (0,)), ((), ())), preferred_element_type=F32)
        decay_x = jnp.concatenate([_expand_heads(decay, L), decay], axis=1)
        cn_ref[...] = decay_x * cn_ref[...] + jnp.where(cn_mask, upd, 0.0)
        m_ref[...] = m_new
        return carry

    lax.fori_loop(0, nct, chunk, 0)


def _mlstm(zm, grow, cw, cb, gbc, gbr, tt):
    b, s, _ = zm.shape
    nct = tt // CHUNK
    w2 = 2 * GROUP_W
    return pl.pallas_call(
        functools.partial(_mlstm_kernel, tt=tt),
        grid=(b, s // tt),
        in_specs=[pl.BlockSpec((1, tt, ZM_W), lambda i, t: (i, t, 0)),
                  pl.BlockSpec((1, 2, nct, GROUP_W), lambda i, t: (i, 0, t, 0)),
                  _const_spec(cw.shape), _const_spec(cb.shape), _const_spec(gbc.shape), _const_spec(gbr.shape)],
        out_specs=pl.BlockSpec((1, tt, GROUP_W), lambda i, t: (i, t, 0)),
        out_shape=jax.ShapeDtypeStruct((b, s, GROUP_W), F32),
        scratch_shapes=[pltpu.VMEM((tt + 8, w2), F32), pltpu.VMEM((tt, GROUP_W), BF16),
                        pltpu.VMEM((tt, GROUP_W), F32), pltpu.VMEM((nct, GROUP_W), F32),
                        pltpu.VMEM((GROUP_W, GROUP_W + LANES), F32), pltpu.VMEM((1, LANES), F32)],
        compiler_params=_cparams(("parallel", "arbitrary")),
        name="mlstm",
    )(zm, grow, cw, cb, gbc, gbr)


def _gla_kernel(zg_ref, wa_ref, ba_ref, y_ref, q_ref, k_ref, g_ref, o_ref, st_ref, *, tt):
    nct = tt // CHUNK
    L = CHUNK
    WK = GLA_HEADS * GLA_DK
    WV = GLA_HEADS * GLA_DV

    @pl.when(pl.program_id(1) == 0)
    def _():
        st_ref[...] = jnp.zeros_like(st_ref)

    q_ref[...] = zg_ref[0, :, 0:WK] * (GLA_DK ** -0.5)
    k_ref[...] = zg_ref[0, :, WK:2 * WK]
    a = zg_ref[0, :, 2 * WK + 2 * WV:2 * WK + 2 * WV + LANES]
    pre = jnp.dot(a, wa_ref[...], precision=HI, preferred_element_type=F32) + ba_ref[...]
    g_ref[...] = _log_sigmoid(pre) * (1.0 / GLA_TAU)

    tri = _tri(L)
    hs2 = ((_iota((WK, WV), 0) // GLA_DK) == (_iota((WK, WV), 1) // GLA_DV)).astype(BF16)
    st_mask = (_iota((WV, WK), 0) // GLA_DV) == (_iota((WV, WK), 1) // GLA_DK)

    def chunk(c, carry):
        r0 = pl.multiple_of(c * L, L)
        qc = q_ref[pl.ds(r0, L), :]
        kc = k_ref[pl.ds(r0, L), :]
        vc = zg_ref[0, pl.ds(r0, L), 2 * WK:2 * WK + WV]
        b = jnp.dot(tri, g_ref[pl.ds(r0, L), :], precision=HI, preferred_element_type=F32)
        st = st_ref[...]
        qd = (qc * jnp.exp(b)).astype(BF16)
        o = lax.dot_general(qd, st.astype(BF16), (((1,), (1,)), ((), ())), preferred_element_type=F32)

        pieces = []
        for jb in range(L // 8):
            t0 = 8 * jb
            rows = L - t0
            qs, bs = qc[t0:, :], b[t0:, :]
            tl = _iota((rows, WK), 0)
            ps = []
            for jj in range(8):
                j = t0 + jj
                e = jnp.exp(jnp.minimum(bs - b[j:j + 1, :], 0.0))
                ps.append(jnp.where(tl >= jj, qs * kc[j:j + 1, :] * e, 0.0).astype(BF16))
            rexp = jnp.dot(jnp.concatenate(ps, axis=0), hs2, preferred_element_type=F32)
            acc = rexp[0:rows, :] * vc[t0:t0 + 1, :]
            for jj in range(1, 8):
                acc = acc + rexp[jj * rows:(jj + 1) * rows, :] * vc[t0 + jj:t0 + jj + 1, :]
            pieces.append(acc)
        bands = []
        for band in range(L // 8):
            tot = None
            for jb in range(band + 1):
                lo = 8 * (band - jb)
                part = pieces[jb][lo:lo + 8, :]
                tot = part if tot is None else tot + part
            bands.append(tot)
        o = o + jnp.concatenate(bands, axis=0)
        o_ref[pl.ds(r0, L), :] = o

        b_end = b[L - 1:L, :]
        kd = (kc * jnp.exp(b_end - b)).astype(BF16)
        upd = lax.dot_general(vc.astype(BF16), kd, (((0,), (0,)), ((), ())), preferred_element_type=F32)
        st_ref[...] = st * jnp.exp(b_end) + jnp.where(st_mask, upd, 0.0)
        return carry

    lax.fori_loop(0, nct, chunk, 0)

    o = o_ref[...]
    hsame = ((_iota((WV, WV), 0) // GLA_DV) == (_iota((WV, WV), 1) // GLA_DV)).astype(F32)
    ms = jnp.dot(o * o, hsame, precision=HI, preferred_element_type=F32) * (1.0 / GLA_DV)
    r = zg_ref[0, :, 2 * WK + WV:2 * WK + 2 * WV]
    y_ref[0] = o * lax.rsqrt(ms + EPS) * (r * _sigmoid(r))


def _gla(zg, wa, ba, tt):
    b, s, _ = zg.shape
    return pl.pallas_call(
        functools.partial(_gla_kernel, tt=tt),
        grid=(b, s // tt),
        in_specs=[pl.BlockSpec((1, tt, ZG_W), lambda i, t: (i, t, 0)), _const_spec(wa.shape), _const_spec(ba.shape)],
        out_specs=pl.BlockSpec((1, tt, GROUP_W), lambda i, t: (i, t, 0)),
        out_shape=jax.ShapeDtypeStruct((b, s, GROUP_W), F32),
        scratch_shapes=[pltpu.VMEM((tt, LANES), F32), pltpu.VMEM((tt, LANES), F32), pltpu.VMEM((tt, LANES), F32),
                        pltpu.VMEM((tt, GROUP_W), F32), pltpu.VMEM((GROUP_W, LANES), F32)],
        compiler_params=_cparams(("parallel", "arbitrary")),
        name="gla",
    )(zg, wa, ba)


CONV_HIST = 32


def _conv_kernel(zc_ref, cw_ref, cb_ref, lg_ref, lb_ref, y_ref, xpad_ref, *, tt):
    C = GROUP_W

    @pl.when(pl.program_id(1) == 0)
    def _():
        xpad_ref[0:CONV_HIST, :] = jnp.zeros((CONV_HIST, C), F32)

    val = zc_ref[0, :, 0:C]
    gate = zc_ref[0, :, C:2 * C]
    xpad_ref[CONV_HIST:CONV_HIST + tt, :] = val * _sigmoid(gate)
    acc = jnp.broadcast_to(cb_ref[...], (tt, C))
    for j in range(CONV_WIDTH):
        acc = acc + cw_ref[j:j + 1, :] * xpad_ref[pl.ds(CONV_HIST - (CONV_WIDTH - 1) + j, tt), :]
    xpad_ref[0:CONV_HIST, :] = xpad_ref[tt:tt + CONV_HIST, :]
    mu = jnp.mean(acc, axis=-1, keepdims=True)
    d = acc - mu
    var = jnp.mean(d * d, axis=-1, keepdims=True)
    yn = d * lax.rsqrt(var + EPS) * lg_ref[...] + lb_ref[...]
    y_ref[0] = yn * _sigmoid(yn)


def _conformer(zc, cw, cb, lg, lb, tt):
    b, s, _ = zc.shape
    return pl.pallas_call(
        functools.partial(_conv_kernel, tt=tt),
        grid=(b, s // tt),
        in_specs=[pl.BlockSpec((1, tt, ZC_W), lambda i, t: (i, t, 0)), _const_spec(cw.shape), _const_spec(cb.shape),
                  _const_spec(lg.shape), _const_spec(lb.shape)],
        out_specs=pl.BlockSpec((1, tt, GROUP_W), lambda i, t: (i, t, 0)),
        out_shape=jax.ShapeDtypeStruct((b, s, GROUP_W), F32),
        scratch_shapes=[pltpu.VMEM((tt + CONV_HIST, GROUP_W), F32)],
        compiler_params=_cparams(("parallel", "arbitrary")),
        name="conformer",
    )(zc, cw, cb, lg, lb)


MLA_HP = 128


def _mla_prep_kernel(za_ref, rc_ref, rs_ref, qg_ref, kg_ref, wqa_ref, wqb_ref, wka_ref, wkb_ref, wv_ref,
                     q_ref, k_ref, v_ref):
    cq = za_ref[0, :, 0:256]
    ckv = za_ref[0, :, 256:384]
    kr = za_ref[0, :, 384:512]
    qn = (cq * lax.rsqrt(jnp.sum(cq * cq, axis=-1, keepdims=True) * (1.0 / MLA_Q_RANK) + EPS) * qg_ref[...]).astype(BF16)
    kvn = ckv * lax.rsqrt(jnp.mean(ckv * ckv, axis=-1, keepdims=True) + EPS) * kg_ref[...]
    kin = jnp.concatenate([kvn, kr], axis=1).astype(BF16)
    cos = rc_ref[0]
    sin = rs_ref[0]
    qa = jnp.dot(qn, wqa_ref[...], preferred_element_type=F32)
    qb = jnp.dot(qn, wqb_ref[...], preferred_element_type=F32)
    ka = jnp.dot(kin, wka_ref[...], preferred_element_type=F32)
    kb = jnp.dot(kin, wkb_ref[...], preferred_element_type=F32)
    vv = jnp.dot(kin, wv_ref[...], preferred_element_type=F32)
    scale = (MLA_NOPE + MLA_ROPE) ** -0.5
    for h in range(MLA_HEADS):
        sl = slice(h * MLA_HP, (h + 1) * MLA_HP)
        q_ref[0, h] = ((qa[:, sl] * cos + qb[:, sl] * sin) * scale).astype(BF16)
        k_ref[0, h] = (ka[:, sl] * cos + kb[:, sl] * sin).astype(BF16)
        v_ref[0, h] = vv[:, sl].astype(BF16)


def _mla_prep(za, rc, rs, qg, kg, wqa, wqb, wka, wkb, wv, tt):
    b, s, _ = za.shape
    hspec = pl.BlockSpec((1, MLA_HEADS, tt, MLA_HP), lambda i, t: (i, 0, t, 0))
    tspec = pl.BlockSpec((1, tt, MLA_HP), lambda i, t: (i, t, 0))
    hshape = jax.ShapeDtypeStruct((b, MLA_HEADS, s, MLA_HP), BF16)
    return pl.pallas_call(
        _mla_prep_kernel,
        grid=(b, s // tt),
        in_specs=[pl.BlockSpec((1, tt, ZA_W), lambda i, t: (i, t, 0)), tspec, tspec,
                  _const_spec(qg.shape), _const_spec(kg.shape), _const_spec(wqa.shape), _const_spec(wqb.shape),
                  _const_spec(wka.shape), _const_spec(wkb.shape), _const_spec(wv.shape)],
        out_specs=[hspec, hspec, hspec],
        out_shape=[hshape, hshape, hshape],
        compiler_params=_cparams(("parallel", "parallel")),
        name="mla_prep",
    )(za, rc, rs, qg, kg, wqa, wqb, wka, wkb, wv)


def _mla_attn_kernel(q_ref, k_ref, v_ref, y_ref, m_ref, l_ref, acc_ref, *, tq):
    qi = pl.program_id(1)
    ki = pl.program_id(2)

    @pl.when(ki == 0)
    def _():
        m_ref[...] = jnp.full_like(m_ref, -jnp.inf)
        l_ref[...] = jnp.zeros_like(l_ref)
        acc_ref[...] = jnp.zeros_like(acc_ref)

    def step(masked):
        lane_lo = _iota((tq, LANES), 1) < MLA_V
        for h in range(MLA_HEADS):
            s = lax.dot_general(q_ref[0, h], k_ref[0, h], (((1,), (1,)), ((), ())), preferred_element_type=F32)
            if masked:
                s = jnp.where(_iota((tq, tq), 1) <= _iota((tq, tq), 0), s, -jnp.inf)
            m_prev = m_ref[h]
            m_new = jnp.maximum(m_prev, jnp.max(s, axis=-1, keepdims=True))
            alpha = jnp.exp(m_prev - m_new)
            p = jnp.exp(s - m_new)
            l_ref[h] = alpha * l_ref[h] + jnp.sum(p, axis=-1, keepdims=True)
            m_ref[h] = m_new
            pv = jnp.dot(p.astype(BF16), v_ref[0, h], preferred_element_type=F32)
            pair = h // 2
            mine = lane_lo if h % 2 == 0 else jnp.logical_not(lane_lo)
            acc_ref[pair] = jnp.where(mine, alpha * acc_ref[pair], acc_ref[pair]) + pv

    @pl.when(ki < qi)
    def _():
        step(False)

    @pl.when(ki == qi)
    def _():
        step(True)
        lane_lo = _iota((tq, LANES), 1) < MLA_V
        for pair in range(MLA_HEADS // 2):
            inv = jnp.where(lane_lo, 1.0 / l_ref[2 * pair], 1.0 / l_ref[2 * pair + 1])
            y_ref[0, :, pair * LANES:(pair + 1) * LANES] = acc_ref[pair] * inv


def _mla_attn(q, k, v, tq):
    b, _, s, _ = q.shape
    nq = s // tq
    qspec = pl.BlockSpec((1, MLA_HEADS, tq, MLA_HP), lambda i, qi, ki: (i, 0, qi, 0))
    kspec = pl.BlockSpec((1, MLA_HEADS, tq, MLA_HP), lambda i, qi, ki: (i, 0, jnp.minimum(ki, qi), 0))
    return pl.pallas_call(
        functools.partial(_mla_attn_kernel, tq=tq),
        grid=(b, nq, nq),
        in_specs=[qspec, kspec, kspec],
        out_specs=pl.BlockSpec((1, tq, GROUP_W), lambda i, qi, ki: (i, qi, 0)),
        out_shape=jax.ShapeDtypeStruct((b, s, GROUP_W), F32),
        scratch_shapes=[pltpu.VMEM((MLA_HEADS, tq, 1), F32), pltpu.VMEM((MLA_HEADS, tq, 1), F32),
                        pltpu.VMEM((MLA_HEADS // 2, tq, LANES), F32)],
        compiler_params=_cparams(("parallel", "parallel", "arbitrary")),
        name="mla_attn",
    )(q, k, v)


def _outproj_kernel(*refs, routed):
    if routed:
        ym, ya, yg, yc, h_ref, fg_ref, wo_ref, n2_ref, rt_ref, hn_ref, u_ref, route_ref = refs
    else:
        ym, ya, yg, yc, h_ref, fg_ref, wo_ref, n2_ref, hn_ref, u_ref = refs
    parts = []
    for gi, y_ref in enumerate((ym, ya, yg, yc)):
        y = y_ref[...]
        yn = y * lax.rsqrt(jnp.mean(y * y, axis=-1, keepdims=True) + EPS)
        parts.append((yn * fg_ref[:, gi * GROUP_W:(gi + 1) * GROUP_W]).astype(BF16))
    ycat = jnp.concatenate(parts, axis=1)
    hn = h_ref[...] + jnp.dot(ycat, wo_ref[...], preferred_element_type=F32)
    hn_ref[...] = hn
    u = hn * lax.rsqrt(jnp.mean(hn * hn, axis=-1, keepdims=True) + EPS) * n2_ref[...]
    u_ref[...] = u.astype(u_ref.dtype)
    if routed:
        tm = u.shape[0]
        lane = _iota((tm, LANES), 1)
        logits = jnp.dot(u, rt_ref[...], precision=HI, preferred_element_type=F32)
        logits = jnp.where(lane < N_EXPERTS, logits, -jnp.inf)
        m1 = jnp.max(logits, axis=-1, keepdims=True)
        i1 = jnp.min(jnp.where(logits == m1, lane, LANES), axis=-1, keepdims=True)
        rest = jnp.where(lane == i1, -jnp.inf, logits)
        m2 = jnp.max(rest, axis=-1, keepdims=True)
        i2 = jnp.min(jnp.where(rest == m2, lane, LANES), axis=-1, keepdims=True)
        e2 = jnp.exp(m2 - m1)
        w1 = 1.0 / (1.0 + e2)
        w2 = e2 / (1.0 + e2)
        route_ref[...] = jnp.where(lane == 0, i1.astype(F32),
                                   jnp.where(lane == 1, i2.astype(F32),
                                             jnp.where(lane == 2, w1, jnp.where(lane == 3, w2, 0.0))))


def _outproj(ym, ya, yg, yc, h, fg, wo, n2, router, tm):
    n = h.shape[0]
    routed = router is not None
    row = lambda w: pl.BlockSpec((tm, w), lambda i: (i, 0))
    in_specs = [row(GROUP_W)] * 4 + [row(D_MODEL), _const_spec(fg.shape), _const_spec(wo.shape), _const_spec(n2.shape)]
    args = [ym, ya, yg, yc, h, fg, wo, n2]
    out_specs = [row(D_MODEL), row(D_MODEL)]
    out_shape = [jax.ShapeDtypeStruct((n, D_MODEL), F32), jax.ShapeDtypeStruct((n, D_MODEL), F32 if routed else BF16)]
    if routed:
        in_specs.append(_const_spec(router.shape))
        args.append(router)
        out_specs.append(row(LANES))
        out_shape.append(jax.ShapeDtypeStruct((n, LANES), F32))
    return pl.pallas_call(
        functools.partial(_outproj_kernel, routed=routed),
        grid=(n // tm,),
        in_specs=in_specs, out_specs=out_specs, out_shape=out_shape,
        compiler_params=_cparams(("parallel",)),
        name="outproj_routed" if routed else "outproj",
    )(*args)


def _swiglu(x, wg, wu, wd):
    g = jnp.dot(x, wg, preferred_element_type=F32)
    up = jnp.dot(x, wu, preferred_element_type=F32)
    a = (g * _sigmoid(g) * up).astype(BF16)
    return jnp.dot(a, wd, preferred_element_type=F32)


def _ffn_kernel(u_ref, h_ref, wg_ref, wu_ref, wd_ref, o_ref):
    o_ref[...] = h_ref[...] + _swiglu(u_ref[...], wg_ref[...], wu_ref[...], wd_ref[...])


def _ffn(u, h, wg, wu, wd, tm):
    n = h.shape[0]
    row = pl.BlockSpec((tm, D_MODEL), lambda i: (i, 0))
    return pl.pallas_call(
        _ffn_kernel,
        grid=(n // tm,),
        in_specs=[row, row, _const_spec(wg.shape), _const_spec(wu.shape), _const_spec(wd.shape)],
        out_specs=row,
        out_shape=jax.ShapeDtypeStruct((n, D_MODEL), F32),
        compiler_params=_cparams(("parallel",)),
        name="ffn_dense",
    )(u, h, wg, wu, wd)


def _row_copy(src_hbm, row, dst, slot, sem):
    return pltpu.make_async_copy(src_hbm.at[pl.ds(row, 1), :], dst.at[pl.ds(slot, 1), :], sem)


def _moe_kernel(be_ref, tok_ref, u_hbm, wg_ref, wu_ref, wd_ref, y_ref, xbuf, sem):
    del be_ref
    bm = xbuf.shape[0]

    def start(r, c):
        _row_copy(u_hbm, tok_ref[0, 0, r], xbuf, r, sem).start()
        return c

    lax.fori_loop(0, bm, start, 0)

    def wait(r, c):
        _row_copy(u_hbm, 0, xbuf, r, sem).wait()
        return c

    lax.fori_loop(0, bm, wait, 0)
    y_ref[...] = _swiglu(xbuf[...].astype(BF16), wg_ref[0], wu_ref[0], wd_ref[0])


def _moe(block_e, row_tok, u, wg, wu, wd):
    n_blocks = block_e.shape[0]
    bm = row_tok.shape[-1]
    wspec = lambda shp: pl.BlockSpec((1,) + shp, lambda i, be: (be[i], 0, 0))
    grid_spec = pltpu.PrefetchScalarGridSpec(
        num_scalar_prefetch=1,
        grid=(n_blocks,),
        in_specs=[pl.BlockSpec((1, 1, bm), lambda i, be: (i, 0, 0), memory_space=pltpu.SMEM),
                  pl.BlockSpec(memory_space=pl.ANY),
                  wspec((D_MODEL, D_FF)), wspec((D_MODEL, D_FF)), wspec((D_FF, D_MODEL))],
        out_specs=pl.BlockSpec((bm, D_MODEL), lambda i, be: (i, 0)),
        scratch_shapes=[pltpu.VMEM((bm, D_MODEL), F32), pltpu.SemaphoreType.DMA(())],
    )
    return pl.pallas_call(
        _moe_kernel,
        grid_spec=grid_spec,
        out_shape=jax.ShapeDtypeStruct((n_blocks * bm, D_MODEL), F32),
        compiler_params=_cparams(("arbitrary",)),
        name="moe_experts",
    )(block_e, row_tok, u, wg, wu, wd)


def _combine_kernel(pos_ref, y_hbm, h_ref, route_ref, fg_ref, o_ref, ybuf, sem, *, final):
    tm = h_ref.shape[0]

    def start(r, c):
        _row_copy(y_hbm, pos_ref[0, 0, 2 * r], ybuf.at[0], r, sem).start()
        _row_copy(y_hbm, pos_ref[0, 0, 2 * r + 1], ybuf.at[1], r, sem).start()
        return c

    lax.fori_loop(0, tm, start, 0)

    def wait(r, c):
        _row_copy(y_hbm, 0, ybuf.at[0], r, sem).wait()
        _row_copy(y_hbm, 0, ybuf.at[1], r, sem).wait()
        return c

    lax.fori_loop(0, tm, wait, 0)
    w1 = route_ref[:, 2:3]
    w2 = route_ref[:, 3:4]
    hn = h_ref[...] + w1 * ybuf[0] + w2 * ybuf[1]
    if final:
        hn = hn * lax.rsqrt(jnp.mean(hn * hn, axis=-1, keepdims=True) + EPS) * fg_ref[...]
    o_ref[...] = hn


def _combine(pos, yb, h, route, fg, tm, final):
    n = h.shape[0]
    return pl.pallas_call(
        functools.partial(_combine_kernel, final=final),
        grid=(n // tm,),
        in_specs=[pl.BlockSpec((1, 1, 2 * tm), lambda i: (i, 0, 0), memory_space=pltpu.SMEM),
                  pl.BlockSpec(memory_space=pl.ANY),
                  pl.BlockSpec((tm, D_MODEL), lambda i: (i, 0)),
                  pl.BlockSpec((tm, LANES), lambda i: (i, 0)),
                  _const_spec(fg.shape)],
        out_specs=pl.BlockSpec((tm, D_MODEL), lambda i: (i, 0)),
        out_shape=jax.ShapeDtypeStruct((n, D_MODEL), F32),
        scratch_shapes=[pltpu.VMEM((2, tm, D_MODEL), F32), pltpu.SemaphoreType.DMA(())],
        compiler_params=_cparams(("arbitrary",)),
        name="moe_combine",
    )(pos, yb, h, route, fg)


def _rmsnorm_kernel(h_ref, g_ref, o_ref):
    x = h_ref[...]
    o_ref[...] = x * lax.rsqrt(jnp.mean(x * x, axis=-1, keepdims=True) + EPS) * g_ref[...]


def _final_norm(h, g, tm):
    n = h.shape[0]
    row = pl.BlockSpec((tm, D_MODEL), lambda i: (i, 0))
    return pl.pallas_call(
        _rmsnorm_kernel, grid=(n // tm,), in_specs=[row, _const_spec(g.shape)], out_specs=row,
        out_shape=jax.ShapeDtypeStruct((n, D_MODEL), F32),
        compiler_params=_cparams(("parallel",)), name="final_norm",
    )(h, g)


def _pad_cols(w, width):
    return jnp.pad(w, ((0, 0), (0, width - w.shape[1])))


def _pad_rows(w, height):
    return jnp.pad(w, ((0, height - w.shape[0]), (0, 0)))


def _inproj_weights(w):
    o = np.cumsum([0, 512, 256, 256, 8, 192, 128, 32, 128, 128, 256, 256, 16, 512])
    seg = lambda i: w[:, o[i]:o[i + 1]]
    gates = seg(3)
    wm = jnp.concatenate([seg(0), seg(1), seg(2), _pad_cols(gates[:, :4], LANES), _pad_cols(gates[:, 4:], LANES)], axis=1)
    wa = jnp.concatenate([_pad_cols(seg(4), 256), seg(5), _pad_cols(seg(6), LANES)], axis=1)
    wg = jnp.concatenate([seg(7), seg(8), seg(9), seg(10), _pad_cols(seg(11), LANES)], axis=1)
    wc = seg(12)
    return [x.astype(BF16) for x in (wm, wa, wg, wc)]


def _mla_weights(w_uq, w_ukv):
    half = MLA_ROPE // 2
    zq = jnp.zeros((MLA_Q_RANK, half), F32)
    qa, qb, ka, kb, wv = [], [], [], [], []
    eye = jnp.eye(LANES, dtype=F32)[:, :MLA_ROPE]
    zk = jnp.zeros((MLA_KV_RANK, MLA_HP), F32)
    for h in range(MLA_HEADS):
        wq = w_uq[:, h * 96:(h + 1) * 96]
        nope, r1, r2 = wq[:, :64], wq[:, 64:64 + half], wq[:, 64 + half:]
        qa.append(_pad_cols(jnp.concatenate([nope, r1, r2], axis=1), MLA_HP))
        qb.append(_pad_cols(jnp.concatenate([jnp.zeros_like(nope), -r2, r1], axis=1), MLA_HP))
        wkv = w_ukv[:, h * 128:(h + 1) * 128]
        kn, vh = wkv[:, :64], wkv[:, 64:]
        e1, e2 = eye[:, :half], eye[:, half:]
        z64 = jnp.zeros((LANES, 64), F32)
        ka.append(jnp.concatenate([_pad_cols(kn, MLA_HP), _pad_cols(jnp.concatenate([z64, e1, e2], axis=1), MLA_HP)], axis=0))
        kb.append(jnp.concatenate([zk, _pad_cols(jnp.concatenate([z64, -e2, e1], axis=1), MLA_HP)], axis=0))
        vpad = jnp.concatenate([vh, jnp.zeros_like(vh)], axis=1) if h % 2 == 0 else jnp.concatenate([jnp.zeros_like(vh), vh], axis=1)
        wv.append(jnp.concatenate([vpad, jnp.zeros((LANES, MLA_HP), F32)], axis=0))
    cat = lambda xs, rows: _pad_rows(jnp.concatenate(xs, axis=1), rows).astype(BF16)
    return cat(qa, 256), cat(qb, 256), cat(ka, 256), cat(kb, 256), cat(wv, 256)


def _rope_tables(positions):
    half = MLA_ROPE // 2
    inv_freq = 1.0 / (ROPE_BASE ** (jnp.arange(0, MLA_ROPE, 2, dtype=F32) / MLA_ROPE))
    ang = positions.astype(F32)[..., None] * inv_freq
    cos, sin = jnp.cos(ang), jnp.sin(ang)
    shp = cos.shape[:-1]
    rc = jnp.concatenate([jnp.ones(shp + (64,), F32), cos, cos, jnp.zeros(shp + (MLA_HP - 64 - 2 * half,), F32)], axis=-1)
    rs = jnp.concatenate([jnp.zeros(shp + (64,), F32), sin, sin, jnp.zeros(shp + (MLA_HP - 64 - 2 * half,), F32)], axis=-1)
    return rc, rs


def _routing_tables(route, bm):
    n = route.shape[0]
    a = n * TOP_K
    flat_e = route[:, :TOP_K].astype(jnp.int32).reshape(-1)
    onehot = (flat_e[:, None] == jnp.arange(N_EXPERTS, dtype=jnp.int32)[None, :]).astype(jnp.int32)
    csum = jnp.cumsum(onehot, axis=0)
    rank = jnp.sum(onehot * csum, axis=1) - 1
    counts = csum[-1]
    padded = (counts + bm - 1) // bm * bm
    padded_end = jnp.cumsum(padded)
    padded_start = padded_end - padded
    dest = padded_start[flat_e] + rank
    n_blocks = -(-a // bm) + N_EXPERTS
    row_tok = jnp.zeros((n_blocks * bm,), jnp.int32).at[dest].set(jnp.arange(a, dtype=jnp.int32) // TOP_K)
    block_start = jnp.arange(n_blocks, dtype=jnp.int32) * bm
    block_e = jnp.sum((padded_end[None, :] <= block_start[:, None]).astype(jnp.int32), axis=1)
    block_e = jnp.minimum(block_e, N_EXPERTS - 1).astype(jnp.int32)
    return block_e, row_tok.reshape(n_blocks, 1, bm), dest


def kernel(x, positions, norm1_g, w_in, mlstm_conv_w, mlstm_conv_b, mlstm_gate_b, mla_q_norm_g, mla_w_uq,
           mla_kv_norm_g, mla_w_ukv, gla_w_alpha, gla_b_alpha, conv_w, conv_b, conv_ln_g, conv_ln_b, fuse_g,
           w_out, norm2_g, ffn_w_gate, ffn_w_up, ffn_w_down, moe_router, moe_w_gate, moe_w_up, moe_w_down,
           final_norm_g):
    bsz, seq, _ = x.shape
    n = bsz * seq
    depth = w_in.shape[0]
    tm = min(512, n)
    tt = min(512, seq)
    tq = min(512, seq)
    tc = min(256, n)
    nct = seq // CHUNK
    row2 = lambda v: v.reshape(1, -1).astype(F32)

    rc, rs = _rope_tables(positions)
    h = x.reshape(n, D_MODEL)
    for l in range(depth):
        wm, wa, wg, wc = _inproj_weights(w_in[l])
        zm, za, zg, zc = _inproj(h, row2(norm1_g[l]), wm, wa, wg, wc, tm)
        zm = zm.reshape(bsz, seq, ZM_W)

        gates = jnp.concatenate([zm[:, :, 1024:1028], zm[:, :, 1152:1156]], axis=-1)
        grow = gates.reshape(bsz, nct, CHUNK, 2, MLSTM_HEADS).transpose(0, 3, 1, 4, 2).reshape(bsz, 2, nct, GROUP_W)
        gb = mlstm_gate_b[l].astype(F32)
        gbc = _pad_cols(gb.reshape(2, MLSTM_HEADS), LANES)
        gbr = jnp.repeat(gb.reshape(2, MLSTM_HEADS), CHUNK, axis=1)
        ym = _mlstm(zm, grow, mlstm_conv_w[l], row2(mlstm_conv_b[l]), gbc, gbr, tt)

        wqa, wqb, wka, wkb, wv = _mla_weights(mla_w_uq[l], mla_w_ukv[l])
        qh, kh, vh = _mla_prep(za.reshape(bsz, seq, ZA_W), rc, rs, _pad_cols(row2(mla_q_norm_g[l]), 256),
                               row2(mla_kv_norm_g[l]), wqa, wqb, wka, wkb, wv, tt)
        ya = _mla_attn(qh, kh, vh, tq)

        yg = _gla(zg.reshape(bsz, seq, ZG_W), _pad_rows(gla_w_alpha[l], LANES), row2(gla_b_alpha[l]), tt)
        yc = _conformer(zc.reshape(bsz, seq, ZC_W), conv_w[l], row2(conv_b[l]), row2(conv_ln_g[l]),
                        row2(conv_ln_b[l]), tt)

        flat = lambda y: y.reshape(n, GROUP_W)
        j = l // 2
        if l % 2 == 0:
            h, u = _outproj(flat(ym), flat(ya), flat(yg), flat(yc), h, row2(fuse_g[l]), w_out[l].astype(BF16),
                            row2(norm2_g[l]), None, tm)
            h = _ffn(u, h, ffn_w_gate[j].astype(BF16), ffn_w_up[j].astype(BF16), ffn_w_down[j].astype(BF16), tm)
            if l == depth - 1:
                h = _final_norm(h, row2(final_norm_g), tm)
        else:
            h, u, route = _outproj(flat(ym), flat(ya), flat(yg), flat(yc), h, row2(fuse_g[l]),
                                   w_out[l].astype(BF16), row2(norm2_g[l]), _pad_cols(moe_router[j], LANES), tm)
            block_e, row_tok, dest = _routing_tables(route, MOE_BM)
            yb = _moe(block_e, row_tok, u, moe_w_gate[j].astype(BF16), moe_w_up[j].astype(BF16),
                      moe_w_down[j].astype(BF16))
            h = _combine(dest.reshape(n // tc, 1, 2 * tc), yb, h, route, row2(final_norm_g), tc, l == depth - 1)
    return h.reshape(bsz, seq, D_MODEL)
```

```python
import functools

import jax
import jax.numpy as jnp
import numpy as np
from jax import lax
from jax.experimental import pallas as pl
from jax.experimental.pallas import tpu as pltpu

F32 = jnp.float32
BF16 = jnp.bfloat16
HI = lax.Precision.HIGHEST

D_MODEL = 1024
GROUP_W = 256
EPS = 1e-6
LOG2E = 1.4426950408889634
LANES = 128
CHUNK = 64
MLSTM_HEADS = 4
MLSTM_CONV = 4
MLA_HEADS = 4
MLA_NOPE = 64
MLA_ROPE = 32
MLA_V = 64
MLA_Q_RANK = 192
MLA_KV_RANK = 128
ROPE_BASE = 10000.0
GLA_HEADS = 4
GLA_DK = 32
GLA_DV = 64
GLA_GATE_RANK = 16
GLA_TAU = 16.0
CONV_WIDTH = 31
D_FF = 2816
N_EXPERTS = 8
TOP_K = 2
MOE_BM = 256
VMEM_LIMIT = 56 * 1024 * 1024

ZM_W = 1280
ZA_W = 512
ZG_W = 896
ZC_W = 512


def _cparams(sem):
    return pltpu.CompilerParams(dimension_semantics=sem, vmem_limit_bytes=VMEM_LIMIT)


def _const_spec(shape):
    nd = len(shape)
    return pl.BlockSpec(shape, lambda *_: (0,) * nd, pipeline_mode=pl.Buffered(1))


def _sigmoid(x):
    return 1.0 / (1.0 + jnp.exp(-x))


def _log_sigmoid(x):
    return jnp.minimum(x, 0.0) - jnp.log(1.0 + jnp.exp(-jnp.abs(x)))


def _iota(shape, dim):
    return lax.broadcasted_iota(jnp.int32, shape, dim)


def _tri(n):
    return (_iota((n, n), 0) >= _iota((n, n), 1)).astype(F32)


def _inproj_kernel(h_ref, g_ref, wm_ref, wa_ref, wg_ref, wc_ref, zm_ref, za_ref, zg_ref, zc_ref):
    x = h_ref[...]
    ms = jnp.mean(x * x, axis=-1, keepdims=True)
    xn = (x * lax.rsqrt(ms + EPS) * g_ref[...]).astype(BF16)
    zm_ref[...] = jnp.dot(xn, wm_ref[...], preferred_element_type=F32)
    za_ref[...] = jnp.dot(xn, wa_ref[...], preferred_element_type=F32)
    zg_ref[...] = jnp.dot(xn, wg_ref[...], preferred_element_type=F32)
    zc_ref[...] = jnp.dot(xn, wc_ref[...], preferred_element_type=F32)


def _inproj(h, g, wm, wa, wg, wc, tm):
    n = h.shape[0]
    row = lambda w: pl.BlockSpec((tm, w), lambda i: (i, 0))
    return pl.pallas_call(
        _inproj_kernel,
        grid=(n // tm,),
        in_specs=[row(D_MODEL), _const_spec((1, D_MODEL)), _const_spec(wm.shape), _const_spec(wa.shape),
                  _const_spec(wg.shape), _const_spec(wc.shape)],
        out_specs=[row(ZM_W), row(ZA_W), row(ZG_W), row(ZC_W)],
        out_shape=[jax.ShapeDtypeStruct((n, w), F32) for w in (ZM_W, ZA_W, ZG_W, ZC_W)],
        compiler_params=_cparams(("parallel",)),
        name="inproj",
    )(h, g, wm, wa, wg, wc)


def _expand_heads(x, width):
    r = x.shape[0]
    lane = _iota((r, 4 * width), 1)
    out = jnp.broadcast_to(x[:, 3:4], (r, 4 * width))
    for h in (2, 1, 0):
        out = jnp.where(lane < (h + 1) * width, jnp.broadcast_to(x[:, h:h + 1], (r, 4 * width)), out)
    return out


def _mlstm_kernel(zm_ref, grow_ref, cw_ref, cb_ref, gbc_ref, gbr_ref, y_ref,
                  xpad_ref, q_ref, k_ref, grs_ref, cn_ref, m_ref, *, tt):
    nct = tt // CHUNK
    L = CHUNK
    W = MLSTM_HEADS * 64

    @pl.when(pl.program_id(1) == 0)
    def _():
        xpad_ref[0:8, :] = jnp.zeros((8, 2 * W), F32)
        cn_ref[...] = jnp.zeros_like(cn_ref)
        m_ref[...] = jnp.zeros_like(m_ref)

    xpad_ref[8:8 + tt, :] = zm_ref[0, :, 0:2 * W]
    acc = jnp.broadcast_to(cb_ref[...], (tt, 2 * W))
    for j in range(MLSTM_CONV):
        acc = acc + cw_ref[j:j + 1, :] * xpad_ref[pl.ds(8 - (MLSTM_CONV - 1) + j, tt), :]
    xpad_ref[0:8, :] = xpad_ref[tt:tt + 8, :]
    qk = acc * _sigmoid(acc)
    q_ref[...] = (qk[:, :W] * (64 ** -0.5)).astype(BF16)
    k_ref[...] = qk[:, W:]

    hsame = ((_iota((W, W), 0) // L) == (_iota((W, W), 1) // L))
    trit = (hsame & ((_iota((W, W), 0) % L) <= (_iota((W, W), 1) % L))).astype(F32)
    i_row = grow_ref[0, 0] + gbr_ref[0:1, :]
    lf_row = _log_sigmoid(grow_ref[0, 1] + gbr_ref[1:2, :])
    grs_ref[...] = i_row - jnp.dot(lf_row, trit, precision=HI, preferred_element_type=F32)

    tri = _tri(L)
    row_t = _iota((L, W), 0)
    lane_j = _iota((L, W), 1) % L
    causal = lane_j <= row_t
    rowblk = _iota((W, W), 0) // L
    blockmask = rowblk == (_iota((W, W), 1) // L)
    cn_mask = jnp.concatenate([blockmask, (_iota((W, LANES), 0) // L) == _iota((W, LANES), 1)], axis=1)
    ones_col = (_iota((L, LANES), 1) < MLSTM_HEADS).astype(F32)
    row_l = _iota((L, LANES), 0)

    def chunk(c, carry):
        r0 = pl.multiple_of(c * L, L)
        ic = zm_ref[0, pl.ds(r0, L), 4 * W:4 * W + LANES] + gbc_ref[0:1, :]
        lf = _log_sigmoid(zm_ref[0, pl.ds(r0, L), 4 * W + LANES:4 * W + 2 * LANES] + gbc_ref[1:2, :])
        b = jnp.dot(tri, lf, precision=HI, preferred_element_type=F32)
        g = ic - b
        cm = g
        s = 1
        while s < L:
            cm = jnp.maximum(cm, jnp.where(row_l >= s, pltpu.roll(cm, s, 0), -jnp.inf))
            s *= 2
        m_prev = m_ref[...]
        mx = jnp.maximum(m_prev, cm)
        w_inter = jnp.exp(m_prev - mx)
        b_end = b[L - 1:L, :]
        mx_end = mx[L - 1:L, :]
        m_new = b_end + mx_end
        decay = jnp.exp(m_prev - mx_end)
        kw_col = jnp.exp(g - mx_end)

        qc = q_ref[pl.ds(r0, L), :]
        kc = k_ref[pl.ds(r0, L), :]
        vc = zm_ref[0, pl.ds(r0, L), 2 * W:3 * W]
        oc = zm_ref[0, pl.ds(r0, L), 3 * W:4 * W]

        g_row = grs_ref[pl.ds(c, 1), :]
        dmat = jnp.where(causal, jnp.exp(jnp.where(causal, g_row - _expand_heads(mx, L), 0.0)), 0.0)
        kbd = jnp.where(blockmask, jnp.concatenate([kc] * 4, axis=0), 0.0).astype(BF16)
        s_qk = lax.dot_general(qc, kbd, (((1,), (1,)), ((), ())), preferred_element_type=F32)
        s_w = (s_qk * dmat).astype(BF16)
        vaug = jnp.concatenate([vc, ones_col], axis=1)
        vbd = jnp.where(cn_mask, jnp.concatenate([vaug] * 4, axis=0), 0.0).astype(BF16)
        intra = jnp.dot(s_w, vbd, preferred_element_type=F32)
        inter = jnp.dot(qc, cn_ref[...].astype(BF16), preferred_element_type=F32)
        num = _expand_heads(w_inter, L) * inter[:, :W] + intra[:, :W]
        den = w_inter * inter[:, W:] + intra[:, W:]
        hden = jnp.maximum(jnp.abs(den), jnp.exp(-(b + mx)))
        hout = num * _expand_heads(1.0 / hden, L)
        y_ref[0, pl.ds(r0, L), :] = _sigmoid(oc) * hout

        kw = (_expand_heads(kw_col, L) * kc).astype(BF16)
        upd = lax.dot_general(kw, vaug.astype(BF16), (((0,), (0,)), ((), ())), preferred_element_type=F32)
        decay_x = jnp.concatenate([_expand_heads(decay, L), decay], axis=1)
        cn_ref[...] = decay_x * cn_ref[...] + jnp.where(cn_mask, upd, 0.0)
        m_ref[...] = m_new
        return carry

    lax.fori_loop(0, nct, chunk, 0)


def _mlstm(zm, grow, cw, cb, gbc, gbr, tt):
    b, s, _ = zm.shape
    nct = tt // CHUNK
    w2 = 2 * GROUP_W
    return pl.pallas_call(
        functools.partial(_mlstm_kernel, tt=tt),
        grid=(b, s // tt),
        in_specs=[pl.BlockSpec((1, tt, ZM_W), lambda i, t: (i, t, 0)),
                  pl.BlockSpec((1, 2, nct, GROUP_W), lambda i, t: (i, 0, t, 0)),
                  _const_spec(cw.shape), _const_spec(cb.shape), _const_spec(gbc.shape), _const_spec(gbr.shape)],
        out_specs=pl.BlockSpec((1, tt, GROUP_W), lambda i, t: (i, t, 0)),
        out_shape=jax.ShapeDtypeStruct((b, s, GROUP_W), F32),
        scratch_shapes=[pltpu.VMEM((tt + 8, w2), F32), pltpu.VMEM((tt, GROUP_W), BF16),
                        pltpu.VMEM((tt, GROUP_W), F32), pltpu.VMEM((nct, GROUP_W), F32),
                        pltpu.VMEM((GROUP_W, GROUP_W + LANES), F32), pltpu.VMEM((1, LANES), F32)],
        compiler_params=_cparams(("parallel", "arbitrary")),
        name="mlstm",
    )(zm, grow, cw, cb, gbc, gbr)


def _gla_kernel(zg_ref, wa_ref, ba_ref, y_ref, q_ref, k_ref, g_ref, o_ref, st_ref, *, tt):
    nct = tt // CHUNK
    L = CHUNK
    WK = GLA_HEADS * GLA_DK
    WV = GLA_HEADS * GLA_DV

    @pl.when(pl.program_id(1) == 0)
    def _():
        st_ref[...] = jnp.zeros_like(st_ref)

    q_ref[...] = zg_ref[0, :, 0:WK] * (GLA_DK ** -0.5)
    k_ref[...] = zg_ref[0, :, WK:2 * WK]
    a = zg_ref[0, :, 2 * WK + 2 * WV:2 * WK + 2 * WV + LANES]
    pre = jnp.dot(a, wa_ref[...], precision=HI, preferred_element_type=F32) + ba_ref[...]
    g_ref[...] = _log_sigmoid(pre) * (1.0 / GLA_TAU)

    tri = _tri(L)
    hs2 = ((_iota((WK, WV), 0) // GLA_DK) == (_iota((WK, WV), 1) // GLA_DV)).astype(BF16)
    st_mask = (_iota((WV, WK), 0) // GLA_DV) == (_iota((WV, WK), 1) // GLA_DK)

    def chunk(c, carry):
        r0 = pl.multiple_of(c * L, L)
        qc = q_ref[pl.ds(r0, L), :]
        kc = k_ref[pl.ds(r0, L), :]
        vc = zg_ref[0, pl.ds(r0, L), 2 * WK:2 * WK + WV]
        b = jnp.dot(tri, g_ref[pl.ds(r0, L), :], precision=HI, preferred_element_type=F32)
        st = st_ref[...]
        qd = (qc * jnp.exp(b)).astype(BF16)
        o = lax.dot_general(qd, st.astype(BF16), (((1,), (1,)), ((), ())), preferred_element_type=F32)

        pieces = []
        for jb in range(L // 8):
            t0 = 8 * jb
            rows = L - t0
            qs, bs = qc[t0:, :], b[t0:, :]
            tl = _iota((rows, WK), 0)
            ps = []
            for jj in range(8):
                j = t0 + jj
                e = jnp.exp(jnp.minimum(bs - b[j:j + 1, :], 0.0))
                ps.append(jnp.where(tl >= jj, qs * kc[j:j + 1, :] * e, 0.0).astype(BF16))
            rexp = jnp.dot(jnp.concatenate(ps, axis=0), hs2, preferred_element_type=F32)
            acc = rexp[0:rows, :] * vc[t0:t0 + 1, :]
            for jj in range(1, 8):
                acc = acc + rexp[jj * rows:(jj + 1) * rows, :] * vc[t0 + jj:t0 + jj + 1, :]
            pieces.append(acc)
        bands = []
        for band in range(L // 8):
            tot = None
            for jb in range(band + 1):
                lo = 8 * (band - jb)
                part = pieces[jb][lo:lo + 8, :]
                tot = part if tot is None else tot + part
            bands.append(tot)
        o = o + jnp.concatenate(bands, axis=0)
        o_ref[pl.ds(r0, L), :] = o

        b_end = b[L - 1:L, :]
        kd = (kc * jnp.exp(b_end - b)).astype(BF16)
        upd = lax.dot_general(vc.astype(BF16), kd, (((0,), (0,)), ((), ())), preferred_element_type=F32)
        st_ref[...] = st * jnp.exp(b_end) + jnp.where(st_mask, upd, 0.0)
        return carry

    lax.fori_loop(0, nct, chunk, 0)

    o = o_ref[...]
    hsame = ((_iota((WV, WV), 0) // GLA_DV) == (_iota((WV, WV), 1) // GLA_DV)).astype(F32)
    ms = jnp.dot(o * o, hsame, precision=HI, preferred_element_type=F32) * (1.0 / GLA_DV)
    r = zg_ref[0, :, 2 * WK + WV:2 * WK + 2 * WV]
    y_ref[0] = o * lax.rsqrt(ms + EPS) * (r * _sigmoid(r))


def _gla(zg, wa, ba, tt):
    b, s, _ = zg.shape
    return pl.pallas_call(
        functools.partial(_gla_kernel, tt=tt),
        grid=(b, s // tt),
        in_specs=[pl.BlockSpec((1, tt, ZG_W), lambda i, t: (i, t, 0)), _const_spec(wa.shape), _const_spec(ba.shape)],
        out_specs=pl.BlockSpec((1, tt, GROUP_W), lambda i, t: (i, t, 0)),
        out_shape=jax.ShapeDtypeStruct((b, s, GROUP_W), F32),
        scratch_shapes=[pltpu.VMEM((tt, LANES), F32), pltpu.VMEM((tt, LANES), F32), pltpu.VMEM((tt, LANES), F32),
                        pltpu.VMEM((tt, GROUP_W), F32), pltpu.VMEM((GROUP_W, LANES), F32)],
        compiler_params=_cparams(("parallel", "arbitrary")),
        name="gla",
    )(zg, wa, ba)


CONV_HIST = 32


def _conv_kernel(zc_ref, cw_ref, cb_ref, lg_ref, lb_ref, y_ref, xpad_ref, *, tt):
    C = GROUP_W

    @pl.when(pl.program_id(1) == 0)
    def _():
        xpad_ref[0:CONV_HIST, :] = jnp.zeros((CONV_HIST, C), F32)

    val = zc_ref[0, :, 0:C]
    gate = zc_ref[0, :, C:2 * C]
    xpad_ref[CONV_HIST:CONV_HIST + tt, :] = val * _sigmoid(gate)
    acc = jnp.broadcast_to(cb_ref[...], (tt, C))
    for j in range(CONV_WIDTH):
        acc = acc + cw_ref[j:j + 1, :] * xpad_ref[pl.ds(CONV_HIST - (CONV_WIDTH - 1) + j, tt), :]
    xpad_ref[0:CONV_HIST, :] = xpad_ref[tt:tt + CONV_HIST, :]
    mu = jnp.mean(acc, axis=-1, keepdims=True)
    d = acc - mu
    var = jnp.mean(d * d, axis=-1, keepdims=True)
    yn = d * lax.rsqrt(var + EPS) * lg_ref[...] + lb_ref[...]
    y_ref[0] = yn * _sigmoid(yn)


def _conformer(zc, cw, cb, lg, lb, tt):
    b, s, _ = zc.shape
    return pl.pallas_call(
        functools.partial(_conv_kernel, tt=tt),
        grid=(b, s // tt),
        in_specs=[pl.BlockSpec((1, tt, ZC_W), lambda i, t: (i, t, 0)), _const_spec(cw.shape), _const_spec(cb.shape),
                  _const_spec(lg.shape), _const_spec(lb.shape)],
        out_specs=pl.BlockSpec((1, tt, GROUP_W), lambda i, t: (i, t, 0)),
        out_shape=jax.ShapeDtypeStruct((b, s, GROUP_W), F32),
        scratch_shapes=[pltpu.VMEM((tt + CONV_HIST, GROUP_W), F32)],
        compiler_params=_cparams(("parallel", "arbitrary")),
        name="conformer",
    )(zc, cw, cb, lg, lb)


MLA_HP = 128


def _mla_prep_kernel(za_ref, rc_ref, rs_ref, qg_ref, kg_ref, wqa_ref, wqb_ref, wka_ref, wkb_ref, wv_ref,
                     q_ref, k_ref, v_ref):
    cq = za_ref[0, :, 0:256]
    ckv = za_ref[0, :, 256:384]
    kr = za_ref[0, :, 384:512]
    qn = (cq * lax.rsqrt(jnp.sum(cq * cq, axis=-1, keepdims=True) * (1.0 / MLA_Q_RANK) + EPS) * qg_ref[...]).astype(BF16)
    kvn = ckv * lax.rsqrt(jnp.mean(ckv * ckv, axis=-1, keepdims=True) + EPS) * kg_ref[...]
    kin = jnp.concatenate([kvn, kr], axis=1).astype(BF16)
    cos = rc_ref[0]
    sin = rs_ref[0]
    qa = jnp.dot(qn, wqa_ref[...], preferred_element_type=F32)
    qb = jnp.dot(qn, wqb_ref[...], preferred_element_type=F32)
    ka = jnp.dot(kin, wka_ref[...], preferred_element_type=F32)
    kb = jnp.dot(kin, wkb_ref[...], preferred_element_type=F32)
    vt = lax.dot_general(wv_ref[...], kin, (((1,), (1,)), ((), ())), preferred_element_type=F32)
    scale = (MLA_NOPE + MLA_ROPE) ** -0.5 * LOG2E
    for h in range(MLA_HEADS):
        sl = slice(h * MLA_HP, (h + 1) * MLA_HP)
        q_ref[0, h] = ((qa[:, sl] * cos + qb[:, sl] * sin) * scale).astype(BF16)
        k_ref[0, h] = (ka[:, sl] * cos + kb[:, sl] * sin).astype(BF16)
        v_ref[0, h] = vt[h * MLA_V:(h + 1) * MLA_V, :].astype(BF16)


def _mla_prep(za, rc, rs, qg, kg, wqa, wqb, wka, wkb, wv, tt):
    b, s, _ = za.shape
    hspec = pl.BlockSpec((1, MLA_HEADS, tt, MLA_HP), lambda i, t: (i, 0, t, 0))
    tspec = pl.BlockSpec((1, tt, MLA_HP), lambda i, t: (i, t, 0))
    hshape = jax.ShapeDtypeStruct((b, MLA_HEADS, s, MLA_HP), BF16)
    vspec = pl.BlockSpec((1, MLA_HEADS, MLA_V, tt), lambda i, t: (i, 0, 0, t))
    vshape = jax.ShapeDtypeStruct((b, MLA_HEADS, MLA_V, s), BF16)
    return pl.pallas_call(
        _mla_prep_kernel,
        grid=(b, s // tt),
        in_specs=[pl.BlockSpec((1, tt, ZA_W), lambda i, t: (i, t, 0)), tspec, tspec,
                  _const_spec(qg.shape), _const_spec(kg.shape), _const_spec(wqa.shape), _const_spec(wqb.shape),
                  _const_spec(wka.shape), _const_spec(wkb.shape), _const_spec(wv.shape)],
        out_specs=[hspec, hspec, vspec],
        out_shape=[hshape, hshape, vshape],
        compiler_params=_cparams(("parallel", "parallel")),
        name="mla_prep",
    )(za, rc, rs, qg, kg, wqa, wqb, wka, wkb, wv)


def _mla_attn_kernel(q_ref, k_ref, v_ref, y_ref, m_ref, l_ref, acc_ref, *, tq):
    qi = pl.program_id(1)
    ki = pl.program_id(2)

    @pl.when(ki == 0)
    def _():
        m_ref[...] = jnp.full_like(m_ref, -jnp.inf)
        l_ref[...] = jnp.zeros_like(l_ref)
        acc_ref[...] = jnp.zeros_like(acc_ref)

    def step(masked):
        for h in range(MLA_HEADS):
            s = lax.dot_general(k_ref[0, h], q_ref[0, h], (((1,), (1,)), ((), ())), preferred_element_type=F32)
            if masked:
                s = jnp.where(_iota((tq, tq), 0) <= _iota((tq, tq), 1), s, -jnp.inf)
            m_prev = m_ref[h]
            m_new = jnp.maximum(m_prev, jnp.max(s, axis=0, keepdims=True))
            alpha = jnp.exp2(m_prev - m_new)
            p = jnp.exp2(s - m_new)
            l_ref[h] = alpha * l_ref[h] + jnp.sum(p, axis=0, keepdims=True)
            m_ref[h] = m_new
            pv = jnp.dot(v_ref[0, h], p.astype(BF16), preferred_element_type=F32)
            acc_ref[h] = alpha * acc_ref[h] + pv

    @pl.when(ki < qi)
    def _():
        step(False)

    @pl.when(ki == qi)
    def _():
        step(True)
        ot = jnp.concatenate([acc_ref[h] * (1.0 / l_ref[h]) for h in range(MLA_HEADS)], axis=0)
        y_ref[0] = ot.T


def _mla_attn(q, k, v, tq):
    b, _, s, _ = q.shape
    nq = s // tq
    qspec = pl.BlockSpec((1, MLA_HEADS, tq, MLA_HP), lambda i, qi, ki: (i, 0, qi, 0))
    kspec = pl.BlockSpec((1, MLA_HEADS, tq, MLA_HP), lambda i, qi, ki: (i, 0, jnp.minimum(ki, qi), 0))
    vspec = pl.BlockSpec((1, MLA_HEADS, MLA_V, tq), lambda i, qi, ki: (i, 0, 0, jnp.minimum(ki, qi)))
    return pl.pallas_call(
        functools.partial(_mla_attn_kernel, tq=tq),
        grid=(b, nq, nq),
        in_specs=[qspec, kspec, vspec],
        out_specs=pl.BlockSpec((1, tq, GROUP_W), lambda i, qi, ki: (i, qi, 0)),
        out_shape=jax.ShapeDtypeStruct((b, s, GROUP_W), F32),
        scratch_shapes=[pltpu.VMEM((MLA_HEADS, 1, tq), F32), pltpu.VMEM((MLA_HEADS, 1, tq), F32),
                        pltpu.VMEM((MLA_HEADS, MLA_V, tq), F32)],
        compiler_params=_cparams(("parallel", "parallel", "arbitrary")),
        name="mla_attn",
    )(q, k, v)


def _outproj_kernel(*refs, routed):
    if routed:
        ym, ya, yg, yc, h_ref, fg_ref, wo_ref, n2_ref, rt_ref, hn_ref, u_ref, route_ref = refs
    else:
        ym, ya, yg, yc, h_ref, fg_ref, wo_ref, n2_ref, hn_ref, u_ref = refs
    parts = []
    for gi, y_ref in enumerate((ym, ya, yg, yc)):
        y = y_ref[...]
        yn = y * lax.rsqrt(jnp.mean(y * y, axis=-1, keepdims=True) + EPS)
        parts.append((yn * fg_ref[:, gi * GROUP_W:(gi + 1) * GROUP_W]).astype(BF16))
    ycat = jnp.concatenate(parts, axis=1)
    hn = h_ref[...] + jnp.dot(ycat, wo_ref[...], preferred_element_type=F32)
    hn_ref[...] = hn
    u = hn * lax.rsqrt(jnp.mean(hn * hn, axis=-1, keepdims=True) + EPS) * n2_ref[...]
    u_ref[...] = u.astype(u_ref.dtype)
    if routed:
        tm = u.shape[0]
        lane = _iota((tm, LANES), 1)
        logits = jnp.dot(u, rt_ref[...], precision=HI, preferred_element_type=F32)
        logits = jnp.where(lane < N_EXPERTS, logits, -jnp.inf)
        m1 = jnp.max(logits, axis=-1, keepdims=True)
        i1 = jnp.min(jnp.where(logits == m1, lane, LANES), axis=-1, keepdims=True)
        rest = jnp.where(lane == i1, -jnp.inf, logits)
        m2 = jnp.max(rest, axis=-1, keepdims=True)
        i2 = jnp.min(jnp.where(rest == m2, lane, LANES), axis=-1, keepdims=True)
        e2 = jnp.exp(m2 - m1)
        w1 = 1.0 / (1.0 + e2)
        w2 = e2 / (1.0 + e2)
        route_ref[...] = jnp.where(lane == 0, i1.astype(F32),
                                   jnp.where(lane == 1, i2.astype(F32),
                                             jnp.where(lane == 2, w1, jnp.where(lane == 3, w2, 0.0))))


def _outproj(ym, ya, yg, yc, h, fg, wo, n2, router, tm):
    n = h.shape[0]
    routed = router is not None
    row = lambda w: pl.BlockSpec((tm, w), lambda i: (i, 0))
    in_specs = [row(GROUP_W)] * 4 + [row(D_MODEL), _const_spec(fg.shape), _const_spec(wo.shape), _const_spec(n2.shape)]
    args = [ym, ya, yg, yc, h, fg, wo, n2]
    out_specs = [row(D_MODEL), row(D_MODEL)]
    out_shape = [jax.ShapeDtypeStruct((n, D_MODEL), F32), jax.ShapeDtypeStruct((n, D_MODEL), F32 if routed else BF16)]
    if routed:
        in_specs.append(_const_spec(router.shape))
        args.append(router)
        out_specs.append(row(LANES))
        out_shape.append(jax.ShapeDtypeStruct((n, LANES), F32))
    return pl.pallas_call(
        functools.partial(_outproj_kernel, routed=routed),
        grid=(n // tm,),
        in_specs=in_specs, out_specs=out_specs, out_shape=out_shape,
        compiler_params=_cparams(("parallel",)),
        name="outproj_routed" if routed else "outproj",
    )(*args)


def _swiglu(x, wg, wu, wd):
    g = jnp.dot(x, wg, preferred_element_type=F32)
    up = jnp.dot(x, wu, preferred_element_type=F32)
    a = (g * _sigmoid(g) * up).astype(BF16)
    return jnp.dot(a, wd, preferred_element_type=F32)


def _ffn_kernel(u_ref, h_ref, wg_ref, wu_ref, wd_ref, o_ref):
    o_ref[...] = h_ref[...] + _swiglu(u_ref[...], wg_ref[...], wu_ref[...], wd_ref[...])


def _ffn(u, h, wg, wu, wd, tm):
    n = h.shape[0]
    row = pl.BlockSpec((tm, D_MODEL), lambda i: (i, 0))
    return pl.pallas_call(
        _ffn_kernel,
        grid=(n // tm,),
        in_specs=[row, row, _const_spec(wg.shape), _const_spec(wu.shape), _const_spec(wd.shape)],
        out_specs=row,
        out_shape=jax.ShapeDtypeStruct((n, D_MODEL), F32),
        compiler_params=_cparams(("parallel",)),
        name="ffn_dense",
    )(u, h, wg, wu, wd)


def _moe_kernel(be_ref, tokc_ref, tokn_ref, dst_ref, u_hbm, wg_ref, wu_ref, wd_ref, y_hbm,
                xbuf, ybuf, gsem, ssem):
    del be_ref
    i = pl.program_id(0)
    nb = pl.num_programs(0)
    bm = xbuf.shape[1]
    slot = i % 2

    def gather(tok_ref, s):
        def body(r, c):
            pltpu.make_async_copy(u_hbm.at[pl.ds(tok_ref[0, 0, r], 1), :], xbuf.at[s, pl.ds(r, 1), :],
                                  gsem.at[s]).start()
            return c
        lax.fori_loop(0, bm, body, 0, unroll=8)

    def gather_wait(s):
        pltpu.make_async_copy(u_hbm.at[pl.ds(0, bm), :], xbuf.at[s], gsem.at[s]).wait()

    def scatter_wait(s):
        pltpu.make_async_copy(ybuf.at[s], y_hbm.at[pl.ds(0, bm), :], ssem.at[s]).wait()

    @pl.when(i == 0)
    def _():
        gather(tokc_ref, slot)

    gather_wait(slot)

    @pl.when(i + 1 < nb)
    def _():
        gather(tokn_ref, 1 - slot)

    @pl.when(i >= 2)
    def _():
        scatter_wait(slot)

    ybuf[slot] = _swiglu(xbuf[slot].astype(BF16), wg_ref[0], wu_ref[0], wd_ref[0])

    def scatter(r, c):
        pltpu.make_async_copy(ybuf.at[slot, pl.ds(r, 1), :], y_hbm.at[pl.ds(dst_ref[0, 0, r], 1), :],
                              ssem.at[slot]).start()
        return c

    lax.fori_loop(0, bm, scatter, 0, unroll=8)

    @pl.when(i == nb - 1)
    def _():
        scatter_wait(slot)

        @pl.when(nb >= 2)
        def _():
            scatter_wait(1 - slot)


def _moe(block_e, row_tok, row_dst, u, wg, wu, wd):
    n = u.shape[0]
    n_blocks = block_e.shape[0]
    bm = row_tok.shape[-1]
    wspec = lambda shp: pl.BlockSpec((1,) + shp, lambda i, be: (be[i], 0, 0))
    ispec = lambda f: pl.BlockSpec((1, 1, bm), f, memory_space=pltpu.SMEM)
    grid_spec = pltpu.PrefetchScalarGridSpec(
        num_scalar_prefetch=1,
        grid=(n_blocks,),
        in_specs=[ispec(lambda i, be: (i, 0, 0)),
                  ispec(lambda i, be: (jnp.minimum(i + 1, n_blocks - 1), 0, 0)),
                  ispec(lambda i, be: (i, 0, 0)),
                  pl.BlockSpec(memory_space=pl.ANY),
                  wspec((D_MODEL, D_FF)), wspec((D_MODEL, D_FF)), wspec((D_FF, D_MODEL))],
        out_specs=pl.BlockSpec(memory_space=pl.ANY),
        scratch_shapes=[pltpu.VMEM((2, bm, D_MODEL), F32), pltpu.VMEM((2, bm, D_MODEL), F32),
                        pltpu.SemaphoreType.DMA((2,)), pltpu.SemaphoreType.DMA((2,))],
    )
    return pl.pallas_call(
        _moe_kernel,
        grid_spec=grid_spec,
        out_shape=jax.ShapeDtypeStruct((n_blocks * bm, D_MODEL), F32),
        compiler_params=_cparams(("arbitrary",)),
        name="moe_experts",
    )(block_e, row_tok, row_tok, row_dst, u, wg, wu, wd)


def _combine_kernel(y_ref, h_ref, route_ref, fg_ref, o_ref, *, final):
    w1 = route_ref[:, 2:3]
    w2 = route_ref[:, 3:4]
    hn = h_ref[...] + w1 * y_ref[:, 0:D_MODEL] + w2 * y_ref[:, D_MODEL:2 * D_MODEL]
    if final:
        hn = hn * lax.rsqrt(jnp.mean(hn * hn, axis=-1, keepdims=True) + EPS) * fg_ref[...]
    o_ref[...] = hn


def _combine(y2, h, route, fg, tm, final):
    n = h.shape[0]
    return pl.pallas_call(
        functools.partial(_combine_kernel, final=final),
        grid=(n // tm,),
        in_specs=[pl.BlockSpec((tm, TOP_K * D_MODEL), lambda i: (i, 0)),
                  pl.BlockSpec((tm, D_MODEL), lambda i: (i, 0)),
                  pl.BlockSpec((tm, LANES), lambda i: (i, 0)),
                  _const_spec(fg.shape)],
        out_specs=pl.BlockSpec((tm, D_MODEL), lambda i: (i, 0)),
        out_shape=jax.ShapeDtypeStruct((n, D_MODEL), F32),
        compiler_params=_cparams(("parallel",)),
        name="moe_combine",
    )(y2, h, route, fg)


def _rmsnorm_kernel(h_ref, g_ref, o_ref):
    x = h_ref[...]
    o_ref[...] = x * lax.rsqrt(jnp.mean(x * x, axis=-1, keepdims=True) + EPS) * g_ref[...]


def _final_norm(h, g, tm):
    n = h.shape[0]
    row = pl.BlockSpec((tm, D_MODEL), lambda i: (i, 0))
    return pl.pallas_call(
        _rmsnorm_kernel, grid=(n // tm,), in_specs=[row, _const_spec(g.shape)], out_specs=row,
        out_shape=jax.ShapeDtypeStruct((n, D_MODEL), F32),
        compiler_params=_cparams(("parallel",)), name="final_norm",
    )(h, g)


def _pad_cols(w, width):
    return jnp.pad(w, ((0, 0), (0, width - w.shape[1])))


def _pad_rows(w, height):
    return jnp.pad(w, ((0, height - w.shape[0]), (0, 0)))


def _inproj_weights(w):
    o = np.cumsum([0, 512, 256, 256, 8, 192, 128, 32, 128, 128, 256, 256, 16, 512])
    seg = lambda i: w[:, o[i]:o[i + 1]]
    gates = seg(3)
    wm = jnp.concatenate([seg(0), seg(1), seg(2), _pad_cols(gates[:, :4], LANES), _pad_cols(gates[:, 4:], LANES)], axis=1)
    wa = jnp.concatenate([_pad_cols(seg(4), 256), seg(5), _pad_cols(seg(6), LANES)], axis=1)
    wg = jnp.concatenate([seg(7), seg(8), seg(9), seg(10), _pad_cols(seg(11), LANES)], axis=1)
    wc = seg(12)
    return [x.astype(BF16) for x in (wm, wa, wg, wc)]


def _mla_weights(w_uq, w_ukv):
    half = MLA_ROPE // 2
    zq = jnp.zeros((MLA_Q_RANK, half), F32)
    qa, qb, ka, kb, wv = [], [], [], [], []
    eye = jnp.eye(LANES, dtype=F32)[:, :MLA_ROPE]
    zk = jnp.zeros((MLA_KV_RANK, MLA_HP), F32)
    for h in range(MLA_HEADS):
        wq = w_uq[:, h * 96:(h + 1) * 96]
        nope, r1, r2 = wq[:, :64], wq[:, 64:64 + half], wq[:, 64 + half:]
        qa.append(_pad_cols(jnp.concatenate([nope, r1, r2], axis=1), MLA_HP))
        qb.append(_pad_cols(jnp.concatenate([jnp.zeros_like(nope), -r2, r1], axis=1), MLA_HP))
        wkv = w_ukv[:, h * 128:(h + 1) * 128]
        kn, vh = wkv[:, :64], wkv[:, 64:]
        e1, e2 = eye[:, :half], eye[:, half:]
        z64 = jnp.zeros((LANES, 64), F32)
        ka.append(jnp.concatenate([_pad_cols(kn, MLA_HP), _pad_cols(jnp.concatenate([z64, e1, e2], axis=1), MLA_HP)], axis=0))
        kb.append(jnp.concatenate([zk, _pad_cols(jnp.concatenate([z64, -e2, e1], axis=1), MLA_HP)], axis=0))
        wv.append(vh.T)
    cat = lambda xs, rows: _pad_rows(jnp.concatenate(xs, axis=1), rows).astype(BF16)
    wvt = _pad_cols(jnp.concatenate(wv, axis=0), 256).astype(BF16)
    return cat(qa, 256), cat(qb, 256), cat(ka, 256), cat(kb, 256), wvt


def _rope_tables(positions):
    half = MLA_ROPE // 2
    inv_freq = 1.0 / (ROPE_BASE ** (jnp.arange(0, MLA_ROPE, 2, dtype=F32) / MLA_ROPE))
    ang = positions.astype(F32)[..., None] * inv_freq
    cos, sin = jnp.cos(ang), jnp.sin(ang)
    shp = cos.shape[:-1]
    rc = jnp.concatenate([jnp.ones(shp + (64,), F32), cos, cos, jnp.zeros(shp + (MLA_HP - 64 - 2 * half,), F32)], axis=-1)
    rs = jnp.concatenate([jnp.zeros(shp + (64,), F32), sin, sin, jnp.zeros(shp + (MLA_HP - 64 - 2 * half,), F32)], axis=-1)
    return rc, rs


def _routing_tables(route, bm):
    n = route.shape[0]
    a = n * TOP_K
    flat_e = route[:, :TOP_K].astype(jnp.int32).reshape(-1)
    onehot = (flat_e[:, None] == jnp.arange(N_EXPERTS, dtype=jnp.int32)[None, :]).astype(jnp.int32)
    csum = jnp.cumsum(onehot, axis=0)
    rank = jnp.sum(onehot * csum, axis=1) - 1
    counts = csum[-1]
    padded = (counts + bm - 1) // bm * bm
    padded_end = jnp.cumsum(padded)
    padded_start = padded_end - padded
    dest = padded_start[flat_e] + rank
    n_blocks = -(-a // bm) + N_EXPERTS
    rows = n_blocks * bm
    row_asg = jnp.full((rows,), -1, jnp.int32).at[dest].set(jnp.arange(a, dtype=jnp.int32))
    row_tok = jnp.maximum(row_asg, 0) // TOP_K
    pad_rank = jnp.cumsum((row_asg < 0).astype(jnp.int32)) - 1
    row_dst = jnp.where(row_asg >= 0, row_asg, a + pad_rank)
    block_start = jnp.arange(n_blocks, dtype=jnp.int32) * bm
    block_e = jnp.sum((padded_end[None, :] <= block_start[:, None]).astype(jnp.int32), axis=1)
    block_e = jnp.minimum(block_e, N_EXPERTS - 1).astype(jnp.int32)
    return block_e, row_tok.reshape(n_blocks, 1, bm), row_dst.reshape(n_blocks, 1, bm)


def kernel(x, positions, norm1_g, w_in, mlstm_conv_w, mlstm_conv_b, mlstm_gate_b, mla_q_norm_g, mla_w_uq,
           mla_kv_norm_g, mla_w_ukv, gla_w_alpha, gla_b_alpha, conv_w, conv_b, conv_ln_g, conv_ln_b, fuse_g,
           w_out, norm2_g, ffn_w_gate, ffn_w_up, ffn_w_down, moe_router, moe_w_gate, moe_w_up, moe_w_down,
           final_norm_g):
    bsz, seq, _ = x.shape
    n = bsz * seq
    depth = w_in.shape[0]
    tm = min(512, n)
    tt = min(512, seq)
    tq = min(512, seq)
    nct = seq // CHUNK
    row2 = lambda v: v.reshape(1, -1).astype(F32)

    rc, rs = _rope_tables(positions)
    h = x.reshape(n, D_MODEL)
    for l in range(depth):
        wm, wa, wg, wc = _inproj_weights(w_in[l])
        zm, za, zg, zc = _inproj(h, row2(norm1_g[l]), wm, wa, wg, wc, tm)
        zm = zm.reshape(bsz, seq, ZM_W)

        gates = jnp.concatenate([zm[:, :, 1024:1028], zm[:, :, 1152:1156]], axis=-1)
        grow = gates.reshape(bsz, nct, CHUNK, 2, MLSTM_HEADS).transpose(0, 3, 1, 4, 2).reshape(bsz, 2, nct, GROUP_W)
        gb = mlstm_gate_b[l].astype(F32)
        gbc = _pad_cols(gb.reshape(2, MLSTM_HEADS), LANES)
        gbr = jnp.repeat(gb.reshape(2, MLSTM_HEADS), CHUNK, axis=1)
        ym = _mlstm(zm, grow, mlstm_conv_w[l], row2(mlstm_conv_b[l]), gbc, gbr, tt)

        wqa, wqb, wka, wkb, wv = _mla_weights(mla_w_uq[l], mla_w_ukv[l])
        qh, kh, vh = _mla_prep(za.reshape(bsz, seq, ZA_W), rc, rs, _pad_cols(row2(mla_q_norm_g[l]), 256),
                               row2(mla_kv_norm_g[l]), wqa, wqb, wka, wkb, wv, tt)
        ya = _mla_attn(qh, kh, vh, tq)

        yg = _gla(zg.reshape(bsz, seq, ZG_W), _pad_rows(gla_w_alpha[l], LANES), row2(gla_b_alpha[l]), tt)
        yc = _conformer(zc.reshape(bsz, seq, ZC_W), conv_w[l], row2(conv_b[l]), row2(conv_ln_g[l]),
                        row2(conv_ln_b[l]), tt)

        flat = lambda y: y.reshape(n, GROUP_W)
        j = l // 2
        if l % 2 == 0:
            h, u = _outproj(flat(ym), flat(ya), flat(yg), flat(yc), h, row2(fuse_g[l]), w_out[l].astype(BF16),
                            row2(norm2_g[l]), None, tm)
            h = _ffn(u, h, ffn_w_gate[j].astype(BF16), ffn_w_up[j].astype(BF16), ffn_w_down[j].astype(BF16), tm)
            if l == depth - 1:
                h = _final_norm(h, row2(final_norm_g), tm)
        else:
            h, u, route = _outproj(flat(ym), flat(ya), flat(yg), flat(yc), h, row2(fuse_g[l]),
                                   w_out[l].astype(BF16), row2(norm2_g[l]), _pad_cols(moe_router[j], LANES), tm)
            block_e, row_tok, row_dst = _routing_tables(route, MOE_BM)
            y2 = _moe(block_e, row_tok, row_dst, u, moe_w_gate[j].astype(BF16), moe_w_up[j].astype(BF16),
                      moe_w_down[j].astype(BF16))
            y2 = y2.reshape(-1, TOP_K * D_MODEL)
            h = _combine(y2, h, route, row2(final_norm_g), tm, l == depth - 1)
    return h.reshape(bsz, seq, D_MODEL)
```

```python
import functools

import jax
import jax.numpy as jnp
import numpy as np
from jax import lax
from jax.experimental import pallas as pl
from jax.experimental.pallas import tpu as pltpu

F32 = jnp.float32
BF16 = jnp.bfloat16
HI = lax.Precision.HIGHEST

D_MODEL = 1024
GROUP_W = 256
EPS = 1e-6
LOG2E = 1.4426950408889634
LANES = 128
CHUNK = 64
MLSTM_HEADS = 4
MLSTM_CONV = 4
MLA_HEADS = 4
MLA_NOPE = 64
MLA_ROPE = 32
MLA_V = 64
MLA_Q_RANK = 192
MLA_KV_RANK = 128
ROPE_BASE = 10000.0
GLA_HEADS = 4
GLA_DK = 32
GLA_DV = 64
GLA_GATE_RANK = 16
GLA_TAU = 16.0
CONV_WIDTH = 31
D_FF = 2816
N_EXPERTS = 8
TOP_K = 2
MOE_BM = 256
VMEM_LIMIT = 56 * 1024 * 1024

ZM_W = 1280
ZA_W = 512
ZG_W = 896
ZC_W = 512


def _cparams(sem):
    return pltpu.CompilerParams(dimension_semantics=sem, vmem_limit_bytes=VMEM_LIMIT)


def _const_spec(shape):
    nd = len(shape)
    return pl.BlockSpec(shape, lambda *_: (0,) * nd, pipeline_mode=pl.Buffered(1))


def _sigmoid(x):
    return 1.0 / (1.0 + jnp.exp(-x))


def _log_sigmoid(x):
    return jnp.minimum(x, 0.0) - jnp.log(1.0 + jnp.exp(-jnp.abs(x)))


def _iota(shape, dim):
    return lax.broadcasted_iota(jnp.int32, shape, dim)


def _tri(n):
    return (_iota((n, n), 0) >= _iota((n, n), 1)).astype(F32)


def _inproj_kernel(h_ref, g_ref, wm_ref, wa_ref, wg_ref, wc_ref, zm_ref, za_ref, zg_ref, zc_ref):
    x = h_ref[...]
    ms = jnp.mean(x * x, axis=-1, keepdims=True)
    xn = (x * lax.rsqrt(ms + EPS) * g_ref[...]).astype(BF16)
    zm_ref[...] = jnp.dot(xn, wm_ref[...], preferred_element_type=F32)
    za_ref[...] = jnp.dot(xn, wa_ref[...], preferred_element_type=F32)
    zg_ref[...] = jnp.dot(xn, wg_ref[...], preferred_element_type=F32)
    zc_ref[...] = jnp.dot(xn, wc_ref[...], preferred_element_type=F32)


def _inproj(h, g, wm, wa, wg, wc, tm):
    n = h.shape[0]
    row = lambda w: pl.BlockSpec((tm, w), lambda i: (i, 0))
    return pl.pallas_call(
        _inproj_kernel,
        grid=(n // tm,),
        in_specs=[row(D_MODEL), _const_spec((1, D_MODEL)), _const_spec(wm.shape), _const_spec(wa.shape),
                  _const_spec(wg.shape), _const_spec(wc.shape)],
        out_specs=[row(ZM_W), row(ZA_W), row(ZG_W), row(ZC_W)],
        out_shape=[jax.ShapeDtypeStruct((n, w), F32) for w in (ZM_W, ZA_W, ZG_W, ZC_W)],
        compiler_params=_cparams(("parallel",)),
        name="inproj",
    )(h, g, wm, wa, wg, wc)


def _expand_heads(x, width):
    r = x.shape[0]
    lane = _iota((r, 4 * width), 1)
    out = jnp.broadcast_to(x[:, 3:4], (r, 4 * width))
    for h in (2, 1, 0):
        out = jnp.where(lane < (h + 1) * width, jnp.broadcast_to(x[:, h:h + 1], (r, 4 * width)), out)
    return out


def _mlstm_kernel(zm_ref, grow_ref, cw_ref, cb_ref, gbc_ref, gbr_ref, y_ref,
                  xpad_ref, q_ref, k_ref, grs_ref, cn_ref, m_ref, *, tt):
    nct = tt // CHUNK
    L = CHUNK
    W = MLSTM_HEADS * 64

    @pl.when(pl.program_id(1) == 0)
    def _():
        xpad_ref[0:8, :] = jnp.zeros((8, 2 * W), F32)
        cn_ref[...] = jnp.zeros_like(cn_ref)
        m_ref[...] = jnp.zeros_like(m_ref)

    xpad_ref[8:8 + tt, :] = zm_ref[0, :, 0:2 * W]
    acc = jnp.broadcast_to(cb_ref[...], (tt, 2 * W))
    for j in range(MLSTM_CONV):
        acc = acc + cw_ref[j:j + 1, :] * xpad_ref[pl.ds(8 - (MLSTM_CONV - 1) + j, tt), :]
    xpad_ref[0:8, :] = xpad_ref[tt:tt + 8, :]
    qk = acc * _sigmoid(acc)
    q_ref[...] = (qk[:, :W] * (64 ** -0.5)).astype(BF16)
    k_ref[...] = qk[:, W:]

    hsame = ((_iota((W, W), 0) // L) == (_iota((W, W), 1) // L))
    trit = (hsame & ((_iota((W, W), 0) % L) <= (_iota((W, W), 1) % L))).astype(F32)
    i_row = grow_ref[0, 0] + gbr_ref[0:1, :]
    lf_row = _log_sigmoid(grow_ref[0, 1] + gbr_ref[1:2, :])
    grs_ref[...] = i_row - jnp.dot(lf_row, trit, precision=HI, preferred_element_type=F32)

    tri = _tri(L)
    row_t = _iota((L, W), 0)
    lane_j = _iota((L, W), 1) % L
    causal = lane_j <= row_t
    rowblk = _iota((W, W), 0) // L
    blockmask = rowblk == (_iota((W, W), 1) // L)
    cn_mask = jnp.concatenate([blockmask, (_iota((W, LANES), 0) // L) == _iota((W, LANES), 1)], axis=1)
    ones_col = (_iota((L, LANES), 1) < MLSTM_HEADS).astype(F32)
    row_l = _iota((L, LANES), 0)

    def chunk(c, carry):
        r0 = pl.multiple_of(c * L, L)
        ic = zm_ref[0, pl.ds(r0, L), 4 * W:4 * W + LANES] + gbc_ref[0:1, :]
        lf = _log_sigmoid(zm_ref[0, pl.ds(r0, L), 4 * W + LANES:4 * W + 2 * LANES] + gbc_ref[1:2, :])
        b = jnp.dot(tri, lf, precision=HI, preferred_element_type=F32)
        g = ic - b
        cm = g
        s = 1
        while s < L:
            cm = jnp.maximum(cm, jnp.where(row_l >= s, pltpu.roll(cm, s, 0), -jnp.inf))
            s *= 2
        m_prev = m_ref[...]
        mx = jnp.maximum(m_prev, cm)
        w_inter = jnp.exp(m_prev - mx)
        b_end = b[L - 1:L, :]
        mx_end = mx[L - 1:L, :]
        m_new = b_end + mx_end
        decay = jnp.exp(m_prev - mx_end)
        kw_col = jnp.exp(g - mx_end)

        qc = q_ref[pl.ds(r0, L), :]
        kc = k_ref[pl.ds(r0, L), :]
        vc = zm_ref[0, pl.ds(r0, L), 2 * W:3 * W]
        oc = zm_ref[0, pl.ds(r0, L), 3 * W:4 * W]

        g_row = grs_ref[pl.ds(c, 1), :]
        dmat = jnp.where(causal, jnp.exp(jnp.where(causal, g_row - _expand_heads(mx, L), 0.0)), 0.0)
        kbd = jnp.where(blockmask, jnp.concatenate([kc] * 4, axis=0), 0.0).astype(BF16)
        s_qk = lax.dot_general(qc, kbd, (((1,), (1,)), ((), ())), preferred_element_type=F32)
        s_w = (s_qk * dmat).astype(BF16)
        vaug = jnp.concatenate([vc, ones_col], axis=1)
        vbd = jnp.where(cn_mask, jnp.concatenate([vaug] * 4, axis=0), 0.0).astype(BF16)
        intra = jnp.dot(s_w, vbd, preferred_element_type=F32)
        inter = jnp.dot(qc, cn_ref[...].astype(BF16), preferred_element_type=F32)
        num = _expand_heads(w_inter, L) * inter[:, :W] + intra[:, :W]
        den = w_inter * inter[:, W:] + intra[:, W:]
        hden = jnp.maximum(jnp.abs(den), jnp.exp(-(b + mx)))
        hout = num * _expand_heads(1.0 / hden, L)
        y_ref[0, pl.ds(r0, L), :] = _sigmoid(oc) * hout

        kw = (_expand_heads(kw_col, L) * kc).astype(BF16)
        upd = lax.dot_general(kw, vaug.astype(BF16), (((0,), (0,)), ((), ())), preferred_element_type=F32)
        decay_x = jnp.concatenate([_expand_heads(decay, L), decay], axis=1)
        cn_ref[...] = decay_x * cn_ref[...] + jnp.where(cn_mask, upd, 0.0)
        m_ref[...] = m_new
        return carry

    lax.fori_loop(0, nct, chunk, 0)


def _mlstm(zm, grow, cw, cb, gbc, gbr, tt):
    b, s, _ = zm.shape
    nct = tt // CHUNK
    w2 = 2 * GROUP_W
    return pl.pallas_call(
        functools.partial(_mlstm_kernel, tt=tt),
        grid=(b, s // tt),
        in_specs=[pl.BlockSpec((1, tt, ZM_W), lambda i, t: (i, t, 0)),
                  pl.BlockSpec((1, 2, nct, GROUP_W), lambda i, t: (i, 0, t, 0)),
                  _const_spec(cw.shape), _const_spec(cb.shape), _const_spec(gbc.shape), _const_spec(gbr.shape)],
        out_specs=pl.BlockSpec((1, tt, GROUP_W), lambda i, t: (i, t, 0)),
        out_shape=jax.ShapeDtypeStruct((b, s, GROUP_W), F32),
        scratch_shapes=[pltpu.VMEM((tt + 8, w2), F32), pltpu.VMEM((tt, GROUP_W), BF16),
                        pltpu.VMEM((tt, GROUP_W), F32), pltpu.VMEM((nct, GROUP_W), F32),
                        pltpu.VMEM((GROUP_W, GROUP_W + LANES), F32), pltpu.VMEM((1, LANES), F32)],
        compiler_params=_cparams(("parallel", "arbitrary")),
        name="mlstm",
    )(zm, grow, cw, cb, gbc, gbr)


def _gla_kernel(zg_ref, wa_ref, ba_ref, y_ref, q_ref, k_ref, g_ref, o_ref, st_ref, *, tt):
    nct = tt // CHUNK
    L = CHUNK
    WK = GLA_HEADS * GLA_DK
    WV = GLA_HEADS * GLA_DV

    @pl.when(pl.program_id(1) == 0)
    def _():
        st_ref[...] = jnp.zeros_like(st_ref)

    q_ref[...] = zg_ref[0, :, 0:WK] * (GLA_DK ** -0.5)
    k_ref[...] = zg_ref[0, :, WK:2 * WK]
    a = zg_ref[0, :, 2 * WK + 2 * WV:2 * WK + 2 * WV + LANES]
    pre = jnp.dot(a, wa_ref[...], precision=HI, preferred_element_type=F32) + ba_ref[...]
    g_ref[...] = _log_sigmoid(pre) * (1.0 / GLA_TAU)

    tri = _tri(L)
    hs2 = ((_iota((WK, WV), 0) // GLA_DK) == (_iota((WK, WV), 1) // GLA_DV)).astype(BF16)
    st_mask = (_iota((WV, WK), 0) // GLA_DV) == (_iota((WV, WK), 1) // GLA_DK)

    def chunk(c, carry):
        r0 = pl.multiple_of(c * L, L)
        qc = q_ref[pl.ds(r0, L), :]
        kc = k_ref[pl.ds(r0, L), :]
        vc = zg_ref[0, pl.ds(r0, L), 2 * WK:2 * WK + WV]
        b = jnp.dot(tri, g_ref[pl.ds(r0, L), :], precision=HI, preferred_element_type=F32)
        st = st_ref[...]
        qd = (qc * jnp.exp(b)).astype(BF16)
        o = lax.dot_general(qd, st.astype(BF16), (((1,), (1,)), ((), ())), preferred_element_type=F32)

        pieces = []
        for jb in range(L // 8):
            t0 = 8 * jb
            rows = L - t0
            qs, bs = qc[t0:, :], b[t0:, :]
            tl = _iota((rows, WK), 0)
            ps = []
            for jj in range(8):
                j = t0 + jj
                e = jnp.exp(jnp.minimum(bs - b[j:j + 1, :], 0.0))
                ps.append(jnp.where(tl >= jj, qs * kc[j:j + 1, :] * e, 0.0).astype(BF16))
            rexp = jnp.dot(jnp.concatenate(ps, axis=0), hs2, preferred_element_type=F32)
            acc = rexp[0:rows, :] * vc[t0:t0 + 1, :]
            for jj in range(1, 8):
                acc = acc + rexp[jj * rows:(jj + 1) * rows, :] * vc[t0 + jj:t0 + jj + 1, :]
            pieces.append(acc)
        bands = []
        for band in range(L // 8):
            tot = None
            for jb in range(band + 1):
                lo = 8 * (band - jb)
                part = pieces[jb][lo:lo + 8, :]
                tot = part if tot is None else tot + part
            bands.append(tot)
        o = o + jnp.concatenate(bands, axis=0)
        o_ref[pl.ds(r0, L), :] = o

        b_end = b[L - 1:L, :]
        kd = (kc * jnp.exp(b_end - b)).astype(BF16)
        upd = lax.dot_general(vc.astype(BF16), kd, (((0,), (0,)), ((), ())), preferred_element_type=F32)
        st_ref[...] = st * jnp.exp(b_end) + jnp.where(st_mask, upd, 0.0)
        return carry

    lax.fori_loop(0, nct, chunk, 0)

    o = o_ref[...]
    hsame = ((_iota((WV, WV), 0) // GLA_DV) == (_iota((WV, WV), 1) // GLA_DV)).astype(F32)
    ms = jnp.dot(o * o, hsame, precision=HI, preferred_element_type=F32) * (1.0 / GLA_DV)
    r = zg_ref[0, :, 2 * WK + WV:2 * WK + 2 * WV]
    y_ref[0] = o * lax.rsqrt(ms + EPS) * (r * _sigmoid(r))


def _gla(zg, wa, ba, tt):
    b, s, _ = zg.shape
    return pl.pallas_call(
        functools.partial(_gla_kernel, tt=tt),
        grid=(b, s // tt),
        in_specs=[pl.BlockSpec((1, tt, ZG_W), lambda i, t: (i, t, 0)), _const_spec(wa.shape), _const_spec(ba.shape)],
        out_specs=pl.BlockSpec((1, tt, GROUP_W), lambda i, t: (i, t, 0)),
        out_shape=jax.ShapeDtypeStruct((b, s, GROUP_W), F32),
        scratch_shapes=[pltpu.VMEM((tt, LANES), F32), pltpu.VMEM((tt, LANES), F32), pltpu.VMEM((tt, LANES), F32),
                        pltpu.VMEM((tt, GROUP_W), F32), pltpu.VMEM((GROUP_W, LANES), F32)],
        compiler_params=_cparams(("parallel", "arbitrary")),
        name="gla",
    )(zg, wa, ba)


CONV_HIST = 32


def _conv_kernel(zc_ref, cw_ref, cb_ref, lg_ref, lb_ref, y_ref, xpad_ref, *, tt):
    C = GROUP_W

    @pl.when(pl.program_id(1) == 0)
    def _():
        xpad_ref[0:CONV_HIST, :] = jnp.zeros((CONV_HIST, C), F32)

    val = zc_ref[0, :, 0:C]
    gate = zc_ref[0, :, C:2 * C]
    xpad_ref[CONV_HIST:CONV_HIST + tt, :] = val * _sigmoid(gate)
    acc = jnp.broadcast_to(cb_ref[...], (tt, C))
    for j in range(CONV_WIDTH):
        acc = acc + cw_ref[j:j + 1, :] * xpad_ref[pl.ds(CONV_HIST - (CONV_WIDTH - 1) + j, tt), :]
    xpad_ref[0:CONV_HIST, :] = xpad_ref[tt:tt + CONV_HIST, :]
    mu = jnp.mean(acc, axis=-1, keepdims=True)
    d = acc - mu
    var = jnp.mean(d * d, axis=-1, keepdims=True)
    yn = d * lax.rsqrt(var + EPS) * lg_ref[...] + lb_ref[...]
    y_ref[0] = yn * _sigmoid(yn)


def _conformer(zc, cw, cb, lg, lb, tt):
    b, s, _ = zc.shape
    return pl.pallas_call(
        functools.partial(_conv_kernel, tt=tt),
        grid=(b, s // tt),
        in_specs=[pl.BlockSpec((1, tt, ZC_W), lambda i, t: (i, t, 0)), _const_spec(cw.shape), _const_spec(cb.shape),
                  _const_spec(lg.shape), _const_spec(lb.shape)],
        out_specs=pl.BlockSpec((1, tt, GROUP_W), lambda i, t: (i, t, 0)),
        out_shape=jax.ShapeDtypeStruct((b, s, GROUP_W), F32),
        scratch_shapes=[pltpu.VMEM((tt + CONV_HIST, GROUP_W), F32)],
        compiler_params=_cparams(("parallel", "arbitrary")),
        name="conformer",
    )(zc, cw, cb, lg, lb)


MLA_HP = 128


def _mla_prep_kernel(za_ref, rc_ref, rs_ref, qg_ref, kg_ref, wqa_ref, wqb_ref, wka_ref, wkb_ref, wv_ref,
                     q_ref, k_ref, v_ref):
    cq = za_ref[0, :, 0:256]
    ckv = za_ref[0, :, 256:384]
    kr = za_ref[0, :, 384:512]
    qn = (cq * lax.rsqrt(jnp.sum(cq * cq, axis=-1, keepdims=True) * (1.0 / MLA_Q_RANK) + EPS) * qg_ref[...]).astype(BF16)
    kvn = ckv * lax.rsqrt(jnp.mean(ckv * ckv, axis=-1, keepdims=True) + EPS) * kg_ref[...]
    kin = jnp.concatenate([kvn, kr], axis=1).astype(BF16)
    cos = rc_ref[0]
    sin = rs_ref[0]
    qa = jnp.dot(qn, wqa_ref[...], preferred_element_type=F32)
    qb = jnp.dot(qn, wqb_ref[...], preferred_element_type=F32)
    ka = jnp.dot(kin, wka_ref[...], preferred_element_type=F32)
    kb = jnp.dot(kin, wkb_ref[...], preferred_element_type=F32)
    vt = lax.dot_general(wv_ref[...], kin, (((1,), (1,)), ((), ())), preferred_element_type=F32)
    scale = (MLA_NOPE + MLA_ROPE) ** -0.5 * LOG2E
    for h in range(MLA_HEADS):
        sl = slice(h * MLA_HP, (h + 1) * MLA_HP)
        q_ref[0, h] = ((qa[:, sl] * cos + qb[:, sl] * sin) * scale).astype(BF16)
        k_ref[0, h] = (ka[:, sl] * cos + kb[:, sl] * sin).astype(BF16)
        v_ref[0, h] = vt[h * MLA_V:(h + 1) * MLA_V, :].astype(BF16)


def _mla_prep(za, rc, rs, qg, kg, wqa, wqb, wka, wkb, wv, tt):
    b, s, _ = za.shape
    hspec = pl.BlockSpec((1, MLA_HEADS, tt, MLA_HP), lambda i, t: (i, 0, t, 0))
    tspec = pl.BlockSpec((1, tt, MLA_HP), lambda i, t: (i, t, 0))
    hshape = jax.ShapeDtypeStruct((b, MLA_HEADS, s, MLA_HP), BF16)
    vspec = pl.BlockSpec((1, MLA_HEADS, MLA_V, tt), lambda i, t: (i, 0, 0, t))
    vshape = jax.ShapeDtypeStruct((b, MLA_HEADS, MLA_V, s), BF16)
    return pl.pallas_call(
        _mla_prep_kernel,
        grid=(b, s // tt),
        in_specs=[pl.BlockSpec((1, tt, ZA_W), lambda i, t: (i, t, 0)), tspec, tspec,
                  _const_spec(qg.shape), _const_spec(kg.shape), _const_spec(wqa.shape), _const_spec(wqb.shape),
                  _const_spec(wka.shape), _const_spec(wkb.shape), _const_spec(wv.shape)],
        out_specs=[hspec, hspec, vspec],
        out_shape=[hshape, hshape, vshape],
        compiler_params=_cparams(("parallel", "parallel")),
        name="mla_prep",
    )(za, rc, rs, qg, kg, wqa, wqb, wka, wkb, wv)


def _mla_attn_kernel(q_ref, k_ref, v_ref, y_ref, m_ref, l_ref, acc_ref, *, tq):
    qi = pl.program_id(1)
    ki = pl.program_id(2)

    @pl.when(ki == 0)
    def _():
        m_ref[...] = jnp.full_like(m_ref, -jnp.inf)
        l_ref[...] = jnp.zeros_like(l_ref)
        acc_ref[...] = jnp.zeros_like(acc_ref)

    def step(masked):
        for h in range(MLA_HEADS):
            s = lax.dot_general(k_ref[0, h], q_ref[0, h], (((1,), (1,)), ((), ())), preferred_element_type=F32)
            if masked:
                s = jnp.where(_iota((tq, tq), 0) <= _iota((tq, tq), 1), s, -jnp.inf)
            m_prev = m_ref[h]
            m_new = jnp.maximum(m_prev, jnp.max(s, axis=0, keepdims=True))
            alpha = jnp.exp2(m_prev - m_new)
            p = jnp.exp2(s - m_new)
            l_ref[h] = alpha * l_ref[h] + jnp.sum(p, axis=0, keepdims=True)
            m_ref[h] = m_new
            pv = jnp.dot(v_ref[0, h], p.astype(BF16), preferred_element_type=F32)
            acc_ref[h] = alpha * acc_ref[h] + pv

    @pl.when(ki < qi)
    def _():
        step(False)

    @pl.when(ki == qi)
    def _():
        step(True)
        ot = jnp.concatenate([acc_ref[h] * (1.0 / l_ref[h]) for h in range(MLA_HEADS)], axis=0)
        y_ref[0] = ot.T


def _mla_attn(q, k, v, tq):
    b, _, s, _ = q.shape
    nq = s // tq
    qspec = pl.BlockSpec((1, MLA_HEADS, tq, MLA_HP), lambda i, qi, ki: (i, 0, qi, 0))
    kspec = pl.BlockSpec((1, MLA_HEADS, tq, MLA_HP), lambda i, qi, ki: (i, 0, jnp.minimum(ki, qi), 0))
    vspec = pl.BlockSpec((1, MLA_HEADS, MLA_V, tq), lambda i, qi, ki: (i, 0, 0, jnp.minimum(ki, qi)))
    return pl.pallas_call(
        functools.partial(_mla_attn_kernel, tq=tq),
        grid=(b, nq, nq),
        in_specs=[qspec, kspec, vspec],
        out_specs=pl.BlockSpec((1, tq, GROUP_W), lambda i, qi, ki: (i, qi, 0)),
        out_shape=jax.ShapeDtypeStruct((b, s, GROUP_W), F32),
        scratch_shapes=[pltpu.VMEM((MLA_HEADS, 1, tq), F32), pltpu.VMEM((MLA_HEADS, 1, tq), F32),
                        pltpu.VMEM((MLA_HEADS, MLA_V, tq), F32)],
        compiler_params=_cparams(("parallel", "parallel", "arbitrary")),
        name="mla_attn",
    )(q, k, v)


def _outproj_kernel(*refs, routed):
    if routed:
        ym, ya, yg, yc, h_ref, fg_ref, wo_ref, n2_ref, rt_ref, hn_ref, u_ref, route_ref = refs
    else:
        ym, ya, yg, yc, h_ref, fg_ref, wo_ref, n2_ref, hn_ref, u_ref = refs
    parts = []
    for gi, y_ref in enumerate((ym, ya, yg, yc)):
        y = y_ref[...]
        yn = y * lax.rsqrt(jnp.mean(y * y, axis=-1, keepdims=True) + EPS)
        parts.append((yn * fg_ref[:, gi * GROUP_W:(gi + 1) * GROUP_W]).astype(BF16))
    ycat = jnp.concatenate(parts, axis=1)
    hn = h_ref[...] + jnp.dot(ycat, wo_ref[...], preferred_element_type=F32)
    hn_ref[...] = hn
    u = hn * lax.rsqrt(jnp.mean(hn * hn, axis=-1, keepdims=True) + EPS) * n2_ref[...]
    u_ref[...] = u.astype(u_ref.dtype)
    if routed:
        tm = u.shape[0]
        lane = _iota((tm, LANES), 1)
        logits = jnp.dot(u, rt_ref[...], precision=HI, preferred_element_type=F32)
        logits = jnp.where(lane < N_EXPERTS, logits, -jnp.inf)
        m1 = jnp.max(logits, axis=-1, keepdims=True)
        i1 = jnp.min(jnp.where(logits == m1, lane, LANES), axis=-1, keepdims=True)
        rest = jnp.where(lane == i1, -jnp.inf, logits)
        m2 = jnp.max(rest, axis=-1, keepdims=True)
        i2 = jnp.min(jnp.where(rest == m2, lane, LANES), axis=-1, keepdims=True)
        e2 = jnp.exp(m2 - m1)
        w1 = 1.0 / (1.0 + e2)
        w2 = e2 / (1.0 + e2)
        route_ref[...] = jnp.where(lane == 0, i1.astype(F32),
                                   jnp.where(lane == 1, i2.astype(F32),
                                             jnp.where(lane == 2, w1, jnp.where(lane == 3, w2, 0.0))))


def _outproj(ym, ya, yg, yc, h, fg, wo, n2, router, tm):
    n = h.shape[0]
    routed = router is not None
    row = lambda w: pl.BlockSpec((tm, w), lambda i: (i, 0))
    in_specs = [row(GROUP_W)] * 4 + [row(D_MODEL), _const_spec(fg.shape), _const_spec(wo.shape), _const_spec(n2.shape)]
    args = [ym, ya, yg, yc, h, fg, wo, n2]
    out_specs = [row(D_MODEL), row(D_MODEL)]
    out_shape = [jax.ShapeDtypeStruct((n, D_MODEL), F32), jax.ShapeDtypeStruct((n, D_MODEL), F32 if routed else BF16)]
    if routed:
        in_specs.append(_const_spec(router.shape))
        args.append(router)
        out_specs.append(row(LANES))
        out_shape.append(jax.ShapeDtypeStruct((n, LANES), F32))
    return pl.pallas_call(
        functools.partial(_outproj_kernel, routed=routed),
        grid=(n // tm,),
        in_specs=in_specs, out_specs=out_specs, out_shape=out_shape,
        compiler_params=_cparams(("parallel",)),
        name="outproj_routed" if routed else "outproj",
    )(*args)


def _swiglu(x, wg, wu, wd):
    g = jnp.dot(x, wg, preferred_element_type=F32)
    up = jnp.dot(x, wu, preferred_element_type=F32)
    a = (g * _sigmoid(g) * up).astype(BF16)
    return jnp.dot(a, wd, preferred_element_type=F32)


def _ffn_kernel(u_ref, h_ref, wg_ref, wu_ref, wd_ref, o_ref):
    o_ref[...] = h_ref[...] + _swiglu(u_ref[...], wg_ref[...], wu_ref[...], wd_ref[...])


def _ffn(u, h, wg, wu, wd, tm):
    n = h.shape[0]
    row = pl.BlockSpec((tm, D_MODEL), lambda i: (i, 0))
    return pl.pallas_call(
        _ffn_kernel,
        grid=(n // tm,),
        in_specs=[row, row, _const_spec(wg.shape), _const_spec(wu.shape), _const_spec(wd.shape)],
        out_specs=row,
        out_shape=jax.ShapeDtypeStruct((n, D_MODEL), F32),
        compiler_params=_cparams(("parallel",)),
        name="ffn_dense",
    )(u, h, wg, wu, wd)


def _moe_kernel(be_ref, tokc_ref, tokn_ref, dstp_ref, dstc_ref, u_hbm, wg_ref, wu_ref, wd_ref, y_hbm,
                xbuf, ybuf, gsem, ssem):
    del be_ref
    i = pl.program_id(0)
    nb = pl.num_programs(0)
    bm = xbuf.shape[1]

    def gather_wait(x, s):
        pltpu.make_async_copy(u_hbm.at[pl.ds(0, bm), :], x, gsem.at[s]).wait()

    def scatter_wait(y, s):
        pltpu.make_async_copy(y, y_hbm.at[pl.ds(0, bm), :], ssem.at[s]).wait()

    def scatter_row(y, dst_ref, r, s):
        pltpu.make_async_copy(y.at[pl.ds(r, 1), :], y_hbm.at[pl.ds(dst_ref[0, 0, r], 1), :], ssem.at[s]).start()

    @pl.when(i == 0)
    def _():
        def body(r, c):
            pltpu.make_async_copy(u_hbm.at[pl.ds(tokc_ref[0, 0, r], 1), :], xbuf.at[0, pl.ds(r, 1), :],
                                  gsem.at[0]).start()
            return c
        lax.fori_loop(0, bm, body, 0, unroll=8)
        ybuf[1] = jnp.zeros((bm, D_MODEL), F32)

    def step(xc, xn, yc, yp, c):
        n = 1 - c
        gather_wait(xc, c)

        @pl.when(i >= 1)
        def _():
            scatter_wait(yc, c)

        xb = xc[...].astype(BF16)
        for r in range(bm):
            pltpu.make_async_copy(u_hbm.at[pl.ds(tokn_ref[0, 0, r], 1), :], xn.at[pl.ds(r, 1), :], gsem.at[n]).start()
        for r in range(bm):
            scatter_row(yp, dstp_ref, r, n)
        yc[...] = _swiglu(xb, wg_ref[0], wu_ref[0], wd_ref[0])

        @pl.when(i == nb - 1)
        def _():
            def body(r, carry):
                scatter_row(yc, dstc_ref, r, c)
                return carry
            lax.fori_loop(0, bm, body, 0, unroll=8)
            scatter_wait(yp, n)
            scatter_wait(yc, c)
            gather_wait(xn, n)

    c = i % 2
    step(xbuf.at[c], xbuf.at[1 - c], ybuf.at[c], ybuf.at[1 - c], c)


def _moe(block_e, row_tok, row_dst, u, wg, wu, wd, n_rows):
    n_blocks = block_e.shape[0]
    bm = row_tok.shape[-1]
    wspec = lambda shp: pl.BlockSpec((1,) + shp, lambda i, be: (be[i], 0, 0))
    ispec = lambda f: pl.BlockSpec((1, 1, bm), f, memory_space=pltpu.SMEM)
    grid_spec = pltpu.PrefetchScalarGridSpec(
        num_scalar_prefetch=1,
        grid=(n_blocks,),
        in_specs=[ispec(lambda i, be: (i, 0, 0)),
                  ispec(lambda i, be: (jnp.minimum(i + 1, n_blocks - 1), 0, 0)),
                  ispec(lambda i, be: (i, 0, 0)),
                  ispec(lambda i, be: (i + 1, 0, 0)),
                  pl.BlockSpec(memory_space=pl.ANY),
                  wspec((D_MODEL, D_FF)), wspec((D_MODEL, D_FF)), wspec((D_FF, D_MODEL))],
        out_specs=pl.BlockSpec(memory_space=pl.ANY),
        scratch_shapes=[pltpu.VMEM((2, bm, D_MODEL), F32), pltpu.VMEM((2, bm, D_MODEL), F32),
                        pltpu.SemaphoreType.DMA((2,)), pltpu.SemaphoreType.DMA((2,))],
    )
    return pl.pallas_call(
        _moe_kernel,
        grid_spec=grid_spec,
        out_shape=jax.ShapeDtypeStruct((n_rows, D_MODEL), F32),
        compiler_params=_cparams(("arbitrary",)),
        name="moe_experts",
    )(block_e, row_tok, row_tok, row_dst, row_dst, u, wg, wu, wd)


def _combine_kernel(ya_ref, yb_ref, h_ref, route_ref, fg_ref, o_ref, *, final):
    w1 = route_ref[:, 2:3]
    w2 = route_ref[:, 3:4]
    hn = h_ref[...] + w1 * ya_ref[...] + w2 * yb_ref[...]
    if final:
        hn = hn * lax.rsqrt(jnp.mean(hn * hn, axis=-1, keepdims=True) + EPS) * fg_ref[...]
    o_ref[...] = hn


def _combine(y2, h, route, fg, tm, final):
    n = h.shape[0]
    return pl.pallas_call(
        functools.partial(_combine_kernel, final=final),
        grid=(n // tm,),
        in_specs=[pl.BlockSpec((tm, D_MODEL), lambda i: (i, 0)),
                  pl.BlockSpec((tm, D_MODEL), lambda i: (i + n // tm, 0)),
                  pl.BlockSpec((tm, D_MODEL), lambda i: (i, 0)),
                  pl.BlockSpec((tm, LANES), lambda i: (i, 0)),
                  _const_spec(fg.shape)],
        out_specs=pl.BlockSpec((tm, D_MODEL), lambda i: (i, 0)),
        out_shape=jax.ShapeDtypeStruct((n, D_MODEL), F32),
        compiler_params=_cparams(("parallel",)),
        name="moe_combine",
    )(y2, y2, h, route, fg)


def _rmsnorm_kernel(h_ref, g_ref, o_ref):
    x = h_ref[...]
    o_ref[...] = x * lax.rsqrt(jnp.mean(x * x, axis=-1, keepdims=True) + EPS) * g_ref[...]


def _final_norm(h, g, tm):
    n = h.shape[0]
    row = pl.BlockSpec((tm, D_MODEL), lambda i: (i, 0))
    return pl.pallas_call(
        _rmsnorm_kernel, grid=(n // tm,), in_specs=[row, _const_spec(g.shape)], out_specs=row,
        out_shape=jax.ShapeDtypeStruct((n, D_MODEL), F32),
        compiler_params=_cparams(("parallel",)), name="final_norm",
    )(h, g)


def _pad_cols(w, width):
    return jnp.pad(w, ((0, 0), (0, width - w.shape[1])))


def _pad_rows(w, height):
    return jnp.pad(w, ((0, height - w.shape[0]), (0, 0)))


def _inproj_weights(w):
    o = np.cumsum([0, 512, 256, 256, 8, 192, 128, 32, 128, 128, 256, 256, 16, 512])
    seg = lambda i: w[:, o[i]:o[i + 1]]
    gates = seg(3)
    wm = jnp.concatenate([seg(0), seg(1), seg(2), _pad_cols(gates[:, :4], LANES), _pad_cols(gates[:, 4:], LANES)], axis=1)
    wa = jnp.concatenate([_pad_cols(seg(4), 256), seg(5), _pad_cols(seg(6), LANES)], axis=1)
    wg = jnp.concatenate([seg(7), seg(8), seg(9), seg(10), _pad_cols(seg(11), LANES)], axis=1)
    wc = seg(12)
    return [x.astype(BF16) for x in (wm, wa, wg, wc)]


def _mla_weights(w_uq, w_ukv):
    half = MLA_ROPE // 2
    zq = jnp.zeros((MLA_Q_RANK, half), F32)
    qa, qb, ka, kb, wv = [], [], [], [], []
    eye = jnp.eye(LANES, dtype=F32)[:, :MLA_ROPE]
    zk = jnp.zeros((MLA_KV_RANK, MLA_HP), F32)
    for h in range(MLA_HEADS):
        wq = w_uq[:, h * 96:(h + 1) * 96]
        nope, r1, r2 = wq[:, :64], wq[:, 64:64 + half], wq[:, 64 + half:]
        qa.append(_pad_cols(jnp.concatenate([nope, r1, r2], axis=1), MLA_HP))
        qb.append(_pad_cols(jnp.concatenate([jnp.zeros_like(nope), -r2, r1], axis=1), MLA_HP))
        wkv = w_ukv[:, h * 128:(h + 1) * 128]
        kn, vh = wkv[:, :64], wkv[:, 64:]
        e1, e2 = eye[:, :half], eye[:, half:]
        z64 = jnp.zeros((LANES, 64), F32)
        ka.append(jnp.concatenate([_pad_cols(kn, MLA_HP), _pad_cols(jnp.concatenate([z64, e1, e2], axis=1), MLA_HP)], axis=0))
        kb.append(jnp.concatenate([zk, _pad_cols(jnp.concatenate([z64, -e2, e1], axis=1), MLA_HP)], axis=0))
        wv.append(vh.T)
    cat = lambda xs, rows: _pad_rows(jnp.concatenate(xs, axis=1), rows).astype(BF16)
    wvt = _pad_cols(jnp.concatenate(wv, axis=0), 256).astype(BF16)
    return cat(qa, 256), cat(qb, 256), cat(ka, 256), cat(kb, 256), wvt


def _rope_tables(positions):
    half = MLA_ROPE // 2
    inv_freq = 1.0 / (ROPE_BASE ** (jnp.arange(0, MLA_ROPE, 2, dtype=F32) / MLA_ROPE))
    ang = positions.astype(F32)[..., None] * inv_freq
    cos, sin = jnp.cos(ang), jnp.sin(ang)
    shp = cos.shape[:-1]
    rc = jnp.concatenate([jnp.ones(shp + (64,), F32), cos, cos, jnp.zeros(shp + (MLA_HP - 64 - 2 * half,), F32)], axis=-1)
    rs = jnp.concatenate([jnp.zeros(shp + (64,), F32), sin, sin, jnp.zeros(shp + (MLA_HP - 64 - 2 * half,), F32)], axis=-1)
    return rc, rs


def _routing_tables(route, bm):
    n = route.shape[0]
    a = n * TOP_K
    flat_e = route[:, :TOP_K].astype(jnp.int32).reshape(-1)
    onehot = (flat_e[:, None] == jnp.arange(N_EXPERTS, dtype=jnp.int32)[None, :]).astype(jnp.int32)
    csum = jnp.cumsum(onehot, axis=0)
    rank = jnp.sum(onehot * csum, axis=1) - 1
    counts = csum[-1]
    padded = (counts + bm - 1) // bm * bm
    padded_end = jnp.cumsum(padded)
    padded_start = padded_end - padded
    dest = padded_start[flat_e] + rank
    n_blocks = -(-a // bm) + N_EXPERTS
    rows = n_blocks * bm
    row_asg = jnp.full((rows,), -1, jnp.int32).at[dest].set(jnp.arange(a, dtype=jnp.int32))
    row_tok = jnp.maximum(row_asg, 0) // TOP_K
    pad_rank = jnp.cumsum((row_asg < 0).astype(jnp.int32)) - 1
    slot_row = (row_asg % TOP_K) * n + row_asg // TOP_K
    row_dst = jnp.where(row_asg >= 0, slot_row, a + pad_rank)
    first = a + (rows - a) + jnp.arange(bm, dtype=jnp.int32)
    row_dst = jnp.concatenate([first, row_dst]).reshape(n_blocks + 1, 1, bm)
    block_start = jnp.arange(n_blocks, dtype=jnp.int32) * bm
    block_e = jnp.sum((padded_end[None, :] <= block_start[:, None]).astype(jnp.int32), axis=1)
    block_e = jnp.minimum(block_e, N_EXPERTS - 1).astype(jnp.int32)
    return block_e, row_tok.reshape(n_blocks, 1, bm), row_dst, rows + bm


def kernel(x, positions, norm1_g, w_in, mlstm_conv_w, mlstm_conv_b, mlstm_gate_b, mla_q_norm_g, mla_w_uq,
           mla_kv_norm_g, mla_w_ukv, gla_w_alpha, gla_b_alpha, conv_w, conv_b, conv_ln_g, conv_ln_b, fuse_g,
           w_out, norm2_g, ffn_w_gate, ffn_w_up, ffn_w_down, moe_router, moe_w_gate, moe_w_up, moe_w_down,
           final_norm_g):
    bsz, seq, _ = x.shape
    n = bsz * seq
    depth = w_in.shape[0]
    tm = min(512, n)
    tt = min(512, seq)
    tq = min(512, seq)
    nct = seq // CHUNK
    row2 = lambda v: v.reshape(1, -1).astype(F32)

    rc, rs = _rope_tables(positions)
    h = x.reshape(n, D_MODEL)
    for l in range(depth):
        wm, wa, wg, wc = _inproj_weights(w_in[l])
        zm, za, zg, zc = _inproj(h, row2(norm1_g[l]), wm, wa, wg, wc, tm)
        zm = zm.reshape(bsz, seq, ZM_W)

        gates = jnp.concatenate([zm[:, :, 1024:1028], zm[:, :, 1152:1156]], axis=-1)
        grow = gates.reshape(bsz, nct, CHUNK, 2, MLSTM_HEADS).transpose(0, 3, 1, 4, 2).reshape(bsz, 2, nct, GROUP_W)
        gb = mlstm_gate_b[l].astype(F32)
        gbc = _pad_cols(gb.reshape(2, MLSTM_HEADS), LANES)
        gbr = jnp.repeat(gb.reshape(2, MLSTM_HEADS), CHUNK, axis=1)
        ym = _mlstm(zm, grow, mlstm_conv_w[l], row2(mlstm_conv_b[l]), gbc, gbr, tt)

        wqa, wqb, wka, wkb, wv = _mla_weights(mla_w_uq[l], mla_w_ukv[l])
        qh, kh, vh = _mla_prep(za.reshape(bsz, seq, ZA_W), rc, rs, _pad_cols(row2(mla_q_norm_g[l]), 256),
                               row2(mla_kv_norm_g[l]), wqa, wqb, wka, wkb, wv, tt)
        ya = _mla_attn(qh, kh, vh, tq)

        yg = _gla(zg.reshape(bsz, seq, ZG_W), _pad_rows(gla_w_alpha[l], LANES), row2(gla_b_alpha[l]), tt)
        yc = _conformer(zc.reshape(bsz, seq, ZC_W), conv_w[l], row2(conv_b[l]), row2(conv_ln_g[l]),
                        row2(conv_ln_b[l]), tt)

        flat = lambda y: y.reshape(n, GROUP_W)
        j = l // 2
        if l % 2 == 0:
            h, u = _outproj(flat(ym), flat(ya), flat(yg), flat(yc), h, row2(fuse_g[l]), w_out[l].astype(BF16),
                            row2(norm2_g[l]), None, tm)
            h = _ffn(u, h, ffn_w_gate[j].astype(BF16), ffn_w_up[j].astype(BF16), ffn_w_down[j].astype(BF16), tm)
            if l == depth - 1:
                h = _final_norm(h, row2(final_norm_g), tm)
        else:
            h, u, route = _outproj(flat(ym), flat(ya), flat(yg), flat(yc), h, row2(fuse_g[l]),
                                   w_out[l].astype(BF16), row2(norm2_g[l]), _pad_cols(moe_router[j], LANES), tm)
            block_e, row_tok, row_dst, n_rows = _routing_tables(route, MOE_BM)
            y2 = _moe(block_e, row_tok, row_dst, u, moe_w_gate[j].astype(BF16), moe_w_up[j].astype(BF16),
                      moe_w_down[j].astype(BF16), n_rows)
            h = _combine(y2, h, route, row2(final_norm_g), tm, l == depth - 1)
    return h.reshape(bsz, seq, D_MODEL)
```

```python
import functools

import jax
import jax.numpy as jnp
import numpy as np
from jax import lax
from jax.experimental import pallas as pl
from jax.experimental.pallas import tpu as pltpu

F32 = jnp.float32
BF16 = jnp.bfloat16
HI = lax.Precision.HIGHEST

D_MODEL = 1024
GROUP_W = 256
EPS = 1e-6
LOG2E = 1.4426950408889634
LANES = 128
CHUNK = 64
MLSTM_HEADS = 4
MLSTM_CONV = 4
MLA_HEADS = 4
MLA_NOPE = 64
MLA_ROPE = 32
MLA_V = 64
MLA_Q_RANK = 192
MLA_KV_RANK = 128
ROPE_BASE = 10000.0
GLA_HEADS = 4
GLA_DK = 32
GLA_DV = 64
GLA_GATE_RANK = 16
GLA_TAU = 16.0
CONV_WIDTH = 31
D_FF = 2816
N_EXPERTS = 8
TOP_K = 2
MOE_BM = 256
VMEM_LIMIT = 56 * 1024 * 1024

ZM_W = 1280
ZA_W = 512
ZG_W = 896
ZC_W = 512


def _cparams(sem):
    return pltpu.CompilerParams(dimension_semantics=sem, vmem_limit_bytes=VMEM_LIMIT)


def _const_spec(shape):
    nd = len(shape)
    return pl.BlockSpec(shape, lambda *_: (0,) * nd, pipeline_mode=pl.Buffered(1))


def _sigmoid(x):
    return 1.0 / (1.0 + jnp.exp(-x))


def _log_sigmoid(x):
    return jnp.minimum(x, 0.0) - jnp.log(1.0 + jnp.exp(-jnp.abs(x)))


def _iota(shape, dim):
    return lax.broadcasted_iota(jnp.int32, shape, dim)


def _tri(n):
    return (_iota((n, n), 0) >= _iota((n, n), 1)).astype(BF16)


def _split3(x):
    hi = x.astype(BF16)
    r1 = x - hi.astype(F32)
    mid = r1.astype(BF16)
    lo = (r1 - mid.astype(F32)).astype(BF16)
    return hi, mid, lo


def _sel_dot(sel, x):
    hi, mid, lo = _split3(x)
    d = lambda p: jnp.dot(sel, p, preferred_element_type=F32)
    return d(hi) + d(mid) + d(lo)


def _dot_sel(x, sel):
    hi, mid, lo = _split3(x)
    d = lambda p: jnp.dot(p, sel, preferred_element_type=F32)
    return d(hi) + d(mid) + d(lo)


def _inproj_kernel(h_ref, g_ref, wm_ref, wa_ref, wg_ref, wc_ref, zm_ref, za_ref, zg_ref, zc_ref):
    x = h_ref[...]
    ms = jnp.mean(x * x, axis=-1, keepdims=True)
    xn = (x * lax.rsqrt(ms + EPS) * g_ref[...]).astype(BF16)
    zm_ref[...] = jnp.dot(xn, wm_ref[...], preferred_element_type=F32)
    za_ref[...] = jnp.dot(xn, wa_ref[...], preferred_element_type=F32)
    zg_ref[...] = jnp.dot(xn, wg_ref[...], preferred_element_type=F32)
    zc_ref[...] = jnp.dot(xn, wc_ref[...], preferred_element_type=F32)


def _inproj(h, g, wm, wa, wg, wc, tm):
    n = h.shape[0]
    row = lambda w: pl.BlockSpec((tm, w), lambda i: (i, 0))
    return pl.pallas_call(
        _inproj_kernel,
        grid=(n // tm,),
        in_specs=[row(D_MODEL), _const_spec((1, D_MODEL)), _const_spec(wm.shape), _const_spec(wa.shape),
                  _const_spec(wg.shape), _const_spec(wc.shape)],
        out_specs=[row(ZM_W), row(ZA_W), row(ZG_W), row(ZC_W)],
        out_shape=[jax.ShapeDtypeStruct((n, w), F32) for w in (ZM_W, ZA_W, ZG_W, ZC_W)],
        compiler_params=_cparams(("parallel",)),
        name="inproj",
    )(h, g, wm, wa, wg, wc)


def _expand_heads(x, width):
    r = x.shape[0]
    lane = _iota((r, 4 * width), 1)
    out = jnp.broadcast_to(x[:, 3:4], (r, 4 * width))
    for h in (2, 1, 0):
        out = jnp.where(lane < (h + 1) * width, jnp.broadcast_to(x[:, h:h + 1], (r, 4 * width)), out)
    return out


def _mlstm_kernel(zm_ref, grow_ref, cw_ref, cb_ref, gbc_ref, gbr_ref, y_ref,
                  xpad_ref, q_ref, k_ref, grs_ref, cn_ref, m_ref, cnm_ref, trit_ref, *, tt):
    nct = tt // CHUNK
    L = CHUNK
    W = MLSTM_HEADS * 64

    @pl.when(pl.program_id(1) == 0)
    def _():
        xpad_ref[0:8, :] = jnp.zeros((8, 2 * W), F32)
        cn_ref[...] = jnp.zeros_like(cn_ref)
        m_ref[...] = jnp.zeros_like(m_ref)
        hsame = (_iota((W, W), 0) // L) == (_iota((W, W), 1) // L)
        ncol = (_iota((W, LANES), 0) // L) == _iota((W, LANES), 1)
        cnm_ref[...] = jnp.concatenate([hsame, ncol], axis=1).astype(F32)
        trit_ref[...] = (hsame & ((_iota((W, W), 0) % L) <= (_iota((W, W), 1) % L))).astype(BF16)

    xpad_ref[8:8 + tt, :] = zm_ref[0, :, 0:2 * W]
    for r in range(0, tt, L):
        acc = jnp.broadcast_to(cb_ref[...], (L, 2 * W))
        for j in range(MLSTM_CONV):
            acc = acc + cw_ref[j:j + 1, :] * xpad_ref[pl.ds(r + 8 - (MLSTM_CONV - 1) + j, L), :]
        qk = acc * _sigmoid(acc)
        q_ref[r:r + L, :] = (qk[:, :W] * (64 ** -0.5)).astype(BF16)
        k_ref[r:r + L, :] = qk[:, W:]
    xpad_ref[0:8, :] = xpad_ref[tt:tt + 8, :]

    i_row = grow_ref[0, 0] + gbr_ref[0:1, :]
    lf_row = _log_sigmoid(grow_ref[0, 1] + gbr_ref[1:2, :])
    grs_ref[...] = i_row - _dot_sel(lf_row, trit_ref[...])

    tri = _tri(L)
    row_t = _iota((L, W), 0)
    lane_j = _iota((L, W), 1) % L
    causal = lane_j <= row_t
    ones_col = (_iota((L, LANES), 1) < MLSTM_HEADS).astype(F32)
    row_l = _iota((L, LANES), 0)

    def chunk(c, carry):
        r0 = pl.multiple_of(c * L, L)
        ic = zm_ref[0, pl.ds(r0, L), 4 * W:4 * W + LANES] + gbc_ref[0:1, :]
        lf = _log_sigmoid(zm_ref[0, pl.ds(r0, L), 4 * W + LANES:4 * W + 2 * LANES] + gbc_ref[1:2, :])
        b = _sel_dot(tri, lf)
        g = ic - b
        cm = g
        s = 1
        while s < L:
            cm = jnp.maximum(cm, jnp.where(row_l >= s, pltpu.roll(cm, s, 0), -jnp.inf))
            s *= 2
        m_prev = m_ref[...]
        mx = jnp.maximum(m_prev, cm)
        w_inter = jnp.exp(m_prev - mx)
        b_end = b[L - 1:L, :]
        mx_end = mx[L - 1:L, :]
        m_new = b_end + mx_end
        decay = jnp.exp(m_prev - mx_end)
        kw_col = jnp.exp(g - mx_end)

        qc = q_ref[pl.ds(r0, L), :]
        kc = k_ref[pl.ds(r0, L), :]
        vc = zm_ref[0, pl.ds(r0, L), 2 * W:3 * W]
        oc = zm_ref[0, pl.ds(r0, L), 3 * W:4 * W]

        g_row = grs_ref[pl.ds(c, 1), :]
        dmat = jnp.where(causal, jnp.exp(jnp.where(causal, g_row - _expand_heads(mx, L), 0.0)), 0.0)
        kbd = (jnp.concatenate([kc] * 4, axis=0) * cnm_ref[:, 0:W]).astype(BF16)
        s_qk = lax.dot_general(qc, kbd, (((1,), (1,)), ((), ())), preferred_element_type=F32)
        s_w = (s_qk * dmat).astype(BF16)
        vaug = jnp.concatenate([vc, ones_col], axis=1)
        vbd = (jnp.concatenate([vaug] * 4, axis=0) * cnm_ref[...]).astype(BF16)
        intra = jnp.dot(s_w, vbd, preferred_element_type=F32)
        inter = jnp.dot(qc, cn_ref[...].astype(BF16), preferred_element_type=F32)
        num = _expand_heads(w_inter, L) * inter[:, :W] + intra[:, :W]
        den = w_inter * inter[:, W:] + intra[:, W:]
        hden = jnp.maximum(jnp.abs(den), jnp.exp(-(b + mx)))
        hout = num * _expand_heads(1.0 / hden, L)
        y_ref[0, pl.ds(r0, L), :] = _sigmoid(oc) * hout

        kw = (_expand_heads(kw_col, L) * kc).astype(BF16)
        upd = lax.dot_general(kw, vaug.astype(BF16), (((0,), (0,)), ((), ())), preferred_element_type=F32)
        decay_x = jnp.concatenate([_expand_heads(decay, L), decay], axis=1)
        cn_ref[...] = decay_x * cn_ref[...] + upd * cnm_ref[...]
        m_ref[...] = m_new
        return carry

    lax.fori_loop(0, nct, chunk, 0, unroll=2)


def _mlstm(zm, grow, cw, cb, gbc, gbr, tt):
    b, s, _ = zm.shape
    nct = tt // CHUNK
    w2 = 2 * GROUP_W
    return pl.pallas_call(
        functools.partial(_mlstm_kernel, tt=tt),
        grid=(b, s // tt),
        in_specs=[pl.BlockSpec((1, tt, ZM_W), lambda i, t: (i, t, 0)),
                  pl.BlockSpec((1, 2, nct, GROUP_W), lambda i, t: (i, 0, t, 0)),
                  _const_spec(cw.shape), _const_spec(cb.shape), _const_spec(gbc.shape), _const_spec(gbr.shape)],
        out_specs=pl.BlockSpec((1, tt, GROUP_W), lambda i, t: (i, t, 0)),
        out_shape=jax.ShapeDtypeStruct((b, s, GROUP_W), F32),
        scratch_shapes=[pltpu.VMEM((tt + 8, w2), F32), pltpu.VMEM((tt, GROUP_W), BF16),
                        pltpu.VMEM((tt, GROUP_W), F32), pltpu.VMEM((nct, GROUP_W), F32),
                        pltpu.VMEM((GROUP_W, GROUP_W + LANES), F32), pltpu.VMEM((1, LANES), F32),
                        pltpu.VMEM((GROUP_W, GROUP_W + LANES), F32), pltpu.VMEM((GROUP_W, GROUP_W), BF16)],
        compiler_params=_cparams(("parallel", "arbitrary")),
        name="mlstm",
    )(zm, grow, cw, cb, gbc, gbr)


def _gla_kernel(zg_ref, wa_ref, ba_ref, y_ref, q_ref, k_ref, g_ref, o_ref, st_ref, *, tt):
    nct = tt // CHUNK
    L = CHUNK
    WK = GLA_HEADS * GLA_DK
    WV = GLA_HEADS * GLA_DV

    @pl.when(pl.program_id(1) == 0)
    def _():
        st_ref[...] = jnp.zeros_like(st_ref)

    q_ref[...] = zg_ref[0, :, 0:WK] * (GLA_DK ** -0.5)
    k_ref[...] = zg_ref[0, :, WK:2 * WK]
    a = zg_ref[0, :, 2 * WK + 2 * WV:2 * WK + 2 * WV + LANES]
    pre = jnp.dot(a, wa_ref[...], precision=HI, preferred_element_type=F32) + ba_ref[...]
    g_ref[...] = _log_sigmoid(pre) * (1.0 / GLA_TAU)

    tri = _tri(L)
    hs2 = ((_iota((WK, WV), 0) // GLA_DK) == (_iota((WK, WV), 1) // GLA_DV)).astype(BF16)
    st_mask = (_iota((WV, WK), 0) // GLA_DV) == (_iota((WV, WK), 1) // GLA_DK)

    def chunk(c, carry):
        r0 = pl.multiple_of(c * L, L)
        qc = q_ref[pl.ds(r0, L), :]
        kc = k_ref[pl.ds(r0, L), :]
        vc = zg_ref[0, pl.ds(r0, L), 2 * WK:2 * WK + WV]
        b = _sel_dot(tri, g_ref[pl.ds(r0, L), :])
        st = st_ref[...]
        qd = (qc * jnp.exp(b)).astype(BF16)
        o = lax.dot_general(qd, st.astype(BF16), (((1,), (1,)), ((), ())), preferred_element_type=F32)

        pieces = []
        for jb in range(L // 8):
            t0 = 8 * jb
            rows = L - t0
            qs, bs = qc[t0:, :], b[t0:, :]
            tl = _iota((rows, WK), 0)
            ps = []
            for jj in range(8):
                j = t0 + jj
                e = jnp.exp(jnp.minimum(bs - b[j:j + 1, :], 0.0))
                ps.append(jnp.where(tl >= jj, qs * kc[j:j + 1, :] * e, 0.0).astype(BF16))
            rexp = jnp.dot(jnp.concatenate(ps, axis=0), hs2, preferred_element_type=F32)
            acc = rexp[0:rows, :] * vc[t0:t0 + 1, :]
            for jj in range(1, 8):
                acc = acc + rexp[jj * rows:(jj + 1) * rows, :] * vc[t0 + jj:t0 + jj + 1, :]
            pieces.append(acc)
        bands = []
        for band in range(L // 8):
            tot = None
            for jb in range(band + 1):
                lo = 8 * (band - jb)
                part = pieces[jb][lo:lo + 8, :]
                tot = part if tot is None else tot + part
            bands.append(tot)
        o = o + jnp.concatenate(bands, axis=0)
        o_ref[pl.ds(r0, L), :] = o

        b_end = b[L - 1:L, :]
        kd = (kc * jnp.exp(b_end - b)).astype(BF16)
        upd = lax.dot_general(vc.astype(BF16), kd, (((0,), (0,)), ((), ())), preferred_element_type=F32)
        st_ref[...] = st * jnp.exp(b_end) + jnp.where(st_mask, upd, 0.0)
        return carry

    lax.fori_loop(0, nct, chunk, 0, unroll=2)

    o = o_ref[...]
    hsame = ((_iota((WV, WV), 0) // GLA_DV) == (_iota((WV, WV), 1) // GLA_DV)).astype(BF16)
    ms = _dot_sel(o * o, hsame) * (1.0 / GLA_DV)
    r = zg_ref[0, :, 2 * WK + WV:2 * WK + 2 * WV]
    y_ref[0] = o * lax.rsqrt(ms + EPS) * (r * _sigmoid(r))


def _gla(zg, wa, ba, tt):
    b, s, _ = zg.shape
    return pl.pallas_call(
        functools.partial(_gla_kernel, tt=tt),
        grid=(b, s // tt),
        in_specs=[pl.BlockSpec((1, tt, ZG_W), lambda i, t: (i, t, 0)), _const_spec(wa.shape), _const_spec(ba.shape)],
        out_specs=pl.BlockSpec((1, tt, GROUP_W), lambda i, t: (i, t, 0)),
        out_shape=jax.ShapeDtypeStruct((b, s, GROUP_W), F32),
        scratch_shapes=[pltpu.VMEM((tt, LANES), F32), pltpu.VMEM((tt, LANES), F32), pltpu.VMEM((tt, LANES), F32),
                        pltpu.VMEM((tt, GROUP_W), F32), pltpu.VMEM((GROUP_W, LANES), F32)],
        compiler_params=_cparams(("parallel", "arbitrary")),
        name="gla",
    )(zg, wa, ba)


CONV_HIST = 32
CONV_SUB = 64


def _conv_kernel(zc_ref, cw_ref, cb_ref, lg_ref, lb_ref, y_ref, xpad_ref, xs_ref, *, tt):
    C = GROUP_W
    span = tt + CONV_HIST - 8

    @pl.when(pl.program_id(1) == 0)
    def _():
        xpad_ref[0:CONV_HIST, :] = jnp.zeros((CONV_HIST, C), F32)

    val = zc_ref[0, :, 0:C]
    gate = zc_ref[0, :, C:2 * C]
    xpad_ref[CONV_HIST:CONV_HIST + tt, :] = val * _sigmoid(gate)
    for sh in range(1, 8):
        xs_ref[sh - 1, 0:span, :] = xpad_ref[pl.ds(sh, span), :]
    sub = min(CONV_SUB, tt)
    for r in range(0, tt, sub):
        acc = jnp.broadcast_to(cb_ref[...], (sub, C))
        for j in range(CONV_WIDTH):
            off = CONV_HIST - (CONV_WIDTH - 1) + j
            sh, base = off % 8, r + off - off % 8
            src = xpad_ref[base:base + sub, :] if sh == 0 else xs_ref[sh - 1, base:base + sub, :]
            acc = acc + cw_ref[j:j + 1, :] * src
        mu = jnp.mean(acc, axis=-1, keepdims=True)
        d = acc - mu
        var = jnp.mean(d * d, axis=-1, keepdims=True)
        yn = d * lax.rsqrt(var + EPS) * lg_ref[...] + lb_ref[...]
        y_ref[0, r:r + sub, :] = yn * _sigmoid(yn)
    xpad_ref[0:CONV_HIST, :] = xpad_ref[tt:tt + CONV_HIST, :]


def _conformer(zc, cw, cb, lg, lb, tt):
    b, s, _ = zc.shape
    return pl.pallas_call(
        functools.partial(_conv_kernel, tt=tt),
        grid=(b, s // tt),
        in_specs=[pl.BlockSpec((1, tt, ZC_W), lambda i, t: (i, t, 0)), _const_spec(cw.shape), _const_spec(cb.shape),
                  _const_spec(lg.shape), _const_spec(lb.shape)],
        out_specs=pl.BlockSpec((1, tt, GROUP_W), lambda i, t: (i, t, 0)),
        out_shape=jax.ShapeDtypeStruct((b, s, GROUP_W), F32),
        scratch_shapes=[pltpu.VMEM((tt + CONV_HIST, GROUP_W), F32), pltpu.VMEM((7, tt + CONV_HIST, GROUP_W), F32)],
        compiler_params=_cparams(("parallel", "arbitrary")),
        name="conformer",
    )(zc, cw, cb, lg, lb)


MLA_HP = 128


def _mla_prep_kernel(za_ref, rc_ref, rs_ref, qg_ref, kg_ref, wqa_ref, wqb_ref, wka_ref, wkb_ref, wv_ref,
                     q_ref, k_ref, v_ref):
    cq = za_ref[0, :, 0:256]
    ckv = za_ref[0, :, 256:384]
    kr = za_ref[0, :, 384:512]
    qn = (cq * lax.rsqrt(jnp.sum(cq * cq, axis=-1, keepdims=True) * (1.0 / MLA_Q_RANK) + EPS) * qg_ref[...]).astype(BF16)
    kvn = ckv * lax.rsqrt(jnp.mean(ckv * ckv, axis=-1, keepdims=True) + EPS) * kg_ref[...]
    kin = jnp.concatenate([kvn, kr], axis=1).astype(BF16)
    cos = rc_ref[0]
    sin = rs_ref[0]
    qa = jnp.dot(qn, wqa_ref[...], preferred_element_type=F32)
    qb = jnp.dot(qn, wqb_ref[...], preferred_element_type=F32)
    ka = jnp.dot(kin, wka_ref[...], preferred_element_type=F32)
    kb = jnp.dot(kin, wkb_ref[...], preferred_element_type=F32)
    vt = lax.dot_general(wv_ref[...], kin, (((1,), (1,)), ((), ())), preferred_element_type=F32)
    scale = (MLA_NOPE + MLA_ROPE) ** -0.5 * LOG2E
    for h in range(MLA_HEADS):
        sl = slice(h * MLA_HP, (h + 1) * MLA_HP)
        q_ref[0, h] = ((qa[:, sl] * cos + qb[:, sl] * sin) * scale).astype(BF16)
        k_ref[0, h] = (ka[:, sl] * cos + kb[:, sl] * sin).astype(BF16)
        v_ref[0, h] = vt[h * MLA_V:(h + 1) * MLA_V, :].astype(BF16)


def _mla_prep(za, rc, rs, qg, kg, wqa, wqb, wka, wkb, wv, tt):
    b, s, _ = za.shape
    hspec = pl.BlockSpec((1, MLA_HEADS, tt, MLA_HP), lambda i, t: (i, 0, t, 0))
    tspec = pl.BlockSpec((1, tt, MLA_HP), lambda i, t: (i, t, 0))
    hshape = jax.ShapeDtypeStruct((b, MLA_HEADS, s, MLA_HP), BF16)
    vspec = pl.BlockSpec((1, MLA_HEADS, MLA_V, tt), lambda i, t: (i, 0, 0, t))
    vshape = jax.ShapeDtypeStruct((b, MLA_HEADS, MLA_V, s), BF16)
    return pl.pallas_call(
        _mla_prep_kernel,
        grid=(b, s // tt),
        in_specs=[pl.BlockSpec((1, tt, ZA_W), lambda i, t: (i, t, 0)), tspec, tspec,
                  _const_spec(qg.shape), _const_spec(kg.shape), _const_spec(wqa.shape), _const_spec(wqb.shape),
                  _const_spec(wka.shape), _const_spec(wkb.shape), _const_spec(wv.shape)],
        out_specs=[hspec, hspec, vspec],
        out_shape=[hshape, hshape, vshape],
        compiler_params=_cparams(("parallel", "parallel")),
        name="mla_prep",
    )(za, rc, rs, qg, kg, wqa, wqb, wka, wkb, wv)


def _mla_attn_kernel(qi_ref, ki_ref, q_ref, k_ref, v_ref, y_ref, m_ref, l_ref, acc_ref, *, tq):
    qi = qi_ref[pl.program_id(1)]
    ki = ki_ref[pl.program_id(1)]

    @pl.when(ki == 0)
    def _():
        m_ref[...] = jnp.full_like(m_ref, -jnp.inf)
        l_ref[...] = jnp.zeros_like(l_ref)
        acc_ref[...] = jnp.zeros_like(acc_ref)

    def step(masked):
        for h in range(MLA_HEADS):
            s = lax.dot_general(k_ref[0, h], q_ref[0, h], (((1,), (1,)), ((), ())), preferred_element_type=F32)
            if masked:
                s = jnp.where(_iota((tq, tq), 0) <= _iota((tq, tq), 1), s, -jnp.inf)
            m_prev = m_ref[h]
            m_new = jnp.maximum(m_prev, jnp.max(s, axis=0, keepdims=True))
            alpha = jnp.exp2(m_prev - m_new)
            p = jnp.exp2(s - m_new)
            l_ref[h] = alpha * l_ref[h] + jnp.sum(p, axis=0, keepdims=True)
            m_ref[h] = m_new
            pv = jnp.dot(v_ref[0, h], p.astype(BF16), preferred_element_type=F32)
            acc_ref[h] = alpha * acc_ref[h] + pv

    @pl.when(ki < qi)
    def _():
        step(False)

    @pl.when(ki == qi)
    def _():
        step(True)
        ot = jnp.concatenate([acc_ref[h] * (1.0 / l_ref[h]) for h in range(MLA_HEADS)], axis=0)
        y_ref[0] = ot.T


def _mla_attn(q, k, v, tq):
    b, _, s, _ = q.shape
    nq = s // tq
    pairs = [(qi, ki) for qi in range(nq) for ki in range(qi + 1)]
    qi_tab = jnp.asarray([p[0] for p in pairs], jnp.int32)
    ki_tab = jnp.asarray([p[1] for p in pairs], jnp.int32)
    grid_spec = pltpu.PrefetchScalarGridSpec(
        num_scalar_prefetch=2,
        grid=(b, len(pairs)),
        in_specs=[pl.BlockSpec((1, MLA_HEADS, tq, MLA_HP), lambda i, p, qt, kt: (i, 0, qt[p], 0)),
                  pl.BlockSpec((1, MLA_HEADS, tq, MLA_HP), lambda i, p, qt, kt: (i, 0, kt[p], 0)),
                  pl.BlockSpec((1, MLA_HEADS, MLA_V, tq), lambda i, p, qt, kt: (i, 0, 0, kt[p]))],
        out_specs=pl.BlockSpec((1, tq, GROUP_W), lambda i, p, qt, kt: (i, qt[p], 0)),
        scratch_shapes=[pltpu.VMEM((MLA_HEADS, 1, tq), F32), pltpu.VMEM((MLA_HEADS, 1, tq), F32),
                        pltpu.VMEM((MLA_HEADS, MLA_V, tq), F32)],
    )
    return pl.pallas_call(
        functools.partial(_mla_attn_kernel, tq=tq),
        grid_spec=grid_spec,
        out_shape=jax.ShapeDtypeStruct((b, s, GROUP_W), F32),
        compiler_params=_cparams(("parallel", "arbitrary")),
        name="mla_attn",
    )(qi_tab, ki_tab, q, k, v)


def _outproj_kernel(*refs, routed):
    if routed:
        ym, ya, yg, yc, h_ref, fg_ref, wo_ref, n2_ref, rt_ref, hn_ref, u_ref, route_ref = refs
    else:
        ym, ya, yg, yc, h_ref, fg_ref, wo_ref, n2_ref, hn_ref, u_ref = refs
    parts = []
    for gi, y_ref in enumerate((ym, ya, yg, yc)):
        y = y_ref[...]
        yn = y * lax.rsqrt(jnp.mean(y * y, axis=-1, keepdims=True) + EPS)
        parts.append((yn * fg_ref[:, gi * GROUP_W:(gi + 1) * GROUP_W]).astype(BF16))
    ycat = jnp.concatenate(parts, axis=1)
    hn = h_ref[...] + jnp.dot(ycat, wo_ref[...], preferred_element_type=F32)
    hn_ref[...] = hn
    u = hn * lax.rsqrt(jnp.mean(hn * hn, axis=-1, keepdims=True) + EPS) * n2_ref[...]
    u_ref[...] = u.astype(u_ref.dtype)
    if routed:
        tm = u.shape[0]
        lane = _iota((tm, LANES), 1)
        u_hi = u.astype(BF16)
        u_lo = (u - u_hi.astype(F32)).astype(BF16)
        rt = rt_ref[...]
        r_hi = rt.astype(BF16)
        r_lo = (rt - r_hi.astype(F32)).astype(BF16)
        d = lambda a, b: jnp.dot(a, b, preferred_element_type=F32)
        logits = d(u_hi, r_hi) + (d(u_lo, r_hi) + d(u_hi, r_lo))
        logits = jnp.where(lane < N_EXPERTS, logits, -jnp.inf)
        m1 = jnp.max(logits, axis=-1, keepdims=True)
        i1 = jnp.min(jnp.where(logits == m1, lane, LANES), axis=-1, keepdims=True)
        rest = jnp.where(lane == i1, -jnp.inf, logits)
        m2 = jnp.max(rest, axis=-1, keepdims=True)
        i2 = jnp.min(jnp.where(rest == m2, lane, LANES), axis=-1, keepdims=True)
        e2 = jnp.exp(m2 - m1)
        w1 = 1.0 / (1.0 + e2)
        w2 = e2 / (1.0 + e2)
        route_ref[...] = jnp.where(lane == 0, i1.astype(F32),
                                   jnp.where(lane == 1, i2.astype(F32),
                                             jnp.where(lane == 2, w1, jnp.where(lane == 3, w2, 0.0))))


def _outproj(ym, ya, yg, yc, h, fg, wo, n2, router, tm):
    n = h.shape[0]
    routed = router is not None
    row = lambda w: pl.BlockSpec((tm, w), lambda i: (i, 0))
    in_specs = [row(GROUP_W)] * 4 + [row(D_MODEL), _const_spec(fg.shape), _const_spec(wo.shape), _const_spec(n2.shape)]
    args = [ym, ya, yg, yc, h, fg, wo, n2]
    out_specs = [row(D_MODEL), row(D_MODEL)]
    out_shape = [jax.ShapeDtypeStruct((n, D_MODEL), F32), jax.ShapeDtypeStruct((n, D_MODEL), F32 if routed else BF16)]
    if routed:
        in_specs.append(_const_spec(router.shape))
        args.append(router)
        out_specs.append(row(LANES))
        out_shape.append(jax.ShapeDtypeStruct((n, LANES), F32))
    return pl.pallas_call(
        functools.partial(_outproj_kernel, routed=routed),
        grid=(n // tm,),
        in_specs=in_specs, out_specs=out_specs, out_shape=out_shape,
        compiler_params=_cparams(("parallel",)),
        name="outproj_routed" if routed else "outproj",
    )(*args)


def _swiglu(x, wg, wu, wd):
    g = jnp.dot(x, wg, preferred_element_type=F32)
    up = jnp.dot(x, wu, preferred_element_type=F32)
    a = (g * _sigmoid(g) * up).astype(BF16)
    return jnp.dot(a, wd, preferred_element_type=F32)


def _ffn_kernel(u_ref, h_ref, wg_ref, wu_ref, wd_ref, o_ref):
    o_ref[...] = h_ref[...] + _swiglu(u_ref[...], wg_ref[...], wu_ref[...], wd_ref[...])


def _ffn(u, h, wg, wu, wd, tm):
    n = h.shape[0]
    row = pl.BlockSpec((tm, D_MODEL), lambda i: (i, 0))
    return pl.pallas_call(
        _ffn_kernel,
        grid=(n // tm,),
        in_specs=[row, row, _const_spec(wg.shape), _const_spec(wu.shape), _const_spec(wd.shape)],
        out_specs=row,
        out_shape=jax.ShapeDtypeStruct((n, D_MODEL), F32),
        compiler_params=_cparams(("parallel",)),
        name="ffn_dense",
    )(u, h, wg, wu, wd)


def _moe_kernel(be_ref, tokc_ref, tokn_ref, dstp_ref, dstc_ref, u_hbm, wg_ref, wu_ref, wd_ref, y_hbm,
                xbuf, ybuf, gsem, ssem):
    i = pl.program_id(0)
    nb = pl.num_programs(0)
    bm = xbuf.shape[1]

    def gather_wait(x, s):
        pltpu.make_async_copy(u_hbm.at[pl.ds(0, bm), :], x, gsem.at[s]).wait()

    def scatter_wait(y, s):
        pltpu.make_async_copy(y, y_hbm.at[pl.ds(0, bm), :], ssem.at[s]).wait()

    def scatter_row(y, dst_ref, r, s):
        pltpu.make_async_copy(y.at[pl.ds(r, 1), :], y_hbm.at[pl.ds(dst_ref[0, 0, r], 1), :], ssem.at[s]).start()

    @pl.when(i == 0)
    def _():
        def body(r, c):
            pltpu.make_async_copy(u_hbm.at[pl.ds(tokc_ref[0, 0, r], 1), :], xbuf.at[0, pl.ds(r, 1), :],
                                  gsem.at[0]).start()
            return c
        lax.fori_loop(0, bm, body, 0, unroll=8)
        ybuf[2] = jnp.zeros((bm, D_MODEL), F32)

    c = i % 2
    n = 1 - c
    yc_i = i % 3
    yp_i = (i + 2) % 3
    xc, xn, yc, yp = xbuf.at[c], xbuf.at[n], ybuf.at[yc_i], ybuf.at[yp_i]
    gather_wait(xc, c)

    @pl.when(i >= 2)
    def _():
        scatter_wait(yc, yc_i)

    xb = xc[...].astype(BF16)
    for r in range(bm):
        pltpu.make_async_copy(u_hbm.at[pl.ds(tokn_ref[0, 0, r], 1), :], xn.at[pl.ds(r, 1), :], gsem.at[n]).start()
    for r in range(bm):
        scatter_row(yp, dstp_ref, r, yp_i)
    yc[...] = _swiglu(xb, wg_ref[0], wu_ref[0], wd_ref[0])

    @pl.when(i == nb - 1)
    def _():
        def body(r, carry):
            scatter_row(yc, dstc_ref, r, yc_i)
            return carry
        lax.fori_loop(0, bm, body, 0, unroll=8)
        scatter_wait(yc, (i + 1) % 3)
        scatter_wait(yp, yp_i)
        scatter_wait(yc, yc_i)
        gather_wait(xn, n)


def _moe(block_e, row_tok, row_dst, u, wg, wu, wd, n_rows):
    n_blocks = block_e.shape[0]
    bm = row_tok.shape[-1]
    wspec = lambda shp: pl.BlockSpec((1,) + shp, lambda i, be: (be[i], 0, 0))
    ispec = lambda f: pl.BlockSpec((1, 1, bm), f, memory_space=pltpu.SMEM)
    grid_spec = pltpu.PrefetchScalarGridSpec(
        num_scalar_prefetch=1,
        grid=(n_blocks,),
        in_specs=[ispec(lambda i, be: (i, 0, 0)),
                  ispec(lambda i, be: (jnp.minimum(i + 1, n_blocks - 1), 0, 0)),
                  ispec(lambda i, be: (i, 0, 0)),
                  ispec(lambda i, be: (i + 1, 0, 0)),
                  pl.BlockSpec(memory_space=pl.ANY),
                  wspec((D_MODEL, D_FF)), wspec((D_MODEL, D_FF)), wspec((D_FF, D_MODEL))],
        out_specs=pl.BlockSpec(memory_space=pl.ANY),
        scratch_shapes=[pltpu.VMEM((2, bm, D_MODEL), F32), pltpu.VMEM((3, bm, D_MODEL), F32),
                        pltpu.SemaphoreType.DMA((2,)), pltpu.SemaphoreType.DMA((3,))],
    )
    return pl.pallas_call(
        _moe_kernel,
        grid_spec=grid_spec,
        out_shape=jax.ShapeDtypeStruct((n_rows, D_MODEL), F32),
        compiler_params=_cparams(("arbitrary",)),
        name="moe_experts",
    )(block_e, row_tok, row_tok, row_dst, row_dst, u, wg, wu, wd)


def _combine_kernel(ya_ref, yb_ref, h_ref, route_ref, fg_ref, o_ref, *, final):
    w1 = route_ref[:, 2:3]
    w2 = route_ref[:, 3:4]
    hn = h_ref[...] + w1 * ya_ref[...] + w2 * yb_ref[...]
    if final:
        hn = hn * lax.rsqrt(jnp.mean(hn * hn, axis=-1, keepdims=True) + EPS) * fg_ref[...]
    o_ref[...] = hn


def _combine(y2, h, route, fg, tm, final):
    n = h.shape[0]
    return pl.pallas_call(
        functools.partial(_combine_kernel, final=final),
        grid=(n // tm,),
        in_specs=[pl.BlockSpec((tm, D_MODEL), lambda i: (i, 0)),
                  pl.BlockSpec((tm, D_MODEL), lambda i: (i + n // tm, 0)),
                  pl.BlockSpec((tm, D_MODEL), lambda i: (i, 0)),
                  pl.BlockSpec((tm, LANES), lambda i: (i, 0)),
                  _const_spec(fg.shape)],
        out_specs=pl.BlockSpec((tm, D_MODEL), lambda i: (i, 0)),
        out_shape=jax.ShapeDtypeStruct((n, D_MODEL), F32),
        compiler_params=_cparams(("parallel",)),
        name="moe_combine",
    )(y2, y2, h, route, fg)


def _rmsnorm_kernel(h_ref, g_ref, o_ref):
    x = h_ref[...]
    o_ref[...] = x * lax.rsqrt(jnp.mean(x * x, axis=-1, keepdims=True) + EPS) * g_ref[...]


def _final_norm(h, g, tm):
    n = h.shape[0]
    row = pl.BlockSpec((tm, D_MODEL), lambda i: (i, 0))
    return pl.pallas_call(
        _rmsnorm_kernel, grid=(n // tm,), in_specs=[row, _const_spec(g.shape)], out_specs=row,
        out_shape=jax.ShapeDtypeStruct((n, D_MODEL), F32),
        compiler_params=_cparams(("parallel",)), name="final_norm",
    )(h, g)


def _pad_cols(w, width):
    return jnp.pad(w, ((0, 0), (0, width - w.shape[1])))


def _pad_rows(w, height):
    return jnp.pad(w, ((0, height - w.shape[0]), (0, 0)))


def _inproj_weights(w):
    o = np.cumsum([0, 512, 256, 256, 8, 192, 128, 32, 128, 128, 256, 256, 16, 512])
    seg = lambda i: w[:, o[i]:o[i + 1]]
    gates = seg(3)
    wm = jnp.concatenate([seg(0), seg(1), seg(2), _pad_cols(gates[:, :4], LANES), _pad_cols(gates[:, 4:], LANES)], axis=1)
    wa = jnp.concatenate([_pad_cols(seg(4), 256), seg(5), _pad_cols(seg(6), LANES)], axis=1)
    wg = jnp.concatenate([seg(7), seg(8), seg(9), seg(10), _pad_cols(seg(11), LANES)], axis=1)
    wc = seg(12)
    return [x.astype(BF16) for x in (wm, wa, wg, wc)]


def _mla_weights(w_uq, w_ukv):
    half = MLA_ROPE // 2
    zq = jnp.zeros((MLA_Q_RANK, half), F32)
    qa, qb, ka, kb, wv = [], [], [], [], []
    eye = jnp.eye(LANES, dtype=F32)[:, :MLA_ROPE]
    zk = jnp.zeros((MLA_KV_RANK, MLA_HP), F32)
    for h in range(MLA_HEADS):
        wq = w_uq[:, h * 96:(h + 1) * 96]
        nope, r1, r2 = wq[:, :64], wq[:, 64:64 + half], wq[:, 64 + half:]
        qa.append(_pad_cols(jnp.concatenate([nope, r1, r2], axis=1), MLA_HP))
        qb.append(_pad_cols(jnp.concatenate([jnp.zeros_like(nope), -r2, r1], axis=1), MLA_HP))
        wkv = w_ukv[:, h * 128:(h + 1) * 128]
        kn, vh = wkv[:, :64], wkv[:, 64:]
        e1, e2 = eye[:, :half], eye[:, half:]
        z64 = jnp.zeros((LANES, 64), F32)
        ka.append(jnp.concatenate([_pad_cols(kn, MLA_HP), _pad_cols(jnp.concatenate([z64, e1, e2], axis=1), MLA_HP)], axis=0))
        kb.append(jnp.concatenate([zk, _pad_cols(jnp.concatenate([z64, -e2, e1], axis=1), MLA_HP)], axis=0))
        wv.append(vh.T)
    cat = lambda xs, rows: _pad_rows(jnp.concatenate(xs, axis=1), rows).astype(BF16)
    wvt = _pad_cols(jnp.concatenate(wv, axis=0), 256).astype(BF16)
    return cat(qa, 256), cat(qb, 256), cat(ka, 256), cat(kb, 256), wvt


def _rope_tables(positions):
    half = MLA_ROPE // 2
    inv_freq = 1.0 / (ROPE_BASE ** (jnp.arange(0, MLA_ROPE, 2, dtype=F32) / MLA_ROPE))
    ang = positions.astype(F32)[..., None] * inv_freq
    cos, sin = jnp.cos(ang), jnp.sin(ang)
    shp = cos.shape[:-1]
    rc = jnp.concatenate([jnp.ones(shp + (64,), F32), cos, cos, jnp.zeros(shp + (MLA_HP - 64 - 2 * half,), F32)], axis=-1)
    rs = jnp.concatenate([jnp.zeros(shp + (64,), F32), sin, sin, jnp.zeros(shp + (MLA_HP - 64 - 2 * half,), F32)], axis=-1)
    return rc, rs


def _routing_tables(route, bm):
    n = route.shape[0]
    a = n * TOP_K
    flat_e = route[:, :TOP_K].astype(jnp.int32).reshape(-1)
    onehot = (flat_e[:, None] == jnp.arange(N_EXPERTS, dtype=jnp.int32)[None, :]).astype(jnp.int32)
    csum = jnp.cumsum(onehot, axis=0)
    rank = jnp.sum(onehot * csum, axis=1) - 1
    counts = csum[-1]
    padded = (counts + bm - 1) // bm * bm
    padded_end = jnp.cumsum(padded)
    padded_start = padded_end - padded
    dest = padded_start[flat_e] + rank
    n_blocks = -(-a // bm) + N_EXPERTS
    rows = n_blocks * bm
    row_asg = jnp.full((rows,), -1, jnp.int32).at[dest].set(jnp.arange(a, dtype=jnp.int32))
    row_tok = jnp.maximum(row_asg, 0) // TOP_K
    pad_rank = jnp.cumsum((row_asg < 0).astype(jnp.int32)) - 1
    slot_row = (row_asg % TOP_K) * n + row_asg // TOP_K
    row_dst = jnp.where(row_asg >= 0, slot_row, a + pad_rank)
    first = a + (rows - a) + jnp.arange(bm, dtype=jnp.int32)
    row_dst = jnp.concatenate([first, row_dst]).reshape(n_blocks + 1, 1, bm)
    block_start = jnp.arange(n_blocks, dtype=jnp.int32) * bm
    block_e = jnp.sum((padded_end[None, :] <= block_start[:, None]).astype(jnp.int32), axis=1)
    block_e = jnp.minimum(block_e, N_EXPERTS - 1).astype(jnp.int32)
    return block_e, row_tok.reshape(n_blocks, 1, bm), row_dst, rows + bm


def kernel(x, positions, norm1_g, w_in, mlstm_conv_w, mlstm_conv_b, mlstm_gate_b, mla_q_norm_g, mla_w_uq,
           mla_kv_norm_g, mla_w_ukv, gla_w_alpha, gla_b_alpha, conv_w, conv_b, conv_ln_g, conv_ln_b, fuse_g,
           w_out, norm2_g, ffn_w_gate, ffn_w_up, ffn_w_down, moe_router, moe_w_gate, moe_w_up, moe_w_down,
           final_norm_g):
    bsz, seq, _ = x.shape
    n = bsz * seq
    depth = w_in.shape[0]
    tm = min(512, n)
    tt = min(512, seq)
    tq = min(512, seq)
    nct = seq // CHUNK
    row2 = lambda v: v.reshape(1, -1).astype(F32)

    rc, rs = _rope_tables(positions)
    h = x.reshape(n, D_MODEL)
    for l in range(depth):
        wm, wa, wg, wc = _inproj_weights(w_in[l])
        zm, za, zg, zc = _inproj(h, row2(norm1_g[l]), wm, wa, wg, wc, tm)
        zm = zm.reshape(bsz, seq, ZM_W)

        gates = jnp.concatenate([zm[:, :, 1024:1028], zm[:, :, 1152:1156]], axis=-1)
        grow = gates.reshape(bsz, nct, CHUNK, 2, MLSTM_HEADS).transpose(0, 3, 1, 4, 2).reshape(bsz, 2, nct, GROUP_W)
        gb = mlstm_gate_b[l].astype(F32)
        gbc = _pad_cols(gb.reshape(2, MLSTM_HEADS), LANES)
        gbr = jnp.repeat(gb.reshape(2, MLSTM_HEADS), CHUNK, axis=1)
        ym = _mlstm(zm, grow, mlstm_conv_w[l], row2(mlstm_conv_b[l]), gbc, gbr, tt)

        wqa, wqb, wka, wkb, wv = _mla_weights(mla_w_uq[l], mla_w_ukv[l])
        qh, kh, vh = _mla_prep(za.reshape(bsz, seq, ZA_W), rc, rs, _pad_cols(row2(mla_q_norm_g[l]), 256),
                               row2(mla_kv_norm_g[l]), wqa, wqb, wka, wkb, wv, tt)
        ya = _mla_attn(qh, kh, vh, tq)

        yg = _gla(zg.reshape(bsz, seq, ZG_W), _pad_rows(gla_w_alpha[l], LANES), row2(gla_b_alpha[l]), tt)
        yc = _conformer(zc.reshape(bsz, seq, ZC_W), conv_w[l], row2(conv_b[l]), row2(conv_ln_g[l]),
                        row2(conv_ln_b[l]), tt)

        flat = lambda y: y.reshape(n, GROUP_W)
        j = l // 2
        if l % 2 == 0:
            h, u = _outproj(flat(ym), flat(ya), flat(yg), flat(yc), h, row2(fuse_g[l]), w_out[l].astype(BF16),
                            row2(norm2_g[l]), None, tm)
            h = _ffn(u, h, ffn_w_gate[j].astype(BF16), ffn_w_up[j].astype(BF16), ffn_w_down[j].astype(BF16), tm)
            if l == depth - 1:
                h = _final_norm(h, row2(final_norm_g), tm)
        else:
            h, u, route = _outproj(flat(ym), flat(ya), flat(yg), flat(yc), h, row2(fuse_g[l]),
                                   w_out[l].astype(BF16), row2(norm2_g[l]), _pad_cols(moe_router[j], LANES), tm)
            block_e, row_tok, row_dst, n_rows = _routing_tables(route, MOE_BM)
            y2 = _moe(block_e, row_tok, row_dst, u, moe_w_gate[j].astype(BF16), moe_w_up[j].astype(BF16),
                      moe_w_down[j].astype(BF16), n_rows)
            h = _combine(y2, h, route, row2(final_norm_g), tm, l == depth - 1)
    return h.reshape(bsz, seq, D_MODEL)
```

```python
import functools

import jax
import jax.numpy as jnp
import numpy as np
from jax import lax
from jax.experimental import pallas as pl
from jax.experimental.pallas import tpu as pltpu

F32 = jnp.float32
BF16 = jnp.bfloat16
HI = lax.Precision.HIGHEST

D_MODEL = 1024
GROUP_W = 256
EPS = 1e-6
LOG2E = 1.4426950408889634
LANES = 128
CHUNK = 64
MLSTM_HEADS = 4
MLSTM_CONV = 4
MLA_HEADS = 4
MLA_NOPE = 64
MLA_ROPE = 32
MLA_V = 64
MLA_Q_RANK = 192
MLA_KV_RANK = 128
ROPE_BASE = 10000.0
GLA_HEADS = 4
GLA_DK = 32
GLA_DV = 64
GLA_GATE_RANK = 16
GLA_TAU = 16.0
CONV_WIDTH = 31
D_FF = 2816
N_EXPERTS = 8
TOP_K = 2
MOE_BM = 256
VMEM_LIMIT = 56 * 1024 * 1024

ZM_W = 1280
ZA_W = 512
ZG_W = 896
ZC_W = 512


def _cparams(sem):
    return pltpu.CompilerParams(dimension_semantics=sem, vmem_limit_bytes=VMEM_LIMIT)


def _const_spec(shape):
    nd = len(shape)
    return pl.BlockSpec(shape, lambda *_: (0,) * nd, pipeline_mode=pl.Buffered(1))


def _sigmoid(x):
    return 1.0 / (1.0 + jnp.exp(-x))


def _log_sigmoid(x):
    return jnp.minimum(x, 0.0) - jnp.log(1.0 + jnp.exp(-jnp.abs(x)))


def _iota(shape, dim):
    return lax.broadcasted_iota(jnp.int32, shape, dim)


def _tri(n):
    return (_iota((n, n), 0) >= _iota((n, n), 1)).astype(BF16)


def _split3(x):
    hi = x.astype(BF16)
    r1 = x - hi.astype(F32)
    mid = r1.astype(BF16)
    lo = (r1 - mid.astype(F32)).astype(BF16)
    return hi, mid, lo


def _sel_dot(sel, x):
    hi, mid, lo = _split3(x)
    d = lambda p: jnp.dot(sel, p, preferred_element_type=F32)
    return d(hi) + d(mid) + d(lo)


def _dot_sel(x, sel):
    hi, mid, lo = _split3(x)
    d = lambda p: jnp.dot(p, sel, preferred_element_type=F32)
    return d(hi) + d(mid) + d(lo)


def _inproj_kernel(h_ref, g_ref, wm_ref, wa_ref, wg_ref, wc_ref, zm_ref, za_ref, zg_ref, zc_ref):
    x = h_ref[...]
    ms = jnp.mean(x * x, axis=-1, keepdims=True)
    xn = (x * lax.rsqrt(ms + EPS) * g_ref[...]).astype(BF16)
    zm_ref[...] = jnp.dot(xn, wm_ref[...], preferred_element_type=F32)
    za_ref[...] = jnp.dot(xn, wa_ref[...], preferred_element_type=F32)
    zg_ref[...] = jnp.dot(xn, wg_ref[...], preferred_element_type=F32)
    zc_ref[...] = jnp.dot(xn, wc_ref[...], preferred_element_type=F32)


def _inproj(h, g, wm, wa, wg, wc, tm):
    n = h.shape[0]
    row = lambda w: pl.BlockSpec((tm, w), lambda i: (i, 0))
    return pl.pallas_call(
        _inproj_kernel,
        grid=(n // tm,),
        in_specs=[row(D_MODEL), _const_spec((1, D_MODEL)), _const_spec(wm.shape), _const_spec(wa.shape),
                  _const_spec(wg.shape), _const_spec(wc.shape)],
        out_specs=[row(ZM_W), row(ZA_W), row(ZG_W), row(ZC_W)],
        out_shape=[jax.ShapeDtypeStruct((n, w), F32) for w in (ZM_W, ZA_W, ZG_W, ZC_W)],
        compiler_params=_cparams(("parallel",)),
        name="inproj",
    )(h, g, wm, wa, wg, wc)


def _expand_heads(x, width):
    r = x.shape[0]
    lane = _iota((r, 4 * width), 1)
    out = jnp.broadcast_to(x[:, 3:4], (r, 4 * width))
    for h in (2, 1, 0):
        out = jnp.where(lane < (h + 1) * width, jnp.broadcast_to(x[:, h:h + 1], (r, 4 * width)), out)
    return out


def _mlstm_kernel(zm_ref, grow_ref, cw_ref, cb_ref, gbc_ref, gbr_ref, y_ref,
                  xpad_ref, q_ref, k_ref, grs_ref, cn_ref, m_ref, cnm_ref, trit_ref, *, tt):
    nct = tt // CHUNK
    L = CHUNK
    W = MLSTM_HEADS * 64

    @pl.when(pl.program_id(1) == 0)
    def _():
        xpad_ref[0:8, :] = jnp.zeros((8, 2 * W), F32)
        cn_ref[...] = jnp.zeros_like(cn_ref)
        m_ref[...] = jnp.zeros_like(m_ref)
        hsame = (_iota((W, W), 0) // L) == (_iota((W, W), 1) // L)
        ncol = (_iota((W, LANES), 0) // L) == _iota((W, LANES), 1)
        cnm_ref[...] = jnp.concatenate([hsame, ncol], axis=1).astype(F32)
        trit_ref[...] = (hsame & ((_iota((W, W), 0) % L) <= (_iota((W, W), 1) % L))).astype(BF16)

    xpad_ref[8:8 + tt, :] = zm_ref[0, :, 0:2 * W]
    for r in range(0, tt, L):
        acc = jnp.broadcast_to(cb_ref[...], (L, 2 * W))
        for j in range(MLSTM_CONV):
            acc = acc + cw_ref[j:j + 1, :] * xpad_ref[pl.ds(r + 8 - (MLSTM_CONV - 1) + j, L), :]
        qk = acc * _sigmoid(acc)
        q_ref[r:r + L, :] = (qk[:, :W] * (64 ** -0.5)).astype(BF16)
        k_ref[r:r + L, :] = qk[:, W:]
    xpad_ref[0:8, :] = xpad_ref[tt:tt + 8, :]

    i_row = grow_ref[0, 0] + gbr_ref[0:1, :]
    lf_row = _log_sigmoid(grow_ref[0, 1] + gbr_ref[1:2, :])
    grs_ref[...] = i_row - _dot_sel(lf_row, trit_ref[...])

    tri = _tri(L)
    row_t = _iota((L, W), 0)
    lane_j = _iota((L, W), 1) % L
    causal = lane_j <= row_t
    ones_col = (_iota((L, LANES), 1) < MLSTM_HEADS).astype(F32)
    row_l = _iota((L, LANES), 0)

    def chunk(c, carry):
        r0 = pl.multiple_of(c * L, L)
        ic = zm_ref[0, pl.ds(r0, L), 4 * W:4 * W + LANES] + gbc_ref[0:1, :]
        lf = _log_sigmoid(zm_ref[0, pl.ds(r0, L), 4 * W + LANES:4 * W + 2 * LANES] + gbc_ref[1:2, :])
        b = _sel_dot(tri, lf)
        g = ic - b
        cm = g
        s = 1
        while s < L:
            cm = jnp.maximum(cm, jnp.where(row_l >= s, pltpu.roll(cm, s, 0), -jnp.inf))
            s *= 2
        m_prev = m_ref[...]
        mx = jnp.maximum(m_prev, cm)
        w_inter = jnp.exp(m_prev - mx)
        b_end = b[L - 1:L, :]
        mx_end = mx[L - 1:L, :]
        m_new = b_end + mx_end
        decay = jnp.exp(m_prev - mx_end)
        kw_col = jnp.exp(g - mx_end)

        qc = q_ref[pl.ds(r0, L), :]
        kc = k_ref[pl.ds(r0, L), :]
        vc = zm_ref[0, pl.ds(r0, L), 2 * W:3 * W]
        oc = zm_ref[0, pl.ds(r0, L), 3 * W:4 * W]

        g_row = grs_ref[pl.ds(c, 1), :]
        dmat = jnp.where(causal, jnp.exp(jnp.where(causal, g_row - _expand_heads(mx, L), 0.0)), 0.0)
        kbd = (jnp.concatenate([kc] * 4, axis=0) * cnm_ref[:, 0:W]).astype(BF16)
        s_qk = lax.dot_general(qc, kbd, (((1,), (1,)), ((), ())), preferred_element_type=F32)
        s_w = (s_qk * dmat).astype(BF16)
        vaug = jnp.concatenate([vc, ones_col], axis=1)
        vbd = (jnp.concatenate([vaug] * 4, axis=0) * cnm_ref[...]).astype(BF16)
        intra = jnp.dot(s_w, vbd, preferred_element_type=F32)
        inter = jnp.dot(qc, cn_ref[...].astype(BF16), preferred_element_type=F32)
        num = _expand_heads(w_inter, L) * inter[:, :W] + intra[:, :W]
        den = w_inter * inter[:, W:] + intra[:, W:]
        hden = jnp.maximum(jnp.abs(den), jnp.exp(-(b + mx)))
        hout = num * _expand_heads(1.0 / hden, L)
        y_ref[0, pl.ds(r0, L), :] = _sigmoid(oc) * hout

        kw = (_expand_heads(kw_col, L) * kc).astype(BF16)
        upd = lax.dot_general(kw, vaug.astype(BF16), (((0,), (0,)), ((), ())), preferred_element_type=F32)
        decay_x = jnp.concatenate([_expand_heads(decay, L), decay], axis=1)
        cn_ref[...] = decay_x * cn_ref[...] + upd * cnm_ref[...]
        m_ref[...] = m_new
        return carry

    lax.fori_loop(0, nct, chunk, 0, unroll=2)


def _mlstm(zm, grow, cw, cb, gbc, gbr, tt):
    b, s, _ = zm.shape
    nct = tt // CHUNK
    w2 = 2 * GROUP_W
    return pl.pallas_call(
        functools.partial(_mlstm_kernel, tt=tt),
        grid=(b, s // tt),
        in_specs=[pl.BlockSpec((1, tt, ZM_W), lambda i, t: (i, t, 0)),
                  pl.BlockSpec((1, 2, nct, GROUP_W), lambda i, t: (i, 0, t, 0)),
                  _const_spec(cw.shape), _const_spec(cb.shape), _const_spec(gbc.shape), _const_spec(gbr.shape)],
        out_specs=pl.BlockSpec((1, tt, GROUP_W), lambda i, t: (i, t, 0)),
        out_shape=jax.ShapeDtypeStruct((b, s, GROUP_W), F32),
        scratch_shapes=[pltpu.VMEM((tt + 8, w2), F32), pltpu.VMEM((tt, GROUP_W), BF16),
                        pltpu.VMEM((tt, GROUP_W), F32), pltpu.VMEM((nct, GROUP_W), F32),
                        pltpu.VMEM((GROUP_W, GROUP_W + LANES), F32), pltpu.VMEM((1, LANES), F32),
                        pltpu.VMEM((GROUP_W, GROUP_W + LANES), F32), pltpu.VMEM((GROUP_W, GROUP_W), BF16)],
        compiler_params=_cparams(("parallel", "arbitrary")),
        name="mlstm",
    )(zm, grow, cw, cb, gbc, gbr)


def _gla_kernel(zg_ref, wa_ref, ba_ref, y_ref, q_ref, k_ref, g_ref, o_ref, st_ref, *, tt):
    nct = tt // CHUNK
    L = CHUNK
    WK = GLA_HEADS * GLA_DK
    WV = GLA_HEADS * GLA_DV

    @pl.when(pl.program_id(1) == 0)
    def _():
        st_ref[...] = jnp.zeros_like(st_ref)

    q_ref[...] = zg_ref[0, :, 0:WK] * (GLA_DK ** -0.5)
    k_ref[...] = zg_ref[0, :, WK:2 * WK]
    a = zg_ref[0, :, 2 * WK + 2 * WV:2 * WK + 2 * WV + LANES]
    pre = jnp.dot(a, wa_ref[...], precision=HI, preferred_element_type=F32) + ba_ref[...]
    g_ref[...] = _log_sigmoid(pre) * (1.0 / GLA_TAU)

    tri = _tri(L)
    hs2 = ((_iota((WK, WV), 0) // GLA_DK) == (_iota((WK, WV), 1) // GLA_DV)).astype(BF16)
    st_mask = (_iota((WV, WK), 0) // GLA_DV) == (_iota((WV, WK), 1) // GLA_DK)

    def chunk(c, carry):
        r0 = pl.multiple_of(c * L, L)
        qc = q_ref[pl.ds(r0, L), :]
        kc = k_ref[pl.ds(r0, L), :]
        vc = zg_ref[0, pl.ds(r0, L), 2 * WK:2 * WK + WV]
        b = _sel_dot(tri, g_ref[pl.ds(r0, L), :])
        st = st_ref[...]
        qd = (qc * jnp.exp(b)).astype(BF16)
        o = lax.dot_general(qd, st.astype(BF16), (((1,), (1,)), ((), ())), preferred_element_type=F32)

        pieces = []
        for jb in range(L // 8):
            t0 = 8 * jb
            rows = L - t0
            qs, bs = qc[t0:, :], b[t0:, :]
            tl = _iota((rows, WK), 0)
            ps = []
            for jj in range(8):
                j = t0 + jj
                e = jnp.exp(jnp.minimum(bs - b[j:j + 1, :], 0.0))
                ps.append(jnp.where(tl >= jj, qs * kc[j:j + 1, :] * e, 0.0).astype(BF16))
            rexp = jnp.dot(jnp.concatenate(ps, axis=0), hs2, preferred_element_type=F32)
            acc = rexp[0:rows, :] * vc[t0:t0 + 1, :]
            for jj in range(1, 8):
                acc = acc + rexp[jj * rows:(jj + 1) * rows, :] * vc[t0 + jj:t0 + jj + 1, :]
            pieces.append(acc)
        bands = []
        for band in range(L // 8):
            tot = None
            for jb in range(band + 1):
                lo = 8 * (band - jb)
                part = pieces[jb][lo:lo + 8, :]
                tot = part if tot is None else tot + part
            bands.append(tot)
        o = o + jnp.concatenate(bands, axis=0)
        o_ref[pl.ds(r0, L), :] = o

        b_end = b[L - 1:L, :]
        kd = (kc * jnp.exp(b_end - b)).astype(BF16)
        upd = lax.dot_general(vc.astype(BF16), kd, (((0,), (0,)), ((), ())), preferred_element_type=F32)
        st_ref[...] = st * jnp.exp(b_end) + jnp.where(st_mask, upd, 0.0)
        return carry

    lax.fori_loop(0, nct, chunk, 0, unroll=2)

    o = o_ref[...]
    hsame = ((_iota((WV, WV), 0) // GLA_DV) == (_iota((WV, WV), 1) // GLA_DV)).astype(BF16)
    ms = _dot_sel(o * o, hsame) * (1.0 / GLA_DV)
    r = zg_ref[0, :, 2 * WK + WV:2 * WK + 2 * WV]
    y_ref[0] = o * lax.rsqrt(ms + EPS) * (r * _sigmoid(r))


def _gla(zg, wa, ba, tt):
    b, s, _ = zg.shape
    return pl.pallas_call(
        functools.partial(_gla_kernel, tt=tt),
        grid=(b, s // tt),
        in_specs=[pl.BlockSpec((1, tt, ZG_W), lambda i, t: (i, t, 0)), _const_spec(wa.shape), _const_spec(ba.shape)],
        out_specs=pl.BlockSpec((1, tt, GROUP_W), lambda i, t: (i, t, 0)),
        out_shape=jax.ShapeDtypeStruct((b, s, GROUP_W), F32),
        scratch_shapes=[pltpu.VMEM((tt, LANES), F32), pltpu.VMEM((tt, LANES), F32), pltpu.VMEM((tt, LANES), F32),
                        pltpu.VMEM((tt, GROUP_W), F32), pltpu.VMEM((GROUP_W, LANES), F32)],
        compiler_params=_cparams(("parallel", "arbitrary")),
        name="gla",
    )(zg, wa, ba)


CONV_HIST = 32
CONV_SUB = 64


def _conv_kernel(zc_ref, cw_ref, cb_ref, lg_ref, lb_ref, y_ref, xpad_ref, xs_ref, *, tt):
    C = GROUP_W
    span = tt + CONV_HIST - 8

    @pl.when(pl.program_id(1) == 0)
    def _():
        xpad_ref[0:CONV_HIST, :] = jnp.zeros((CONV_HIST, C), F32)

    val = zc_ref[0, :, 0:C]
    gate = zc_ref[0, :, C:2 * C]
    xpad_ref[CONV_HIST:CONV_HIST + tt, :] = val * _sigmoid(gate)
    for sh in range(1, 8):
        xs_ref[sh - 1, 0:span, :] = xpad_ref[pl.ds(sh, span), :]
    sub = min(CONV_SUB, tt)
    for r in range(0, tt, sub):
        acc = jnp.broadcast_to(cb_ref[...], (sub, C))
        for j in range(CONV_WIDTH):
            off = CONV_HIST - (CONV_WIDTH - 1) + j
            sh, base = off % 8, r + off - off % 8
            src = xpad_ref[base:base + sub, :] if sh == 0 else xs_ref[sh - 1, base:base + sub, :]
            acc = acc + cw_ref[j:j + 1, :] * src
        mu = jnp.mean(acc, axis=-1, keepdims=True)
        d = acc - mu
        var = jnp.mean(d * d, axis=-1, keepdims=True)
        yn = d * lax.rsqrt(var + EPS) * lg_ref[...] + lb_ref[...]
        y_ref[0, r:r + sub, :] = yn * _sigmoid(yn)
    xpad_ref[0:CONV_HIST, :] = xpad_ref[tt:tt + CONV_HIST, :]


def _conformer(zc, cw, cb, lg, lb, tt):
    b, s, _ = zc.shape
    return pl.pallas_call(
        functools.partial(_conv_kernel, tt=tt),
        grid=(b, s // tt),
        in_specs=[pl.BlockSpec((1, tt, ZC_W), lambda i, t: (i, t, 0)), _const_spec(cw.shape), _const_spec(cb.shape),
                  _const_spec(lg.shape), _const_spec(lb.shape)],
        out_specs=pl.BlockSpec((1, tt, GROUP_W), lambda i, t: (i, t, 0)),
        out_shape=jax.ShapeDtypeStruct((b, s, GROUP_W), F32),
        scratch_shapes=[pltpu.VMEM((tt + CONV_HIST, GROUP_W), F32), pltpu.VMEM((7, tt + CONV_HIST, GROUP_W), F32)],
        compiler_params=_cparams(("parallel", "arbitrary")),
        name="conformer",
    )(zc, cw, cb, lg, lb)


MLA_HP = 128


def _mla_prep_kernel(za_ref, rc_ref, rs_ref, qg_ref, kg_ref, wqa_ref, wqb_ref, wka_ref, wkb_ref, wv_ref,
                     q_ref, k_ref, v_ref):
    cq = za_ref[0, :, 0:256]
    ckv = za_ref[0, :, 256:384]
    kr = za_ref[0, :, 384:512]
    qn = (cq * lax.rsqrt(jnp.sum(cq * cq, axis=-1, keepdims=True) * (1.0 / MLA_Q_RANK) + EPS) * qg_ref[...]).astype(BF16)
    kvn = ckv * lax.rsqrt(jnp.mean(ckv * ckv, axis=-1, keepdims=True) + EPS) * kg_ref[...]
    kin = jnp.concatenate([kvn, kr], axis=1).astype(BF16)
    cos = rc_ref[0]
    sin = rs_ref[0]
    qa = jnp.dot(qn, wqa_ref[...], preferred_element_type=F32)
    qb = jnp.dot(qn, wqb_ref[...], preferred_element_type=F32)
    ka = jnp.dot(kin, wka_ref[...], preferred_element_type=F32)
    kb = jnp.dot(kin, wkb_ref[...], preferred_element_type=F32)
    vt = lax.dot_general(wv_ref[...], kin, (((1,), (1,)), ((), ())), preferred_element_type=F32)
    scale = (MLA_NOPE + MLA_ROPE) ** -0.5 * LOG2E
    for h in range(MLA_HEADS):
        sl = slice(h * MLA_HP, (h + 1) * MLA_HP)
        q_ref[0, h] = ((qa[:, sl] * cos + qb[:, sl] * sin) * scale).astype(BF16)
        k_ref[0, h] = (ka[:, sl] * cos + kb[:, sl] * sin).astype(BF16)
        v_ref[0, h] = vt[h * MLA_V:(h + 1) * MLA_V, :].astype(BF16)


def _mla_prep(za, rc, rs, qg, kg, wqa, wqb, wka, wkb, wv, tt):
    b, s, _ = za.shape
    hspec = pl.BlockSpec((1, MLA_HEADS, tt, MLA_HP), lambda i, t: (i, 0, t, 0))
    tspec = pl.BlockSpec((1, tt, MLA_HP), lambda i, t: (i, t, 0))
    hshape = jax.ShapeDtypeStruct((b, MLA_HEADS, s, MLA_HP), BF16)
    vspec = pl.BlockSpec((1, MLA_HEADS, MLA_V, tt), lambda i, t: (i, 0, 0, t))
    vshape = jax.ShapeDtypeStruct((b, MLA_HEADS, MLA_V, s), BF16)
    return pl.pallas_call(
        _mla_prep_kernel,
        grid=(b, s // tt),
        in_specs=[pl.BlockSpec((1, tt, ZA_W), lambda i, t: (i, t, 0)), tspec, tspec,
                  _const_spec(qg.shape), _const_spec(kg.shape), _const_spec(wqa.shape), _const_spec(wqb.shape),
                  _const_spec(wka.shape), _const_spec(wkb.shape), _const_spec(wv.shape)],
        out_specs=[hspec, hspec, vspec],
        out_shape=[hshape, hshape, vshape],
        compiler_params=_cparams(("parallel", "parallel")),
        name="mla_prep",
    )(za, rc, rs, qg, kg, wqa, wqb, wka, wkb, wv)


def _mla_attn_kernel(qi_ref, ki_ref, q_ref, k_ref, v_ref, y_ref, m_ref, l_ref, acc_ref, *, tq):
    qi = qi_ref[pl.program_id(1)]
    ki = ki_ref[pl.program_id(1)]

    @pl.when(ki == 0)
    def _():
        m_ref[...] = jnp.full_like(m_ref, -jnp.inf)
        l_ref[...] = jnp.zeros_like(l_ref)
        acc_ref[...] = jnp.zeros_like(acc_ref)

    def step(masked):
        def scores(h):
            s = lax.dot_general(k_ref[0, h], q_ref[0, h], (((1,), (1,)), ((), ())), preferred_element_type=F32)
            if masked:
                s = jnp.where(_iota((tq, tq), 0) <= _iota((tq, tq), 1), s, -jnp.inf)
            return s

        def softmax(h, s):
            m_prev = m_ref[h]
            m_new = jnp.maximum(m_prev, jnp.max(s, axis=0, keepdims=True))
            alpha = jnp.exp2(m_prev - m_new)
            p = jnp.exp2(s - m_new)
            l_ref[h] = alpha * l_ref[h] + jnp.sum(p, axis=0, keepdims=True)
            m_ref[h] = m_new
            return p.astype(BF16), alpha

        def values(h, p, alpha):
            acc_ref[h] = alpha * acc_ref[h] + jnp.dot(v_ref[0, h], p, preferred_element_type=F32)

        s_next = scores(0)
        for h in range(MLA_HEADS):
            s_cur = s_next
            if h + 1 < MLA_HEADS:
                s_next = scores(h + 1)
            values(h, *softmax(h, s_cur))

    @pl.when(ki < qi)
    def _():
        step(False)

    @pl.when(ki == qi)
    def _():
        step(True)
        ot = jnp.concatenate([acc_ref[h] * (1.0 / l_ref[h]) for h in range(MLA_HEADS)], axis=0)
        y_ref[0] = ot.T


def _mla_attn(q, k, v, tq):
    b, _, s, _ = q.shape
    nq = s // tq
    pairs = [(qi, ki) for qi in range(nq) for ki in range(qi + 1)]
    qi_tab = jnp.asarray([p[0] for p in pairs], jnp.int32)
    ki_tab = jnp.asarray([p[1] for p in pairs], jnp.int32)
    grid_spec = pltpu.PrefetchScalarGridSpec(
        num_scalar_prefetch=2,
        grid=(b, len(pairs)),
        in_specs=[pl.BlockSpec((1, MLA_HEADS, tq, MLA_HP), lambda i, p, qt, kt: (i, 0, qt[p], 0)),
                  pl.BlockSpec((1, MLA_HEADS, tq, MLA_HP), lambda i, p, qt, kt: (i, 0, kt[p], 0)),
                  pl.BlockSpec((1, MLA_HEADS, MLA_V, tq), lambda i, p, qt, kt: (i, 0, 0, kt[p]))],
        out_specs=pl.BlockSpec((1, tq, GROUP_W), lambda i, p, qt, kt: (i, qt[p], 0)),
        scratch_shapes=[pltpu.VMEM((MLA_HEADS, 1, tq), F32), pltpu.VMEM((MLA_HEADS, 1, tq), F32),
                        pltpu.VMEM((MLA_HEADS, MLA_V, tq), F32)],
    )
    return pl.pallas_call(
        functools.partial(_mla_attn_kernel, tq=tq),
        grid_spec=grid_spec,
        out_shape=jax.ShapeDtypeStruct((b, s, GROUP_W), F32),
        compiler_params=_cparams(("parallel", "arbitrary")),
        name="mla_attn",
    )(qi_tab, ki_tab, q, k, v)


def _outproj_kernel(*refs, routed):
    if routed:
        ym, ya, yg, yc, h_ref, fg_ref, wo_ref, n2_ref, rt_ref, hn_ref, u_ref, route_ref = refs
    else:
        ym, ya, yg, yc, h_ref, fg_ref, wo_ref, n2_ref, hn_ref, u_ref = refs
    parts = []
    for gi, y_ref in enumerate((ym, ya, yg, yc)):
        y = y_ref[...]
        yn = y * lax.rsqrt(jnp.mean(y * y, axis=-1, keepdims=True) + EPS)
        parts.append((yn * fg_ref[:, gi * GROUP_W:(gi + 1) * GROUP_W]).astype(BF16))
    ycat = jnp.concatenate(parts, axis=1)
    hn = h_ref[...] + jnp.dot(ycat, wo_ref[...], preferred_element_type=F32)
    hn_ref[...] = hn
    u = hn * lax.rsqrt(jnp.mean(hn * hn, axis=-1, keepdims=True) + EPS) * n2_ref[...]
    u_ref[...] = u.astype(u_ref.dtype)
    if routed:
        tm = u.shape[0]
        lane = _iota((tm, LANES), 1)
        u_hi = u.astype(BF16)
        u_lo = (u - u_hi.astype(F32)).astype(BF16)
        rt = rt_ref[...]
        r_hi = rt.astype(BF16)
        r_lo = (rt - r_hi.astype(F32)).astype(BF16)
        d = lambda a, b: jnp.dot(a, b, preferred_element_type=F32)
        logits = d(u_hi, r_hi) + (d(u_lo, r_hi) + d(u_hi, r_lo))
        logits = jnp.where(lane < N_EXPERTS, logits, -jnp.inf)
        m1 = jnp.max(logits, axis=-1, keepdims=True)
        i1 = jnp.min(jnp.where(logits == m1, lane, LANES), axis=-1, keepdims=True)
        rest = jnp.where(lane == i1, -jnp.inf, logits)
        m2 = jnp.max(rest, axis=-1, keepdims=True)
        i2 = jnp.min(jnp.where(rest == m2, lane, LANES), axis=-1, keepdims=True)
        e2 = jnp.exp(m2 - m1)
        w1 = 1.0 / (1.0 + e2)
        w2 = e2 / (1.0 + e2)
        route_ref[...] = jnp.where(lane == 0, i1.astype(F32),
                                   jnp.where(lane == 1, i2.astype(F32),
                                             jnp.where(lane == 2, w1, jnp.where(lane == 3, w2, 0.0))))


def _outproj(ym, ya, yg, yc, h, fg, wo, n2, router, tm):
    n = h.shape[0]
    routed = router is not None
    row = lambda w: pl.BlockSpec((tm, w), lambda i: (i, 0))
    in_specs = [row(GROUP_W)] * 4 + [row(D_MODEL), _const_spec(fg.shape), _const_spec(wo.shape), _const_spec(n2.shape)]
    args = [ym, ya, yg, yc, h, fg, wo, n2]
    out_specs = [row(D_MODEL), row(D_MODEL)]
    out_shape = [jax.ShapeDtypeStruct((n, D_MODEL), F32), jax.ShapeDtypeStruct((n, D_MODEL), F32 if routed else BF16)]
    if routed:
        in_specs.append(_const_spec(router.shape))
        args.append(router)
        out_specs.append(row(LANES))
        out_shape.append(jax.ShapeDtypeStruct((n, LANES), F32))
    return pl.pallas_call(
        functools.partial(_outproj_kernel, routed=routed),
        grid=(n // tm,),
        in_specs=in_specs, out_specs=out_specs, out_shape=out_shape,
        compiler_params=_cparams(("parallel",)),
        name="outproj_routed" if routed else "outproj",
    )(*args)


def _swiglu(x, wg, wu, wd):
    g = jnp.dot(x, wg, preferred_element_type=F32)
    up = jnp.dot(x, wu, preferred_element_type=F32)
    a = (g * _sigmoid(g) * up).astype(BF16)
    return jnp.dot(a, wd, preferred_element_type=F32)


def _ffn_kernel(u_ref, h_ref, wg_ref, wu_ref, wd_ref, o_ref):
    o_ref[...] = h_ref[...] + _swiglu(u_ref[...], wg_ref[...], wu_ref[...], wd_ref[...])


def _ffn(u, h, wg, wu, wd, tm):
    n = h.shape[0]
    row = pl.BlockSpec((tm, D_MODEL), lambda i: (i, 0))
    return pl.pallas_call(
        _ffn_kernel,
        grid=(n // tm,),
        in_specs=[row, row, _const_spec(wg.shape), _const_spec(wu.shape), _const_spec(wd.shape)],
        out_specs=row,
        out_shape=jax.ShapeDtypeStruct((n, D_MODEL), F32),
        compiler_params=_cparams(("parallel",)),
        name="ffn_dense",
    )(u, h, wg, wu, wd)


def _moe_kernel(be_ref, tokc_ref, tokn_ref, dstp_ref, dstc_ref, u_hbm, wg_ref, wu_ref, wd_ref, y_hbm,
                xbuf, ybuf, gsem, ssem):
    i = pl.program_id(0)
    nb = pl.num_programs(0)
    bm = xbuf.shape[1]

    def gather_wait(x, s):
        pltpu.make_async_copy(u_hbm.at[pl.ds(0, bm), :], x, gsem.at[s]).wait()

    def scatter_wait(y, s):
        pltpu.make_async_copy(y, y_hbm.at[pl.ds(0, bm), :], ssem.at[s]).wait()

    def scatter_row(y, dst_ref, r, s):
        pltpu.make_async_copy(y.at[pl.ds(r, 1), :], y_hbm.at[pl.ds(dst_ref[0, 0, r], 1), :], ssem.at[s]).start()

    @pl.when(i == 0)
    def _():
        def body(r, c):
            pltpu.make_async_copy(u_hbm.at[pl.ds(tokc_ref[0, 0, r], 1), :], xbuf.at[0, pl.ds(r, 1), :],
                                  gsem.at[0]).start()
            return c
        lax.fori_loop(0, bm, body, 0, unroll=8)
        ybuf[2] = jnp.zeros((bm, D_MODEL), F32)

    c = i % 2
    n = 1 - c
    yc_i = i % 3
    yp_i = (i + 2) % 3
    xc, xn, yc, yp = xbuf.at[c], xbuf.at[n], ybuf.at[yc_i], ybuf.at[yp_i]
    gather_wait(xc, c)

    @pl.when(i >= 2)
    def _():
        scatter_wait(yc, yc_i)

    g = jnp.dot(xc[...].astype(BF16), wg_ref[0], preferred_element_type=F32)
    for r in range(bm):
        pltpu.make_async_copy(u_hbm.at[pl.ds(tokn_ref[0, 0, r], 1), :], xn.at[pl.ds(r, 1), :], gsem.at[n]).start()
    up = jnp.dot(xc[...].astype(BF16), wu_ref[0], preferred_element_type=F32)
    for r in range(bm):
        scatter_row(yp, dstp_ref, r, yp_i)
    a = (g * _sigmoid(g) * up).astype(BF16)
    yc[...] = jnp.dot(a, wd_ref[0], preferred_element_type=F32)

    @pl.when(i == nb - 1)
    def _():
        def body(r, carry):
            scatter_row(yc, dstc_ref, r, yc_i)
            return carry
        lax.fori_loop(0, bm, body, 0, unroll=8)
        scatter_wait(yc, (i + 1) % 3)
        scatter_wait(yp, yp_i)
        scatter_wait(yc, yc_i)
        gather_wait(xn, n)


def _moe(block_e, row_tok, row_dst, u, wg, wu, wd, n_rows):
    n_blocks = block_e.shape[0]
    bm = row_tok.shape[-1]
    wspec = lambda shp: pl.BlockSpec((1,) + shp, lambda i, be: (be[i], 0, 0))
    ispec = lambda f: pl.BlockSpec((1, 1, bm), f, memory_space=pltpu.SMEM)
    grid_spec = pltpu.PrefetchScalarGridSpec(
        num_scalar_prefetch=1,
        grid=(n_blocks,),
        in_specs=[ispec(lambda i, be: (i, 0, 0)),
                  ispec(lambda i, be: (jnp.minimum(i + 1, n_blocks - 1), 0, 0)),
                  ispec(lambda i, be: (i, 0, 0)),
                  ispec(lambda i, be: (i + 1, 0, 0)),
                  pl.BlockSpec(memory_space=pl.ANY),
                  wspec((D_MODEL, D_FF)), wspec((D_MODEL, D_FF)), wspec((D_FF, D_MODEL))],
        out_specs=pl.BlockSpec(memory_space=pl.ANY),
        scratch_shapes=[pltpu.VMEM((2, bm, D_MODEL), F32), pltpu.VMEM((3, bm, D_MODEL), F32),
                        pltpu.SemaphoreType.DMA((2,)), pltpu.SemaphoreType.DMA((3,))],
    )
    return pl.pallas_call(
        _moe_kernel,
        grid_spec=grid_spec,
        out_shape=jax.ShapeDtypeStruct((n_rows, D_MODEL), F32),
        compiler_params=_cparams(("arbitrary",)),
        name="moe_experts",
    )(block_e, row_tok, row_tok, row_dst, row_dst, u, wg, wu, wd)


def _combine_kernel(ya_ref, yb_ref, h_ref, route_ref, fg_ref, o_ref, *, final):
    w1 = route_ref[:, 2:3]
    w2 = route_ref[:, 3:4]
    hn = h_ref[...] + w1 * ya_ref[...] + w2 * yb_ref[...]
    if final:
        hn = hn * lax.rsqrt(jnp.mean(hn * hn, axis=-1, keepdims=True) + EPS) * fg_ref[...]
    o_ref[...] = hn


def _combine(y2, h, route, fg, tm, final):
    n = h.shape[0]
    return pl.pallas_call(
        functools.partial(_combine_kernel, final=final),
        grid=(n // tm,),
        in_specs=[pl.BlockSpec((tm, D_MODEL), lambda i: (i, 0)),
                  pl.BlockSpec((tm, D_MODEL), lambda i: (i + n // tm, 0)),
                  pl.BlockSpec((tm, D_MODEL), lambda i: (i, 0)),
                  pl.BlockSpec((tm, LANES), lambda i: (i, 0)),
                  _const_spec(fg.shape)],
        out_specs=pl.BlockSpec((tm, D_MODEL), lambda i: (i, 0)),
        out_shape=jax.ShapeDtypeStruct((n, D_MODEL), F32),
        compiler_params=_cparams(("parallel",)),
        name="moe_combine",
    )(y2, y2, h, route, fg)


def _rmsnorm_kernel(h_ref, g_ref, o_ref):
    x = h_ref[...]
    o_ref[...] = x * lax.rsqrt(jnp.mean(x * x, axis=-1, keepdims=True) + EPS) * g_ref[...]


def _final_norm(h, g, tm):
    n = h.shape[0]
    row = pl.BlockSpec((tm, D_MODEL), lambda i: (i, 0))
    return pl.pallas_call(
        _rmsnorm_kernel, grid=(n // tm,), in_specs=[row, _const_spec(g.shape)], out_specs=row,
        out_shape=jax.ShapeDtypeStruct((n, D_MODEL), F32),
        compiler_params=_cparams(("parallel",)), name="final_norm",
    )(h, g)


def _pad_cols(w, width):
    return jnp.pad(w, ((0, 0), (0, width - w.shape[1])))


def _pad_rows(w, height):
    return jnp.pad(w, ((0, height - w.shape[0]), (0, 0)))


def _inproj_weights(w):
    o = np.cumsum([0, 512, 256, 256, 8, 192, 128, 32, 128, 128, 256, 256, 16, 512])
    seg = lambda i: w[:, o[i]:o[i + 1]]
    gates = seg(3)
    wm = jnp.concatenate([seg(0), seg(1), seg(2), _pad_cols(gates[:, :4], LANES), _pad_cols(gates[:, 4:], LANES)], axis=1)
    wa = jnp.concatenate([_pad_cols(seg(4), 256), seg(5), _pad_cols(seg(6), LANES)], axis=1)
    wg = jnp.concatenate([seg(7), seg(8), seg(9), seg(10), _pad_cols(seg(11), LANES)], axis=1)
    wc = seg(12)
    return [x.astype(BF16) for x in (wm, wa, wg, wc)]


def _mla_weights(w_uq, w_ukv):
    half = MLA_ROPE // 2
    zq = jnp.zeros((MLA_Q_RANK, half), F32)
    qa, qb, ka, kb, wv = [], [], [], [], []
    eye = jnp.eye(LANES, dtype=F32)[:, :MLA_ROPE]
    zk = jnp.zeros((MLA_KV_RANK, MLA_HP), F32)
    for h in range(MLA_HEADS):
        wq = w_uq[:, h * 96:(h + 1) * 96]
        nope, r1, r2 = wq[:, :64], wq[:, 64:64 + half], wq[:, 64 + half:]
        qa.append(_pad_cols(jnp.concatenate([nope, r1, r2], axis=1), MLA_HP))
        qb.append(_pad_cols(jnp.concatenate([jnp.zeros_like(nope), -r2, r1], axis=1), MLA_HP))
        wkv = w_ukv[:, h * 128:(h + 1) * 128]
        kn, vh = wkv[:, :64], wkv[:, 64:]
        e1, e2 = eye[:, :half], eye[:, half:]
        z64 = jnp.zeros((LANES, 64), F32)
        ka.append(jnp.concatenate([_pad_cols(kn, MLA_HP), _pad_cols(jnp.concatenate([z64, e1, e2], axis=1), MLA_HP)], axis=0))
        kb.append(jnp.concatenate([zk, _pad_cols(jnp.concatenate([z64, -e2, e1], axis=1), MLA_HP)], axis=0))
        wv.append(vh.T)
    cat = lambda xs, rows: _pad_rows(jnp.concatenate(xs, axis=1), rows).astype(BF16)
    wvt = _pad_cols(jnp.concatenate(wv, axis=0), 256).astype(BF16)
    return cat(qa, 256), cat(qb, 256), cat(ka, 256), cat(kb, 256), wvt


def _rope_tables(positions):
    half = MLA_ROPE // 2
    inv_freq = 1.0 / (ROPE_BASE ** (jnp.arange(0, MLA_ROPE, 2, dtype=F32) / MLA_ROPE))
    ang = positions.astype(F32)[..., None] * inv_freq
    cos, sin = jnp.cos(ang), jnp.sin(ang)
    shp = cos.shape[:-1]
    rc = jnp.concatenate([jnp.ones(shp + (64,), F32), cos, cos, jnp.zeros(shp + (MLA_HP - 64 - 2 * half,), F32)], axis=-1)
    rs = jnp.concatenate([jnp.zeros(shp + (64,), F32), sin, sin, jnp.zeros(shp + (MLA_HP - 64 - 2 * half,), F32)], axis=-1)
    return rc, rs


def _routing_tables(route, bm):
    n = route.shape[0]
    a = n * TOP_K
    flat_e = route[:, :TOP_K].astype(jnp.int32).reshape(-1)
    onehot = (flat_e[:, None] == jnp.arange(N_EXPERTS, dtype=jnp.int32)[None, :]).astype(jnp.int32)
    csum = jnp.cumsum(onehot, axis=0)
    rank = jnp.sum(onehot * csum, axis=1) - 1
    counts = csum[-1]
    padded = (counts + bm - 1) // bm * bm
    padded_end = jnp.cumsum(padded)
    padded_start = padded_end - padded
    dest = padded_start[flat_e] + rank
    n_blocks = -(-a // bm) + N_EXPERTS
    rows = n_blocks * bm
    row_asg = jnp.full((rows,), -1, jnp.int32).at[dest].set(jnp.arange(a, dtype=jnp.int32))
    row_tok = jnp.maximum(row_asg, 0) // TOP_K
    pad_rank = jnp.cumsum((row_asg < 0).astype(jnp.int32)) - 1
    slot_row = (row_asg % TOP_K) * n + row_asg // TOP_K
    row_dst = jnp.where(row_asg >= 0, slot_row, a + pad_rank)
    first = a + (rows - a) + jnp.arange(bm, dtype=jnp.int32)
    row_dst = jnp.concatenate([first, row_dst]).reshape(n_blocks + 1, 1, bm)
    block_start = jnp.arange(n_blocks, dtype=jnp.int32) * bm
    block_e = jnp.sum((padded_end[None, :] <= block_start[:, None]).astype(jnp.int32), axis=1)
    block_e = jnp.minimum(block_e, N_EXPERTS - 1).astype(jnp.int32)
    return block_e, row_tok.reshape(n_blocks, 1, bm), row_dst, rows + bm


def kernel(x, positions, norm1_g, w_in, mlstm_conv_w, mlstm_conv_b, mlstm_gate_b, mla_q_norm_g, mla_w_uq,
           mla_kv_norm_g, mla_w_ukv, gla_w_alpha, gla_b_alpha, conv_w, conv_b, conv_ln_g, conv_ln_b, fuse_g,
           w_out, norm2_g, ffn_w_gate, ffn_w_up, ffn_w_down, moe_router, moe_w_gate, moe_w_up, moe_w_down,
           final_norm_g):
    bsz, seq, _ = x.shape
    n = bsz * seq
    depth = w_in.shape[0]
    tm = min(512, n)
    tt = min(512, seq)
    tq = min(512, seq)
    nct = seq // CHUNK
    row2 = lambda v: v.reshape(1, -1).astype(F32)

    rc, rs = _rope_tables(positions)
    h = x.reshape(n, D_MODEL)
    for l in range(depth):
        wm, wa, wg, wc = _inproj_weights(w_in[l])
        zm, za, zg, zc = _inproj(h, row2(norm1_g[l]), wm, wa, wg, wc, tm)
        zm = zm.reshape(bsz, seq, ZM_W)

        gates = jnp.concatenate([zm[:, :, 1024:1028], zm[:, :, 1152:1156]], axis=-1)
        grow = gates.reshape(bsz, nct, CHUNK, 2, MLSTM_HEADS).transpose(0, 3, 1, 4, 2).reshape(bsz, 2, nct, GROUP_W)
        gb = mlstm_gate_b[l].astype(F32)
        gbc = _pad_cols(gb.reshape(2, MLSTM_HEADS), LANES)
        gbr = jnp.repeat(gb.reshape(2, MLSTM_HEADS), CHUNK, axis=1)
        ym = _mlstm(zm, grow, mlstm_conv_w[l], row2(mlstm_conv_b[l]), gbc, gbr, tt)

        wqa, wqb, wka, wkb, wv = _mla_weights(mla_w_uq[l], mla_w_ukv[l])
        qh, kh, vh = _mla_prep(za.reshape(bsz, seq, ZA_W), rc, rs, _pad_cols(row2(mla_q_norm_g[l]), 256),
                               row2(mla_kv_norm_g[l]), wqa, wqb, wka, wkb, wv, tt)
        ya = _mla_attn(qh, kh, vh, tq)

        yg = _gla(zg.reshape(bsz, seq, ZG_W), _pad_rows(gla_w_alpha[l], LANES), row2(gla_b_alpha[l]), tt)
        yc = _conformer(zc.reshape(bsz, seq, ZC_W), conv_w[l], row2(conv_b[l]), row2(conv_ln_g[l]),
                        row2(conv_ln_b[l]), tt)

        flat = lambda y: y.reshape(n, GROUP_W)
        j = l // 2
        if l % 2 == 0:
            h, u = _outproj(flat(ym), flat(ya), flat(yg), flat(yc), h, row2(fuse_g[l]), w_out[l].astype(BF16),
                            row2(norm2_g[l]), None, tm)
            h = _ffn(u, h, ffn_w_gate[j].astype(BF16), ffn_w_up[j].astype(BF16), ffn_w_down[j].astype(BF16), tm)
            if l == depth - 1:
                h = _final_norm(h, row2(final_norm_g), tm)
        else:
            h, u, route = _outproj(flat(ym), flat(ya), flat(yg), flat(yc), h, row2(fuse_g[l]),
                                   w_out[l].astype(BF16), row2(norm2_g[l]), _pad_cols(moe_router[j], LANES), tm)
            block_e, row_tok, row_dst, n_rows = _routing_tables(route, MOE_BM)
            y2 = _moe(block_e, row_tok, row_dst, u, moe_w_gate[j].astype(BF16), moe_w_up[j].astype(BF16),
                      moe_w_down[j].astype(BF16), n_rows)
            h = _combine(y2, h, route, row2(final_norm_g), tm, l == depth - 1)
    return h.reshape(bsz, seq, D_MODEL)
```

```python
import functools

import jax
import jax.numpy as jnp
import numpy as np
from jax import lax
from jax.experimental import pallas as pl
from jax.experimental.pallas import tpu as pltpu

F32 = jnp.float32
BF16 = jnp.bfloat16
HI = lax.Precision.HIGHEST

D_MODEL = 1024
GROUP_W = 256
EPS = 1e-6
LOG2E = 1.4426950408889634
LANES = 128
CHUNK = 64
MLSTM_HEADS = 4
MLSTM_CONV = 4
MLA_HEADS = 4
MLA_NOPE = 64
MLA_ROPE = 32
MLA_V = 64
MLA_Q_RANK = 192
MLA_KV_RANK = 128
ROPE_BASE = 10000.0
GLA_HEADS = 4
GLA_DK = 32
GLA_DV = 64
GLA_GATE_RANK = 16
GLA_TAU = 16.0
CONV_WIDTH = 31
D_FF = 2816
N_EXPERTS = 8
TOP_K = 2
MOE_BM = 256
VMEM_LIMIT = 56 * 1024 * 1024

ZM_W = 1280
ZA_W = 512
ZG_W = 896
ZC_W = 512


def _cparams(sem):
    return pltpu.CompilerParams(dimension_semantics=sem, vmem_limit_bytes=VMEM_LIMIT)


def _const_spec(shape):
    nd = len(shape)
    return pl.BlockSpec(shape, lambda *_: (0,) * nd, pipeline_mode=pl.Buffered(1))


def _sigmoid(x):
    return 1.0 / (1.0 + jnp.exp(-x))


def _log_sigmoid(x):
    return jnp.minimum(x, 0.0) - jnp.log(1.0 + jnp.exp(-jnp.abs(x)))


def _iota(shape, dim):
    return lax.broadcasted_iota(jnp.int32, shape, dim)


def _tri(n):
    return (_iota((n, n), 0) >= _iota((n, n), 1)).astype(BF16)


def _split3(x):
    hi = x.astype(BF16)
    r1 = x - hi.astype(F32)
    mid = r1.astype(BF16)
    lo = (r1 - mid.astype(F32)).astype(BF16)
    return hi, mid, lo


def _sel_dot(sel, x):
    hi, mid, lo = _split3(x)
    d = lambda p: jnp.dot(sel, p, preferred_element_type=F32)
    return d(hi) + d(mid) + d(lo)


def _dot_sel(x, sel):
    hi, mid, lo = _split3(x)
    d = lambda p: jnp.dot(p, sel, preferred_element_type=F32)
    return d(hi) + d(mid) + d(lo)


def _inproj_kernel(h_ref, g_ref, wm_ref, wa_ref, wg_ref, wc_ref, zm_ref, za_ref, zg_ref, zc_ref):
    x = h_ref[...]
    ms = jnp.mean(x * x, axis=-1, keepdims=True)
    xn = (x * lax.rsqrt(ms + EPS) * g_ref[...]).astype(BF16)
    zm_ref[...] = jnp.dot(xn, wm_ref[...], preferred_element_type=F32)
    za_ref[...] = jnp.dot(xn, wa_ref[...], preferred_element_type=F32)
    zg_ref[...] = jnp.dot(xn, wg_ref[...], preferred_element_type=F32)
    zc_ref[...] = jnp.dot(xn, wc_ref[...], preferred_element_type=F32)


def _inproj(h, g, wm, wa, wg, wc, tm):
    n = h.shape[0]
    row = lambda w: pl.BlockSpec((tm, w), lambda i: (i, 0))
    return pl.pallas_call(
        _inproj_kernel,
        grid=(n // tm,),
        in_specs=[row(D_MODEL), _const_spec((1, D_MODEL)), _const_spec(wm.shape), _const_spec(wa.shape),
                  _const_spec(wg.shape), _const_spec(wc.shape)],
        out_specs=[row(ZM_W), row(ZA_W), row(ZG_W), row(ZC_W)],
        out_shape=[jax.ShapeDtypeStruct((n, w), F32) for w in (ZM_W, ZA_W, ZG_W, ZC_W)],
        compiler_params=_cparams(("parallel",)),
        name="inproj",
    )(h, g, wm, wa, wg, wc)


def _expand_heads(x, width):
    r = x.shape[0]
    lane = _iota((r, 4 * width), 1)
    out = jnp.broadcast_to(x[:, 3:4], (r, 4 * width))
    for h in (2, 1, 0):
        out = jnp.where(lane < (h + 1) * width, jnp.broadcast_to(x[:, h:h + 1], (r, 4 * width)), out)
    return out


def _mlstm_setup(zm_ref, grow_ref, cw_ref, cb_ref, gbc_ref, gbr_ref, y_ref,
                 xpad_ref, q_ref, k_ref, grs_ref, cn_ref, m_ref, cnm_ref, trit_ref, b_ref, gc_ref, cm_ref, *, tt):
    L = CHUNK
    W = MLSTM_HEADS * 64

    @pl.when(pl.program_id(1) == 0)
    def _():
        xpad_ref[0:8, :] = jnp.zeros((8, 2 * W), F32)
        cn_ref[...] = jnp.zeros_like(cn_ref)
        m_ref[...] = jnp.zeros_like(m_ref)
        hsame = (_iota((W, W), 0) // L) == (_iota((W, W), 1) // L)
        ncol = (_iota((W, LANES), 0) // L) == _iota((W, LANES), 1)
        cnm_ref[...] = jnp.concatenate([hsame, ncol], axis=1).astype(F32)
        trit_ref[...] = (hsame & ((_iota((W, W), 0) % L) <= (_iota((W, W), 1) % L))).astype(BF16)

    xpad_ref[8:8 + tt, :] = zm_ref[0, :, 0:2 * W]
    for r in range(0, tt, L):
        acc = jnp.broadcast_to(cb_ref[...], (L, 2 * W))
        for j in range(MLSTM_CONV):
            acc = acc + cw_ref[j:j + 1, :] * xpad_ref[pl.ds(r + 8 - (MLSTM_CONV - 1) + j, L), :]
        qk = acc * _sigmoid(acc)
        q_ref[r:r + L, :] = (qk[:, :W] * (64 ** -0.5)).astype(BF16)
        k_ref[r:r + L, :] = qk[:, W:]
    xpad_ref[0:8, :] = xpad_ref[tt:tt + 8, :]

    i_row = grow_ref[0, 0] + gbr_ref[0:1, :]
    lf_row = _log_sigmoid(grow_ref[0, 1] + gbr_ref[1:2, :])
    grs_ref[...] = i_row - _dot_sel(lf_row, trit_ref[...])

    tri = _tri(L)
    lf_all = _log_sigmoid(zm_ref[0, :, 4 * W + LANES:4 * W + 2 * LANES] + gbc_ref[1:2, :])
    for r in range(0, tt, L):
        b_ref[r:r + L, :] = _sel_dot(tri, lf_all[r:r + L, :])
    g_all = zm_ref[0, :, 4 * W:4 * W + LANES] + gbc_ref[0:1, :] - b_ref[...]
    gc_ref[...] = g_all
    pos = _iota((tt, LANES), 0) % L
    cm_all = g_all
    s = 1
    while s < L:
        cm_all = jnp.maximum(cm_all, jnp.where(pos >= s, pltpu.roll(cm_all, s, 0), -jnp.inf))
        s *= 2
    cm_ref[...] = cm_all

    row_t = _iota((L, W), 0)
    lane_j = _iota((L, W), 1) % L
    causal = lane_j <= row_t
    ones_col = (_iota((L, LANES), 1) < MLSTM_HEADS).astype(F32)

    def chunk(c):
        r0 = pl.multiple_of(c * L, L)
        b = b_ref[pl.ds(r0, L), :]
        g = gc_ref[pl.ds(r0, L), :]
        cm = cm_ref[pl.ds(r0, L), :]
        m_prev = m_ref[...]
        mx = jnp.maximum(m_prev, cm)
        w_inter = jnp.exp(m_prev - mx)
        b_end = b[L - 1:L, :]
        mx_end = mx[L - 1:L, :]
        m_new = b_end + mx_end
        decay = jnp.exp(m_prev - mx_end)
        kw_col = jnp.exp(g - mx_end)

        qc = q_ref[pl.ds(r0, L), :]
        kc = k_ref[pl.ds(r0, L), :]
        vc = zm_ref[0, pl.ds(r0, L), 2 * W:3 * W]
        oc = zm_ref[0, pl.ds(r0, L), 3 * W:4 * W]

        kbd = (jnp.concatenate([kc] * 4, axis=0) * cnm_ref[:, 0:W]).astype(BF16)
        s_qk = lax.dot_general(qc, kbd, (((1,), (1,)), ((), ())), preferred_element_type=F32)
        inter = jnp.dot(qc, cn_ref[...].astype(BF16), preferred_element_type=F32)
        vaug = jnp.concatenate([vc, ones_col], axis=1)
        kw = (_expand_heads(kw_col, L) * kc).astype(BF16)
        upd = lax.dot_general(kw, vaug.astype(BF16), (((0,), (0,)), ((), ())), preferred_element_type=F32)
        decay_x = jnp.concatenate([_expand_heads(decay, L), decay], axis=1)
        cn_ref[...] = decay_x * cn_ref[...] + upd * cnm_ref[...]
        m_ref[...] = m_new

        def output():
            g_row = grs_ref[pl.ds(c, 1), :]
            dmat = jnp.where(causal, jnp.exp(jnp.where(causal, g_row - _expand_heads(mx, L), 0.0)), 0.0)
            s_w = (s_qk * dmat).astype(BF16)
            vbd = (jnp.concatenate([vaug] * 4, axis=0) * cnm_ref[...]).astype(BF16)
            intra = jnp.dot(s_w, vbd, preferred_element_type=F32)
            num = _expand_heads(w_inter, L) * inter[:, :W] + intra[:, :W]
            den = w_inter * inter[:, W:] + intra[:, W:]
            hden = jnp.maximum(jnp.abs(den), jnp.exp(-(b + mx)))
            hout = num * _expand_heads(1.0 / hden, L)
            y_ref[0, pl.ds(r0, L), :] = _sigmoid(oc) * hout

        return output

    return chunk


def _mlstm_scratch(tt):
    nct = tt // CHUNK
    return [pltpu.VMEM((tt + 8, 2 * GROUP_W), F32), pltpu.VMEM((tt, GROUP_W), BF16),
            pltpu.VMEM((tt, GROUP_W), F32), pltpu.VMEM((nct, GROUP_W), F32),
            pltpu.VMEM((GROUP_W, GROUP_W + LANES), F32), pltpu.VMEM((1, LANES), F32),
            pltpu.VMEM((GROUP_W, GROUP_W + LANES), F32), pltpu.VMEM((GROUP_W, GROUP_W), BF16),
            pltpu.VMEM((tt, LANES), F32), pltpu.VMEM((tt, LANES), F32), pltpu.VMEM((tt, LANES), F32)]


def _gla_setup(zg_ref, wa_ref, ba_ref, y_ref, q_ref, k_ref, g_ref, o_ref, st_ref, *, tt):
    L = CHUNK
    WK = GLA_HEADS * GLA_DK
    WV = GLA_HEADS * GLA_DV

    @pl.when(pl.program_id(1) == 0)
    def _():
        st_ref[...] = jnp.zeros_like(st_ref)

    q_ref[...] = zg_ref[0, :, 0:WK] * (GLA_DK ** -0.5)
    k_ref[...] = zg_ref[0, :, WK:2 * WK]
    a = zg_ref[0, :, 2 * WK + 2 * WV:2 * WK + 2 * WV + LANES]
    pre = jnp.dot(a, wa_ref[...], precision=HI, preferred_element_type=F32) + ba_ref[...]
    g_all = _log_sigmoid(pre) * (1.0 / GLA_TAU)
    tri = _tri(L)
    for r in range(0, tt, L):
        g_ref[r:r + L, :] = _sel_dot(tri, g_all[r:r + L, :])

    hs2 = ((_iota((WK, WV), 0) // GLA_DK) == (_iota((WK, WV), 1) // GLA_DV)).astype(BF16)
    st_mask = (_iota((WV, WK), 0) // GLA_DV) == (_iota((WV, WK), 1) // GLA_DK)

    def chunk(c):
        r0 = pl.multiple_of(c * L, L)
        qc = q_ref[pl.ds(r0, L), :]
        kc = k_ref[pl.ds(r0, L), :]
        vc = zg_ref[0, pl.ds(r0, L), 2 * WK:2 * WK + WV]
        b = g_ref[pl.ds(r0, L), :]
        st = st_ref[...]
        qd = (qc * jnp.exp(b)).astype(BF16)
        o = lax.dot_general(qd, st.astype(BF16), (((1,), (1,)), ((), ())), preferred_element_type=F32)
        b_end = b[L - 1:L, :]
        kd = (kc * jnp.exp(b_end - b)).astype(BF16)
        upd = lax.dot_general(vc.astype(BF16), kd, (((0,), (0,)), ((), ())), preferred_element_type=F32)
        st_ref[...] = st * jnp.exp(b_end) + jnp.where(st_mask, upd, 0.0)

        def output(o=o):
            pieces = []
            for jb in range(L // 8):
                t0 = 8 * jb
                rows = L - t0
                qs, bs = qc[t0:, :], b[t0:, :]
                tl = _iota((rows, WK), 0)
                ps = []
                for jj in range(8):
                    j = t0 + jj
                    e = jnp.exp(jnp.minimum(bs - b[j:j + 1, :], 0.0))
                    ps.append(jnp.where(tl >= jj, qs * kc[j:j + 1, :] * e, 0.0).astype(BF16))
                rexp = jnp.dot(jnp.concatenate(ps, axis=0), hs2, preferred_element_type=F32)
                acc = rexp[0:rows, :] * vc[t0:t0 + 1, :]
                for jj in range(1, 8):
                    acc = acc + rexp[jj * rows:(jj + 1) * rows, :] * vc[t0 + jj:t0 + jj + 1, :]
                pieces.append(acc)
            bands = []
            for band in range(L // 8):
                tot = None
                for jb in range(band + 1):
                    lo = 8 * (band - jb)
                    part = pieces[jb][lo:lo + 8, :]
                    tot = part if tot is None else tot + part
                bands.append(tot)
            o = o + jnp.concatenate(bands, axis=0)
            o_ref[pl.ds(r0, L), :] = o

        return output

    def finish():
        o = o_ref[...]
        hsame = ((_iota((WV, WV), 0) // GLA_DV) == (_iota((WV, WV), 1) // GLA_DV)).astype(BF16)
        ms = _dot_sel(o * o, hsame) * (1.0 / GLA_DV)
        r = zg_ref[0, :, 2 * WK + WV:2 * WK + 2 * WV]
        y_ref[0] = o * lax.rsqrt(ms + EPS) * (r * _sigmoid(r))

    return chunk, finish


def _gla_scratch(tt):
    return [pltpu.VMEM((tt, LANES), F32), pltpu.VMEM((tt, LANES), F32), pltpu.VMEM((tt, LANES), F32),
            pltpu.VMEM((tt, GROUP_W), F32), pltpu.VMEM((GROUP_W, LANES), F32)]


def _recurrent_kernel(zm_ref, grow_ref, cw_ref, cb_ref, gbc_ref, gbr_ref, zg_ref, wa_ref, ba_ref,
                      ym_ref, yg_ref, *scratch, tt):
    n_m = len(_mlstm_scratch(tt))
    m_chunk = _mlstm_setup(zm_ref, grow_ref, cw_ref, cb_ref, gbc_ref, gbr_ref, ym_ref, *scratch[:n_m], tt=tt)
    g_chunk, g_finish = _gla_setup(zg_ref, wa_ref, ba_ref, yg_ref, *scratch[n_m:], tt=tt)

    def both(c, carry):
        m_output = m_chunk(c)
        g_output = g_chunk(c)
        m_output()
        g_output()
        return carry

    lax.fori_loop(0, tt // CHUNK, both, 0, unroll=2)
    g_finish()


def _recurrent_mixers(zm, grow, cw, cb, gbc, gbr, zg, wa, ba, tt):
    b, s, _ = zm.shape
    nct = tt // CHUNK
    tile = lambda w: pl.BlockSpec((1, tt, w), lambda i, t: (i, t, 0))
    yshape = jax.ShapeDtypeStruct((b, s, GROUP_W), F32)
    return pl.pallas_call(
        functools.partial(_recurrent_kernel, tt=tt),
        grid=(b, s // tt),
        in_specs=[tile(ZM_W), pl.BlockSpec((1, 2, nct, GROUP_W), lambda i, t: (i, 0, t, 0)),
                  _const_spec(cw.shape), _const_spec(cb.shape), _const_spec(gbc.shape), _const_spec(gbr.shape),
                  tile(ZG_W), _const_spec(wa.shape), _const_spec(ba.shape)],
        out_specs=[tile(GROUP_W), tile(GROUP_W)],
        out_shape=[yshape, yshape],
        scratch_shapes=_mlstm_scratch(tt) + _gla_scratch(tt),
        compiler_params=_cparams(("parallel", "arbitrary")),
        name="mlstm_gla",
    )(zm, grow, cw, cb, gbc, gbr, zg, wa, ba)


CONV_HIST = 32
CONV_SUB = 64


def _conv_kernel(zc_ref, cw_ref, cb_ref, lg_ref, lb_ref, y_ref, xpad_ref, xs_ref, *, tt):
    C = GROUP_W
    span = tt + CONV_HIST - 8

    @pl.when(pl.program_id(1) == 0)
    def _():
        xpad_ref[0:CONV_HIST, :] = jnp.zeros((CONV_HIST, C), F32)

    val = zc_ref[0, :, 0:C]
    gate = zc_ref[0, :, C:2 * C]
    xpad_ref[CONV_HIST:CONV_HIST + tt, :] = val * _sigmoid(gate)
    for sh in range(1, 8):
        xs_ref[sh - 1, 0:span, :] = xpad_ref[pl.ds(sh, span), :]
    sub = min(CONV_SUB, tt)
    for r in range(0, tt, sub):
        acc = jnp.broadcast_to(cb_ref[...], (sub, C))
        for j in range(CONV_WIDTH):
            off = CONV_HIST - (CONV_WIDTH - 1) + j
            sh, base = off % 8, r + off - off % 8
            src = xpad_ref[base:base + sub, :] if sh == 0 else xs_ref[sh - 1, base:base + sub, :]
            acc = acc + cw_ref[j:j + 1, :] * src
        mu = jnp.mean(acc, axis=-1, keepdims=True)
        d = acc - mu
        var = jnp.mean(d * d, axis=-1, keepdims=True)
        yn = d * lax.rsqrt(var + EPS) * lg_ref[...] + lb_ref[...]
        y_ref[0, r:r + sub, :] = yn * _sigmoid(yn)
    xpad_ref[0:CONV_HIST, :] = xpad_ref[tt:tt + CONV_HIST, :]


def _conformer(zc, cw, cb, lg, lb, tt):
    b, s, _ = zc.shape
    return pl.pallas_call(
        functools.partial(_conv_kernel, tt=tt),
        grid=(b, s // tt),
        in_specs=[pl.BlockSpec((1, tt, ZC_W), lambda i, t: (i, t, 0)), _const_spec(cw.shape), _const_spec(cb.shape),
                  _const_spec(lg.shape), _const_spec(lb.shape)],
        out_specs=pl.BlockSpec((1, tt, GROUP_W), lambda i, t: (i, t, 0)),
        out_shape=jax.ShapeDtypeStruct((b, s, GROUP_W), F32),
        scratch_shapes=[pltpu.VMEM((tt + CONV_HIST, GROUP_W), F32), pltpu.VMEM((7, tt + CONV_HIST, GROUP_W), F32)],
        compiler_params=_cparams(("parallel", "arbitrary")),
        name="conformer",
    )(zc, cw, cb, lg, lb)


MLA_HP = 128


def _mla_prep_kernel(za_ref, rc_ref, rs_ref, qg_ref, kg_ref, wqa_ref, wqb_ref, wka_ref, wkb_ref, wv_ref,
                     q_ref, k_ref, v_ref):
    cq = za_ref[0, :, 0:256]
    ckv = za_ref[0, :, 256:384]
    kr = za_ref[0, :, 384:512]
    qn = (cq * lax.rsqrt(jnp.sum(cq * cq, axis=-1, keepdims=True) * (1.0 / MLA_Q_RANK) + EPS) * qg_ref[...]).astype(BF16)
    kvn = ckv * lax.rsqrt(jnp.mean(ckv * ckv, axis=-1, keepdims=True) + EPS) * kg_ref[...]
    kin = jnp.concatenate([kvn, kr], axis=1).astype(BF16)
    cos = rc_ref[0]
    sin = rs_ref[0]
    qa = jnp.dot(qn, wqa_ref[...], preferred_element_type=F32)
    qb = jnp.dot(qn, wqb_ref[...], preferred_element_type=F32)
    ka = jnp.dot(kin, wka_ref[...], preferred_element_type=F32)
    kb = jnp.dot(kin, wkb_ref[...], preferred_element_type=F32)
    vt = lax.dot_general(wv_ref[...], kin, (((1,), (1,)), ((), ())), preferred_element_type=F32)
    scale = (MLA_NOPE + MLA_ROPE) ** -0.5 * LOG2E
    for h in range(MLA_HEADS):
        sl = slice(h * MLA_HP, (h + 1) * MLA_HP)
        q_ref[0, h] = ((qa[:, sl] * cos + qb[:, sl] * sin) * scale).astype(BF16)
        k_ref[0, h] = (ka[:, sl] * cos + kb[:, sl] * sin).astype(BF16)
        v_ref[0, h] = vt[h * MLA_V:(h + 1) * MLA_V, :].astype(BF16)


def _mla_prep(za, rc, rs, qg, kg, wqa, wqb, wka, wkb, wv, tt):
    b, s, _ = za.shape
    hspec = pl.BlockSpec((1, MLA_HEADS, tt, MLA_HP), lambda i, t: (i, 0, t, 0))
    tspec = pl.BlockSpec((1, tt, MLA_HP), lambda i, t: (i, t, 0))
    hshape = jax.ShapeDtypeStruct((b, MLA_HEADS, s, MLA_HP), BF16)
    vspec = pl.BlockSpec((1, MLA_HEADS, MLA_V, tt), lambda i, t: (i, 0, 0, t))
    vshape = jax.ShapeDtypeStruct((b, MLA_HEADS, MLA_V, s), BF16)
    return pl.pallas_call(
        _mla_prep_kernel,
        grid=(b, s // tt),
        in_specs=[pl.BlockSpec((1, tt, ZA_W), lambda i, t: (i, t, 0)), tspec, tspec,
                  _const_spec(qg.shape), _const_spec(kg.shape), _const_spec(wqa.shape), _const_spec(wqb.shape),
                  _const_spec(wka.shape), _const_spec(wkb.shape), _const_spec(wv.shape)],
        out_specs=[hspec, hspec, vspec],
        out_shape=[hshape, hshape, vshape],
        compiler_params=_cparams(("parallel", "parallel")),
        name="mla_prep",
    )(za, rc, rs, qg, kg, wqa, wqb, wka, wkb, wv)


def _mla_attn_kernel(qi_ref, ki_ref, q_ref, k_ref, v_ref, y_ref, m_ref, l_ref, acc_ref, *, tq):
    qi = qi_ref[pl.program_id(1)]
    ki = ki_ref[pl.program_id(1)]

    @pl.when(ki == 0)
    def _():
        m_ref[...] = jnp.full_like(m_ref, -jnp.inf)
        l_ref[...] = jnp.zeros_like(l_ref)
        acc_ref[...] = jnp.zeros_like(acc_ref)

    def step(masked):
        def scores(h):
            s = lax.dot_general(k_ref[0, h], q_ref[0, h], (((1,), (1,)), ((), ())), preferred_element_type=F32)
            if masked:
                s = jnp.where(_iota((tq, tq), 0) <= _iota((tq, tq), 1), s, -jnp.inf)
            return s

        def softmax(h, s):
            m_prev = m_ref[h]
            m_new = jnp.maximum(m_prev, jnp.max(s, axis=0, keepdims=True))
            alpha = jnp.exp2(m_prev - m_new)
            p = jnp.exp2(s - m_new)
            l_ref[h] = alpha * l_ref[h] + jnp.sum(p, axis=0, keepdims=True)
            m_ref[h] = m_new
            return p.astype(BF16), alpha

        def values(h, p, alpha):
            acc_ref[h] = alpha * acc_ref[h] + jnp.dot(v_ref[0, h], p, preferred_element_type=F32)

        s_next = scores(0)
        for h in range(MLA_HEADS):
            s_cur = s_next
            if h + 1 < MLA_HEADS:
                s_next = scores(h + 1)
            values(h, *softmax(h, s_cur))

    @pl.when(ki < qi)
    def _():
        step(False)

    @pl.when(ki == qi)
    def _():
        step(True)
        ot = jnp.concatenate([acc_ref[h] * (1.0 / l_ref[h]) for h in range(MLA_HEADS)], axis=0)
        y_ref[0] = ot.T


def _mla_attn(q, k, v, tq):
    b, _, s, _ = q.shape
    nq = s // tq
    pairs = [(qi, ki) for qi in range(nq) for ki in range(qi + 1)]
    qi_tab = jnp.asarray([p[0] for p in pairs], jnp.int32)
    ki_tab = jnp.asarray([p[1] for p in pairs], jnp.int32)
    grid_spec = pltpu.PrefetchScalarGridSpec(
        num_scalar_prefetch=2,
        grid=(b, len(pairs)),
        in_specs=[pl.BlockSpec((1, MLA_HEADS, tq, MLA_HP), lambda i, p, qt, kt: (i, 0, qt[p], 0)),
                  pl.BlockSpec((1, MLA_HEADS, tq, MLA_HP), lambda i, p, qt, kt: (i, 0, kt[p], 0)),
                  pl.BlockSpec((1, MLA_HEADS, MLA_V, tq), lambda i, p, qt, kt: (i, 0, 0, kt[p]))],
        out_specs=pl.BlockSpec((1, tq, GROUP_W), lambda i, p, qt, kt: (i, qt[p], 0)),
        scratch_shapes=[pltpu.VMEM((MLA_HEADS, 1, tq), F32), pltpu.VMEM((MLA_HEADS, 1, tq), F32),
                        pltpu.VMEM((MLA_HEADS, MLA_V, tq), F32)],
    )
    return pl.pallas_call(
        functools.partial(_mla_attn_kernel, tq=tq),
        grid_spec=grid_spec,
        out_shape=jax.ShapeDtypeStruct((b, s, GROUP_W), F32),
        compiler_params=_cparams(("parallel", "arbitrary")),
        name="mla_attn",
    )(qi_tab, ki_tab, q, k, v)


def _outproj_kernel(*refs, routed):
    if routed:
        ym, ya, yg, yc, h_ref, fg_ref, wo_ref, n2_ref, rt_ref, hn_ref, u_ref, route_ref = refs
    else:
        ym, ya, yg, yc, h_ref, fg_ref, wo_ref, n2_ref, hn_ref, u_ref = refs
    parts = []
    for gi, y_ref in enumerate((ym, ya, yg, yc)):
        y = y_ref[...]
        yn = y * lax.rsqrt(jnp.mean(y * y, axis=-1, keepdims=True) + EPS)
        parts.append((yn * fg_ref[:, gi * GROUP_W:(gi + 1) * GROUP_W]).astype(BF16))
    ycat = jnp.concatenate(parts, axis=1)
    hn = h_ref[...] + jnp.dot(ycat, wo_ref[...], preferred_element_type=F32)
    hn_ref[...] = hn
    u = hn * lax.rsqrt(jnp.mean(hn * hn, axis=-1, keepdims=True) + EPS) * n2_ref[...]
    u_ref[...] = u.astype(u_ref.dtype)
    if routed:
        tm = u.shape[0]
        lane = _iota((tm, LANES), 1)
        u_hi = u.astype(BF16)
        u_lo = (u - u_hi.astype(F32)).astype(BF16)
        rt = rt_ref[...]
        r_hi = rt.astype(BF16)
        r_lo = (rt - r_hi.astype(F32)).astype(BF16)
        d = lambda a, b: jnp.dot(a, b, preferred_element_type=F32)
        logits = d(u_hi, r_hi) + (d(u_lo, r_hi) + d(u_hi, r_lo))
        logits = jnp.where(lane < N_EXPERTS, logits, -jnp.inf)
        m1 = jnp.max(logits, axis=-1, keepdims=True)
        i1 = jnp.min(jnp.where(logits == m1, lane, LANES), axis=-1, keepdims=True)
        rest = jnp.where(lane == i1, -jnp.inf, logits)
        m2 = jnp.max(rest, axis=-1, keepdims=True)
        i2 = jnp.min(jnp.where(rest == m2, lane, LANES), axis=-1, keepdims=True)
        e2 = jnp.exp(m2 - m1)
        w1 = 1.0 / (1.0 + e2)
        w2 = e2 / (1.0 + e2)
        route_ref[...] = jnp.where(lane == 0, i1.astype(F32),
                                   jnp.where(lane == 1, i2.astype(F32),
                                             jnp.where(lane == 2, w1, jnp.where(lane == 3, w2, 0.0))))


def _outproj(ym, ya, yg, yc, h, fg, wo, n2, router, tm):
    n = h.shape[0]
    routed = router is not None
    row = lambda w: pl.BlockSpec((tm, w), lambda i: (i, 0))
    in_specs = [row(GROUP_W)] * 4 + [row(D_MODEL), _const_spec(fg.shape), _const_spec(wo.shape), _const_spec(n2.shape)]
    args = [ym, ya, yg, yc, h, fg, wo, n2]
    out_specs = [row(D_MODEL), row(D_MODEL)]
    out_shape = [jax.ShapeDtypeStruct((n, D_MODEL), F32), jax.ShapeDtypeStruct((n, D_MODEL), F32 if routed else BF16)]
    if routed:
        in_specs.append(_const_spec(router.shape))
        args.append(router)
        out_specs.append(row(LANES))
        out_shape.append(jax.ShapeDtypeStruct((n, LANES), F32))
    return pl.pallas_call(
        functools.partial(_outproj_kernel, routed=routed),
        grid=(n // tm,),
        in_specs=in_specs, out_specs=out_specs, out_shape=out_shape,
        compiler_params=_cparams(("parallel",)),
        name="outproj_routed" if routed else "outproj",
    )(*args)


def _swiglu(x, wg, wu, wd):
    g = jnp.dot(x, wg, preferred_element_type=F32)
    up = jnp.dot(x, wu, preferred_element_type=F32)
    a = (g * _sigmoid(g) * up).astype(BF16)
    return jnp.dot(a, wd, preferred_element_type=F32)


def _ffn_kernel(u_ref, h_ref, wg_ref, wu_ref, wd_ref, o_ref):
    o_ref[...] = h_ref[...] + _swiglu(u_ref[...], wg_ref[...], wu_ref[...], wd_ref[...])


def _ffn(u, h, wg, wu, wd, tm):
    n = h.shape[0]
    row = pl.BlockSpec((tm, D_MODEL), lambda i: (i, 0))
    return pl.pallas_call(
        _ffn_kernel,
        grid=(n // tm,),
        in_specs=[row, row, _const_spec(wg.shape), _const_spec(wu.shape), _const_spec(wd.shape)],
        out_specs=row,
        out_shape=jax.ShapeDtypeStruct((n, D_MODEL), F32),
        compiler_params=_cparams(("parallel",)),
        name="ffn_dense",
    )(u, h, wg, wu, wd)


def _moe_kernel(be_ref, tokc_ref, tokn_ref, dstp_ref, dstc_ref, u_hbm, wg_ref, wu_ref, wd_ref, y_hbm,
                xbuf, ybuf, gsem, ssem):
    i = pl.program_id(0)
    nb = pl.num_programs(0)
    bm = xbuf.shape[1]

    def gather_wait(x, s):
        pltpu.make_async_copy(u_hbm.at[pl.ds(0, bm), :], x, gsem.at[s]).wait()

    def scatter_wait(y, s):
        pltpu.make_async_copy(y, y_hbm.at[pl.ds(0, bm), :], ssem.at[s]).wait()

    def scatter_row(y, dst_ref, r, s):
        pltpu.make_async_copy(y.at[pl.ds(r, 1), :], y_hbm.at[pl.ds(dst_ref[0, 0, r], 1), :], ssem.at[s]).start()

    @pl.when(i == 0)
    def _():
        def body(r, c):
            pltpu.make_async_copy(u_hbm.at[pl.ds(tokc_ref[0, 0, r], 1), :], xbuf.at[0, pl.ds(r, 1), :],
                                  gsem.at[0]).start()
            return c
        lax.fori_loop(0, bm, body, 0, unroll=8)
        ybuf[2] = jnp.zeros((bm, D_MODEL), F32)

    c = i % 2
    n = 1 - c
    yc_i = i % 3
    yp_i = (i + 2) % 3
    xc, xn, yc, yp = xbuf.at[c], xbuf.at[n], ybuf.at[yc_i], ybuf.at[yp_i]
    gather_wait(xc, c)

    @pl.when(i >= 2)
    def _():
        scatter_wait(yc, yc_i)

    g = jnp.dot(xc[...].astype(BF16), wg_ref[0], preferred_element_type=F32)
    for r in range(bm):
        pltpu.make_async_copy(u_hbm.at[pl.ds(tokn_ref[0, 0, r], 1), :], xn.at[pl.ds(r, 1), :], gsem.at[n]).start()
    up = jnp.dot(xc[...].astype(BF16), wu_ref[0], preferred_element_type=F32)
    for r in range(bm):
        scatter_row(yp, dstp_ref, r, yp_i)
    a = (g * _sigmoid(g) * up).astype(BF16)
    yc[...] = jnp.dot(a, wd_ref[0], preferred_element_type=F32)

    @pl.when(i == nb - 1)
    def _():
        def body(r, carry):
            scatter_row(yc, dstc_ref, r, yc_i)
            return carry
        lax.fori_loop(0, bm, body, 0, unroll=8)
        scatter_wait(yc, (i + 1) % 3)
        scatter_wait(yp, yp_i)
        scatter_wait(yc, yc_i)
        gather_wait(xn, n)


def _moe(block_e, row_tok, row_dst, u, wg, wu, wd, n_rows):
    n_blocks = block_e.shape[0]
    bm = row_tok.shape[-1]
    wspec = lambda shp: pl.BlockSpec((1,) + shp, lambda i, be: (be[i], 0, 0))
    ispec = lambda f: pl.BlockSpec((1, 1, bm), f, memory_space=pltpu.SMEM)
    grid_spec = pltpu.PrefetchScalarGridSpec(
        num_scalar_prefetch=1,
        grid=(n_blocks,),
        in_specs=[ispec(lambda i, be: (i, 0, 0)),
                  ispec(lambda i, be: (jnp.minimum(i + 1, n_blocks - 1), 0, 0)),
                  ispec(lambda i, be: (i, 0, 0)),
                  ispec(lambda i, be: (i + 1, 0, 0)),
                  pl.BlockSpec(memory_space=pl.ANY),
                  wspec((D_MODEL, D_FF)), wspec((D_MODEL, D_FF)), wspec((D_FF, D_MODEL))],
        out_specs=pl.BlockSpec(memory_space=pl.ANY),
        scratch_shapes=[pltpu.VMEM((2, bm, D_MODEL), F32), pltpu.VMEM((3, bm, D_MODEL), F32),
                        pltpu.SemaphoreType.DMA((2,)), pltpu.SemaphoreType.DMA((3,))],
    )
    return pl.pallas_call(
        _moe_kernel,
        grid_spec=grid_spec,
        out_shape=jax.ShapeDtypeStruct((n_rows, D_MODEL), F32),
        compiler_params=_cparams(("arbitrary",)),
        name="moe_experts",
    )(block_e, row_tok, row_tok, row_dst, row_dst, u, wg, wu, wd)


def _combine_kernel(ya_ref, yb_ref, h_ref, route_ref, fg_ref, o_ref, *, final):
    w1 = route_ref[:, 2:3]
    w2 = route_ref[:, 3:4]
    hn = h_ref[...] + w1 * ya_ref[...] + w2 * yb_ref[...]
    if final:
        hn = hn * lax.rsqrt(jnp.mean(hn * hn, axis=-1, keepdims=True) + EPS) * fg_ref[...]
    o_ref[...] = hn


def _combine(y2, h, route, fg, tm, final):
    n = h.shape[0]
    return pl.pallas_call(
        functools.partial(_combine_kernel, final=final),
        grid=(n // tm,),
        in_specs=[pl.BlockSpec((tm, D_MODEL), lambda i: (i, 0)),
                  pl.BlockSpec((tm, D_MODEL), lambda i: (i + n // tm, 0)),
                  pl.BlockSpec((tm, D_MODEL), lambda i: (i, 0)),
                  pl.BlockSpec((tm, LANES), lambda i: (i, 0)),
                  _const_spec(fg.shape)],
        out_specs=pl.BlockSpec((tm, D_MODEL), lambda i: (i, 0)),
        out_shape=jax.ShapeDtypeStruct((n, D_MODEL), F32),
        compiler_params=_cparams(("parallel",)),
        name="moe_combine",
    )(y2, y2, h, route, fg)


def _rmsnorm_kernel(h_ref, g_ref, o_ref):
    x = h_ref[...]
    o_ref[...] = x * lax.rsqrt(jnp.mean(x * x, axis=-1, keepdims=True) + EPS) * g_ref[...]


def _final_norm(h, g, tm):
    n = h.shape[0]
    row = pl.BlockSpec((tm, D_MODEL), lambda i: (i, 0))
    return pl.pallas_call(
        _rmsnorm_kernel, grid=(n // tm,), in_specs=[row, _const_spec(g.shape)], out_specs=row,
        out_shape=jax.ShapeDtypeStruct((n, D_MODEL), F32),
        compiler_params=_cparams(("parallel",)), name="final_norm",
    )(h, g)


def _pad_cols(w, width):
    return jnp.pad(w, ((0, 0), (0, width - w.shape[1])))


def _pad_rows(w, height):
    return jnp.pad(w, ((0, height - w.shape[0]), (0, 0)))


def _inproj_weights(w):
    o = np.cumsum([0, 512, 256, 256, 8, 192, 128, 32, 128, 128, 256, 256, 16, 512])
    seg = lambda i: w[:, o[i]:o[i + 1]]
    gates = seg(3)
    wm = jnp.concatenate([seg(0), seg(1), seg(2), _pad_cols(gates[:, :4], LANES), _pad_cols(gates[:, 4:], LANES)], axis=1)
    wa = jnp.concatenate([_pad_cols(seg(4), 256), seg(5), _pad_cols(seg(6), LANES)], axis=1)
    wg = jnp.concatenate([seg(7), seg(8), seg(9), seg(10), _pad_cols(seg(11), LANES)], axis=1)
    wc = seg(12)
    return [x.astype(BF16) for x in (wm, wa, wg, wc)]


def _mla_weights(w_uq, w_ukv):
    half = MLA_ROPE // 2
    zq = jnp.zeros((MLA_Q_RANK, half), F32)
    qa, qb, ka, kb, wv = [], [], [], [], []
    eye = jnp.eye(LANES, dtype=F32)[:, :MLA_ROPE]
    zk = jnp.zeros((MLA_KV_RANK, MLA_HP), F32)
    for h in range(MLA_HEADS):
        wq = w_uq[:, h * 96:(h + 1) * 96]
        nope, r1, r2 = wq[:, :64], wq[:, 64:64 + half], wq[:, 64 + half:]
        qa.append(_pad_cols(jnp.concatenate([nope, r1, r2], axis=1), MLA_HP))
        qb.append(_pad_cols(jnp.concatenate([jnp.zeros_like(nope), -r2, r1], axis=1), MLA_HP))
        wkv = w_ukv[:, h * 128:(h + 1) * 128]
        kn, vh = wkv[:, :64], wkv[:, 64:]
        e1, e2 = eye[:, :half], eye[:, half:]
        z64 = jnp.zeros((LANES, 64), F32)
        ka.append(jnp.concatenate([_pad_cols(kn, MLA_HP), _pad_cols(jnp.concatenate([z64, e1, e2], axis=1), MLA_HP)], axis=0))
        kb.append(jnp.concatenate([zk, _pad_cols(jnp.concatenate([z64, -e2, e1], axis=1), MLA_HP)], axis=0))
        wv.append(vh.T)
    cat = lambda xs, rows: _pad_rows(jnp.concatenate(xs, axis=1), rows).astype(BF16)
    wvt = _pad_cols(jnp.concatenate(wv, axis=0), 256).astype(BF16)
    return cat(qa, 256), cat(qb, 256), cat(ka, 256), cat(kb, 256), wvt


def _rope_tables(positions):
    half = MLA_ROPE // 2
    inv_freq = 1.0 / (ROPE_BASE ** (jnp.arange(0, MLA_ROPE, 2, dtype=F32) / MLA_ROPE))
    ang = positions.astype(F32)[..., None] * inv_freq
    cos, sin = jnp.cos(ang), jnp.sin(ang)
    shp = cos.shape[:-1]
    rc = jnp.concatenate([jnp.ones(shp + (64,), F32), cos, cos, jnp.zeros(shp + (MLA_HP - 64 - 2 * half,), F32)], axis=-1)
    rs = jnp.concatenate([jnp.zeros(shp + (64,), F32), sin, sin, jnp.zeros(shp + (MLA_HP - 64 - 2 * half,), F32)], axis=-1)
    return rc, rs


def _routing_tables(route, bm):
    n = route.shape[0]
    a = n * TOP_K
    flat_e = route[:, :TOP_K].astype(jnp.int32).reshape(-1)
    onehot = (flat_e[:, None] == jnp.arange(N_EXPERTS, dtype=jnp.int32)[None, :]).astype(jnp.int32)
    csum = jnp.cumsum(onehot, axis=0)
    rank = jnp.sum(onehot * csum, axis=1) - 1
    counts = csum[-1]
    padded = (counts + bm - 1) // bm * bm
    padded_end = jnp.cumsum(padded)
    padded_start = padded_end - padded
    dest = padded_start[flat_e] + rank
    n_blocks = -(-a // bm) + N_EXPERTS
    rows = n_blocks * bm
    row_asg = jnp.full((rows,), -1, jnp.int32).at[dest].set(jnp.arange(a, dtype=jnp.int32))
    row_tok = jnp.maximum(row_asg, 0) // TOP_K
    pad_rank = jnp.cumsum((row_asg < 0).astype(jnp.int32)) - 1
    slot_row = (row_asg % TOP_K) * n + row_asg // TOP_K
    row_dst = jnp.where(row_asg >= 0, slot_row, a + pad_rank)
    first = a + (rows - a) + jnp.arange(bm, dtype=jnp.int32)
    row_dst = jnp.concatenate([first, row_dst]).reshape(n_blocks + 1, 1, bm)
    block_start = jnp.arange(n_blocks, dtype=jnp.int32) * bm
    block_e = jnp.sum((padded_end[None, :] <= block_start[:, None]).astype(jnp.int32), axis=1)
    block_e = jnp.minimum(block_e, N_EXPERTS - 1).astype(jnp.int32)
    return block_e, row_tok.reshape(n_blocks, 1, bm), row_dst, rows + bm


def kernel(x, positions, norm1_g, w_in, mlstm_conv_w, mlstm_conv_b, mlstm_gate_b, mla_q_norm_g, mla_w_uq,
           mla_kv_norm_g, mla_w_ukv, gla_w_alpha, gla_b_alpha, conv_w, conv_b, conv_ln_g, conv_ln_b, fuse_g,
           w_out, norm2_g, ffn_w_gate, ffn_w_up, ffn_w_down, moe_router, moe_w_gate, moe_w_up, moe_w_down,
           final_norm_g):
    bsz, seq, _ = x.shape
    n = bsz * seq
    depth = w_in.shape[0]
    tm = min(512, n)
    tt = min(512, seq)
    tq = min(512, seq)
    nct = seq // CHUNK
    row2 = lambda v: v.reshape(1, -1).astype(F32)

    rc, rs = _rope_tables(positions)
    h = x.reshape(n, D_MODEL)
    for l in range(depth):
        wm, wa, wg, wc = _inproj_weights(w_in[l])
        zm, za, zg, zc = _inproj(h, row2(norm1_g[l]), wm, wa, wg, wc, tm)
        zm = zm.reshape(bsz, seq, ZM_W)

        gates = jnp.concatenate([zm[:, :, 1024:1028], zm[:, :, 1152:1156]], axis=-1)
        grow = gates.reshape(bsz, nct, CHUNK, 2, MLSTM_HEADS).transpose(0, 3, 1, 4, 2).reshape(bsz, 2, nct, GROUP_W)
        gb = mlstm_gate_b[l].astype(F32)
        gbc = _pad_cols(gb.reshape(2, MLSTM_HEADS), LANES)
        gbr = jnp.repeat(gb.reshape(2, MLSTM_HEADS), CHUNK, axis=1)
        ym, yg = _recurrent_mixers(zm, grow, mlstm_conv_w[l], row2(mlstm_conv_b[l]), gbc, gbr,
                                   zg.reshape(bsz, seq, ZG_W), _pad_rows(gla_w_alpha[l], LANES),
                                   row2(gla_b_alpha[l]), tt)

        wqa, wqb, wka, wkb, wv = _mla_weights(mla_w_uq[l], mla_w_ukv[l])
        qh, kh, vh = _mla_prep(za.reshape(bsz, seq, ZA_W), rc, rs, _pad_cols(row2(mla_q_norm_g[l]), 256),
                               row2(mla_kv_norm_g[l]), wqa, wqb, wka, wkb, wv, tt)
        ya = _mla_attn(qh, kh, vh, tq)

        yc = _conformer(zc.reshape(bsz, seq, ZC_W), conv_w[l], row2(conv_b[l]), row2(conv_ln_g[l]),
                        row2(conv_ln_b[l]), tt)

        flat = lambda y: y.reshape(n, GROUP_W)
        j = l // 2
        if l % 2 == 0:
            h, u = _outproj(flat(ym), flat(ya), flat(yg), flat(yc), h, row2(fuse_g[l]), w_out[l].astype(BF16),
                            row2(norm2_g[l]), None, tm)
            h = _ffn(u, h, ffn_w_gate[j].astype(BF16), ffn_w_up[j].astype(BF16), ffn_w_down[j].astype(BF16), tm)
            if l == depth - 1:
                h = _final_norm(h, row2(final_norm_g), tm)
        else:
            h, u, route = _outproj(flat(ym), flat(ya), flat(yg), flat(yc), h, row2(fuse_g[l]),
                                   w_out[l].astype(BF16), row2(norm2_g[l]), _pad_cols(moe_router[j], LANES), tm)
            block_e, row_tok, row_dst, n_rows = _routing_tables(route, MOE_BM)
            y2 = _moe(block_e, row_tok, row_dst, u, moe_w_gate[j].astype(BF16), moe_w_up[j].astype(BF16),
                      moe_w_down[j].astype(BF16), n_rows)
            h = _combine(y2, h, route, row2(final_norm_g), tm, l == depth - 1)
    return h.reshape(bsz, seq, D_MODEL)
```

```python
import functools

import jax
import jax.numpy as jnp
import numpy as np
from jax import lax
from jax.experimental import pallas as pl
from jax.experimental.pallas import tpu as pltpu

F32 = jnp.float32
BF16 = jnp.bfloat16
HI = lax.Precision.HIGHEST

D_MODEL = 1024
GROUP_W = 256
EPS = 1e-6
LOG2E = 1.4426950408889634
LANES = 128
CHUNK = 64
MLSTM_HEADS = 4
MLSTM_CONV = 4
MLA_HEADS = 4
MLA_NOPE = 64
MLA_ROPE = 32
MLA_V = 64
MLA_Q_RANK = 192
MLA_KV_RANK = 128
ROPE_BASE = 10000.0
GLA_HEADS = 4
GLA_DK = 32
GLA_DV = 64
GLA_GATE_RANK = 16
GLA_TAU = 16.0
CONV_WIDTH = 31
D_FF = 2816
N_EXPERTS = 8
TOP_K = 2
MOE_BM = 256
VMEM_LIMIT = 56 * 1024 * 1024

ZM_W = 1280
ZA_W = 512
ZG_W = 896
ZC_W = 512


def _cparams(sem):
    return pltpu.CompilerParams(dimension_semantics=sem, vmem_limit_bytes=VMEM_LIMIT)


def _const_spec(shape):
    nd = len(shape)
    return pl.BlockSpec(shape, lambda *_: (0,) * nd, pipeline_mode=pl.Buffered(1))


def _sigmoid(x):
    return 1.0 / (1.0 + jnp.exp(-x))


def _log_sigmoid(x):
    return jnp.minimum(x, 0.0) - jnp.log(1.0 + jnp.exp(-jnp.abs(x)))


def _iota(shape, dim):
    return lax.broadcasted_iota(jnp.int32, shape, dim)


def _tri(n):
    return (_iota((n, n), 0) >= _iota((n, n), 1)).astype(BF16)


def _split3(x):
    hi = x.astype(BF16)
    r1 = x - hi.astype(F32)
    mid = r1.astype(BF16)
    lo = (r1 - mid.astype(F32)).astype(BF16)
    return hi, mid, lo


def _sel_dot(sel, x):
    hi, mid, lo = _split3(x)
    d = lambda p: jnp.dot(sel, p, preferred_element_type=F32)
    return d(hi) + d(mid) + d(lo)


def _dot_sel(x, sel):
    hi, mid, lo = _split3(x)
    d = lambda p: jnp.dot(p, sel, preferred_element_type=F32)
    return d(hi) + d(mid) + d(lo)


def _inproj_kernel(h_ref, g_ref, wm_ref, wa_ref, wg_ref, wc_ref, zm_ref, za_ref, zg_ref, zc_ref, gate_ref):
    x = h_ref[...]
    ms = jnp.mean(x * x, axis=-1, keepdims=True)
    xn = (x * lax.rsqrt(ms + EPS) * g_ref[...]).astype(BF16)
    zm = jnp.dot(xn, wm_ref[...], preferred_element_type=F32)
    zm_ref[...] = zm
    gate_ref[...] = zm[:, ZM_W - 2 * LANES:]
    za_ref[...] = jnp.dot(xn, wa_ref[...], preferred_element_type=F32)
    zg_ref[...] = jnp.dot(xn, wg_ref[...], preferred_element_type=F32)
    zc_ref[...] = jnp.dot(xn, wc_ref[...], preferred_element_type=F32)


def _inproj(h, g, wm, wa, wg, wc, tm):
    n = h.shape[0]
    row = lambda w: pl.BlockSpec((tm, w), lambda i: (i, 0))
    return pl.pallas_call(
        _inproj_kernel,
        grid=(n // tm,),
        in_specs=[row(D_MODEL), _const_spec((1, D_MODEL)), _const_spec(wm.shape), _const_spec(wa.shape),
                  _const_spec(wg.shape), _const_spec(wc.shape)],
        out_specs=[row(ZM_W), row(ZA_W), row(ZG_W), row(ZC_W), row(2 * LANES)],
        out_shape=[jax.ShapeDtypeStruct((n, w), F32) for w in (ZM_W, ZA_W, ZG_W, ZC_W, 2 * LANES)],
        compiler_params=_cparams(("parallel",)),
        name="inproj",
    )(h, g, wm, wa, wg, wc)


def _expand_heads(x, width):
    r = x.shape[0]
    lane = _iota((r, 4 * width), 1)
    out = jnp.broadcast_to(x[:, 3:4], (r, 4 * width))
    for h in (2, 1, 0):
        out = jnp.where(lane < (h + 1) * width, jnp.broadcast_to(x[:, h:h + 1], (r, 4 * width)), out)
    return out


def _mlstm_setup(zm_ref, grow_ref, cw_ref, cb_ref, gbc_ref, gbr_ref, y_ref,
                 xpad_ref, q_ref, k_ref, grs_ref, cn_ref, m_ref, cnm_ref, trit_ref, b_ref, gc_ref, cm_ref, *, tt):
    L = CHUNK
    W = MLSTM_HEADS * 64

    @pl.when(pl.program_id(1) == 0)
    def _():
        xpad_ref[0:8, :] = jnp.zeros((8, 2 * W), F32)
        cn_ref[...] = jnp.zeros_like(cn_ref)
        m_ref[...] = jnp.zeros_like(m_ref)
        hsame = (_iota((W, W), 0) // L) == (_iota((W, W), 1) // L)
        ncol = (_iota((W, LANES), 0) // L) == _iota((W, LANES), 1)
        cnm_ref[...] = jnp.concatenate([hsame, ncol], axis=1).astype(F32)
        trit_ref[...] = (hsame & ((_iota((W, W), 0) % L) <= (_iota((W, W), 1) % L))).astype(BF16)

    xpad_ref[8:8 + tt, :] = zm_ref[0, :, 0:2 * W]
    for r in range(0, tt, L):
        acc = jnp.broadcast_to(cb_ref[...], (L, 2 * W))
        for j in range(MLSTM_CONV):
            acc = acc + cw_ref[j:j + 1, :] * xpad_ref[pl.ds(r + 8 - (MLSTM_CONV - 1) + j, L), :]
        qk = acc * _sigmoid(acc)
        q_ref[r:r + L, :] = (qk[:, :W] * (64 ** -0.5)).astype(BF16)
        k_ref[r:r + L, :] = qk[:, W:]
    xpad_ref[0:8, :] = xpad_ref[tt:tt + 8, :]

    i_row = grow_ref[0, 0] + gbr_ref[0:1, :]
    lf_row = _log_sigmoid(grow_ref[0, 1] + gbr_ref[1:2, :])
    grs_ref[...] = i_row - _dot_sel(lf_row, trit_ref[...])

    tri = _tri(L)
    lf_all = _log_sigmoid(zm_ref[0, :, 4 * W + LANES:4 * W + 2 * LANES] + gbc_ref[1:2, :])
    for r in range(0, tt, L):
        b_ref[r:r + L, :] = _sel_dot(tri, lf_all[r:r + L, :])
    g_all = zm_ref[0, :, 4 * W:4 * W + LANES] + gbc_ref[0:1, :] - b_ref[...]
    gc_ref[...] = g_all
    pos = _iota((tt, LANES), 0) % L
    cm_all = g_all
    s = 1
    while s < L:
        cm_all = jnp.maximum(cm_all, jnp.where(pos >= s, pltpu.roll(cm_all, s, 0), -jnp.inf))
        s *= 2
    cm_ref[...] = cm_all

    row_t = _iota((L, W), 0)
    lane_j = _iota((L, W), 1) % L
    causal = lane_j <= row_t
    ones_col = (_iota((L, LANES), 1) < MLSTM_HEADS).astype(F32)

    def chunk(c):
        r0 = pl.multiple_of(c * L, L)
        b = b_ref[pl.ds(r0, L), :]
        g = gc_ref[pl.ds(r0, L), :]
        cm = cm_ref[pl.ds(r0, L), :]
        m_prev = m_ref[...]
        mx = jnp.maximum(m_prev, cm)
        w_inter = jnp.exp(m_prev - mx)
        b_end = b[L - 1:L, :]
        mx_end = mx[L - 1:L, :]
        m_new = b_end + mx_end
        decay = jnp.exp(m_prev - mx_end)
        kw_col = jnp.exp(g - mx_end)

        qc = q_ref[pl.ds(r0, L), :]
        kc = k_ref[pl.ds(r0, L), :]
        vc = zm_ref[0, pl.ds(r0, L), 2 * W:3 * W]
        oc = zm_ref[0, pl.ds(r0, L), 3 * W:4 * W]

        kbd = (jnp.concatenate([kc] * 4, axis=0) * cnm_ref[:, 0:W]).astype(BF16)
        s_qk = lax.dot_general(qc, kbd, (((1,), (1,)), ((), ())), preferred_element_type=F32)
        inter = jnp.dot(qc, cn_ref[...].astype(BF16), preferred_element_type=F32)
        vaug = jnp.concatenate([vc, ones_col], axis=1)
        kw = (_expand_heads(kw_col, L) * kc).astype(BF16)
        upd = lax.dot_general(kw, vaug.astype(BF16), (((0,), (0,)), ((), ())), preferred_element_type=F32)
        decay_x = jnp.concatenate([_expand_heads(decay, L), decay], axis=1)
        cn_ref[...] = decay_x * cn_ref[...] + upd * cnm_ref[...]
        m_ref[...] = m_new

        def output():
            g_row = grs_ref[pl.ds(c, 1), :]
            dmat = jnp.where(causal, jnp.exp(jnp.where(causal, g_row - _expand_heads(mx, L), 0.0)), 0.0)
            s_w = (s_qk * dmat).astype(BF16)
            vbd = (jnp.concatenate([vaug] * 4, axis=0) * cnm_ref[...]).astype(BF16)
            intra = jnp.dot(s_w, vbd, preferred_element_type=F32)
            num = _expand_heads(w_inter, L) * inter[:, :W] + intra[:, :W]
            den = w_inter * inter[:, W:] + intra[:, W:]
            hden = jnp.maximum(jnp.abs(den), jnp.exp(-(b + mx)))
            hout = num * _expand_heads(1.0 / hden, L)
            y_ref[0, pl.ds(r0, L), :] = _sigmoid(oc) * hout

        return output

    return chunk


def _mlstm_scratch(tt):
    nct = tt // CHUNK
    return [pltpu.VMEM((tt + 8, 2 * GROUP_W), F32), pltpu.VMEM((tt, GROUP_W), BF16),
            pltpu.VMEM((tt, GROUP_W), F32), pltpu.VMEM((nct, GROUP_W), F32),
            pltpu.VMEM((GROUP_W, GROUP_W + LANES), F32), pltpu.VMEM((1, LANES), F32),
            pltpu.VMEM((GROUP_W, GROUP_W + LANES), F32), pltpu.VMEM((GROUP_W, GROUP_W), BF16),
            pltpu.VMEM((tt, LANES), F32), pltpu.VMEM((tt, LANES), F32), pltpu.VMEM((tt, LANES), F32)]


def _gla_setup(zg_ref, wa_ref, ba_ref, y_ref, q_ref, k_ref, g_ref, o_ref, st_ref, *, tt):
    L = CHUNK
    WK = GLA_HEADS * GLA_DK
    WV = GLA_HEADS * GLA_DV

    @pl.when(pl.program_id(1) == 0)
    def _():
        st_ref[...] = jnp.zeros_like(st_ref)

    q_ref[...] = zg_ref[0, :, 0:WK] * (GLA_DK ** -0.5)
    k_ref[...] = zg_ref[0, :, WK:2 * WK]
    a = zg_ref[0, :, 2 * WK + 2 * WV:2 * WK + 2 * WV + LANES]
    pre = jnp.dot(a, wa_ref[...], precision=HI, preferred_element_type=F32) + ba_ref[...]
    g_all = _log_sigmoid(pre) * (1.0 / GLA_TAU)
    tri = _tri(L)
    for r in range(0, tt, L):
        g_ref[r:r + L, :] = _sel_dot(tri, g_all[r:r + L, :])

    hs2 = ((_iota((WK, WV), 0) // GLA_DK) == (_iota((WK, WV), 1) // GLA_DV)).astype(BF16)
    st_mask = (_iota((WV, WK), 0) // GLA_DV) == (_iota((WV, WK), 1) // GLA_DK)

    def chunk(c):
        r0 = pl.multiple_of(c * L, L)
        qc = q_ref[pl.ds(r0, L), :]
        kc = k_ref[pl.ds(r0, L), :]
        vc = zg_ref[0, pl.ds(r0, L), 2 * WK:2 * WK + WV]
        b = g_ref[pl.ds(r0, L), :]
        st = st_ref[...]
        qd = (qc * jnp.exp(b)).astype(BF16)
        o = lax.dot_general(qd, st.astype(BF16), (((1,), (1,)), ((), ())), preferred_element_type=F32)
        b_end = b[L - 1:L, :]
        kd = (kc * jnp.exp(b_end - b)).astype(BF16)
        upd = lax.dot_general(vc.astype(BF16), kd, (((0,), (0,)), ((), ())), preferred_element_type=F32)
        st_ref[...] = st * jnp.exp(b_end) + jnp.where(st_mask, upd, 0.0)

        def output(o=o):
            pieces = []
            for jb in range(L // 8):
                t0 = 8 * jb
                rows = L - t0
                qs, bs = qc[t0:, :], b[t0:, :]
                tl = _iota((8, WK), 0)
                ps = []
                for jj in range(8):
                    j = t0 + jj
                    d = bs - b[j:j + 1, :]
                    e = jnp.exp(jnp.where(tl >= jj, d[0:8, :], -jnp.inf))
                    if rows > 8:
                        e = jnp.concatenate([e, jnp.exp(d[8:, :])], axis=0)
                    ps.append((qs * kc[j:j + 1, :] * e).astype(BF16))
                rexp = jnp.dot(jnp.concatenate(ps, axis=0), hs2, preferred_element_type=F32)
                acc = rexp[0:rows, :] * vc[t0:t0 + 1, :]
                for jj in range(1, 8):
                    acc = acc + rexp[jj * rows:(jj + 1) * rows, :] * vc[t0 + jj:t0 + jj + 1, :]
                pieces.append(acc)
            bands = []
            for band in range(L // 8):
                tot = None
                for jb in range(band + 1):
                    lo = 8 * (band - jb)
                    part = pieces[jb][lo:lo + 8, :]
                    tot = part if tot is None else tot + part
                bands.append(tot)
            o = o + jnp.concatenate(bands, axis=0)
            o_ref[pl.ds(r0, L), :] = o

        return output

    def finish():
        o = o_ref[...]
        hsame = ((_iota((WV, WV), 0) // GLA_DV) == (_iota((WV, WV), 1) // GLA_DV)).astype(BF16)
        ms = _dot_sel(o * o, hsame) * (1.0 / GLA_DV)
        r = zg_ref[0, :, 2 * WK + WV:2 * WK + 2 * WV]
        y_ref[0] = o * lax.rsqrt(ms + EPS) * (r * _sigmoid(r))

    return chunk, finish


def _gla_scratch(tt):
    return [pltpu.VMEM((tt, LANES), F32), pltpu.VMEM((tt, LANES), F32), pltpu.VMEM((tt, LANES), F32),
            pltpu.VMEM((tt, GROUP_W), F32), pltpu.VMEM((GROUP_W, LANES), F32)]


def _recurrent_kernel(zm_ref, grow_ref, cw_ref, cb_ref, gbc_ref, gbr_ref, zg_ref, wa_ref, ba_ref,
                      ym_ref, yg_ref, *scratch, tt):
    n_m = len(_mlstm_scratch(tt))
    m_chunk = _mlstm_setup(zm_ref, grow_ref, cw_ref, cb_ref, gbc_ref, gbr_ref, ym_ref, *scratch[:n_m], tt=tt)
    g_chunk, g_finish = _gla_setup(zg_ref, wa_ref, ba_ref, yg_ref, *scratch[n_m:], tt=tt)

    def both(c, carry):
        m_output = m_chunk(c)
        g_output = g_chunk(c)
        m_output()
        g_output()
        return carry

    lax.fori_loop(0, tt // CHUNK, both, 0, unroll=2)
    g_finish()


def _recurrent_mixers(zm, grow, cw, cb, gbc, gbr, zg, wa, ba, tt):
    b, s, _ = zm.shape
    nct = tt // CHUNK
    tile = lambda w: pl.BlockSpec((1, tt, w), lambda i, t: (i, t, 0))
    yshape = jax.ShapeDtypeStruct((b, s, GROUP_W), F32)
    return pl.pallas_call(
        functools.partial(_recurrent_kernel, tt=tt),
        grid=(b, s // tt),
        in_specs=[tile(ZM_W), pl.BlockSpec((1, 2, nct, GROUP_W), lambda i, t: (i, 0, t, 0)),
                  _const_spec(cw.shape), _const_spec(cb.shape), _const_spec(gbc.shape), _const_spec(gbr.shape),
                  tile(ZG_W), _const_spec(wa.shape), _const_spec(ba.shape)],
        out_specs=[tile(GROUP_W), tile(GROUP_W)],
        out_shape=[yshape, yshape],
        scratch_shapes=_mlstm_scratch(tt) + _gla_scratch(tt),
        compiler_params=_cparams(("parallel", "arbitrary")),
        name="mlstm_gla",
    )(zm, grow, cw, cb, gbc, gbr, zg, wa, ba)


CONV_HIST = 32
CONV_SUB = 64


def _conv_kernel(zc_ref, cw_ref, cb_ref, lg_ref, lb_ref, y_ref, xpad_ref, xs_ref, *, tt):
    C = GROUP_W
    span = tt + CONV_HIST - 8

    @pl.when(pl.program_id(1) == 0)
    def _():
        xpad_ref[0:CONV_HIST, :] = jnp.zeros((CONV_HIST, C), F32)

    val = zc_ref[0, :, 0:C]
    gate = zc_ref[0, :, C:2 * C]
    xpad_ref[CONV_HIST:CONV_HIST + tt, :] = val * _sigmoid(gate)
    for sh in range(1, 8):
        xs_ref[sh - 1, 0:span, :] = xpad_ref[pl.ds(sh, span), :]
    sub = min(CONV_SUB, tt)
    for r in range(0, tt, sub):
        acc = jnp.broadcast_to(cb_ref[...], (sub, C))
        for j in range(CONV_WIDTH):
            off = CONV_HIST - (CONV_WIDTH - 1) + j
            sh, base = off % 8, r + off - off % 8
            src = xpad_ref[base:base + sub, :] if sh == 0 else xs_ref[sh - 1, base:base + sub, :]
            acc = acc + cw_ref[j:j + 1, :] * src
        mu = jnp.mean(acc, axis=-1, keepdims=True)
        d = acc - mu
        var = jnp.mean(d * d, axis=-1, keepdims=True)
        yn = d * lax.rsqrt(var + EPS) * lg_ref[...] + lb_ref[...]
        y_ref[0, r:r + sub, :] = yn * _sigmoid(yn)
    xpad_ref[0:CONV_HIST, :] = xpad_ref[tt:tt + CONV_HIST, :]


def _conformer(zc, cw, cb, lg, lb, tt):
    b, s, _ = zc.shape
    return pl.pallas_call(
        functools.partial(_conv_kernel, tt=tt),
        grid=(b, s // tt),
        in_specs=[pl.BlockSpec((1, tt, ZC_W), lambda i, t: (i, t, 0)), _const_spec(cw.shape), _const_spec(cb.shape),
                  _const_spec(lg.shape), _const_spec(lb.shape)],
        out_specs=pl.BlockSpec((1, tt, GROUP_W), lambda i, t: (i, t, 0)),
        out_shape=jax.ShapeDtypeStruct((b, s, GROUP_W), F32),
        scratch_shapes=[pltpu.VMEM((tt + CONV_HIST, GROUP_W), F32), pltpu.VMEM((7, tt + CONV_HIST, GROUP_W), F32)],
        compiler_params=_cparams(("parallel", "arbitrary")),
        name="conformer",
    )(zc, cw, cb, lg, lb)


MLA_HP = 128
MLA_VP = 80


def _mla_prep_kernel(za_ref, rc_ref, rs_ref, qg_ref, kg_ref, wqa_ref, wqb_ref, wka_ref, wkb_ref, wv_ref,
                     q_ref, k_ref, v_ref):
    cq = za_ref[0, :, 0:256]
    ckv = za_ref[0, :, 256:384]
    kr = za_ref[0, :, 384:512]
    qn = (cq * lax.rsqrt(jnp.sum(cq * cq, axis=-1, keepdims=True) * (1.0 / MLA_Q_RANK) + EPS) * qg_ref[...]).astype(BF16)
    kvn = ckv * lax.rsqrt(jnp.mean(ckv * ckv, axis=-1, keepdims=True) + EPS) * kg_ref[...]
    kin = jnp.concatenate([kvn, kr], axis=1).astype(BF16)
    cos = rc_ref[0]
    sin = rs_ref[0]
    qa = jnp.dot(qn, wqa_ref[...], preferred_element_type=F32)
    qb = jnp.dot(qn, wqb_ref[...], preferred_element_type=F32)
    ka = jnp.dot(kin, wka_ref[...], preferred_element_type=F32)
    kb = jnp.dot(kin, wkb_ref[...], preferred_element_type=F32)
    vt = lax.dot_general(wv_ref[...], kin, (((1,), (1,)), ((), ())), preferred_element_type=F32)
    scale = (MLA_NOPE + MLA_ROPE) ** -0.5 * LOG2E
    for h in range(MLA_HEADS):
        sl = slice(h * MLA_HP, (h + 1) * MLA_HP)
        q_ref[0, h] = ((qa[:, sl] * cos + qb[:, sl] * sin) * scale).astype(BF16)
        k_ref[0, h] = (ka[:, sl] * cos + kb[:, sl] * sin).astype(BF16)
        v_ref[0, h, 0:MLA_V, :] = vt[h * MLA_V:(h + 1) * MLA_V, :].astype(BF16)
        v_ref[0, h, MLA_V:MLA_VP, :] = jnp.ones((MLA_VP - MLA_V, vt.shape[1]), BF16)


def _mla_prep(za, rc, rs, qg, kg, wqa, wqb, wka, wkb, wv, tt):
    b, s, _ = za.shape
    hspec = pl.BlockSpec((1, MLA_HEADS, tt, MLA_HP), lambda i, t: (i, 0, t, 0))
    tspec = pl.BlockSpec((1, tt, MLA_HP), lambda i, t: (i, t, 0))
    hshape = jax.ShapeDtypeStruct((b, MLA_HEADS, s, MLA_HP), BF16)
    vspec = pl.BlockSpec((1, MLA_HEADS, MLA_VP, tt), lambda i, t: (i, 0, 0, t))
    vshape = jax.ShapeDtypeStruct((b, MLA_HEADS, MLA_VP, s), BF16)
    return pl.pallas_call(
        _mla_prep_kernel,
        grid=(b, s // tt),
        in_specs=[pl.BlockSpec((1, tt, ZA_W), lambda i, t: (i, t, 0)), tspec, tspec,
                  _const_spec(qg.shape), _const_spec(kg.shape), _const_spec(wqa.shape), _const_spec(wqb.shape),
                  _const_spec(wka.shape), _const_spec(wkb.shape), _const_spec(wv.shape)],
        out_specs=[hspec, hspec, vspec],
        out_shape=[hshape, hshape, vshape],
        compiler_params=_cparams(("parallel", "parallel")),
        name="mla_prep",
    )(za, rc, rs, qg, kg, wqa, wqb, wka, wkb, wv)


def _mla_attn_kernel(qi_ref, ki_ref, q_ref, k_ref, v_ref, y_ref, m_ref, acc_ref, *, tq):
    qi = qi_ref[pl.program_id(1)]
    ki = ki_ref[pl.program_id(1)]

    @pl.when(ki == 0)
    def _():
        m_ref[...] = jnp.full_like(m_ref, -jnp.inf)
        acc_ref[...] = jnp.zeros_like(acc_ref)

    def step(masked):
        def scores(h):
            s = lax.dot_general(k_ref[0, h], q_ref[0, h], (((1,), (1,)), ((), ())), preferred_element_type=F32)
            if masked:
                s = jnp.where(_iota((tq, tq), 0) <= _iota((tq, tq), 1), s, -jnp.inf)
            return s

        def softmax(h, s):
            m_prev = m_ref[h]
            m_new = jnp.maximum(m_prev, jnp.max(s, axis=0, keepdims=True))
            alpha = jnp.exp2(m_prev - m_new)
            p = jnp.exp2(s - m_new)
            m_ref[h] = m_new
            return p.astype(BF16), alpha

        def values(h, p, alpha):
            acc_ref[h] = alpha * acc_ref[h] + jnp.dot(v_ref[0, h], p, preferred_element_type=F32)

        s_next = scores(0)
        for h in range(MLA_HEADS):
            s_cur = s_next
            if h + 1 < MLA_HEADS:
                s_next = scores(h + 1)
            values(h, *softmax(h, s_cur))

    @pl.when(ki < qi)
    def _():
        step(False)

    @pl.when(ki == qi)
    def _():
        step(True)
        ot = jnp.concatenate([acc_ref[h, 0:MLA_V, :] * (1.0 / acc_ref[h, MLA_V:MLA_V + 1, :])
                              for h in range(MLA_HEADS)], axis=0)
        y_ref[0] = ot.T


def _mla_attn(q, k, v, tq):
    b, _, s, _ = q.shape
    nq = s // tq
    pairs = [(qi, ki) for qi in range(nq) for ki in range(qi + 1)]
    qi_tab = jnp.asarray([p[0] for p in pairs], jnp.int32)
    ki_tab = jnp.asarray([p[1] for p in pairs], jnp.int32)
    grid_spec = pltpu.PrefetchScalarGridSpec(
        num_scalar_prefetch=2,
        grid=(b, len(pairs)),
        in_specs=[pl.BlockSpec((1, MLA_HEADS, tq, MLA_HP), lambda i, p, qt, kt: (i, 0, qt[p], 0)),
                  pl.BlockSpec((1, MLA_HEADS, tq, MLA_HP), lambda i, p, qt, kt: (i, 0, kt[p], 0)),
                  pl.BlockSpec((1, MLA_HEADS, MLA_VP, tq), lambda i, p, qt, kt: (i, 0, 0, kt[p]))],
        out_specs=pl.BlockSpec((1, tq, GROUP_W), lambda i, p, qt, kt: (i, qt[p], 0)),
        scratch_shapes=[pltpu.VMEM((MLA_HEADS, 1, tq), F32), pltpu.VMEM((MLA_HEADS, MLA_VP, tq), F32)],
    )
    return pl.pallas_call(
        functools.partial(_mla_attn_kernel, tq=tq),
        grid_spec=grid_spec,
        out_shape=jax.ShapeDtypeStruct((b, s, GROUP_W), F32),
        compiler_params=_cparams(("parallel", "arbitrary")),
        name="mla_attn",
    )(qi_tab, ki_tab, q, k, v)


def _outproj_kernel(*refs, routed):
    if routed:
        ym, ya, yg, yc, h_ref, fg_ref, wo_ref, n2_ref, rt_ref, hn_ref, u_ref, route_ref = refs
    else:
        ym, ya, yg, yc, h_ref, fg_ref, wo_ref, n2_ref, wg_ref, wu_ref, wd_ref, hn_ref = refs
    parts = []
    for gi, y_ref in enumerate((ym, ya, yg, yc)):
        y = y_ref[...]
        yn = y * lax.rsqrt(jnp.mean(y * y, axis=-1, keepdims=True) + EPS)
        parts.append((yn * fg_ref[:, gi * GROUP_W:(gi + 1) * GROUP_W]).astype(BF16))
    ycat = jnp.concatenate(parts, axis=1)
    hn = h_ref[...] + jnp.dot(ycat, wo_ref[...], preferred_element_type=F32)
    u = hn * lax.rsqrt(jnp.mean(hn * hn, axis=-1, keepdims=True) + EPS) * n2_ref[...]
    if not routed:
        hn_ref[...] = hn + _swiglu(u.astype(BF16), wg_ref[...], wu_ref[...], wd_ref[...])
        return
    hn_ref[...] = hn
    u_ref[...] = u
    if routed:
        tm = u.shape[0]
        lane = _iota((tm, LANES), 1)
        u_hi = u.astype(BF16)
        u_lo = (u - u_hi.astype(F32)).astype(BF16)
        rt = rt_ref[...]
        r_hi = rt.astype(BF16)
        r_lo = (rt - r_hi.astype(F32)).astype(BF16)
        d = lambda a, b: jnp.dot(a, b, preferred_element_type=F32)
        logits = d(u_hi, r_hi) + (d(u_lo, r_hi) + d(u_hi, r_lo))
        logits = jnp.where(lane < N_EXPERTS, logits, -jnp.inf)
        m1 = jnp.max(logits, axis=-1, keepdims=True)
        i1 = jnp.min(jnp.where(logits == m1, lane, LANES), axis=-1, keepdims=True)
        rest = jnp.where(lane == i1, -jnp.inf, logits)
        m2 = jnp.max(rest, axis=-1, keepdims=True)
        i2 = jnp.min(jnp.where(rest == m2, lane, LANES), axis=-1, keepdims=True)
        e2 = jnp.exp(m2 - m1)
        w1 = 1.0 / (1.0 + e2)
        w2 = e2 / (1.0 + e2)
        route_ref[...] = jnp.where(lane == 0, i1.astype(F32),
                                   jnp.where(lane == 1, i2.astype(F32),
                                             jnp.where(lane == 2, w1, jnp.where(lane == 3, w2, 0.0))))


def _outproj(ym, ya, yg, yc, h, fg, wo, n2, tm, router=None, ffn=None):
    n = h.shape[0]
    routed = router is not None
    row = lambda w: pl.BlockSpec((tm, w), lambda i: (i, 0))
    in_specs = [row(GROUP_W)] * 4 + [row(D_MODEL), _const_spec(fg.shape), _const_spec(wo.shape), _const_spec(n2.shape)]
    args = [ym, ya, yg, yc, h, fg, wo, n2]
    hshape = jax.ShapeDtypeStruct((n, D_MODEL), F32)
    if routed:
        in_specs.append(_const_spec(router.shape))
        args.append(router)
        out_specs = [row(D_MODEL), row(D_MODEL), row(LANES)]
        out_shape = [hshape, hshape, jax.ShapeDtypeStruct((n, LANES), F32)]
    else:
        in_specs += [_const_spec(w.shape) for w in ffn]
        args += list(ffn)
        out_specs, out_shape = row(D_MODEL), hshape
    return pl.pallas_call(
        functools.partial(_outproj_kernel, routed=routed),
        grid=(n // tm,),
        in_specs=in_specs, out_specs=out_specs, out_shape=out_shape,
        compiler_params=_cparams(("parallel",)),
        name="outproj_routed" if routed else "outproj_ffn",
    )(*args)


def _swiglu(x, wg, wu, wd):
    g = jnp.dot(x, wg, preferred_element_type=F32)
    up = jnp.dot(x, wu, preferred_element_type=F32)
    a = (g * _sigmoid(g) * up).astype(BF16)
    return jnp.dot(a, wd, preferred_element_type=F32)


def _moe_kernel(be_ref, tokc_ref, tokn_ref, dstp_ref, dstc_ref, u_hbm, wg_ref, wu_ref, wd_ref, y_hbm,
                xbuf, ybuf, gsem, ssem):
    i = pl.program_id(0)
    nb = pl.num_programs(0)
    bm = xbuf.shape[1]

    def gather_wait(x, s):
        pltpu.make_async_copy(u_hbm.at[pl.ds(0, bm), :], x, gsem.at[s]).wait()

    def scatter_wait(y, s):
        pltpu.make_async_copy(y, y_hbm.at[pl.ds(0, bm), :], ssem.at[s]).wait()

    def scatter_row(y, dst_ref, r, s):
        pltpu.make_async_copy(y.at[pl.ds(r, 1), :], y_hbm.at[pl.ds(dst_ref[0, 0, r], 1), :], ssem.at[s]).start()

    @pl.when(i == 0)
    def _():
        def body(r, c):
            pltpu.make_async_copy(u_hbm.at[pl.ds(tokc_ref[0, 0, r], 1), :], xbuf.at[0, pl.ds(r, 1), :],
                                  gsem.at[0]).start()
            return c
        lax.fori_loop(0, bm, body, 0, unroll=8)
        ybuf[2] = jnp.zeros((bm, D_MODEL), F32)

    c = i % 2
    n = 1 - c
    yc_i = i % 3
    yp_i = (i + 2) % 3
    xc, xn, yc, yp = xbuf.at[c], xbuf.at[n], ybuf.at[yc_i], ybuf.at[yp_i]
    gather_wait(xc, c)

    @pl.when(i >= 2)
    def _():
        scatter_wait(yc, yc_i)

    g = jnp.dot(xc[...].astype(BF16), wg_ref[0], preferred_element_type=F32)
    for r in range(bm):
        pltpu.make_async_copy(u_hbm.at[pl.ds(tokn_ref[0, 0, r], 1), :], xn.at[pl.ds(r, 1), :], gsem.at[n]).start()
    up = jnp.dot(xc[...].astype(BF16), wu_ref[0], preferred_element_type=F32)
    for r in range(bm):
        scatter_row(yp, dstp_ref, r, yp_i)
    a = (g * _sigmoid(g) * up).astype(BF16)
    yc[...] = jnp.dot(a, wd_ref[0], preferred_element_type=F32)

    @pl.when(i == nb - 1)
    def _():
        def body(r, carry):
            scatter_row(yc, dstc_ref, r, yc_i)
            return carry
        lax.fori_loop(0, bm, body, 0, unroll=8)
        scatter_wait(yc, (i + 1) % 3)
        scatter_wait(yp, yp_i)
        scatter_wait(yc, yc_i)
        gather_wait(xn, n)


def _moe(block_e, row_tok, row_dst, u, wg, wu, wd, n_rows):
    n_blocks = block_e.shape[0]
    bm = row_tok.shape[-1]
    wspec = lambda shp: pl.BlockSpec((1,) + shp, lambda i, be: (be[i], 0, 0))
    ispec = lambda f: pl.BlockSpec((1, 1, bm), f, memory_space=pltpu.SMEM)
    grid_spec = pltpu.PrefetchScalarGridSpec(
        num_scalar_prefetch=1,
        grid=(n_blocks,),
        in_specs=[ispec(lambda i, be: (i, 0, 0)),
                  ispec(lambda i, be: (jnp.minimum(i + 1, n_blocks - 1), 0, 0)),
                  ispec(lambda i, be: (i, 0, 0)),
                  ispec(lambda i, be: (i + 1, 0, 0)),
                  pl.BlockSpec(memory_space=pl.ANY),
                  wspec((D_MODEL, D_FF)), wspec((D_MODEL, D_FF)), wspec((D_FF, D_MODEL))],
        out_specs=pl.BlockSpec(memory_space=pl.ANY),
        scratch_shapes=[pltpu.VMEM((2, bm, D_MODEL), F32), pltpu.VMEM((3, bm, D_MODEL), F32),
                        pltpu.SemaphoreType.DMA((2,)), pltpu.SemaphoreType.DMA((3,))],
    )
    return pl.pallas_call(
        _moe_kernel,
        grid_spec=grid_spec,
        out_shape=jax.ShapeDtypeStruct((n_rows, D_MODEL), F32),
        compiler_params=_cparams(("arbitrary",)),
        name="moe_experts",
    )(block_e, row_tok, row_tok, row_dst, row_dst, u, wg, wu, wd)


def _combine_kernel(ya_ref, yb_ref, h_ref, route_ref, fg_ref, o_ref, *, final):
    w1 = route_ref[:, 2:3]
    w2 = route_ref[:, 3:4]
    hn = h_ref[...] + w1 * ya_ref[...] + w2 * yb_ref[...]
    if final:
        hn = hn * lax.rsqrt(jnp.mean(hn * hn, axis=-1, keepdims=True) + EPS) * fg_ref[...]
    o_ref[...] = hn


def _combine(y2, h, route, fg, tm, final):
    n = h.shape[0]
    return pl.pallas_call(
        functools.partial(_combine_kernel, final=final),
        grid=(n // tm,),
        in_specs=[pl.BlockSpec((tm, D_MODEL), lambda i: (i, 0)),
                  pl.BlockSpec((tm, D_MODEL), lambda i: (i + n // tm, 0)),
                  pl.BlockSpec((tm, D_MODEL), lambda i: (i, 0)),
                  pl.BlockSpec((tm, LANES), lambda i: (i, 0)),
                  _const_spec(fg.shape)],
        out_specs=pl.BlockSpec((tm, D_MODEL), lambda i: (i, 0)),
        out_shape=jax.ShapeDtypeStruct((n, D_MODEL), F32),
        compiler_params=_cparams(("parallel",)),
        name="moe_combine",
    )(y2, y2, h, route, fg)


def _rmsnorm_kernel(h_ref, g_ref, o_ref):
    x = h_ref[...]
    o_ref[...] = x * lax.rsqrt(jnp.mean(x * x, axis=-1, keepdims=True) + EPS) * g_ref[...]


def _final_norm(h, g, tm):
    n = h.shape[0]
    row = pl.BlockSpec((tm, D_MODEL), lambda i: (i, 0))
    return pl.pallas_call(
        _rmsnorm_kernel, grid=(n // tm,), in_specs=[row, _const_spec(g.shape)], out_specs=row,
        out_shape=jax.ShapeDtypeStruct((n, D_MODEL), F32),
        compiler_params=_cparams(("parallel",)), name="final_norm",
    )(h, g)


def _pad_cols(w, width):
    return jnp.pad(w, ((0, 0), (0, width - w.shape[1])))


def _pad_rows(w, height):
    return jnp.pad(w, ((0, height - w.shape[0]), (0, 0)))


def _inproj_weights(w):
    o = np.cumsum([0, 512, 256, 256, 8, 192, 128, 32, 128, 128, 256, 256, 16, 512])
    seg = lambda i: w[:, o[i]:o[i + 1]]
    gates = seg(3)
    wm = jnp.concatenate([seg(0), seg(1), seg(2), _pad_cols(gates[:, :4], LANES), _pad_cols(gates[:, 4:], LANES)], axis=1)
    wa = jnp.concatenate([_pad_cols(seg(4), 256), seg(5), _pad_cols(seg(6), LANES)], axis=1)
    wg = jnp.concatenate([seg(7), seg(8), seg(9), seg(10), _pad_cols(seg(11), LANES)], axis=1)
    wc = seg(12)
    return [x.astype(BF16) for x in (wm, wa, wg, wc)]


def _mla_weights(w_uq, w_ukv):
    half = MLA_ROPE // 2
    zq = jnp.zeros((MLA_Q_RANK, half), F32)
    qa, qb, ka, kb, wv = [], [], [], [], []
    eye = jnp.eye(LANES, dtype=F32)[:, :MLA_ROPE]
    zk = jnp.zeros((MLA_KV_RANK, MLA_HP), F32)
    for h in range(MLA_HEADS):
        wq = w_uq[:, h * 96:(h + 1) * 96]
        nope, r1, r2 = wq[:, :64], wq[:, 64:64 + half], wq[:, 64 + half:]
        qa.append(_pad_cols(jnp.concatenate([nope, r1, r2], axis=1), MLA_HP))
        qb.append(_pad_cols(jnp.concatenate([jnp.zeros_like(nope), -r2, r1], axis=1), MLA_HP))
        wkv = w_ukv[:, h * 128:(h + 1) * 128]
        kn, vh = wkv[:, :64], wkv[:, 64:]
        e1, e2 = eye[:, :half], eye[:, half:]
        z64 = jnp.zeros((LANES, 64), F32)
        ka.append(jnp.concatenate([_pad_cols(kn, MLA_HP), _pad_cols(jnp.concatenate([z64, e1, e2], axis=1), MLA_HP)], axis=0))
        kb.append(jnp.concatenate([zk, _pad_cols(jnp.concatenate([z64, -e2, e1], axis=1), MLA_HP)], axis=0))
        wv.append(vh.T)
    cat = lambda xs, rows: _pad_rows(jnp.concatenate(xs, axis=1), rows).astype(BF16)
    wvt = _pad_cols(jnp.concatenate(wv, axis=0), 256).astype(BF16)
    return cat(qa, 256), cat(qb, 256), cat(ka, 256), cat(kb, 256), wvt


def _rope_tables(positions):
    half = MLA_ROPE // 2
    inv_freq = 1.0 / (ROPE_BASE ** (jnp.arange(0, MLA_ROPE, 2, dtype=F32) / MLA_ROPE))
    ang = positions.astype(F32)[..., None] * inv_freq
    cos, sin = jnp.cos(ang), jnp.sin(ang)
    shp = cos.shape[:-1]
    rc = jnp.concatenate([jnp.ones(shp + (64,), F32), cos, cos, jnp.zeros(shp + (MLA_HP - 64 - 2 * half,), F32)], axis=-1)
    rs = jnp.concatenate([jnp.zeros(shp + (64,), F32), sin, sin, jnp.zeros(shp + (MLA_HP - 64 - 2 * half,), F32)], axis=-1)
    return rc, rs


def _routing_tables(route, bm):
    n = route.shape[0]
    a = n * TOP_K
    flat_e = route[:, :TOP_K].astype(jnp.int32).reshape(-1)
    onehot = (flat_e[:, None] == jnp.arange(N_EXPERTS, dtype=jnp.int32)[None, :]).astype(jnp.int32)
    csum = jnp.cumsum(onehot, axis=0)
    rank = jnp.sum(onehot * csum, axis=1) - 1
    counts = csum[-1]
    padded = (counts + bm - 1) // bm * bm
    padded_end = jnp.cumsum(padded)
    padded_start = padded_end - padded
    dest = padded_start[flat_e] + rank
    n_blocks = -(-a // bm) + N_EXPERTS
    rows = n_blocks * bm
    row_asg = jnp.full((rows,), -1, jnp.int32).at[dest].set(jnp.arange(a, dtype=jnp.int32))
    row_tok = jnp.maximum(row_asg, 0) // TOP_K
    pad_rank = jnp.cumsum((row_asg < 0).astype(jnp.int32)) - 1
    slot_row = (row_asg % TOP_K) * n + row_asg // TOP_K
    row_dst = jnp.where(row_asg >= 0, slot_row, a + pad_rank)
    first = a + (rows - a) + jnp.arange(bm, dtype=jnp.int32)
    row_dst = jnp.concatenate([first, row_dst]).reshape(n_blocks + 1, 1, bm)
    block_start = jnp.arange(n_blocks, dtype=jnp.int32) * bm
    block_e = jnp.sum((padded_end[None, :] <= block_start[:, None]).astype(jnp.int32), axis=1)
    block_e = jnp.minimum(block_e, N_EXPERTS - 1).astype(jnp.int32)
    return block_e, row_tok.reshape(n_blocks, 1, bm), row_dst, rows + bm


def kernel(x, positions, norm1_g, w_in, mlstm_conv_w, mlstm_conv_b, mlstm_gate_b, mla_q_norm_g, mla_w_uq,
           mla_kv_norm_g, mla_w_ukv, gla_w_alpha, gla_b_alpha, conv_w, conv_b, conv_ln_g, conv_ln_b, fuse_g,
           w_out, norm2_g, ffn_w_gate, ffn_w_up, ffn_w_down, moe_router, moe_w_gate, moe_w_up, moe_w_down,
           final_norm_g):
    bsz, seq, _ = x.shape
    n = bsz * seq
    depth = w_in.shape[0]
    tm = min(512, n)
    tt = min(512, seq)
    tq = min(512, seq)
    nct = seq // CHUNK
    row2 = lambda v: v.reshape(1, -1).astype(F32)

    rc, rs = _rope_tables(positions)
    h = x.reshape(n, D_MODEL)
    for l in range(depth):
        wm, wa, wg, wc = _inproj_weights(w_in[l])
        zm, za, zg, zc, zgate = _inproj(h, row2(norm1_g[l]), wm, wa, wg, wc, tm)
        zm = zm.reshape(bsz, seq, ZM_W)

        gates = jnp.concatenate([zgate[:, 0:MLSTM_HEADS], zgate[:, LANES:LANES + MLSTM_HEADS]], axis=-1)
        grow = gates.reshape(bsz, nct, CHUNK, 2, MLSTM_HEADS).transpose(0, 3, 1, 4, 2).reshape(bsz, 2, nct, GROUP_W)
        gb = mlstm_gate_b[l].astype(F32)
        gbc = _pad_cols(gb.reshape(2, MLSTM_HEADS), LANES)
        gbr = jnp.repeat(gb.reshape(2, MLSTM_HEADS), CHUNK, axis=1)
        ym, yg = _recurrent_mixers(zm, grow, mlstm_conv_w[l], row2(mlstm_conv_b[l]), gbc, gbr,
                                   zg.reshape(bsz, seq, ZG_W), _pad_rows(gla_w_alpha[l], LANES),
                                   row2(gla_b_alpha[l]), tt)

        wqa, wqb, wka, wkb, wv = _mla_weights(mla_w_uq[l], mla_w_ukv[l])
        qh, kh, vh = _mla_prep(za.reshape(bsz, seq, ZA_W), rc, rs, _pad_cols(row2(mla_q_norm_g[l]), 256),
                               row2(mla_kv_norm_g[l]), wqa, wqb, wka, wkb, wv, tt)
        ya = _mla_attn(qh, kh, vh, tq)

        yc = _conformer(zc.reshape(bsz, seq, ZC_W), conv_w[l], row2(conv_b[l]), row2(conv_ln_g[l]),
                        row2(conv_ln_b[l]), tt)

        flat = lambda y: y.reshape(n, GROUP_W)
        j = l // 2
        if l % 2 == 0:
            ffn = (ffn_w_gate[j].astype(BF16), ffn_w_up[j].astype(BF16), ffn_w_down[j].astype(BF16))
            h = _outproj(flat(ym), flat(ya), flat(yg), flat(yc), h, row2(fuse_g[l]), w_out[l].astype(BF16),
                         row2(norm2_g[l]), tm, ffn=ffn)
            if l == depth - 1:
                h = _final_norm(h, row2(final_norm_g), tm)
        else:
            h, u, route = _outproj(flat(ym), flat(ya), flat(yg), flat(yc), h, row2(fuse_g[l]),
                                   w_out[l].astype(BF16), row2(norm2_g[l]), tm,
                                   router=_pad_cols(moe_router[j], LANES))
            block_e, row_tok, row_dst, n_rows = _routing_tables(route, MOE_BM)
            y2 = _moe(block_e, row_tok, row_dst, u, moe_w_gate[j].astype(BF16), moe_w_up[j].astype(BF16),
                      moe_w_down[j].astype(BF16), n_rows)
            h = _combine(y2, h, route, row2(final_norm_g), tm, l == depth - 1)
    return h.reshape(bsz, seq, D_MODEL)
```

```python
import functools

import jax
import jax.numpy as jnp
import numpy as np
from jax import lax
from jax.experimental import pallas as pl
from jax.experimental.pallas import tpu as pltpu

F32 = jnp.float32
BF16 = jnp.bfloat16
HI = lax.Precision.HIGHEST

D_MODEL = 1024
GROUP_W = 256
EPS = 1e-6
LOG2E = 1.4426950408889634
LANES = 128
CHUNK = 64
MLSTM_HEADS = 4
MLSTM_CONV = 4
MLA_HEADS = 4
MLA_NOPE = 64
MLA_ROPE = 32
MLA_V = 64
MLA_Q_RANK = 192
MLA_KV_RANK = 128
ROPE_BASE = 10000.0
GLA_HEADS = 4
GLA_DK = 32
GLA_DV = 64
GLA_GATE_RANK = 16
GLA_TAU = 16.0
GLA_SUB = 16
CONV_WIDTH = 31
D_FF = 2816
N_EXPERTS = 8
TOP_K = 2
MOE_BM = 256
VMEM_LIMIT = 56 * 1024 * 1024

ZM_W = 1280
ZA_W = 512
ZG_W = 896
ZC_W = 512


def _cparams(sem):
    return pltpu.CompilerParams(dimension_semantics=sem, vmem_limit_bytes=VMEM_LIMIT)


def _const_spec(shape):
    nd = len(shape)
    return pl.BlockSpec(shape, lambda *_: (0,) * nd, pipeline_mode=pl.Buffered(1))


def _sigmoid(x):
    return 1.0 / (1.0 + jnp.exp(-x))


def _log_sigmoid(x):
    return jnp.minimum(x, 0.0) - jnp.log(1.0 + jnp.exp(-jnp.abs(x)))


def _iota(shape, dim):
    return lax.broadcasted_iota(jnp.int32, shape, dim)


def _tri(n):
    return (_iota((n, n), 0) >= _iota((n, n), 1)).astype(BF16)


def _split3(x):
    hi = x.astype(BF16)
    r1 = x - hi.astype(F32)
    mid = r1.astype(BF16)
    lo = (r1 - mid.astype(F32)).astype(BF16)
    return hi, mid, lo


def _sel_dot(sel, x):
    hi, mid, lo = _split3(x)
    d = lambda p: jnp.dot(sel, p, preferred_element_type=F32)
    return d(hi) + d(mid) + d(lo)


def _dot_sel(x, sel):
    hi, mid, lo = _split3(x)
    d = lambda p: jnp.dot(p, sel, preferred_element_type=F32)
    return d(hi) + d(mid) + d(lo)


def _inproj_kernel(h_ref, g_ref, wm_ref, wa_ref, wg_ref, wc_ref, zm_ref, za_ref, zg_ref, zc_ref, gate_ref):
    x = h_ref[...]
    ms = jnp.mean(x * x, axis=-1, keepdims=True)
    xn = (x * lax.rsqrt(ms + EPS) * g_ref[...]).astype(BF16)
    zm = jnp.dot(xn, wm_ref[...], preferred_element_type=F32)
    zm_ref[...] = zm
    gate_ref[...] = zm[:, ZM_W - 2 * LANES:]
    za_ref[...] = jnp.dot(xn, wa_ref[...], preferred_element_type=F32)
    zg_ref[...] = jnp.dot(xn, wg_ref[...], preferred_element_type=F32)
    zc_ref[...] = jnp.dot(xn, wc_ref[...], preferred_element_type=F32)


def _inproj(h, g, wm, wa, wg, wc, tm):
    n = h.shape[0]
    row = lambda w: pl.BlockSpec((tm, w), lambda i: (i, 0))
    return pl.pallas_call(
        _inproj_kernel,
        grid=(n // tm,),
        in_specs=[row(D_MODEL), _const_spec((1, D_MODEL)), _const_spec(wm.shape), _const_spec(wa.shape),
                  _const_spec(wg.shape), _const_spec(wc.shape)],
        out_specs=[row(ZM_W), row(ZA_W), row(ZG_W), row(ZC_W), row(2 * LANES)],
        out_shape=[jax.ShapeDtypeStruct((n, w), F32) for w in (ZM_W, ZA_W, ZG_W, ZC_W, 2 * LANES)],
        compiler_params=_cparams(("parallel",)),
        name="inproj",
    )(h, g, wm, wa, wg, wc)


def _expand_heads(x, width):
    r = x.shape[0]
    lane = _iota((r, 4 * width), 1)
    out = jnp.broadcast_to(x[:, 3:4], (r, 4 * width))
    for h in (2, 1, 0):
        out = jnp.where(lane < (h + 1) * width, jnp.broadcast_to(x[:, h:h + 1], (r, 4 * width)), out)
    return out


def _mlstm_setup(zm_ref, grow_ref, cw_ref, cb_ref, gbc_ref, gbr_ref, y_ref,
                 xpad_ref, q_ref, k_ref, grs_ref, cn_ref, m_ref, cnm_ref, trit_ref, b_ref, gc_ref, cm_ref, *, tt):
    L = CHUNK
    W = MLSTM_HEADS * 64

    @pl.when(pl.program_id(1) == 0)
    def _():
        xpad_ref[0:8, :] = jnp.zeros((8, 2 * W), F32)
        cn_ref[...] = jnp.zeros_like(cn_ref)
        m_ref[...] = jnp.zeros_like(m_ref)
        hsame = (_iota((W, W), 0) // L) == (_iota((W, W), 1) // L)
        ncol = (_iota((W, LANES), 0) // L) == _iota((W, LANES), 1)
        cnm_ref[...] = jnp.concatenate([hsame, ncol], axis=1).astype(F32)
        trit_ref[...] = (hsame & ((_iota((W, W), 0) % L) <= (_iota((W, W), 1) % L))).astype(BF16)

    xpad_ref[8:8 + tt, :] = zm_ref[0, :, 0:2 * W]
    for r in range(0, tt, L):
        acc = jnp.broadcast_to(cb_ref[...], (L, 2 * W))
        for j in range(MLSTM_CONV):
            acc = acc + cw_ref[j:j + 1, :] * xpad_ref[pl.ds(r + 8 - (MLSTM_CONV - 1) + j, L), :]
        qk = acc * _sigmoid(acc)
        q_ref[r:r + L, :] = (qk[:, :W] * (64 ** -0.5)).astype(BF16)
        k_ref[r:r + L, :] = qk[:, W:]
    xpad_ref[0:8, :] = xpad_ref[tt:tt + 8, :]

    i_row = grow_ref[0, 0] + gbr_ref[0:1, :]
    lf_row = _log_sigmoid(grow_ref[0, 1] + gbr_ref[1:2, :])
    grs_ref[...] = i_row - _dot_sel(lf_row, trit_ref[...])

    tri = _tri(L)
    lf_all = _log_sigmoid(zm_ref[0, :, 4 * W + LANES:4 * W + 2 * LANES] + gbc_ref[1:2, :])
    for r in range(0, tt, L):
        b_ref[r:r + L, :] = _sel_dot(tri, lf_all[r:r + L, :])
    g_all = zm_ref[0, :, 4 * W:4 * W + LANES] + gbc_ref[0:1, :] - b_ref[...]
    gc_ref[...] = g_all
    pos = _iota((tt, LANES), 0) % L
    cm_all = g_all
    s = 1
    while s < L:
        cm_all = jnp.maximum(cm_all, jnp.where(pos >= s, pltpu.roll(cm_all, s, 0), -jnp.inf))
        s *= 2
    cm_ref[...] = cm_all

    row_t = _iota((L, W), 0)
    lane_j = _iota((L, W), 1) % L
    causal = lane_j <= row_t
    ones_col = (_iota((L, LANES), 1) < MLSTM_HEADS).astype(F32)

    def chunk(c):
        r0 = pl.multiple_of(c * L, L)
        b = b_ref[pl.ds(r0, L), :]
        g = gc_ref[pl.ds(r0, L), :]
        cm = cm_ref[pl.ds(r0, L), :]
        m_prev = m_ref[...]
        mx = jnp.maximum(m_prev, cm)
        w_inter = jnp.exp(m_prev - mx)
        b_end = b[L - 1:L, :]
        mx_end = mx[L - 1:L, :]
        m_new = b_end + mx_end
        decay = jnp.exp(m_prev - mx_end)
        kw_col = jnp.exp(g - mx_end)

        qc = q_ref[pl.ds(r0, L), :]
        kc = k_ref[pl.ds(r0, L), :]
        vc = zm_ref[0, pl.ds(r0, L), 2 * W:3 * W]
        oc = zm_ref[0, pl.ds(r0, L), 3 * W:4 * W]

        kbd = (jnp.concatenate([kc] * 4, axis=0) * cnm_ref[:, 0:W]).astype(BF16)
        s_qk = lax.dot_general(qc, kbd, (((1,), (1,)), ((), ())), preferred_element_type=F32)
        inter = jnp.dot(qc, cn_ref[...].astype(BF16), preferred_element_type=F32)
        vaug = jnp.concatenate([vc, ones_col], axis=1)
        kw = (_expand_heads(kw_col, L) * kc).astype(BF16)
        upd = lax.dot_general(kw, vaug.astype(BF16), (((0,), (0,)), ((), ())), preferred_element_type=F32)
        decay_x = jnp.concatenate([_expand_heads(decay, L), decay], axis=1)
        cn_ref[...] = decay_x * cn_ref[...] + upd * cnm_ref[...]
        m_ref[...] = m_new

        def output():
            g_row = grs_ref[pl.ds(c, 1), :]
            dmat = jnp.where(causal, jnp.exp(jnp.where(causal, g_row - _expand_heads(mx, L), 0.0)), 0.0)
            s_w = (s_qk * dmat).astype(BF16)
            vbd = (jnp.concatenate([vaug] * 4, axis=0) * cnm_ref[...]).astype(BF16)
            intra = jnp.dot(s_w, vbd, preferred_element_type=F32)
            num = _expand_heads(w_inter, L) * inter[:, :W] + intra[:, :W]
            den = w_inter * inter[:, W:] + intra[:, W:]
            hden = jnp.maximum(jnp.abs(den), jnp.exp(-(b + mx)))
            hout = num * _expand_heads(1.0 / hden, L)
            y_ref[0, pl.ds(r0, L), :] = _sigmoid(oc) * hout

        return output

    return chunk


def _mlstm_scratch(tt):
    nct = tt // CHUNK
    return [pltpu.VMEM((tt + 8, 2 * GROUP_W), F32), pltpu.VMEM((tt, GROUP_W), BF16),
            pltpu.VMEM((tt, GROUP_W), F32), pltpu.VMEM((nct, GROUP_W), F32),
            pltpu.VMEM((GROUP_W, GROUP_W + LANES), F32), pltpu.VMEM((1, LANES), F32),
            pltpu.VMEM((GROUP_W, GROUP_W + LANES), F32), pltpu.VMEM((GROUP_W, GROUP_W), BF16),
            pltpu.VMEM((tt, LANES), F32), pltpu.VMEM((tt, LANES), F32), pltpu.VMEM((tt, LANES), F32)]


def _gla_setup(zg_ref, wa_ref, ba_ref, y_ref, q_ref, k_ref, g_ref, o_ref, st_ref, kmask_ref, vmask_ref, *, tt):
    L = CHUNK
    WK = GLA_HEADS * GLA_DK
    WV = GLA_HEADS * GLA_DV

    @pl.when(pl.program_id(1) == 0)
    def _():
        st_ref[...] = jnp.zeros_like(st_ref)
        kmask_ref[...] = ((_iota((WV, WK), 0) // L) == (_iota((WV, WK), 1) // GLA_DK)).astype(F32)
        vmask_ref[...] = ((_iota((WV, WV), 0) // L) == (_iota((WV, WV), 1) // GLA_DV)).astype(F32)

    q_ref[...] = zg_ref[0, :, 0:WK] * (GLA_DK ** -0.5)
    k_ref[...] = zg_ref[0, :, WK:2 * WK]
    a = zg_ref[0, :, 2 * WK + 2 * WV:2 * WK + 2 * WV + LANES]
    pre = jnp.dot(a, wa_ref[...], precision=HI, preferred_element_type=F32) + ba_ref[...]
    g_all = _log_sigmoid(pre) * (1.0 / GLA_TAU)
    tri = _tri(L)
    for r in range(0, tt, L):
        g_ref[r:r + L, :] = _sel_dot(tri, g_all[r:r + L, :])

    hs2 = ((_iota((WK, WV), 0) // GLA_DK) == (_iota((WK, WV), 1) // GLA_DV)).astype(BF16)

    def chunk(c):
        r0 = pl.multiple_of(c * L, L)
        qc = q_ref[pl.ds(r0, L), :]
        kc = k_ref[pl.ds(r0, L), :]
        vc = zg_ref[0, pl.ds(r0, L), 2 * WK:2 * WK + WV]
        b = g_ref[pl.ds(r0, L), :]
        st = st_ref[...]
        qd = (qc * jnp.exp(b)).astype(BF16)
        o = lax.dot_general(qd, st.astype(BF16), (((1,), (1,)), ((), ())), preferred_element_type=F32)
        b_end = b[L - 1:L, :]
        kd = (kc * jnp.exp(b_end - b)).astype(BF16)
        upd = lax.dot_general(vc.astype(BF16), kd, (((0,), (0,)), ((), ())), preferred_element_type=F32)
        st_ref[...] = st * jnp.exp(b_end) + upd * kmask_ref[...]

        def output(o=o):
            S = GLA_SUB
            key_pos = _iota((L, WK), 0)
            vbd = (jnp.concatenate([vc] * GLA_HEADS, axis=0) * vmask_ref[...]).astype(BF16)
            a_rows = []
            for lo in range(S, L, S):
                r = b[lo - 1:lo, :]
                qf = (qc[lo:lo + S, :] * jnp.exp(b[lo:lo + S, :] - r)).astype(BF16)
                kf = kc * jnp.exp(jnp.where(key_pos < lo, r - b, -jnp.inf))
                kbd = (jnp.concatenate([kf] * GLA_HEADS, axis=0) * kmask_ref[...]).astype(BF16)
                a_rows.append(lax.dot_general(qf, kbd, (((1,), (1,)), ((), ())), preferred_element_type=F32))
            o_off = jnp.dot(jnp.concatenate(a_rows, axis=0).astype(BF16), vbd, preferred_element_type=F32)

            tl = _iota((8, WK), 0)
            ps = []
            for lo in range(0, L, S):
                for jj in range(8):
                    j = lo + jj
                    d = b[lo:lo + S, :] - b[j:j + 1, :]
                    e = jnp.exp(jnp.concatenate([jnp.where(tl >= jj, d[0:8, :], -jnp.inf), d[8:, :]], axis=0))
                    ps.append((qc[lo:lo + S, :] * kc[j:j + 1, :] * e).astype(BF16))
                for jj in range(8):
                    j = lo + 8 + jj
                    d = b[lo + 8:lo + S, :] - b[j:j + 1, :]
                    e = jnp.exp(jnp.where(tl >= jj, d, -jnp.inf))
                    ps.append((qc[lo + 8:lo + S, :] * kc[j:j + 1, :] * e).astype(BF16))
            rexp = jnp.dot(jnp.concatenate(ps, axis=0), hs2, preferred_element_type=F32)
            bands = []
            row = 0
            for lo in range(0, L, S):
                acc_a = None
                for jj in range(8):
                    term = rexp[row:row + S, :] * vc[lo + jj:lo + jj + 1, :]
                    acc_a = term if acc_a is None else acc_a + term
                    row += S
                acc_b = None
                for jj in range(8):
                    term = rexp[row:row + 8, :] * vc[lo + 8 + jj:lo + 8 + jj + 1, :]
                    acc_b = term if acc_b is None else acc_b + term
                    row += 8
                band = jnp.concatenate([acc_a[0:8, :], acc_a[8:, :] + acc_b], axis=0)
                bands.append(band if lo == 0 else band + o_off[lo - S:lo, :])
            o_ref[pl.ds(r0, L), :] = o + jnp.concatenate(bands, axis=0)

        return output

    def finish():
        o = o_ref[...]
        hsame = ((_iota((WV, WV), 0) // GLA_DV) == (_iota((WV, WV), 1) // GLA_DV)).astype(BF16)
        ms = _dot_sel(o * o, hsame) * (1.0 / GLA_DV)
        r = zg_ref[0, :, 2 * WK + WV:2 * WK + 2 * WV]
        y_ref[0] = o * lax.rsqrt(ms + EPS) * (r * _sigmoid(r))

    return chunk, finish


def _gla_scratch(tt):
    return [pltpu.VMEM((tt, LANES), F32), pltpu.VMEM((tt, LANES), F32), pltpu.VMEM((tt, LANES), F32),
            pltpu.VMEM((tt, GROUP_W), F32), pltpu.VMEM((GROUP_W, LANES), F32),
            pltpu.VMEM((GROUP_W, LANES), F32), pltpu.VMEM((GROUP_W, GROUP_W), F32)]


def _recurrent_kernel(zm_ref, grow_ref, cw_ref, cb_ref, gbc_ref, gbr_ref, zg_ref, wa_ref, ba_ref,
                      ym_ref, yg_ref, *scratch, tt):
    n_m = len(_mlstm_scratch(tt))
    m_chunk = _mlstm_setup(zm_ref, grow_ref, cw_ref, cb_ref, gbc_ref, gbr_ref, ym_ref, *scratch[:n_m], tt=tt)
    g_chunk, g_finish = _gla_setup(zg_ref, wa_ref, ba_ref, yg_ref, *scratch[n_m:], tt=tt)

    def both(c, carry):
        m_output = m_chunk(c)
        g_output = g_chunk(c)
        m_output()
        g_output()
        return carry

    lax.fori_loop(0, tt // CHUNK, both, 0, unroll=2)
    g_finish()


def _recurrent_mixers(zm, grow, cw, cb, gbc, gbr, zg, wa, ba, tt):
    b, s, _ = zm.shape
    nct = tt // CHUNK
    tile = lambda w: pl.BlockSpec((1, tt, w), lambda i, t: (i, t, 0))
    yshape = jax.ShapeDtypeStruct((b, s, GROUP_W), F32)
    return pl.pallas_call(
        functools.partial(_recurrent_kernel, tt=tt),
        grid=(b, s // tt),
        in_specs=[tile(ZM_W), pl.BlockSpec((1, 2, nct, GROUP_W), lambda i, t: (i, 0, t, 0)),
                  _const_spec(cw.shape), _const_spec(cb.shape), _const_spec(gbc.shape), _const_spec(gbr.shape),
                  tile(ZG_W), _const_spec(wa.shape), _const_spec(ba.shape)],
        out_specs=[tile(GROUP_W), tile(GROUP_W)],
        out_shape=[yshape, yshape],
        scratch_shapes=_mlstm_scratch(tt) + _gla_scratch(tt),
        compiler_params=_cparams(("parallel", "arbitrary")),
        name="mlstm_gla",
    )(zm, grow, cw, cb, gbc, gbr, zg, wa, ba)


CONV_HIST = 32
CONV_SUB = 64


def _conv_kernel(zc_ref, cw_ref, cb_ref, lg_ref, lb_ref, y_ref, xpad_ref, xs_ref, *, tt):
    C = GROUP_W
    span = tt + CONV_HIST - 8

    @pl.when(pl.program_id(1) == 0)
    def _():
        xpad_ref[0:CONV_HIST, :] = jnp.zeros((CONV_HIST, C), F32)

    val = zc_ref[0, :, 0:C]
    gate = zc_ref[0, :, C:2 * C]
    xpad_ref[CONV_HIST:CONV_HIST + tt, :] = val * _sigmoid(gate)
    for sh in range(1, 8):
        xs_ref[sh - 1, 0:span, :] = xpad_ref[pl.ds(sh, span), :]
    sub = min(CONV_SUB, tt)
    for r in range(0, tt, sub):
        acc = jnp.broadcast_to(cb_ref[...], (sub, C))
        for j in range(CONV_WIDTH):
            off = CONV_HIST - (CONV_WIDTH - 1) + j
            sh, base = off % 8, r + off - off % 8
            src = xpad_ref[base:base + sub, :] if sh == 0 else xs_ref[sh - 1, base:base + sub, :]
            acc = acc + cw_ref[j:j + 1, :] * src
        mu = jnp.mean(acc, axis=-1, keepdims=True)
        d = acc - mu
        var = jnp.mean(d * d, axis=-1, keepdims=True)
        yn = d * lax.rsqrt(var + EPS) * lg_ref[...] + lb_ref[...]
        y_ref[0, r:r + sub, :] = yn * _sigmoid(yn)
    xpad_ref[0:CONV_HIST, :] = xpad_ref[tt:tt + CONV_HIST, :]


def _conformer(zc, cw, cb, lg, lb, tt):
    b, s, _ = zc.shape
    return pl.pallas_call(
        functools.partial(_conv_kernel, tt=tt),
        grid=(b, s // tt),
        in_specs=[pl.BlockSpec((1, tt, ZC_W), lambda i, t: (i, t, 0)), _const_spec(cw.shape), _const_spec(cb.shape),
                  _const_spec(lg.shape), _const_spec(lb.shape)],
        out_specs=pl.BlockSpec((1, tt, GROUP_W), lambda i, t: (i, t, 0)),
        out_shape=jax.ShapeDtypeStruct((b, s, GROUP_W), F32),
        scratch_shapes=[pltpu.VMEM((tt + CONV_HIST, GROUP_W), F32), pltpu.VMEM((7, tt + CONV_HIST, GROUP_W), F32)],
        compiler_params=_cparams(("parallel", "arbitrary")),
        name="conformer",
    )(zc, cw, cb, lg, lb)


MLA_HP = 128
MLA_VP = 80


def _mla_prep_kernel(za_ref, rc_ref, rs_ref, qg_ref, kg_ref, wqa_ref, wqb_ref, wka_ref, wkb_ref, wv_ref,
                     q_ref, k_ref, v_ref):
    cq = za_ref[0, :, 0:256]
    ckv = za_ref[0, :, 256:384]
    kr = za_ref[0, :, 384:512]
    qn = (cq * lax.rsqrt(jnp.sum(cq * cq, axis=-1, keepdims=True) * (1.0 / MLA_Q_RANK) + EPS) * qg_ref[...]).astype(BF16)
    kvn = ckv * lax.rsqrt(jnp.mean(ckv * ckv, axis=-1, keepdims=True) + EPS) * kg_ref[...]
    kin = jnp.concatenate([kvn, kr], axis=1).astype(BF16)
    cos = rc_ref[0]
    sin = rs_ref[0]
    qa = jnp.dot(qn, wqa_ref[...], preferred_element_type=F32)
    qb = jnp.dot(qn, wqb_ref[...], preferred_element_type=F32)
    ka = jnp.dot(kin, wka_ref[...], preferred_element_type=F32)
    kb = jnp.dot(kin, wkb_ref[...], preferred_element_type=F32)
    vt = lax.dot_general(wv_ref[...], kin, (((1,), (1,)), ((), ())), preferred_element_type=F32)
    scale = (MLA_NOPE + MLA_ROPE) ** -0.5 * LOG2E
    for h in range(MLA_HEADS):
        sl = slice(h * MLA_HP, (h + 1) * MLA_HP)
        q_ref[0, h] = ((qa[:, sl] * cos + qb[:, sl] * sin) * scale).astype(BF16)
        k_ref[0, h] = (ka[:, sl] * cos + kb[:, sl] * sin).astype(BF16)
        v_ref[0, h, 0:MLA_V, :] = vt[h * MLA_V:(h + 1) * MLA_V, :].astype(BF16)
        v_ref[0, h, MLA_V:MLA_VP, :] = jnp.ones((MLA_VP - MLA_V, vt.shape[1]), BF16)


def _mla_prep(za, rc, rs, qg, kg, wqa, wqb, wka, wkb, wv, tt):
    b, s, _ = za.shape
    hspec = pl.BlockSpec((1, MLA_HEADS, tt, MLA_HP), lambda i, t: (i, 0, t, 0))
    tspec = pl.BlockSpec((1, tt, MLA_HP), lambda i, t: (i, t, 0))
    hshape = jax.ShapeDtypeStruct((b, MLA_HEADS, s, MLA_HP), BF16)
    vspec = pl.BlockSpec((1, MLA_HEADS, MLA_VP, tt), lambda i, t: (i, 0, 0, t))
    vshape = jax.ShapeDtypeStruct((b, MLA_HEADS, MLA_VP, s), BF16)
    return pl.pallas_call(
        _mla_prep_kernel,
        grid=(b, s // tt),
        in_specs=[pl.BlockSpec((1, tt, ZA_W), lambda i, t: (i, t, 0)), tspec, tspec,
                  _const_spec(qg.shape), _const_spec(kg.shape), _const_spec(wqa.shape), _const_spec(wqb.shape),
                  _const_spec(wka.shape), _const_spec(wkb.shape), _const_spec(wv.shape)],
        out_specs=[hspec, hspec, vspec],
        out_shape=[hshape, hshape, vshape],
        compiler_params=_cparams(("parallel", "parallel")),
        name="mla_prep",
    )(za, rc, rs, qg, kg, wqa, wqb, wka, wkb, wv)


def _mla_attn_kernel(qi_ref, ki_ref, q_ref, k_ref, v_ref, y_ref, m_ref, acc_ref, *, tq):
    qi = qi_ref[pl.program_id(1)]
    ki = ki_ref[pl.program_id(1)]

    @pl.when(ki == 0)
    def _():
        m_ref[...] = jnp.full_like(m_ref, -jnp.inf)
        acc_ref[...] = jnp.zeros_like(acc_ref)

    def step(masked):
        def scores(h):
            s = lax.dot_general(k_ref[0, h], q_ref[0, h], (((1,), (1,)), ((), ())), preferred_element_type=F32)
            if masked:
                s = jnp.where(_iota((tq, tq), 0) <= _iota((tq, tq), 1), s, -jnp.inf)
            return s

        def softmax(h, s):
            m_prev = m_ref[h]
            m_new = jnp.maximum(m_prev, jnp.max(s, axis=0, keepdims=True))
            alpha = jnp.exp2(m_prev - m_new)
            p = jnp.exp2(s - m_new)
            m_ref[h] = m_new
            return p.astype(BF16), alpha

        def values(h, p, alpha):
            acc_ref[h] = alpha * acc_ref[h] + jnp.dot(v_ref[0, h], p, preferred_element_type=F32)

        s_next = scores(0)
        for h in range(MLA_HEADS):
            s_cur = s_next
            if h + 1 < MLA_HEADS:
                s_next = scores(h + 1)
            values(h, *softmax(h, s_cur))

    @pl.when(ki < qi)
    def _():
        step(False)

    @pl.when(ki == qi)
    def _():
        step(True)
        ot = jnp.concatenate([acc_ref[h, 0:MLA_V, :] * (1.0 / acc_ref[h, MLA_V:MLA_V + 1, :])
                              for h in range(MLA_HEADS)], axis=0)
        y_ref[0] = ot.T


def _mla_attn(q, k, v, tq):
    b, _, s, _ = q.shape
    nq = s // tq
    pairs = [(qi, ki) for qi in range(nq) for ki in range(qi + 1)]
    qi_tab = jnp.asarray([p[0] for p in pairs], jnp.int32)
    ki_tab = jnp.asarray([p[1] for p in pairs], jnp.int32)
    grid_spec = pltpu.PrefetchScalarGridSpec(
        num_scalar_prefetch=2,
        grid=(b, len(pairs)),
        in_specs=[pl.BlockSpec((1, MLA_HEADS, tq, MLA_HP), lambda i, p, qt, kt: (i, 0, qt[p], 0)),
                  pl.BlockSpec((1, MLA_HEADS, tq, MLA_HP), lambda i, p, qt, kt: (i, 0, kt[p], 0)),
                  pl.BlockSpec((1, MLA_HEADS, MLA_VP, tq), lambda i, p, qt, kt: (i, 0, 0, kt[p]))],
        out_specs=pl.BlockSpec((1, tq, GROUP_W), lambda i, p, qt, kt: (i, qt[p], 0)),
        scratch_shapes=[pltpu.VMEM((MLA_HEADS, 1, tq), F32), pltpu.VMEM((MLA_HEADS, MLA_VP, tq), F32)],
    )
    return pl.pallas_call(
        functools.partial(_mla_attn_kernel, tq=tq),
        grid_spec=grid_spec,
        out_shape=jax.ShapeDtypeStruct((b, s, GROUP_W), F32),
        compiler_params=_cparams(("parallel", "arbitrary")),
        name="mla_attn",
    )(qi_tab, ki_tab, q, k, v)


def _outproj_kernel(*refs, routed):
    if routed:
        ym, ya, yg, yc, h_ref, fg_ref, wo_ref, n2_ref, rt_ref, hn_ref, u_ref, route_ref = refs
    else:
        ym, ya, yg, yc, h_ref, fg_ref, wo_ref, n2_ref, wg_ref, wu_ref, wd_ref, hn_ref = refs
    parts = []
    for gi, y_ref in enumerate((ym, ya, yg, yc)):
        y = y_ref[...]
        yn = y * lax.rsqrt(jnp.mean(y * y, axis=-1, keepdims=True) + EPS)
        parts.append((yn * fg_ref[:, gi * GROUP_W:(gi + 1) * GROUP_W]).astype(BF16))
    ycat = jnp.concatenate(parts, axis=1)
    hn = h_ref[...] + jnp.dot(ycat, wo_ref[...], preferred_element_type=F32)
    u = hn * lax.rsqrt(jnp.mean(hn * hn, axis=-1, keepdims=True) + EPS) * n2_ref[...]
    if not routed:
        hn_ref[...] = hn + _swiglu(u.astype(BF16), wg_ref[...], wu_ref[...], wd_ref[...])
        return
    hn_ref[...] = hn
    u_ref[...] = u
    if routed:
        tm = u.shape[0]
        lane = _iota((tm, LANES), 1)
        u_hi = u.astype(BF16)
        u_lo = (u - u_hi.astype(F32)).astype(BF16)
        rt = rt_ref[...]
        r_hi = rt.astype(BF16)
        r_lo = (rt - r_hi.astype(F32)).astype(BF16)
        d = lambda a, b: jnp.dot(a, b, preferred_element_type=F32)
        logits = d(u_hi, r_hi) + (d(u_lo, r_hi) + d(u_hi, r_lo))
        logits = jnp.where(lane < N_EXPERTS, logits, -jnp.inf)
        m1 = jnp.max(logits, axis=-1, keepdims=True)
        i1 = jnp.min(jnp.where(logits == m1, lane, LANES), axis=-1, keepdims=True)
        rest = jnp.where(lane == i1, -jnp.inf, logits)
        m2 = jnp.max(rest, axis=-1, keepdims=True)
        i2 = jnp.min(jnp.where(rest == m2, lane, LANES), axis=-1, keepdims=True)
        e2 = jnp.exp(m2 - m1)
        w1 = 1.0 / (1.0 + e2)
        w2 = e2 / (1.0 + e2)
        route_ref[...] = jnp.where(lane == 0, i1.astype(F32),
                                   jnp.where(lane == 1, i2.astype(F32),
                                             jnp.where(lane == 2, w1, jnp.where(lane == 3, w2, 0.0))))


def _outproj(ym, ya, yg, yc, h, fg, wo, n2, tm, router=None, ffn=None):
    n = h.shape[0]
    routed = router is not None
    row = lambda w: pl.BlockSpec((tm, w), lambda i: (i, 0))
    in_specs = [row(GROUP_W)] * 4 + [row(D_MODEL), _const_spec(fg.shape), _const_spec(wo.shape), _const_spec(n2.shape)]
    args = [ym, ya, yg, yc, h, fg, wo, n2]
    hshape = jax.ShapeDtypeStruct((n, D_MODEL), F32)
    if routed:
        in_specs.append(_const_spec(router.shape))
        args.append(router)
        out_specs = [row(D_MODEL), row(D_MODEL), row(LANES)]
        out_shape = [hshape, hshape, jax.ShapeDtypeStruct((n, LANES), F32)]
    else:
        in_specs += [_const_spec(w.shape) for w in ffn]
        args += list(ffn)
        out_specs, out_shape = row(D_MODEL), hshape
    return pl.pallas_call(
        functools.partial(_outproj_kernel, routed=routed),
        grid=(n // tm,),
        in_specs=in_specs, out_specs=out_specs, out_shape=out_shape,
        compiler_params=_cparams(("parallel",)),
        name="outproj_routed" if routed else "outproj_ffn",
    )(*args)


def _swiglu(x, wg, wu, wd):
    g = jnp.dot(x, wg, preferred_element_type=F32)
    up = jnp.dot(x, wu, preferred_element_type=F32)
    a = (g * _sigmoid(g) * up).astype(BF16)
    return jnp.dot(a, wd, preferred_element_type=F32)


def _moe_kernel(be_ref, tokc_ref, tokn_ref, dstp_ref, dstc_ref, u_hbm, wg_ref, wu_ref, wd_ref, y_hbm,
                xbuf, ybuf, gsem, ssem):
    i = pl.program_id(0)
    nb = pl.num_programs(0)
    bm = xbuf.shape[1]

    def gather_wait(x, s):
        pltpu.make_async_copy(u_hbm.at[pl.ds(0, bm), :], x, gsem.at[s]).wait()

    def scatter_wait(y, s):
        pltpu.make_async_copy(y, y_hbm.at[pl.ds(0, bm), :], ssem.at[s]).wait()

    def scatter_row(y, dst_ref, r, s):
        pltpu.make_async_copy(y.at[pl.ds(r, 1), :], y_hbm.at[pl.ds(dst_ref[0, 0, r], 1), :], ssem.at[s]).start()

    @pl.when(i == 0)
    def _():
        def body(r, c):
            pltpu.make_async_copy(u_hbm.at[pl.ds(tokc_ref[0, 0, r], 1), :], xbuf.at[0, pl.ds(r, 1), :],
                                  gsem.at[0]).start()
            return c
        lax.fori_loop(0, bm, body, 0, unroll=8)
        ybuf[2] = jnp.zeros((bm, D_MODEL), F32)

    c = i % 2
    n = 1 - c
    yc_i = i % 3
    yp_i = (i + 2) % 3
    xc, xn, yc, yp = xbuf.at[c], xbuf.at[n], ybuf.at[yc_i], ybuf.at[yp_i]
    gather_wait(xc, c)

    @pl.when(i >= 2)
    def _():
        scatter_wait(yc, yc_i)

    g = jnp.dot(xc[...].astype(BF16), wg_ref[0], preferred_element_type=F32)
    for r in range(bm):
        pltpu.make_async_copy(u_hbm.at[pl.ds(tokn_ref[0, 0, r], 1), :], xn.at[pl.ds(r, 1), :], gsem.at[n]).start()
    up = jnp.dot(xc[...].astype(BF16), wu_ref[0], preferred_element_type=F32)
    for r in range(bm):
        scatter_row(yp, dstp_ref, r, yp_i)
    a = (g * _sigmoid(g) * up).astype(BF16)
    yc[...] = jnp.dot(a, wd_ref[0], preferred_element_type=F32)

    @pl.when(i == nb - 1)
    def _():
        def body(r, carry):
            scatter_row(yc, dstc_ref, r, yc_i)
            return carry
        lax.fori_loop(0, bm, body, 0, unroll=8)
        scatter_wait(yc, (i + 1) % 3)
        scatter_wait(yp, yp_i)
        scatter_wait(yc, yc_i)
        gather_wait(xn, n)


def _moe(block_e, row_tok, row_dst, u, wg, wu, wd, n_rows):
    n_blocks = block_e.shape[0]
    bm = row_tok.shape[-1]
    wspec = lambda shp: pl.BlockSpec((1,) + shp, lambda i, be: (be[i], 0, 0))
    ispec = lambda f: pl.BlockSpec((1, 1, bm), f, memory_space=pltpu.SMEM)
    grid_spec = pltpu.PrefetchScalarGridSpec(
        num_scalar_prefetch=1,
        grid=(n_blocks,),
        in_specs=[ispec(lambda i, be: (i, 0, 0)),
                  ispec(lambda i, be: (jnp.minimum(i + 1, n_blocks - 1), 0, 0)),
                  ispec(lambda i, be: (i, 0, 0)),
                  ispec(lambda i, be: (i + 1, 0, 0)),
                  pl.BlockSpec(memory_space=pl.ANY),
                  wspec((D_MODEL, D_FF)), wspec((D_MODEL, D_FF)), wspec((D_FF, D_MODEL))],
        out_specs=pl.BlockSpec(memory_space=pl.ANY),
        scratch_shapes=[pltpu.VMEM((2, bm, D_MODEL), F32), pltpu.VMEM((3, bm, D_MODEL), F32),
                        pltpu.SemaphoreType.DMA((2,)), pltpu.SemaphoreType.DMA((3,))],
    )
    return pl.pallas_call(
        _moe_kernel,
        grid_spec=grid_spec,
        out_shape=jax.ShapeDtypeStruct((n_rows, D_MODEL), F32),
        compiler_params=_cparams(("arbitrary",)),
        name="moe_experts",
    )(block_e, row_tok, row_tok, row_dst, row_dst, u, wg, wu, wd)


def _combine_kernel(ya_ref, yb_ref, h_ref, route_ref, fg_ref, o_ref, *, final):
    w1 = route_ref[:, 2:3]
    w2 = route_ref[:, 3:4]
    hn = h_ref[...] + w1 * ya_ref[...] + w2 * yb_ref[...]
    if final:
        hn = hn * lax.rsqrt(jnp.mean(hn * hn, axis=-1, keepdims=True) + EPS) * fg_ref[...]
    o_ref[...] = hn


def _combine(y2, h, route, fg, tm, final):
    n = h.shape[0]
    return pl.pallas_call(
        functools.partial(_combine_kernel, final=final),
        grid=(n // tm,),
        in_specs=[pl.BlockSpec((tm, D_MODEL), lambda i: (i, 0)),
                  pl.BlockSpec((tm, D_MODEL), lambda i: (i + n // tm, 0)),
                  pl.BlockSpec((tm, D_MODEL), lambda i: (i, 0)),
                  pl.BlockSpec((tm, LANES), lambda i: (i, 0)),
                  _const_spec(fg.shape)],
        out_specs=pl.BlockSpec((tm, D_MODEL), lambda i: (i, 0)),
        out_shape=jax.ShapeDtypeStruct((n, D_MODEL), F32),
        compiler_params=_cparams(("parallel",)),
        name="moe_combine",
    )(y2, y2, h, route, fg)


def _rmsnorm_kernel(h_ref, g_ref, o_ref):
    x = h_ref[...]
    o_ref[...] = x * lax.rsqrt(jnp.mean(x * x, axis=-1, keepdims=True) + EPS) * g_ref[...]


def _final_norm(h, g, tm):
    n = h.shape[0]
    row = pl.BlockSpec((tm, D_MODEL), lambda i: (i, 0))
    return pl.pallas_call(
        _rmsnorm_kernel, grid=(n // tm,), in_specs=[row, _const_spec(g.shape)], out_specs=row,
        out_shape=jax.ShapeDtypeStruct((n, D_MODEL), F32),
        compiler_params=_cparams(("parallel",)), name="final_norm",
    )(h, g)


def _pad_cols(w, width):
    return jnp.pad(w, ((0, 0), (0, width - w.shape[1])))


def _pad_rows(w, height):
    return jnp.pad(w, ((0, height - w.shape[0]), (0, 0)))


def _inproj_weights(w):
    o = np.cumsum([0, 512, 256, 256, 8, 192, 128, 32, 128, 128, 256, 256, 16, 512])
    seg = lambda i: w[:, o[i]:o[i + 1]]
    gates = seg(3)
    wm = jnp.concatenate([seg(0), seg(1), seg(2), _pad_cols(gates[:, :4], LANES), _pad_cols(gates[:, 4:], LANES)], axis=1)
    wa = jnp.concatenate([_pad_cols(seg(4), 256), seg(5), _pad_cols(seg(6), LANES)], axis=1)
    wg = jnp.concatenate([seg(7), seg(8), seg(9), seg(10), _pad_cols(seg(11), LANES)], axis=1)
    wc = seg(12)
    return [x.astype(BF16) for x in (wm, wa, wg, wc)]


def _mla_weights(w_uq, w_ukv):
    half = MLA_ROPE // 2
    zq = jnp.zeros((MLA_Q_RANK, half), F32)
    qa, qb, ka, kb, wv = [], [], [], [], []
    eye = jnp.eye(LANES, dtype=F32)[:, :MLA_ROPE]
    zk = jnp.zeros((MLA_KV_RANK, MLA_HP), F32)
    for h in range(MLA_HEADS):
        wq = w_uq[:, h * 96:(h + 1) * 96]
        nope, r1, r2 = wq[:, :64], wq[:, 64:64 + half], wq[:, 64 + half:]
        qa.append(_pad_cols(jnp.concatenate([nope, r1, r2], axis=1), MLA_HP))
        qb.append(_pad_cols(jnp.concatenate([jnp.zeros_like(nope), -r2, r1], axis=1), MLA_HP))
        wkv = w_ukv[:, h * 128:(h + 1) * 128]
        kn, vh = wkv[:, :64], wkv[:, 64:]
        e1, e2 = eye[:, :half], eye[:, half:]
        z64 = jnp.zeros((LANES, 64), F32)
        ka.append(jnp.concatenate([_pad_cols(kn, MLA_HP), _pad_cols(jnp.concatenate([z64, e1, e2], axis=1), MLA_HP)], axis=0))
        kb.append(jnp.concatenate([zk, _pad_cols(jnp.concatenate([z64, -e2, e1], axis=1), MLA_HP)], axis=0))
        wv.append(vh.T)
    cat = lambda xs, rows: _pad_rows(jnp.concatenate(xs, axis=1), rows).astype(BF16)
    wvt = _pad_cols(jnp.concatenate(wv, axis=0), 256).astype(BF16)
    return cat(qa, 256), cat(qb, 256), cat(ka, 256), cat(kb, 256), wvt


def _rope_tables(positions):
    half = MLA_ROPE // 2
    inv_freq = 1.0 / (ROPE_BASE ** (jnp.arange(0, MLA_ROPE, 2, dtype=F32) / MLA_ROPE))
    ang = positions.astype(F32)[..., None] * inv_freq
    cos, sin = jnp.cos(ang), jnp.sin(ang)
    shp = cos.shape[:-1]
    rc = jnp.concatenate([jnp.ones(shp + (64,), F32), cos, cos, jnp.zeros(shp + (MLA_HP - 64 - 2 * half,), F32)], axis=-1)
    rs = jnp.concatenate([jnp.zeros(shp + (64,), F32), sin, sin, jnp.zeros(shp + (MLA_HP - 64 - 2 * half,), F32)], axis=-1)
    return rc, rs


def _routing_tables(route, bm):
    n = route.shape[0]
    a = n * TOP_K
    flat_e = route[:, :TOP_K].astype(jnp.int32).reshape(-1)
    onehot = (flat_e[:, None] == jnp.arange(N_EXPERTS, dtype=jnp.int32)[None, :]).astype(jnp.int32)
    csum = jnp.cumsum(onehot, axis=0)
    rank = jnp.sum(onehot * csum, axis=1) - 1
    counts = csum[-1]
    padded = (counts + bm - 1) // bm * bm
    padded_end = jnp.cumsum(padded)
    padded_start = padded_end - padded
    dest = padded_start[flat_e] + rank
    n_blocks = -(-a // bm) + N_EXPERTS
    rows = n_blocks * bm
    row_asg = jnp.full((rows,), -1, jnp.int32).at[dest].set(jnp.arange(a, dtype=jnp.int32))
    row_tok = jnp.maximum(row_asg, 0) // TOP_K
    pad_rank = jnp.cumsum((row_asg < 0).astype(jnp.int32)) - 1
    slot_row = (row_asg % TOP_K) * n + row_asg // TOP_K
    row_dst = jnp.where(row_asg >= 0, slot_row, a + pad_rank)
    first = a + (rows - a) + jnp.arange(bm, dtype=jnp.int32)
    row_dst = jnp.concatenate([first, row_dst]).reshape(n_blocks + 1, 1, bm)
    block_start = jnp.arange(n_blocks, dtype=jnp.int32) * bm
    block_e = jnp.sum((padded_end[None, :] <= block_start[:, None]).astype(jnp.int32), axis=1)
    block_e = jnp.minimum(block_e, N_EXPERTS - 1).astype(jnp.int32)
    return block_e, row_tok.reshape(n_blocks, 1, bm), row_dst, rows + bm


def kernel(x, positions, norm1_g, w_in, mlstm_conv_w, mlstm_conv_b, mlstm_gate_b, mla_q_norm_g, mla_w_uq,
           mla_kv_norm_g, mla_w_ukv, gla_w_alpha, gla_b_alpha, conv_w, conv_b, conv_ln_g, conv_ln_b, fuse_g,
           w_out, norm2_g, ffn_w_gate, ffn_w_up, ffn_w_down, moe_router, moe_w_gate, moe_w_up, moe_w_down,
           final_norm_g):
    bsz, seq, _ = x.shape
    n = bsz * seq
    depth = w_in.shape[0]
    tm = min(512, n)
    tt = min(512, seq)
    tq = min(512, seq)
    nct = seq // CHUNK
    row2 = lambda v: v.reshape(1, -1).astype(F32)

    rc, rs = _rope_tables(positions)
    h = x.reshape(n, D_MODEL)
    for l in range(depth):
        wm, wa, wg, wc = _inproj_weights(w_in[l])
        zm, za, zg, zc, zgate = _inproj(h, row2(norm1_g[l]), wm, wa, wg, wc, tm)
        zm = zm.reshape(bsz, seq, ZM_W)

        gates = jnp.concatenate([zgate[:, 0:MLSTM_HEADS], zgate[:, LANES:LANES + MLSTM_HEADS]], axis=-1)
        grow = gates.reshape(bsz, nct, CHUNK, 2, MLSTM_HEADS).transpose(0, 3, 1, 4, 2).reshape(bsz, 2, nct, GROUP_W)
        gb = mlstm_gate_b[l].astype(F32)
        gbc = _pad_cols(gb.reshape(2, MLSTM_HEADS), LANES)
        gbr = jnp.repeat(gb.reshape(2, MLSTM_HEADS), CHUNK, axis=1)
        ym, yg = _recurrent_mixers(zm, grow, mlstm_conv_w[l], row2(mlstm_conv_b[l]), gbc, gbr,
                                   zg.reshape(bsz, seq, ZG_W), _pad_rows(gla_w_alpha[l], LANES),
                                   row2(gla_b_alpha[l]), tt)

        wqa, wqb, wka, wkb, wv = _mla_weights(mla_w_uq[l], mla_w_ukv[l])
        qh, kh, vh = _mla_prep(za.reshape(bsz, seq, ZA_W), rc, rs, _pad_cols(row2(mla_q_norm_g[l]), 256),
                               row2(mla_kv_norm_g[l]), wqa, wqb, wka, wkb, wv, tt)
        ya = _mla_attn(qh, kh, vh, tq)

        yc = _conformer(zc.reshape(bsz, seq, ZC_W), conv_w[l], row2(conv_b[l]), row2(conv_ln_g[l]),
                        row2(conv_ln_b[l]), tt)

        flat = lambda y: y.reshape(n, GROUP_W)
        j = l // 2
        if l % 2 == 0:
            ffn = (ffn_w_gate[j].astype(BF16), ffn_w_up[j].astype(BF16), ffn_w_down[j].astype(BF16))
            h = _outproj(flat(ym), flat(ya), flat(yg), flat(yc), h, row2(fuse_g[l]), w_out[l].astype(BF16),
                         row2(norm2_g[l]), tm, ffn=ffn)
            if l == depth - 1:
                h = _final_norm(h, row2(final_norm_g), tm)
        else:
            h, u, route = _outproj(flat(ym), flat(ya), flat(yg), flat(yc), h, row2(fuse_g[l]),
                                   w_out[l].astype(BF16), row2(norm2_g[l]), tm,
                                   router=_pad_cols(moe_router[j], LANES))
            block_e, row_tok, row_dst, n_rows = _routing_tables(route, MOE_BM)
            y2 = _moe(block_e, row_tok, row_dst, u, moe_w_gate[j].astype(BF16), moe_w_up[j].astype(BF16),
                      moe_w_down[j].astype(BF16), n_rows)
            h = _combine(y2, h, route, row2(final_norm_g), tm, l == depth - 1)
    return h.reshape(bsz, seq, D_MODEL)
```

```python
import functools

import jax
import jax.numpy as jnp
import numpy as np
from jax import lax
from jax.experimental import pallas as pl
from jax.experimental.pallas import tpu as pltpu

F32 = jnp.float32
BF16 = jnp.bfloat16
HI = lax.Precision.HIGHEST

D_MODEL = 1024
GROUP_W = 256
EPS = 1e-6
LOG2E = 1.4426950408889634
LANES = 128
CHUNK = 64
MLSTM_HEADS = 4
MLSTM_CONV = 4
MLA_HEADS = 4
MLA_NOPE = 64
MLA_ROPE = 32
MLA_V = 64
MLA_Q_RANK = 192
MLA_KV_RANK = 128
ROPE_BASE = 10000.0
GLA_HEADS = 4
GLA_DK = 32
GLA_DV = 64
GLA_GATE_RANK = 16
GLA_TAU = 16.0
GLA_SUB = 16
CONV_WIDTH = 31
D_FF = 2816
N_EXPERTS = 8
TOP_K = 2
MOE_BM = 256
INVERT_CHUNK = 8192
VMEM_LIMIT = 56 * 1024 * 1024

ZM_W = 1280
ZA_W = 512
ZG_W = 896
ZC_W = 512


def _cparams(sem):
    return pltpu.CompilerParams(dimension_semantics=sem, vmem_limit_bytes=VMEM_LIMIT)


def _const_spec(shape):
    nd = len(shape)
    return pl.BlockSpec(shape, lambda *_: (0,) * nd, pipeline_mode=pl.Buffered(1))


def _sigmoid(x):
    return 1.0 / (1.0 + jnp.exp(-x))


def _log_sigmoid(x):
    return jnp.minimum(x, 0.0) - jnp.log(1.0 + jnp.exp(-jnp.abs(x)))


def _iota(shape, dim):
    return lax.broadcasted_iota(jnp.int32, shape, dim)


def _tri(n):
    return (_iota((n, n), 0) >= _iota((n, n), 1)).astype(BF16)


def _split3(x):
    hi = x.astype(BF16)
    r1 = x - hi.astype(F32)
    mid = r1.astype(BF16)
    lo = (r1 - mid.astype(F32)).astype(BF16)
    return hi, mid, lo


def _sel_dot(sel, x):
    hi, mid, lo = _split3(x)
    d = lambda p: jnp.dot(sel, p, preferred_element_type=F32)
    return d(hi) + d(mid) + d(lo)


def _dot_sel(x, sel):
    hi, mid, lo = _split3(x)
    d = lambda p: jnp.dot(p, sel, preferred_element_type=F32)
    return d(hi) + d(mid) + d(lo)


def _inproj_kernel(h_ref, g_ref, wm_ref, wa_ref, wg_ref, wc_ref, zm_ref, za_ref, zg_ref, zc_ref, gate_ref):
    x = h_ref[...]
    ms = jnp.mean(x * x, axis=-1, keepdims=True)
    xn = (x * lax.rsqrt(ms + EPS) * g_ref[...]).astype(BF16)
    zm = jnp.dot(xn, wm_ref[...], preferred_element_type=F32)
    zm_ref[...] = zm
    gate_ref[...] = zm[:, ZM_W - 2 * LANES:]
    za_ref[...] = jnp.dot(xn, wa_ref[...], preferred_element_type=F32)
    zg_ref[...] = jnp.dot(xn, wg_ref[...], preferred_element_type=F32)
    zc_ref[...] = jnp.dot(xn, wc_ref[...], preferred_element_type=F32)


def _inproj(h, g, wm, wa, wg, wc, tm):
    n = h.shape[0]
    row = lambda w: pl.BlockSpec((tm, w), lambda i: (i, 0))
    return pl.pallas_call(
        _inproj_kernel,
        grid=(n // tm,),
        in_specs=[row(D_MODEL), _const_spec((1, D_MODEL)), _const_spec(wm.shape), _const_spec(wa.shape),
                  _const_spec(wg.shape), _const_spec(wc.shape)],
        out_specs=[row(ZM_W), row(ZA_W), row(ZG_W), row(ZC_W), row(2 * LANES)],
        out_shape=[jax.ShapeDtypeStruct((n, w), F32) for w in (ZM_W, ZA_W, ZG_W, ZC_W, 2 * LANES)],
        compiler_params=_cparams(("parallel",)),
        name="inproj",
    )(h, g, wm, wa, wg, wc)


def _expand_heads(x, width):
    r = x.shape[0]
    lane = _iota((r, 4 * width), 1)
    out = jnp.broadcast_to(x[:, 3:4], (r, 4 * width))
    for h in (2, 1, 0):
        out = jnp.where(lane < (h + 1) * width, jnp.broadcast_to(x[:, h:h + 1], (r, 4 * width)), out)
    return out


def _mlstm_setup(zm_ref, grow_ref, cw_ref, cb_ref, gbc_ref, gbr_ref, y_ref,
                 xpad_ref, q_ref, k_ref, grs_ref, cn_ref, m_ref, cnm_ref, trit_ref, b_ref, gc_ref, cm_ref, *, tt):
    L = CHUNK
    W = MLSTM_HEADS * 64

    @pl.when(pl.program_id(1) == 0)
    def _():
        xpad_ref[0:8, :] = jnp.zeros((8, 2 * W), F32)
        cn_ref[...] = jnp.zeros_like(cn_ref)
        m_ref[...] = jnp.zeros_like(m_ref)
        hsame = (_iota((W, W), 0) // L) == (_iota((W, W), 1) // L)
        ncol = (_iota((W, LANES), 0) // L) == _iota((W, LANES), 1)
        cnm_ref[...] = jnp.concatenate([hsame, ncol], axis=1).astype(F32)
        trit_ref[...] = (hsame & ((_iota((W, W), 0) % L) <= (_iota((W, W), 1) % L))).astype(BF16)

    xpad_ref[8:8 + tt, :] = zm_ref[0, :, 0:2 * W]
    for r in range(0, tt, L):
        acc = jnp.broadcast_to(cb_ref[...], (L, 2 * W))
        for j in range(MLSTM_CONV):
            acc = acc + cw_ref[j:j + 1, :] * xpad_ref[pl.ds(r + 8 - (MLSTM_CONV - 1) + j, L), :]
        qk = acc * _sigmoid(acc)
        q_ref[r:r + L, :] = (qk[:, :W] * (64 ** -0.5)).astype(BF16)
        k_ref[r:r + L, :] = qk[:, W:]
    xpad_ref[0:8, :] = xpad_ref[tt:tt + 8, :]

    i_row = grow_ref[0, 0] + gbr_ref[0:1, :]
    lf_row = _log_sigmoid(grow_ref[0, 1] + gbr_ref[1:2, :])
    grs_ref[...] = i_row - _dot_sel(lf_row, trit_ref[...])

    tri = _tri(L)
    lf_all = _log_sigmoid(zm_ref[0, :, 4 * W + LANES:4 * W + 2 * LANES] + gbc_ref[1:2, :])
    for r in range(0, tt, L):
        b_ref[r:r + L, :] = _sel_dot(tri, lf_all[r:r + L, :])
    g_all = zm_ref[0, :, 4 * W:4 * W + LANES] + gbc_ref[0:1, :] - b_ref[...]
    gc_ref[...] = g_all
    pos = _iota((tt, LANES), 0) % L
    cm_all = g_all
    s = 1
    while s < L:
        cm_all = jnp.maximum(cm_all, jnp.where(pos >= s, pltpu.roll(cm_all, s, 0), -jnp.inf))
        s *= 2
    cm_ref[...] = cm_all

    row_t = _iota((L, W), 0)
    lane_j = _iota((L, W), 1) % L
    causal = lane_j <= row_t
    ones_col = (_iota((L, LANES), 1) < MLSTM_HEADS).astype(F32)

    def chunk(c):
        r0 = pl.multiple_of(c * L, L)
        b = b_ref[pl.ds(r0, L), :]
        g = gc_ref[pl.ds(r0, L), :]
        cm = cm_ref[pl.ds(r0, L), :]
        m_prev = m_ref[...]
        mx = jnp.maximum(m_prev, cm)
        w_inter = jnp.exp(m_prev - mx)
        b_end = b[L - 1:L, :]
        mx_end = mx[L - 1:L, :]
        m_new = b_end + mx_end
        decay = jnp.exp(m_prev - mx_end)
        kw_col = jnp.exp(g - mx_end)

        qc = q_ref[pl.ds(r0, L), :]
        kc = k_ref[pl.ds(r0, L), :]
        vc = zm_ref[0, pl.ds(r0, L), 2 * W:3 * W]
        oc = zm_ref[0, pl.ds(r0, L), 3 * W:4 * W]

        kbd = (jnp.concatenate([kc] * 4, axis=0) * cnm_ref[:, 0:W]).astype(BF16)
        s_qk = lax.dot_general(qc, kbd, (((1,), (1,)), ((), ())), preferred_element_type=F32)
        inter = jnp.dot(qc, cn_ref[...].astype(BF16), preferred_element_type=F32)
        vaug = jnp.concatenate([vc, ones_col], axis=1)
        kw = (_expand_heads(kw_col, L) * kc).astype(BF16)
        upd = lax.dot_general(kw, vaug.astype(BF16), (((0,), (0,)), ((), ())), preferred_element_type=F32)
        decay_x = jnp.concatenate([_expand_heads(decay, L), decay], axis=1)
        cn_ref[...] = decay_x * cn_ref[...] + upd * cnm_ref[...]
        m_ref[...] = m_new

        def output():
            g_row = grs_ref[pl.ds(c, 1), :]
            dmat = jnp.where(causal, jnp.exp(jnp.where(causal, g_row - _expand_heads(mx, L), 0.0)), 0.0)
            s_w = (s_qk * dmat).astype(BF16)
            vbd = (jnp.concatenate([vaug] * 4, axis=0) * cnm_ref[...]).astype(BF16)
            intra = jnp.dot(s_w, vbd, preferred_element_type=F32)
            num = _expand_heads(w_inter, L) * inter[:, :W] + intra[:, :W]
            den = w_inter * inter[:, W:] + intra[:, W:]
            hden = jnp.maximum(jnp.abs(den), jnp.exp(-(b + mx)))
            hout = num * _expand_heads(1.0 / hden, L)
            y_ref[0, pl.ds(r0, L), :] = _sigmoid(oc) * hout

        return output

    return chunk


def _mlstm_scratch(tt):
    nct = tt // CHUNK
    return [pltpu.VMEM((tt + 8, 2 * GROUP_W), F32), pltpu.VMEM((tt, GROUP_W), BF16),
            pltpu.VMEM((tt, GROUP_W), F32), pltpu.VMEM((nct, GROUP_W), F32),
            pltpu.VMEM((GROUP_W, GROUP_W + LANES), F32), pltpu.VMEM((1, LANES), F32),
            pltpu.VMEM((GROUP_W, GROUP_W + LANES), F32), pltpu.VMEM((GROUP_W, GROUP_W), BF16),
            pltpu.VMEM((tt, LANES), F32), pltpu.VMEM((tt, LANES), F32), pltpu.VMEM((tt, LANES), F32)]


def _gla_setup(zg_ref, wa_ref, ba_ref, y_ref, q_ref, k_ref, g_ref, o_ref, st_ref, kmask_ref, vmask_ref, *, tt):
    L = CHUNK
    WK = GLA_HEADS * GLA_DK
    WV = GLA_HEADS * GLA_DV

    @pl.when(pl.program_id(1) == 0)
    def _():
        st_ref[...] = jnp.zeros_like(st_ref)
        kmask_ref[...] = ((_iota((WV, WK), 0) // L) == (_iota((WV, WK), 1) // GLA_DK)).astype(F32)
        vmask_ref[...] = ((_iota((WV, WV), 0) // L) == (_iota((WV, WV), 1) // GLA_DV)).astype(F32)

    q_ref[...] = zg_ref[0, :, 0:WK] * (GLA_DK ** -0.5)
    k_ref[...] = zg_ref[0, :, WK:2 * WK]
    a = zg_ref[0, :, 2 * WK + 2 * WV:2 * WK + 2 * WV + LANES]
    pre = jnp.dot(a, wa_ref[...], precision=HI, preferred_element_type=F32) + ba_ref[...]
    g_all = _log_sigmoid(pre) * (1.0 / GLA_TAU)
    tri = _tri(L)
    for r in range(0, tt, L):
        g_ref[r:r + L, :] = _sel_dot(tri, g_all[r:r + L, :])

    hs2 = ((_iota((WK, WV), 0) // GLA_DK) == (_iota((WK, WV), 1) // GLA_DV)).astype(BF16)

    def chunk(c):
        r0 = pl.multiple_of(c * L, L)
        qc = q_ref[pl.ds(r0, L), :]
        kc = k_ref[pl.ds(r0, L), :]
        vc = zg_ref[0, pl.ds(r0, L), 2 * WK:2 * WK + WV]
        b = g_ref[pl.ds(r0, L), :]
        st = st_ref[...]
        qd = (qc * jnp.exp(b)).astype(BF16)
        o = lax.dot_general(qd, st.astype(BF16), (((1,), (1,)), ((), ())), preferred_element_type=F32)
        b_end = b[L - 1:L, :]
        kd = (kc * jnp.exp(b_end - b)).astype(BF16)
        upd = lax.dot_general(vc.astype(BF16), kd, (((0,), (0,)), ((), ())), preferred_element_type=F32)
        st_ref[...] = st * jnp.exp(b_end) + upd * kmask_ref[...]

        def output(o=o):
            S = GLA_SUB
            key_pos = _iota((L, WK), 0)
            vbd = (jnp.concatenate([vc] * GLA_HEADS, axis=0) * vmask_ref[...]).astype(BF16)
            a_rows = []
            for lo in range(S, L, S):
                r = b[lo - 1:lo, :]
                qf = (qc[lo:lo + S, :] * jnp.exp(b[lo:lo + S, :] - r)).astype(BF16)
                kf = kc * jnp.exp(jnp.where(key_pos < lo, r - b, -jnp.inf))
                kbd = (jnp.concatenate([kf] * GLA_HEADS, axis=0) * kmask_ref[...]).astype(BF16)
                a_rows.append(lax.dot_general(qf, kbd, (((1,), (1,)), ((), ())), preferred_element_type=F32))
            o_off = jnp.dot(jnp.concatenate(a_rows, axis=0).astype(BF16), vbd, preferred_element_type=F32)

            tl = _iota((8, WK), 0)
            ps = []
            for lo in range(0, L, S):
                for jj in range(8):
                    j = lo + jj
                    d = b[lo:lo + S, :] - b[j:j + 1, :]
                    e = jnp.exp(jnp.concatenate([jnp.where(tl >= jj, d[0:8, :], -jnp.inf), d[8:, :]], axis=0))
                    ps.append((qc[lo:lo + S, :] * kc[j:j + 1, :] * e).astype(BF16))
                for jj in range(8):
                    j = lo + 8 + jj
                    d = b[lo + 8:lo + S, :] - b[j:j + 1, :]
                    e = jnp.exp(jnp.where(tl >= jj, d, -jnp.inf))
                    ps.append((qc[lo + 8:lo + S, :] * kc[j:j + 1, :] * e).astype(BF16))
            rexp = jnp.dot(jnp.concatenate(ps, axis=0), hs2, preferred_element_type=F32)
            bands = []
            row = 0
            for lo in range(0, L, S):
                acc_a = None
                for jj in range(8):
                    term = rexp[row:row + S, :] * vc[lo + jj:lo + jj + 1, :]
                    acc_a = term if acc_a is None else acc_a + term
                    row += S
                acc_b = None
                for jj in range(8):
                    term = rexp[row:row + 8, :] * vc[lo + 8 + jj:lo + 8 + jj + 1, :]
                    acc_b = term if acc_b is None else acc_b + term
                    row += 8
                band = jnp.concatenate([acc_a[0:8, :], acc_a[8:, :] + acc_b], axis=0)
                bands.append(band if lo == 0 else band + o_off[lo - S:lo, :])
            o_ref[pl.ds(r0, L), :] = o + jnp.concatenate(bands, axis=0)

        return output

    def finish():
        o = o_ref[...]
        hsame = ((_iota((WV, WV), 0) // GLA_DV) == (_iota((WV, WV), 1) // GLA_DV)).astype(BF16)
        ms = _dot_sel(o * o, hsame) * (1.0 / GLA_DV)
        r = zg_ref[0, :, 2 * WK + WV:2 * WK + 2 * WV]
        y_ref[0] = o * lax.rsqrt(ms + EPS) * (r * _sigmoid(r))

    return chunk, finish


def _gla_scratch(tt):
    return [pltpu.VMEM((tt, LANES), F32), pltpu.VMEM((tt, LANES), F32), pltpu.VMEM((tt, LANES), F32),
            pltpu.VMEM((tt, GROUP_W), F32), pltpu.VMEM((GROUP_W, LANES), F32),
            pltpu.VMEM((GROUP_W, LANES), F32), pltpu.VMEM((GROUP_W, GROUP_W), F32)]


def _recurrent_kernel(zm_ref, grow_ref, cw_ref, cb_ref, gbc_ref, gbr_ref, zg_ref, wa_ref, ba_ref,
                      ym_ref, yg_ref, *scratch, tt):
    n_m = len(_mlstm_scratch(tt))
    m_chunk = _mlstm_setup(zm_ref, grow_ref, cw_ref, cb_ref, gbc_ref, gbr_ref, ym_ref, *scratch[:n_m], tt=tt)
    g_chunk, g_finish = _gla_setup(zg_ref, wa_ref, ba_ref, yg_ref, *scratch[n_m:], tt=tt)

    def both(c, carry):
        m_output = m_chunk(c)
        g_output = g_chunk(c)
        m_output()
        g_output()
        return carry

    lax.fori_loop(0, tt // CHUNK, both, 0, unroll=2)
    g_finish()


def _recurrent_mixers(zm, grow, cw, cb, gbc, gbr, zg, wa, ba, tt):
    b, s, _ = zm.shape
    nct = tt // CHUNK
    tile = lambda w: pl.BlockSpec((1, tt, w), lambda i, t: (i, t, 0))
    yshape = jax.ShapeDtypeStruct((b, s, GROUP_W), F32)
    return pl.pallas_call(
        functools.partial(_recurrent_kernel, tt=tt),
        grid=(b, s // tt),
        in_specs=[tile(ZM_W), pl.BlockSpec((1, 2, nct, GROUP_W), lambda i, t: (i, 0, t, 0)),
                  _const_spec(cw.shape), _const_spec(cb.shape), _const_spec(gbc.shape), _const_spec(gbr.shape),
                  tile(ZG_W), _const_spec(wa.shape), _const_spec(ba.shape)],
        out_specs=[tile(GROUP_W), tile(GROUP_W)],
        out_shape=[yshape, yshape],
        scratch_shapes=_mlstm_scratch(tt) + _gla_scratch(tt),
        compiler_params=_cparams(("parallel", "arbitrary")),
        name="mlstm_gla",
    )(zm, grow, cw, cb, gbc, gbr, zg, wa, ba)


CONV_HIST = 32
CONV_SUB = 64


def _conv_kernel(zc_ref, cw_ref, cb_ref, lg_ref, lb_ref, y_ref, xpad_ref, xs_ref, *, tt):
    C = GROUP_W
    span = tt + CONV_HIST - 8

    @pl.when(pl.program_id(1) == 0)
    def _():
        xpad_ref[0:CONV_HIST, :] = jnp.zeros((CONV_HIST, C), F32)

    val = zc_ref[0, :, 0:C]
    gate = zc_ref[0, :, C:2 * C]
    xpad_ref[CONV_HIST:CONV_HIST + tt, :] = val * _sigmoid(gate)
    for sh in range(1, 8):
        xs_ref[sh - 1, 0:span, :] = xpad_ref[pl.ds(sh, span), :]
    sub = min(CONV_SUB, tt)
    for r in range(0, tt, sub):
        acc = jnp.broadcast_to(cb_ref[...], (sub, C))
        for j in range(CONV_WIDTH):
            off = CONV_HIST - (CONV_WIDTH - 1) + j
            sh, base = off % 8, r + off - off % 8
            src = xpad_ref[base:base + sub, :] if sh == 0 else xs_ref[sh - 1, base:base + sub, :]
            acc = acc + cw_ref[j:j + 1, :] * src
        mu = jnp.mean(acc, axis=-1, keepdims=True)
        d = acc - mu
        var = jnp.mean(d * d, axis=-1, keepdims=True)
        yn = d * lax.rsqrt(var + EPS) * lg_ref[...] + lb_ref[...]
        y_ref[0, r:r + sub, :] = yn * _sigmoid(yn)
    xpad_ref[0:CONV_HIST, :] = xpad_ref[tt:tt + CONV_HIST, :]


def _conformer(zc, cw, cb, lg, lb, tt):
    b, s, _ = zc.shape
    return pl.pallas_call(
        functools.partial(_conv_kernel, tt=tt),
        grid=(b, s // tt),
        in_specs=[pl.BlockSpec((1, tt, ZC_W), lambda i, t: (i, t, 0)), _const_spec(cw.shape), _const_spec(cb.shape),
                  _const_spec(lg.shape), _const_spec(lb.shape)],
        out_specs=pl.BlockSpec((1, tt, GROUP_W), lambda i, t: (i, t, 0)),
        out_shape=jax.ShapeDtypeStruct((b, s, GROUP_W), F32),
        scratch_shapes=[pltpu.VMEM((tt + CONV_HIST, GROUP_W), F32), pltpu.VMEM((7, tt + CONV_HIST, GROUP_W), F32)],
        compiler_params=_cparams(("parallel", "arbitrary")),
        name="conformer",
    )(zc, cw, cb, lg, lb)


MLA_HP = 128
MLA_VP = 80


def _mla_prep_kernel(za_ref, rc_ref, rs_ref, qg_ref, kg_ref, wqa_ref, wqb_ref, wka_ref, wkb_ref, wv_ref,
                     q_ref, k_ref, v_ref):
    cq = za_ref[0, :, 0:256]
    ckv = za_ref[0, :, 256:384]
    kr = za_ref[0, :, 384:512]
    qn = (cq * lax.rsqrt(jnp.sum(cq * cq, axis=-1, keepdims=True) * (1.0 / MLA_Q_RANK) + EPS) * qg_ref[...]).astype(BF16)
    kvn = ckv * lax.rsqrt(jnp.mean(ckv * ckv, axis=-1, keepdims=True) + EPS) * kg_ref[...]
    kin = jnp.concatenate([kvn, kr], axis=1).astype(BF16)
    cos = rc_ref[0]
    sin = rs_ref[0]
    qa = jnp.dot(qn, wqa_ref[...], preferred_element_type=F32)
    qb = jnp.dot(qn, wqb_ref[...], preferred_element_type=F32)
    ka = jnp.dot(kin, wka_ref[...], preferred_element_type=F32)
    kb = jnp.dot(kin, wkb_ref[...], preferred_element_type=F32)
    vt = lax.dot_general(wv_ref[...], kin, (((1,), (1,)), ((), ())), preferred_element_type=F32)
    scale = (MLA_NOPE + MLA_ROPE) ** -0.5 * LOG2E
    for h in range(MLA_HEADS):
        sl = slice(h * MLA_HP, (h + 1) * MLA_HP)
        q_ref[0, h] = ((qa[:, sl] * cos + qb[:, sl] * sin) * scale).astype(BF16)
        k_ref[0, h] = (ka[:, sl] * cos + kb[:, sl] * sin).astype(BF16)
        v_ref[0, h, 0:MLA_V, :] = vt[h * MLA_V:(h + 1) * MLA_V, :].astype(BF16)
        v_ref[0, h, MLA_V:MLA_VP, :] = jnp.ones((MLA_VP - MLA_V, vt.shape[1]), BF16)


def _mla_prep(za, rc, rs, qg, kg, wqa, wqb, wka, wkb, wv, tt):
    b, s, _ = za.shape
    hspec = pl.BlockSpec((1, MLA_HEADS, tt, MLA_HP), lambda i, t: (i, 0, t, 0))
    tspec = pl.BlockSpec((1, tt, MLA_HP), lambda i, t: (i, t, 0))
    hshape = jax.ShapeDtypeStruct((b, MLA_HEADS, s, MLA_HP), BF16)
    vspec = pl.BlockSpec((1, MLA_HEADS, MLA_VP, tt), lambda i, t: (i, 0, 0, t))
    vshape = jax.ShapeDtypeStruct((b, MLA_HEADS, MLA_VP, s), BF16)
    return pl.pallas_call(
        _mla_prep_kernel,
        grid=(b, s // tt),
        in_specs=[pl.BlockSpec((1, tt, ZA_W), lambda i, t: (i, t, 0)), tspec, tspec,
                  _const_spec(qg.shape), _const_spec(kg.shape), _const_spec(wqa.shape), _const_spec(wqb.shape),
                  _const_spec(wka.shape), _const_spec(wkb.shape), _const_spec(wv.shape)],
        out_specs=[hspec, hspec, vspec],
        out_shape=[hshape, hshape, vshape],
        compiler_params=_cparams(("parallel", "parallel")),
        name="mla_prep",
    )(za, rc, rs, qg, kg, wqa, wqb, wka, wkb, wv)


def _mla_attn_kernel(qi_ref, ki_ref, q_ref, k_ref, v_ref, y_ref, m_ref, acc_ref, *, tq):
    qi = qi_ref[pl.program_id(1)]
    ki = ki_ref[pl.program_id(1)]

    @pl.when(ki == 0)
    def _():
        m_ref[...] = jnp.full_like(m_ref, -jnp.inf)
        acc_ref[...] = jnp.zeros_like(acc_ref)

    def step(masked):
        def scores(h):
            s = lax.dot_general(k_ref[0, h], q_ref[0, h], (((1,), (1,)), ((), ())), preferred_element_type=F32)
            if masked:
                s = jnp.where(_iota((tq, tq), 0) <= _iota((tq, tq), 1), s, -jnp.inf)
            return s

        def softmax(h, s):
            m_prev = m_ref[h]
            m_new = jnp.maximum(m_prev, jnp.max(s, axis=0, keepdims=True))
            alpha = jnp.exp2(m_prev - m_new)
            p = jnp.exp2(s - m_new)
            m_ref[h] = m_new
            return p.astype(BF16), alpha

        def values(h, p, alpha):
            acc_ref[h] = alpha * acc_ref[h] + jnp.dot(v_ref[0, h], p, preferred_element_type=F32)

        s_next = scores(0)
        for h in range(MLA_HEADS):
            s_cur = s_next
            if h + 1 < MLA_HEADS:
                s_next = scores(h + 1)
            values(h, *softmax(h, s_cur))

    @pl.when(ki < qi)
    def _():
        step(False)

    @pl.when(ki == qi)
    def _():
        step(True)
        ot = jnp.concatenate([acc_ref[h, 0:MLA_V, :] * (1.0 / acc_ref[h, MLA_V:MLA_V + 1, :])
                              for h in range(MLA_HEADS)], axis=0)
        y_ref[0] = ot.T


def _mla_attn(q, k, v, tq):
    b, _, s, _ = q.shape
    nq = s // tq
    pairs = [(qi, ki) for qi in range(nq) for ki in range(qi + 1)]
    qi_tab = jnp.asarray([p[0] for p in pairs], jnp.int32)
    ki_tab = jnp.asarray([p[1] for p in pairs], jnp.int32)
    grid_spec = pltpu.PrefetchScalarGridSpec(
        num_scalar_prefetch=2,
        grid=(b, len(pairs)),
        in_specs=[pl.BlockSpec((1, MLA_HEADS, tq, MLA_HP), lambda i, p, qt, kt: (i, 0, qt[p], 0)),
                  pl.BlockSpec((1, MLA_HEADS, tq, MLA_HP), lambda i, p, qt, kt: (i, 0, kt[p], 0)),
                  pl.BlockSpec((1, MLA_HEADS, MLA_VP, tq), lambda i, p, qt, kt: (i, 0, 0, kt[p]))],
        out_specs=pl.BlockSpec((1, tq, GROUP_W), lambda i, p, qt, kt: (i, qt[p], 0)),
        scratch_shapes=[pltpu.VMEM((MLA_HEADS, 1, tq), F32), pltpu.VMEM((MLA_HEADS, MLA_VP, tq), F32)],
    )
    return pl.pallas_call(
        functools.partial(_mla_attn_kernel, tq=tq),
        grid_spec=grid_spec,
        out_shape=jax.ShapeDtypeStruct((b, s, GROUP_W), F32),
        compiler_params=_cparams(("parallel", "arbitrary")),
        name="mla_attn",
    )(qi_tab, ki_tab, q, k, v)


def _outproj_kernel(*refs, routed):
    if routed:
        ym, ya, yg, yc, h_ref, fg_ref, wo_ref, n2_ref, rt_ref, hn_ref, u_ref, route_ref = refs
    else:
        ym, ya, yg, yc, h_ref, fg_ref, wo_ref, n2_ref, wg_ref, wu_ref, wd_ref, hn_ref = refs
    parts = []
    for gi, y_ref in enumerate((ym, ya, yg, yc)):
        y = y_ref[...]
        yn = y * lax.rsqrt(jnp.mean(y * y, axis=-1, keepdims=True) + EPS)
        parts.append((yn * fg_ref[:, gi * GROUP_W:(gi + 1) * GROUP_W]).astype(BF16))
    ycat = jnp.concatenate(parts, axis=1)
    hn = h_ref[...] + jnp.dot(ycat, wo_ref[...], preferred_element_type=F32)
    u = hn * lax.rsqrt(jnp.mean(hn * hn, axis=-1, keepdims=True) + EPS) * n2_ref[...]
    if not routed:
        hn_ref[...] = hn + _swiglu(u.astype(BF16), wg_ref[...], wu_ref[...], wd_ref[...])
        return
    hn_ref[...] = hn
    u_ref[...] = u
    if routed:
        tm = u.shape[0]
        lane = _iota((tm, LANES), 1)
        u_hi = u.astype(BF16)
        u_lo = (u - u_hi.astype(F32)).astype(BF16)
        rt = rt_ref[...]
        r_hi = rt.astype(BF16)
        r_lo = (rt - r_hi.astype(F32)).astype(BF16)
        d = lambda a, b: jnp.dot(a, b, preferred_element_type=F32)
        logits = d(u_hi, r_hi) + (d(u_lo, r_hi) + d(u_hi, r_lo))
        logits = jnp.where(lane < N_EXPERTS, logits, -jnp.inf)
        m1 = jnp.max(logits, axis=-1, keepdims=True)
        i1 = jnp.min(jnp.where(logits == m1, lane, LANES), axis=-1, keepdims=True)
        rest = jnp.where(lane == i1, -jnp.inf, logits)
        m2 = jnp.max(rest, axis=-1, keepdims=True)
        i2 = jnp.min(jnp.where(rest == m2, lane, LANES), axis=-1, keepdims=True)
        e2 = jnp.exp(m2 - m1)
        w1 = 1.0 / (1.0 + e2)
        w2 = e2 / (1.0 + e2)
        route_ref[...] = jnp.where(lane == 0, i1.astype(F32),
                                   jnp.where(lane == 1, i2.astype(F32),
                                             jnp.where(lane == 2, w1, jnp.where(lane == 3, w2, 0.0))))


def _outproj(ym, ya, yg, yc, h, fg, wo, n2, tm, router=None, ffn=None):
    n = h.shape[0]
    routed = router is not None
    row = lambda w: pl.BlockSpec((tm, w), lambda i: (i, 0))
    in_specs = [row(GROUP_W)] * 4 + [row(D_MODEL), _const_spec(fg.shape), _const_spec(wo.shape), _const_spec(n2.shape)]
    args = [ym, ya, yg, yc, h, fg, wo, n2]
    hshape = jax.ShapeDtypeStruct((n, D_MODEL), F32)
    if routed:
        in_specs.append(_const_spec(router.shape))
        args.append(router)
        out_specs = [row(D_MODEL), row(D_MODEL), row(LANES)]
        out_shape = [hshape, hshape, jax.ShapeDtypeStruct((n, LANES), F32)]
    else:
        in_specs += [_const_spec(w.shape) for w in ffn]
        args += list(ffn)
        out_specs, out_shape = row(D_MODEL), hshape
    return pl.pallas_call(
        functools.partial(_outproj_kernel, routed=routed),
        grid=(n // tm,),
        in_specs=in_specs, out_specs=out_specs, out_shape=out_shape,
        compiler_params=_cparams(("parallel",)),
        name="outproj_routed" if routed else "outproj_ffn",
    )(*args)


def _swiglu(x, wg, wu, wd):
    g = jnp.dot(x, wg, preferred_element_type=F32)
    up = jnp.dot(x, wu, preferred_element_type=F32)
    a = (g * _sigmoid(g) * up).astype(BF16)
    return jnp.dot(a, wd, preferred_element_type=F32)


def _moe_kernel(be_ref, tokc_ref, tokn_ref, dstp_ref, dstc_ref, u_hbm, wg_ref, wu_ref, wd_ref, y_hbm,
                xbuf, ybuf, gsem, ssem):
    i = pl.program_id(0)
    nb = pl.num_programs(0)
    bm = xbuf.shape[1]

    def gather_wait(x, s):
        pltpu.make_async_copy(u_hbm.at[pl.ds(0, bm), :], x, gsem.at[s]).wait()

    def scatter_wait(y, s):
        pltpu.make_async_copy(y, y_hbm.at[pl.ds(0, bm), :], ssem.at[s]).wait()

    def scatter_row(y, dst_ref, r, s):
        pltpu.make_async_copy(y.at[pl.ds(r, 1), :], y_hbm.at[pl.ds(dst_ref[0, 0, r], 1), :], ssem.at[s]).start()

    @pl.when(i == 0)
    def _():
        def body(r, c):
            pltpu.make_async_copy(u_hbm.at[pl.ds(tokc_ref[0, 0, r], 1), :], xbuf.at[0, pl.ds(r, 1), :],
                                  gsem.at[0]).start()
            return c
        lax.fori_loop(0, bm, body, 0, unroll=8)
        ybuf[2] = jnp.zeros((bm, D_MODEL), F32)

    c = i % 2
    n = 1 - c
    yc_i = i % 3
    yp_i = (i + 2) % 3
    xc, xn, yc, yp = xbuf.at[c], xbuf.at[n], ybuf.at[yc_i], ybuf.at[yp_i]
    gather_wait(xc, c)

    @pl.when(i >= 2)
    def _():
        scatter_wait(yc, yc_i)

    g = jnp.dot(xc[...].astype(BF16), wg_ref[0], preferred_element_type=F32)
    for r in range(bm):
        pltpu.make_async_copy(u_hbm.at[pl.ds(tokn_ref[0, 0, r], 1), :], xn.at[pl.ds(r, 1), :], gsem.at[n]).start()
    up = jnp.dot(xc[...].astype(BF16), wu_ref[0], preferred_element_type=F32)
    for r in range(bm):
        scatter_row(yp, dstp_ref, r, yp_i)
    a = (g * _sigmoid(g) * up).astype(BF16)
    yc[...] = jnp.dot(a, wd_ref[0], preferred_element_type=F32)

    @pl.when(i == nb - 1)
    def _():
        def body(r, carry):
            scatter_row(yc, dstc_ref, r, yc_i)
            return carry
        lax.fori_loop(0, bm, body, 0, unroll=8)
        scatter_wait(yc, (i + 1) % 3)
        scatter_wait(yp, yp_i)
        scatter_wait(yc, yc_i)
        gather_wait(xn, n)


def _moe(block_e, row_tok, row_dst, u, wg, wu, wd, n_rows):
    n_blocks = block_e.shape[0]
    bm = row_tok.shape[-1]
    wspec = lambda shp: pl.BlockSpec((1,) + shp, lambda i, be: (be[i], 0, 0))
    ispec = lambda f: pl.BlockSpec((1, 1, bm), f, memory_space=pltpu.SMEM)
    grid_spec = pltpu.PrefetchScalarGridSpec(
        num_scalar_prefetch=1,
        grid=(n_blocks,),
        in_specs=[ispec(lambda i, be: (i, 0, 0)),
                  ispec(lambda i, be: (jnp.minimum(i + 1, n_blocks - 1), 0, 0)),
                  ispec(lambda i, be: (i, 0, 0)),
                  ispec(lambda i, be: (i + 1, 0, 0)),
                  pl.BlockSpec(memory_space=pl.ANY),
                  wspec((D_MODEL, D_FF)), wspec((D_MODEL, D_FF)), wspec((D_FF, D_MODEL))],
        out_specs=pl.BlockSpec(memory_space=pl.ANY),
        scratch_shapes=[pltpu.VMEM((2, bm, D_MODEL), F32), pltpu.VMEM((3, bm, D_MODEL), F32),
                        pltpu.SemaphoreType.DMA((2,)), pltpu.SemaphoreType.DMA((3,))],
    )
    return pl.pallas_call(
        _moe_kernel,
        grid_spec=grid_spec,
        out_shape=jax.ShapeDtypeStruct((n_rows, D_MODEL), F32),
        compiler_params=_cparams(("arbitrary",)),
        name="moe_experts",
    )(block_e, row_tok, row_tok, row_dst, row_dst, u, wg, wu, wd)


def _combine_kernel(ya_ref, yb_ref, h_ref, route_ref, fg_ref, o_ref, *, final):
    w1 = route_ref[:, 2:3]
    w2 = route_ref[:, 3:4]
    hn = h_ref[...] + w1 * ya_ref[...] + w2 * yb_ref[...]
    if final:
        hn = hn * lax.rsqrt(jnp.mean(hn * hn, axis=-1, keepdims=True) + EPS) * fg_ref[...]
    o_ref[...] = hn


def _combine(y2, h, route, fg, tm, final):
    n = h.shape[0]
    return pl.pallas_call(
        functools.partial(_combine_kernel, final=final),
        grid=(n // tm,),
        in_specs=[pl.BlockSpec((tm, D_MODEL), lambda i: (i, 0)),
                  pl.BlockSpec((tm, D_MODEL), lambda i: (i + n // tm, 0)),
                  pl.BlockSpec((tm, D_MODEL), lambda i: (i, 0)),
                  pl.BlockSpec((tm, LANES), lambda i: (i, 0)),
                  _const_spec(fg.shape)],
        out_specs=pl.BlockSpec((tm, D_MODEL), lambda i: (i, 0)),
        out_shape=jax.ShapeDtypeStruct((n, D_MODEL), F32),
        compiler_params=_cparams(("parallel",)),
        name="moe_combine",
    )(y2, y2, h, route, fg)


def _rmsnorm_kernel(h_ref, g_ref, o_ref):
    x = h_ref[...]
    o_ref[...] = x * lax.rsqrt(jnp.mean(x * x, axis=-1, keepdims=True) + EPS) * g_ref[...]


def _final_norm(h, g, tm):
    n = h.shape[0]
    row = pl.BlockSpec((tm, D_MODEL), lambda i: (i, 0))
    return pl.pallas_call(
        _rmsnorm_kernel, grid=(n // tm,), in_specs=[row, _const_spec(g.shape)], out_specs=row,
        out_shape=jax.ShapeDtypeStruct((n, D_MODEL), F32),
        compiler_params=_cparams(("parallel",)), name="final_norm",
    )(h, g)


def _pad_cols(w, width):
    return jnp.pad(w, ((0, 0), (0, width - w.shape[1])))


def _pad_rows(w, height):
    return jnp.pad(w, ((0, height - w.shape[0]), (0, 0)))


def _inproj_weights(w):
    o = np.cumsum([0, 512, 256, 256, 8, 192, 128, 32, 128, 128, 256, 256, 16, 512])
    seg = lambda i: w[:, o[i]:o[i + 1]]
    gates = seg(3)
    wm = jnp.concatenate([seg(0), seg(1), seg(2), _pad_cols(gates[:, :4], LANES), _pad_cols(gates[:, 4:], LANES)], axis=1)
    wa = jnp.concatenate([_pad_cols(seg(4), 256), seg(5), _pad_cols(seg(6), LANES)], axis=1)
    wg = jnp.concatenate([seg(7), seg(8), seg(9), seg(10), _pad_cols(seg(11), LANES)], axis=1)
    wc = seg(12)
    return [x.astype(BF16) for x in (wm, wa, wg, wc)]


def _mla_weights(w_uq, w_ukv):
    half = MLA_ROPE // 2
    zq = jnp.zeros((MLA_Q_RANK, half), F32)
    qa, qb, ka, kb, wv = [], [], [], [], []
    eye = jnp.eye(LANES, dtype=F32)[:, :MLA_ROPE]
    zk = jnp.zeros((MLA_KV_RANK, MLA_HP), F32)
    for h in range(MLA_HEADS):
        wq = w_uq[:, h * 96:(h + 1) * 96]
        nope, r1, r2 = wq[:, :64], wq[:, 64:64 + half], wq[:, 64 + half:]
        qa.append(_pad_cols(jnp.concatenate([nope, r1, r2], axis=1), MLA_HP))
        qb.append(_pad_cols(jnp.concatenate([jnp.zeros_like(nope), -r2, r1], axis=1), MLA_HP))
        wkv = w_ukv[:, h * 128:(h + 1) * 128]
        kn, vh = wkv[:, :64], wkv[:, 64:]
        e1, e2 = eye[:, :half], eye[:, half:]
        z64 = jnp.zeros((LANES, 64), F32)
        ka.append(jnp.concatenate([_pad_cols(kn, MLA_HP), _pad_cols(jnp.concatenate([z64, e1, e2], axis=1), MLA_HP)], axis=0))
        kb.append(jnp.concatenate([zk, _pad_cols(jnp.concatenate([z64, -e2, e1], axis=1), MLA_HP)], axis=0))
        wv.append(vh.T)
    cat = lambda xs, rows: _pad_rows(jnp.concatenate(xs, axis=1), rows).astype(BF16)
    wvt = _pad_cols(jnp.concatenate(wv, axis=0), 256).astype(BF16)
    return cat(qa, 256), cat(qb, 256), cat(ka, 256), cat(kb, 256), wvt


def _rope_tables(positions):
    half = MLA_ROPE // 2
    inv_freq = 1.0 / (ROPE_BASE ** (jnp.arange(0, MLA_ROPE, 2, dtype=F32) / MLA_ROPE))
    ang = positions.astype(F32)[..., None] * inv_freq
    cos, sin = jnp.cos(ang), jnp.sin(ang)
    shp = cos.shape[:-1]
    rc = jnp.concatenate([jnp.ones(shp + (64,), F32), cos, cos, jnp.zeros(shp + (MLA_HP - 64 - 2 * half,), F32)], axis=-1)
    rs = jnp.concatenate([jnp.zeros(shp + (64,), F32), sin, sin, jnp.zeros(shp + (MLA_HP - 64 - 2 * half,), F32)], axis=-1)
    return rc, rs


def _invert_kernel(dest_ref, out_ref):
    i = pl.program_id(0)
    ch = dest_ref.shape[-1]

    @pl.when(i == 0)
    def _():
        def init(p, c):
            out_ref[p] = -1
            return c
        lax.fori_loop(0, out_ref.shape[0], init, 0, unroll=8)

    def body(a, c):
        out_ref[dest_ref[0, 0, a]] = i * ch + a
        return c
    lax.fori_loop(0, ch, body, 0, unroll=8)


def _invert_assignment(dest, rows):
    a = dest.shape[0]
    ch = min(INVERT_CHUNK, a)
    return pl.pallas_call(
        _invert_kernel,
        grid=(a // ch,),
        in_specs=[pl.BlockSpec((1, 1, ch), lambda i: (i, 0, 0), memory_space=pltpu.SMEM)],
        out_specs=pl.BlockSpec(memory_space=pltpu.SMEM),
        out_shape=jax.ShapeDtypeStruct((rows,), jnp.int32),
        compiler_params=_cparams(("arbitrary",)),
        name="moe_invert",
    )(dest.reshape(a // ch, 1, ch))


def _routing_tables(route, bm):
    n = route.shape[0]
    a = n * TOP_K
    flat_e = route[:, :TOP_K].astype(jnp.int32).reshape(-1)
    onehot = (flat_e[:, None] == jnp.arange(N_EXPERTS, dtype=jnp.int32)[None, :]).astype(jnp.int32)
    csum = jnp.cumsum(onehot, axis=0)
    rank = jnp.sum(onehot * csum, axis=1) - 1
    counts = csum[-1]
    padded = (counts + bm - 1) // bm * bm
    padded_end = jnp.cumsum(padded)
    padded_start = padded_end - padded
    dest = padded_start[flat_e] + rank
    n_blocks = -(-a // bm) + N_EXPERTS
    rows = n_blocks * bm
    row_asg = _invert_assignment(dest, rows)
    row_tok = jnp.maximum(row_asg, 0) // TOP_K
    pad_rank = jnp.cumsum((row_asg < 0).astype(jnp.int32)) - 1
    slot_row = (row_asg % TOP_K) * n + row_asg // TOP_K
    row_dst = jnp.where(row_asg >= 0, slot_row, a + pad_rank)
    first = a + (rows - a) + jnp.arange(bm, dtype=jnp.int32)
    row_dst = jnp.concatenate([first, row_dst]).reshape(n_blocks + 1, 1, bm)
    block_start = jnp.arange(n_blocks, dtype=jnp.int32) * bm
    block_e = jnp.sum((padded_end[None, :] <= block_start[:, None]).astype(jnp.int32), axis=1)
    block_e = jnp.minimum(block_e, N_EXPERTS - 1).astype(jnp.int32)
    return block_e, row_tok.reshape(n_blocks, 1, bm), row_dst, rows + bm


def kernel(x, positions, norm1_g, w_in, mlstm_conv_w, mlstm_conv_b, mlstm_gate_b, mla_q_norm_g, mla_w_uq,
           mla_kv_norm_g, mla_w_ukv, gla_w_alpha, gla_b_alpha, conv_w, conv_b, conv_ln_g, conv_ln_b, fuse_g,
           w_out, norm2_g, ffn_w_gate, ffn_w_up, ffn_w_down, moe_router, moe_w_gate, moe_w_up, moe_w_down,
           final_norm_g):
    bsz, seq, _ = x.shape
    n = bsz * seq
    depth = w_in.shape[0]
    tm = min(512, n)
    tm2 = min(1024, n)
    tt = min(1024, seq)
    tq = min(512, seq)
    nct = seq // CHUNK
    row2 = lambda v: v.reshape(1, -1).astype(F32)

    rc, rs = _rope_tables(positions)
    h = x.reshape(n, D_MODEL)
    for l in range(depth):
        wm, wa, wg, wc = _inproj_weights(w_in[l])
        zm, za, zg, zc, zgate = _inproj(h, row2(norm1_g[l]), wm, wa, wg, wc, tm2)
        zm = zm.reshape(bsz, seq, ZM_W)

        gates = jnp.concatenate([zgate[:, 0:MLSTM_HEADS], zgate[:, LANES:LANES + MLSTM_HEADS]], axis=-1)
        grow = gates.reshape(bsz, nct, CHUNK, 2, MLSTM_HEADS).transpose(0, 3, 1, 4, 2).reshape(bsz, 2, nct, GROUP_W)
        gb = mlstm_gate_b[l].astype(F32)
        gbc = _pad_cols(gb.reshape(2, MLSTM_HEADS), LANES)
        gbr = jnp.repeat(gb.reshape(2, MLSTM_HEADS), CHUNK, axis=1)
        ym, yg = _recurrent_mixers(zm, grow, mlstm_conv_w[l], row2(mlstm_conv_b[l]), gbc, gbr,
                                   zg.reshape(bsz, seq, ZG_W), _pad_rows(gla_w_alpha[l], LANES),
                                   row2(gla_b_alpha[l]), tt)

        wqa, wqb, wka, wkb, wv = _mla_weights(mla_w_uq[l], mla_w_ukv[l])
        qh, kh, vh = _mla_prep(za.reshape(bsz, seq, ZA_W), rc, rs, _pad_cols(row2(mla_q_norm_g[l]), 256),
                               row2(mla_kv_norm_g[l]), wqa, wqb, wka, wkb, wv, tt)
        ya = _mla_attn(qh, kh, vh, tq)

        yc = _conformer(zc.reshape(bsz, seq, ZC_W), conv_w[l], row2(conv_b[l]), row2(conv_ln_g[l]),
                        row2(conv_ln_b[l]), tt)

        flat = lambda y: y.reshape(n, GROUP_W)
        j = l // 2
        if l % 2 == 0:
            ffn = (ffn_w_gate[j].astype(BF16), ffn_w_up[j].astype(BF16), ffn_w_down[j].astype(BF16))
            h = _outproj(flat(ym), flat(ya), flat(yg), flat(yc), h, row2(fuse_g[l]), w_out[l].astype(BF16),
                         row2(norm2_g[l]), tm, ffn=ffn)
            if l == depth - 1:
                h = _final_norm(h, row2(final_norm_g), tm2)
        else:
            h, u, route = _outproj(flat(ym), flat(ya), flat(yg), flat(yc), h, row2(fuse_g[l]),
                                   w_out[l].astype(BF16), row2(norm2_g[l]), tm2,
                                   router=_pad_cols(moe_router[j], LANES))
            block_e, row_tok, row_dst, n_rows = _routing_tables(route, MOE_BM)
            y2 = _moe(block_e, row_tok, row_dst, u, moe_w_gate[j].astype(BF16), moe_w_up[j].astype(BF16),
                      moe_w_down[j].astype(BF16), n_rows)
            h = _combine(y2, h, route, row2(final_norm_g), tm2, l == depth - 1)
    return h.reshape(bsz, seq, D_MODEL)
```

```python
import functools

import jax
import jax.numpy as jnp
import numpy as np
from jax import lax
from jax.experimental import pallas as pl
from jax.experimental.pallas import tpu as pltpu

F32 = jnp.float32
BF16 = jnp.bfloat16
HI = lax.Precision.HIGHEST

D_MODEL = 1024
GROUP_W = 256
EPS = 1e-6
LOG2E = 1.4426950408889634
LANES = 128
CHUNK = 64
MLSTM_HEADS = 4
MLSTM_CONV = 4
MLA_HEADS = 4
MLA_NOPE = 64
MLA_ROPE = 32
MLA_V = 64
MLA_Q_RANK = 192
MLA_KV_RANK = 128
ROPE_BASE = 10000.0
GLA_HEADS = 4
GLA_DK = 32
GLA_DV = 64
GLA_GATE_RANK = 16
GLA_TAU = 16.0
GLA_SUB = 16
CONV_WIDTH = 31
D_FF = 2816
N_EXPERTS = 8
TOP_K = 2
MOE_BM = 256
INVERT_CHUNK = 8192
VMEM_LIMIT = 56 * 1024 * 1024

ZM_W = 1280
ZA_W = 512
ZG_W = 896
ZC_W = 512


def _cparams(sem):
    return pltpu.CompilerParams(dimension_semantics=sem, vmem_limit_bytes=VMEM_LIMIT)


def _const_spec(shape):
    nd = len(shape)
    return pl.BlockSpec(shape, lambda *_: (0,) * nd, pipeline_mode=pl.Buffered(1))


def _sigmoid(x):
    return 1.0 / (1.0 + jnp.exp(-x))


def _log_sigmoid(x):
    return jnp.minimum(x, 0.0) - jnp.log(1.0 + jnp.exp(-jnp.abs(x)))


def _iota(shape, dim):
    return lax.broadcasted_iota(jnp.int32, shape, dim)


def _tri(n):
    return (_iota((n, n), 0) >= _iota((n, n), 1)).astype(BF16)


def _split3(x):
    hi = x.astype(BF16)
    r1 = x - hi.astype(F32)
    mid = r1.astype(BF16)
    lo = (r1 - mid.astype(F32)).astype(BF16)
    return hi, mid, lo


def _sel_dot(sel, x):
    hi, mid, lo = _split3(x)
    d = lambda p: jnp.dot(sel, p, preferred_element_type=F32)
    return d(hi) + d(mid) + d(lo)


def _dot_sel(x, sel):
    hi, mid, lo = _split3(x)
    d = lambda p: jnp.dot(p, sel, preferred_element_type=F32)
    return d(hi) + d(mid) + d(lo)


def _inproj_kernel(h_ref, g_ref, wm_ref, wa_ref, wg_ref, wc_ref, zm_ref, za_ref, zg_ref, zc_ref, gate_ref):
    x = h_ref[...]
    ms = jnp.mean(x * x, axis=-1, keepdims=True)
    xn = (x * lax.rsqrt(ms + EPS) * g_ref[...]).astype(BF16)
    zm = jnp.dot(xn, wm_ref[...], preferred_element_type=F32)
    zm_ref[...] = zm
    gate_ref[...] = zm[:, ZM_W - 2 * LANES:]
    za_ref[...] = jnp.dot(xn, wa_ref[...], preferred_element_type=F32)
    zg_ref[...] = jnp.dot(xn, wg_ref[...], preferred_element_type=F32)
    zc_ref[...] = jnp.dot(xn, wc_ref[...], preferred_element_type=F32)


def _inproj(h, g, wm, wa, wg, wc, tm):
    n = h.shape[0]
    row = lambda w: pl.BlockSpec((tm, w), lambda i: (i, 0))
    return pl.pallas_call(
        _inproj_kernel,
        grid=(n // tm,),
        in_specs=[row(D_MODEL), _const_spec((1, D_MODEL)), _const_spec(wm.shape), _const_spec(wa.shape),
                  _const_spec(wg.shape), _const_spec(wc.shape)],
        out_specs=[row(ZM_W), row(ZA_W), row(ZG_W), row(ZC_W), row(2 * LANES)],
        out_shape=[jax.ShapeDtypeStruct((n, w), F32) for w in (ZM_W, ZA_W, ZG_W, ZC_W, 2 * LANES)],
        compiler_params=_cparams(("parallel",)),
        name="inproj",
    )(h, g, wm, wa, wg, wc)


def _expand_heads(x, width):
    r = x.shape[0]
    lane = _iota((r, 4 * width), 1)
    out = jnp.broadcast_to(x[:, 3:4], (r, 4 * width))
    for h in (2, 1, 0):
        out = jnp.where(lane < (h + 1) * width, jnp.broadcast_to(x[:, h:h + 1], (r, 4 * width)), out)
    return out


def _mlstm_setup(zm_ref, grow_ref, cw_ref, cb_ref, gbc_ref, gbr_ref, y_ref,
                 xpad_ref, q_ref, k_ref, grs_ref, cn_ref, m_ref, cnm_ref, trit_ref, b_ref, gc_ref, cm_ref, *, tt):
    L = CHUNK
    W = MLSTM_HEADS * 64

    @pl.when(pl.program_id(1) == 0)
    def _():
        xpad_ref[0:8, :] = jnp.zeros((8, 2 * W), F32)
        cn_ref[...] = jnp.zeros_like(cn_ref)
        m_ref[...] = jnp.zeros_like(m_ref)
        hsame = (_iota((W, W), 0) // L) == (_iota((W, W), 1) // L)
        ncol = (_iota((W, LANES), 0) // L) == _iota((W, LANES), 1)
        cnm_ref[...] = jnp.concatenate([hsame, ncol], axis=1).astype(F32)
        trit_ref[...] = (hsame & ((_iota((W, W), 0) % L) <= (_iota((W, W), 1) % L))).astype(BF16)

    xpad_ref[8:8 + tt, :] = zm_ref[0, :, 0:2 * W]
    for r in range(0, tt, L):
        acc = jnp.broadcast_to(cb_ref[...], (L, 2 * W))
        for j in range(MLSTM_CONV):
            acc = acc + cw_ref[j:j + 1, :] * xpad_ref[pl.ds(r + 8 - (MLSTM_CONV - 1) + j, L), :]
        qk = acc * _sigmoid(acc)
        q_ref[r:r + L, :] = (qk[:, :W] * (64 ** -0.5)).astype(BF16)
        k_ref[r:r + L, :] = qk[:, W:]
    xpad_ref[0:8, :] = xpad_ref[tt:tt + 8, :]

    i_row = grow_ref[0, 0] + gbr_ref[0:1, :]
    lf_row = _log_sigmoid(grow_ref[0, 1] + gbr_ref[1:2, :])
    grs_ref[...] = i_row - _dot_sel(lf_row, trit_ref[...])

    tri = _tri(L)
    lf_all = _log_sigmoid(zm_ref[0, :, 4 * W + LANES:4 * W + 2 * LANES] + gbc_ref[1:2, :])
    for r in range(0, tt, L):
        b_ref[r:r + L, :] = _sel_dot(tri, lf_all[r:r + L, :])
    g_all = zm_ref[0, :, 4 * W:4 * W + LANES] + gbc_ref[0:1, :] - b_ref[...]
    gc_ref[...] = g_all
    pos = _iota((tt, LANES), 0) % L
    cm_all = g_all
    s = 1
    while s < L:
        cm_all = jnp.maximum(cm_all, jnp.where(pos >= s, pltpu.roll(cm_all, s, 0), -jnp.inf))
        s *= 2
    cm_ref[...] = cm_all

    row_t = _iota((L, W), 0)
    lane_j = _iota((L, W), 1) % L
    causal = lane_j <= row_t
    ones_col = (_iota((L, LANES), 1) < MLSTM_HEADS).astype(F32)

    def chunk(c):
        r0 = pl.multiple_of(c * L, L)
        b = b_ref[pl.ds(r0, L), :]
        g = gc_ref[pl.ds(r0, L), :]
        cm = cm_ref[pl.ds(r0, L), :]
        m_prev = m_ref[...]
        mx = jnp.maximum(m_prev, cm)
        w_inter = jnp.exp(m_prev - mx)
        b_end = b[L - 1:L, :]
        mx_end = mx[L - 1:L, :]
        m_new = b_end + mx_end
        decay = jnp.exp(m_prev - mx_end)
        kw_col = jnp.exp(g - mx_end)

        qc = q_ref[pl.ds(r0, L), :]
        kc = k_ref[pl.ds(r0, L), :]
        vc = zm_ref[0, pl.ds(r0, L), 2 * W:3 * W]
        oc = zm_ref[0, pl.ds(r0, L), 3 * W:4 * W]

        kbd = (jnp.concatenate([kc] * 4, axis=0) * cnm_ref[:, 0:W]).astype(BF16)
        s_qk = lax.dot_general(qc, kbd, (((1,), (1,)), ((), ())), preferred_element_type=F32)
        inter = jnp.dot(qc, cn_ref[...].astype(BF16), preferred_element_type=F32)
        vaug = jnp.concatenate([vc, ones_col], axis=1)
        kw = (_expand_heads(kw_col, L) * kc).astype(BF16)
        upd = lax.dot_general(kw, vaug.astype(BF16), (((0,), (0,)), ((), ())), preferred_element_type=F32)
        decay_x = jnp.concatenate([_expand_heads(decay, L), decay], axis=1)
        cn_ref[...] = decay_x * cn_ref[...] + upd * cnm_ref[...]
        m_ref[...] = m_new

        def output():
            g_row = grs_ref[pl.ds(c, 1), :]
            dmat = jnp.where(causal, jnp.exp(jnp.where(causal, g_row - _expand_heads(mx, L), 0.0)), 0.0)
            s_w = (s_qk * dmat).astype(BF16)
            vbd = (jnp.concatenate([vaug] * 4, axis=0) * cnm_ref[...]).astype(BF16)
            intra = jnp.dot(s_w, vbd, preferred_element_type=F32)
            num = _expand_heads(w_inter, L) * inter[:, :W] + intra[:, :W]
            den = w_inter * inter[:, W:] + intra[:, W:]
            hden = jnp.maximum(jnp.abs(den), jnp.exp(-(b + mx)))
            hout = num * _expand_heads(1.0 / hden, L)
            y_ref[0, pl.ds(r0, L), :] = _sigmoid(oc) * hout

        return output

    return chunk


def _mlstm_scratch(tt):
    nct = tt // CHUNK
    return [pltpu.VMEM((tt + 8, 2 * GROUP_W), F32), pltpu.VMEM((tt, GROUP_W), BF16),
            pltpu.VMEM((tt, GROUP_W), F32), pltpu.VMEM((nct, GROUP_W), F32),
            pltpu.VMEM((GROUP_W, GROUP_W + LANES), F32), pltpu.VMEM((1, LANES), F32),
            pltpu.VMEM((GROUP_W, GROUP_W + LANES), F32), pltpu.VMEM((GROUP_W, GROUP_W), BF16),
            pltpu.VMEM((tt, LANES), F32), pltpu.VMEM((tt, LANES), F32), pltpu.VMEM((tt, LANES), F32)]


def _gla_setup(zg_ref, wa_ref, ba_ref, y_ref, q_ref, k_ref, g_ref, o_ref, st_ref, kmask_ref, vmask_ref, *, tt):
    L = CHUNK
    WK = GLA_HEADS * GLA_DK
    WV = GLA_HEADS * GLA_DV

    @pl.when(pl.program_id(1) == 0)
    def _():
        st_ref[...] = jnp.zeros_like(st_ref)
        kmask_ref[...] = ((_iota((WV, WK), 0) // L) == (_iota((WV, WK), 1) // GLA_DK)).astype(F32)
        vmask_ref[...] = ((_iota((WV, WV), 0) // L) == (_iota((WV, WV), 1) // GLA_DV)).astype(F32)

    q_ref[...] = zg_ref[0, :, 0:WK] * (GLA_DK ** -0.5)
    k_ref[...] = zg_ref[0, :, WK:2 * WK]
    a = zg_ref[0, :, 2 * WK + 2 * WV:2 * WK + 2 * WV + LANES]
    pre = jnp.dot(a, wa_ref[...], precision=HI, preferred_element_type=F32) + ba_ref[...]
    g_all = _log_sigmoid(pre) * (1.0 / GLA_TAU)
    tri = _tri(L)
    for r in range(0, tt, L):
        g_ref[r:r + L, :] = _sel_dot(tri, g_all[r:r + L, :])

    hs2 = ((_iota((WK, WV), 0) // GLA_DK) == (_iota((WK, WV), 1) // GLA_DV)).astype(BF16)

    def chunk(c):
        r0 = pl.multiple_of(c * L, L)
        qc = q_ref[pl.ds(r0, L), :]
        kc = k_ref[pl.ds(r0, L), :]
        vc = zg_ref[0, pl.ds(r0, L), 2 * WK:2 * WK + WV]
        b = g_ref[pl.ds(r0, L), :]
        st = st_ref[...]
        qd = (qc * jnp.exp(b)).astype(BF16)
        o = lax.dot_general(qd, st.astype(BF16), (((1,), (1,)), ((), ())), preferred_element_type=F32)
        b_end = b[L - 1:L, :]
        kd = (kc * jnp.exp(b_end - b)).astype(BF16)
        upd = lax.dot_general(vc.astype(BF16), kd, (((0,), (0,)), ((), ())), preferred_element_type=F32)
        st_ref[...] = st * jnp.exp(b_end) + upd * kmask_ref[...]

        def output(o=o):
            S = GLA_SUB
            key_pos = _iota((L, WK), 0)
            vbd = (jnp.concatenate([vc] * GLA_HEADS, axis=0) * vmask_ref[...]).astype(BF16)
            a_rows = []
            for lo in range(S, L, S):
                r = b[lo - 1:lo, :]
                qf = (qc[lo:lo + S, :] * jnp.exp(b[lo:lo + S, :] - r)).astype(BF16)
                kf = kc * jnp.exp(jnp.where(key_pos < lo, r - b, -jnp.inf))
                kbd = (jnp.concatenate([kf] * GLA_HEADS, axis=0) * kmask_ref[...]).astype(BF16)
                a_rows.append(lax.dot_general(qf, kbd, (((1,), (1,)), ((), ())), preferred_element_type=F32))
            o_off = jnp.dot(jnp.concatenate(a_rows, axis=0).astype(BF16), vbd, preferred_element_type=F32)

            tl = _iota((8, WK), 0)
            ps = []
            for lo in range(0, L, S):
                for jj in range(8):
                    j = lo + jj
                    d = b[lo:lo + S, :] - b[j:j + 1, :]
                    e = jnp.exp(jnp.concatenate([jnp.where(tl >= jj, d[0:8, :], -jnp.inf), d[8:, :]], axis=0))
                    ps.append((qc[lo:lo + S, :] * kc[j:j + 1, :] * e).astype(BF16))
                for jj in range(8):
                    j = lo + 8 + jj
                    d = b[lo + 8:lo + S, :] - b[j:j + 1, :]
                    e = jnp.exp(jnp.where(tl >= jj, d, -jnp.inf))
                    ps.append((qc[lo + 8:lo + S, :] * kc[j:j + 1, :] * e).astype(BF16))
            rexp = jnp.dot(jnp.concatenate(ps, axis=0), hs2, preferred_element_type=F32)
            bands = []
            row = 0
            for lo in range(0, L, S):
                acc_a = None
                for jj in range(8):
                    term = rexp[row:row + S, :] * vc[lo + jj:lo + jj + 1, :]
                    acc_a = term if acc_a is None else acc_a + term
                    row += S
                acc_b = None
                for jj in range(8):
                    term = rexp[row:row + 8, :] * vc[lo + 8 + jj:lo + 8 + jj + 1, :]
                    acc_b = term if acc_b is None else acc_b + term
                    row += 8
                band = jnp.concatenate([acc_a[0:8, :], acc_a[8:, :] + acc_b], axis=0)
                bands.append(band if lo == 0 else band + o_off[lo - S:lo, :])
            o_ref[pl.ds(r0, L), :] = o + jnp.concatenate(bands, axis=0)

        return output

    def finish():
        o = o_ref[...]
        hsame = ((_iota((WV, WV), 0) // GLA_DV) == (_iota((WV, WV), 1) // GLA_DV)).astype(BF16)
        ms = _dot_sel(o * o, hsame) * (1.0 / GLA_DV)
        r = zg_ref[0, :, 2 * WK + WV:2 * WK + 2 * WV]
        y_ref[0] = o * lax.rsqrt(ms + EPS) * (r * _sigmoid(r))

    return chunk, finish


def _gla_scratch(tt):
    return [pltpu.VMEM((tt, LANES), F32), pltpu.VMEM((tt, LANES), F32), pltpu.VMEM((tt, LANES), F32),
            pltpu.VMEM((tt, GROUP_W), F32), pltpu.VMEM((GROUP_W, LANES), F32),
            pltpu.VMEM((GROUP_W, LANES), F32), pltpu.VMEM((GROUP_W, GROUP_W), F32)]


def _recurrent_kernel(zm_ref, grow_ref, cw_ref, cb_ref, gbc_ref, gbr_ref, zg_ref, wa_ref, ba_ref,
                      ym_ref, yg_ref, *scratch, tt):
    n_m = len(_mlstm_scratch(tt))
    m_chunk = _mlstm_setup(zm_ref, grow_ref, cw_ref, cb_ref, gbc_ref, gbr_ref, ym_ref, *scratch[:n_m], tt=tt)
    g_chunk, g_finish = _gla_setup(zg_ref, wa_ref, ba_ref, yg_ref, *scratch[n_m:], tt=tt)

    def both(c, carry):
        m_output = m_chunk(c)
        g_output = g_chunk(c)
        m_output()
        g_output()
        return carry

    lax.fori_loop(0, tt // CHUNK, both, 0, unroll=2)
    g_finish()


def _recurrent_mixers(zm, grow, cw, cb, gbc, gbr, zg, wa, ba, tt):
    b, s, _ = zm.shape
    nct = tt // CHUNK
    tile = lambda w: pl.BlockSpec((1, tt, w), lambda i, t: (i, t, 0))
    yshape = jax.ShapeDtypeStruct((b, s, GROUP_W), F32)
    return pl.pallas_call(
        functools.partial(_recurrent_kernel, tt=tt),
        grid=(b, s // tt),
        in_specs=[tile(ZM_W), pl.BlockSpec((1, 2, nct, GROUP_W), lambda i, t: (i, 0, t, 0)),
                  _const_spec(cw.shape), _const_spec(cb.shape), _const_spec(gbc.shape), _const_spec(gbr.shape),
                  tile(ZG_W), _const_spec(wa.shape), _const_spec(ba.shape)],
        out_specs=[tile(GROUP_W), tile(GROUP_W)],
        out_shape=[yshape, yshape],
        scratch_shapes=_mlstm_scratch(tt) + _gla_scratch(tt),
        compiler_params=_cparams(("parallel", "arbitrary")),
        name="mlstm_gla",
    )(zm, grow, cw, cb, gbc, gbr, zg, wa, ba)


CONV_HIST = 32
CONV_SUB = 64


def _conv_kernel(zc_ref, cw_ref, cb_ref, lg_ref, lb_ref, y_ref, xpad_ref, xs_ref, *, tt):
    C = GROUP_W
    span = tt + CONV_HIST - 8

    @pl.when(pl.program_id(1) == 0)
    def _():
        xpad_ref[0:CONV_HIST, :] = jnp.zeros((CONV_HIST, C), F32)

    val = zc_ref[0, :, 0:C]
    gate = zc_ref[0, :, C:2 * C]
    xpad_ref[CONV_HIST:CONV_HIST + tt, :] = val * _sigmoid(gate)
    for sh in range(1, 8):
        xs_ref[sh - 1, 0:span, :] = xpad_ref[pl.ds(sh, span), :]
    sub = min(CONV_SUB, tt)
    for r in range(0, tt, sub):
        acc = jnp.broadcast_to(cb_ref[...], (sub, C))
        for j in range(CONV_WIDTH):
            off = CONV_HIST - (CONV_WIDTH - 1) + j
            sh, base = off % 8, r + off - off % 8
            src = xpad_ref[base:base + sub, :] if sh == 0 else xs_ref[sh - 1, base:base + sub, :]
            acc = acc + cw_ref[j:j + 1, :] * src
        mu = jnp.mean(acc, axis=-1, keepdims=True)
        d = acc - mu
        var = jnp.mean(d * d, axis=-1, keepdims=True)
        yn = d * lax.rsqrt(var + EPS) * lg_ref[...] + lb_ref[...]
        y_ref[0, r:r + sub, :] = yn * _sigmoid(yn)
    xpad_ref[0:CONV_HIST, :] = xpad_ref[tt:tt + CONV_HIST, :]


def _conformer(zc, cw, cb, lg, lb, tt):
    b, s, _ = zc.shape
    return pl.pallas_call(
        functools.partial(_conv_kernel, tt=tt),
        grid=(b, s // tt),
        in_specs=[pl.BlockSpec((1, tt, ZC_W), lambda i, t: (i, t, 0)), _const_spec(cw.shape), _const_spec(cb.shape),
                  _const_spec(lg.shape), _const_spec(lb.shape)],
        out_specs=pl.BlockSpec((1, tt, GROUP_W), lambda i, t: (i, t, 0)),
        out_shape=jax.ShapeDtypeStruct((b, s, GROUP_W), F32),
        scratch_shapes=[pltpu.VMEM((tt + CONV_HIST, GROUP_W), F32), pltpu.VMEM((7, tt + CONV_HIST, GROUP_W), F32)],
        compiler_params=_cparams(("parallel", "arbitrary")),
        name="conformer",
    )(zc, cw, cb, lg, lb)


MLA_HP = 128
MLA_VP = 80


def _mla_prep_kernel(za_ref, rc_ref, rs_ref, qg_ref, kg_ref, wqa_ref, wqb_ref, wka_ref, wkb_ref, wv_ref,
                     q_ref, k_ref, v_ref):
    cq = za_ref[0, :, 0:256]
    ckv = za_ref[0, :, 256:384]
    kr = za_ref[0, :, 384:512]
    qn = (cq * lax.rsqrt(jnp.sum(cq * cq, axis=-1, keepdims=True) * (1.0 / MLA_Q_RANK) + EPS) * qg_ref[...]).astype(BF16)
    kvn = ckv * lax.rsqrt(jnp.mean(ckv * ckv, axis=-1, keepdims=True) + EPS) * kg_ref[...]
    kin = jnp.concatenate([kvn, kr], axis=1).astype(BF16)
    cos = rc_ref[0]
    sin = rs_ref[0]
    qa = jnp.dot(qn, wqa_ref[...], preferred_element_type=F32)
    qb = jnp.dot(qn, wqb_ref[...], preferred_element_type=F32)
    ka = jnp.dot(kin, wka_ref[...], preferred_element_type=F32)
    kb = jnp.dot(kin, wkb_ref[...], preferred_element_type=F32)
    vt = lax.dot_general(wv_ref[...], kin, (((1,), (1,)), ((), ())), preferred_element_type=F32)
    scale = (MLA_NOPE + MLA_ROPE) ** -0.5 * LOG2E
    for h in range(MLA_HEADS):
        sl = slice(h * MLA_HP, (h + 1) * MLA_HP)
        q_ref[0, h] = ((qa[:, sl] * cos + qb[:, sl] * sin) * scale).astype(BF16)
        k_ref[0, h] = (ka[:, sl] * cos + kb[:, sl] * sin).astype(BF16)
        v_ref[0, h, 0:MLA_V, :] = vt[h * MLA_V:(h + 1) * MLA_V, :].astype(BF16)
        v_ref[0, h, MLA_V:MLA_VP, :] = jnp.ones((MLA_VP - MLA_V, vt.shape[1]), BF16)


def _mla_prep(za, rc, rs, qg, kg, wqa, wqb, wka, wkb, wv, tt):
    b, s, _ = za.shape
    hspec = pl.BlockSpec((1, MLA_HEADS, tt, MLA_HP), lambda i, t: (i, 0, t, 0))
    tspec = pl.BlockSpec((1, tt, MLA_HP), lambda i, t: (i, t, 0))
    hshape = jax.ShapeDtypeStruct((b, MLA_HEADS, s, MLA_HP), BF16)
    vspec = pl.BlockSpec((1, MLA_HEADS, MLA_VP, tt), lambda i, t: (i, 0, 0, t))
    vshape = jax.ShapeDtypeStruct((b, MLA_HEADS, MLA_VP, s), BF16)
    return pl.pallas_call(
        _mla_prep_kernel,
        grid=(b, s // tt),
        in_specs=[pl.BlockSpec((1, tt, ZA_W), lambda i, t: (i, t, 0)), tspec, tspec,
                  _const_spec(qg.shape), _const_spec(kg.shape), _const_spec(wqa.shape), _const_spec(wqb.shape),
                  _const_spec(wka.shape), _const_spec(wkb.shape), _const_spec(wv.shape)],
        out_specs=[hspec, hspec, vspec],
        out_shape=[hshape, hshape, vshape],
        compiler_params=_cparams(("parallel", "parallel")),
        name="mla_prep",
    )(za, rc, rs, qg, kg, wqa, wqb, wka, wkb, wv)


def _mla_attn_kernel(qi_ref, ki_ref, q_ref, k_ref, v_ref, y_ref, m_ref, acc_ref, *, tq):
    qi = qi_ref[pl.program_id(1)]
    ki = ki_ref[pl.program_id(1)]

    @pl.when(ki == 0)
    def _():
        m_ref[...] = jnp.full_like(m_ref, -jnp.inf)
        acc_ref[...] = jnp.zeros_like(acc_ref)

    def step(masked):
        def scores(h):
            s = lax.dot_general(k_ref[0, h], q_ref[0, h], (((1,), (1,)), ((), ())), preferred_element_type=F32)
            if masked:
                s = jnp.where(_iota((tq, tq), 0) <= _iota((tq, tq), 1), s, -jnp.inf)
            return s

        def softmax(h, s):
            m_prev = m_ref[h]
            m_new = jnp.maximum(m_prev, jnp.max(s, axis=0, keepdims=True))
            alpha = jnp.exp2(m_prev - m_new)
            p = jnp.exp2(s - m_new)
            m_ref[h] = m_new
            return p.astype(BF16), alpha

        def values(h, p, alpha):
            acc_ref[h] = alpha * acc_ref[h] + jnp.dot(v_ref[0, h], p, preferred_element_type=F32)

        s_next = scores(0)
        for h in range(MLA_HEADS):
            s_cur = s_next
            if h + 1 < MLA_HEADS:
                s_next = scores(h + 1)
            values(h, *softmax(h, s_cur))

    @pl.when(ki < qi)
    def _():
        step(False)

    @pl.when(ki == qi)
    def _():
        step(True)
        ot = jnp.concatenate([acc_ref[h, 0:MLA_V, :] * (1.0 / acc_ref[h, MLA_V:MLA_V + 1, :])
                              for h in range(MLA_HEADS)], axis=0)
        y_ref[0] = ot.T


def _mla_attn(q, k, v, tq):
    b, _, s, _ = q.shape
    nq = s // tq
    pairs = [(qi, ki) for qi in range(nq) for ki in range(qi + 1)]
    qi_tab = jnp.asarray([p[0] for p in pairs], jnp.int32)
    ki_tab = jnp.asarray([p[1] for p in pairs], jnp.int32)
    grid_spec = pltpu.PrefetchScalarGridSpec(
        num_scalar_prefetch=2,
        grid=(b, len(pairs)),
        in_specs=[pl.BlockSpec((1, MLA_HEADS, tq, MLA_HP), lambda i, p, qt, kt: (i, 0, qt[p], 0)),
                  pl.BlockSpec((1, MLA_HEADS, tq, MLA_HP), lambda i, p, qt, kt: (i, 0, kt[p], 0)),
                  pl.BlockSpec((1, MLA_HEADS, MLA_VP, tq), lambda i, p, qt, kt: (i, 0, 0, kt[p]))],
        out_specs=pl.BlockSpec((1, tq, GROUP_W), lambda i, p, qt, kt: (i, qt[p], 0)),
        scratch_shapes=[pltpu.VMEM((MLA_HEADS, 1, tq), F32), pltpu.VMEM((MLA_HEADS, MLA_VP, tq), F32)],
    )
    return pl.pallas_call(
        functools.partial(_mla_attn_kernel, tq=tq),
        grid_spec=grid_spec,
        out_shape=jax.ShapeDtypeStruct((b, s, GROUP_W), F32),
        compiler_params=_cparams(("parallel", "arbitrary")),
        name="mla_attn",
    )(qi_tab, ki_tab, q, k, v)


def _outproj_kernel(*refs, routed):
    if routed:
        ym, ya, yg, yc, h_ref, fg_ref, wo_ref, n2_ref, rt_ref, hn_ref, u_ref, route_ref = refs
    else:
        ym, ya, yg, yc, h_ref, fg_ref, wo_ref, n2_ref, wg_ref, wu_ref, wd_ref, hn_ref = refs
    parts = []
    for gi, y_ref in enumerate((ym, ya, yg, yc)):
        y = y_ref[...]
        yn = y * lax.rsqrt(jnp.mean(y * y, axis=-1, keepdims=True) + EPS)
        parts.append((yn * fg_ref[:, gi * GROUP_W:(gi + 1) * GROUP_W]).astype(BF16))
    ycat = jnp.concatenate(parts, axis=1)
    hn = h_ref[...] + jnp.dot(ycat, wo_ref[...], preferred_element_type=F32)
    u = hn * lax.rsqrt(jnp.mean(hn * hn, axis=-1, keepdims=True) + EPS) * n2_ref[...]
    if not routed:
        hn_ref[...] = hn + _swiglu(u.astype(BF16), wg_ref[...], wu_ref[...], wd_ref[...])
        return
    hn_ref[...] = hn
    u_ref[...] = u
    if routed:
        tm = u.shape[0]
        lane = _iota((tm, LANES), 1)
        u_hi = u.astype(BF16)
        u_lo = (u - u_hi.astype(F32)).astype(BF16)
        rt = rt_ref[...]
        r_hi = rt.astype(BF16)
        r_lo = (rt - r_hi.astype(F32)).astype(BF16)
        d = lambda a, b: jnp.dot(a, b, preferred_element_type=F32)
        t = d(u_hi, jnp.concatenate([r_hi, r_lo], axis=1))
        logits = t[:, :LANES] + (d(u_lo, r_hi) + t[:, LANES:])
        logits = jnp.where(lane < N_EXPERTS, logits, -jnp.inf)
        m1 = jnp.max(logits, axis=-1, keepdims=True)
        i1 = jnp.min(jnp.where(logits == m1, lane, LANES), axis=-1, keepdims=True)
        rest = jnp.where(lane == i1, -jnp.inf, logits)
        m2 = jnp.max(rest, axis=-1, keepdims=True)
        i2 = jnp.min(jnp.where(rest == m2, lane, LANES), axis=-1, keepdims=True)
        e2 = jnp.exp(m2 - m1)
        w1 = 1.0 / (1.0 + e2)
        w2 = e2 / (1.0 + e2)
        route_ref[...] = jnp.where(lane == 0, i1.astype(F32),
                                   jnp.where(lane == 1, i2.astype(F32),
                                             jnp.where(lane == 2, w1, jnp.where(lane == 3, w2, 0.0))))


def _outproj(ym, ya, yg, yc, h, fg, wo, n2, tm, router=None, ffn=None):
    n = h.shape[0]
    routed = router is not None
    row = lambda w: pl.BlockSpec((tm, w), lambda i: (i, 0))
    in_specs = [row(GROUP_W)] * 4 + [row(D_MODEL), _const_spec(fg.shape), _const_spec(wo.shape), _const_spec(n2.shape)]
    args = [ym, ya, yg, yc, h, fg, wo, n2]
    hshape = jax.ShapeDtypeStruct((n, D_MODEL), F32)
    if routed:
        in_specs.append(_const_spec(router.shape))
        args.append(router)
        out_specs = [row(D_MODEL), row(D_MODEL), row(LANES)]
        out_shape = [hshape, hshape, jax.ShapeDtypeStruct((n, LANES), F32)]
    else:
        in_specs += [_const_spec(w.shape) for w in ffn]
        args += list(ffn)
        out_specs, out_shape = row(D_MODEL), hshape
    return pl.pallas_call(
        functools.partial(_outproj_kernel, routed=routed),
        grid=(n // tm,),
        in_specs=in_specs, out_specs=out_specs, out_shape=out_shape,
        compiler_params=_cparams(("parallel",)),
        name="outproj_routed" if routed else "outproj_ffn",
    )(*args)


def _swiglu(x, wg, wu, wd):
    g = jnp.dot(x, wg, preferred_element_type=F32)
    up = jnp.dot(x, wu, preferred_element_type=F32)
    a = (g * _sigmoid(g) * up).astype(BF16)
    return jnp.dot(a, wd, preferred_element_type=F32)


def _moe_kernel(be_ref, tokc_ref, tokn_ref, dstp_ref, dstc_ref, u_hbm, wg_ref, wu_ref, wd_ref, y_hbm,
                xbuf, ybuf, gsem, ssem):
    i = pl.program_id(0)
    nb = pl.num_programs(0)
    bm = xbuf.shape[1]

    def gather_wait(x, s):
        pltpu.make_async_copy(u_hbm.at[pl.ds(0, bm), :], x, gsem.at[s]).wait()

    def scatter_wait(y, s):
        pltpu.make_async_copy(y, y_hbm.at[pl.ds(0, bm), :], ssem.at[s]).wait()

    def scatter_row(y, dst_ref, r, s):
        pltpu.make_async_copy(y.at[pl.ds(r, 1), :], y_hbm.at[pl.ds(dst_ref[0, 0, r], 1), :], ssem.at[s]).start()

    @pl.when(i == 0)
    def _():
        def body(r, c):
            pltpu.make_async_copy(u_hbm.at[pl.ds(tokc_ref[0, 0, r], 1), :], xbuf.at[0, pl.ds(r, 1), :],
                                  gsem.at[0]).start()
            return c
        lax.fori_loop(0, bm, body, 0, unroll=8)
        ybuf[2] = jnp.zeros((bm, D_MODEL), F32)

    c = i % 2
    n = 1 - c
    yc_i = i % 3
    yp_i = (i + 2) % 3
    xc, xn, yc, yp = xbuf.at[c], xbuf.at[n], ybuf.at[yc_i], ybuf.at[yp_i]
    gather_wait(xc, c)

    @pl.when(i >= 2)
    def _():
        scatter_wait(yc, yc_i)

    g = jnp.dot(xc[...].astype(BF16), wg_ref[0, 0], preferred_element_type=F32)
    for r in range(bm):
        pltpu.make_async_copy(u_hbm.at[pl.ds(tokn_ref[0, 0, r], 1), :], xn.at[pl.ds(r, 1), :], gsem.at[n]).start()
    up = jnp.dot(xc[...].astype(BF16), wu_ref[0, 0], preferred_element_type=F32)
    for r in range(bm):
        scatter_row(yp, dstp_ref, r, yp_i)
    a = (g * _sigmoid(g) * up).astype(BF16)
    yc[...] = jnp.dot(a, wd_ref[0, 0], preferred_element_type=F32)

    @pl.when(i == nb - 1)
    def _():
        def body(r, carry):
            scatter_row(yc, dstc_ref, r, yc_i)
            return carry
        lax.fori_loop(0, bm, body, 0, unroll=8)
        scatter_wait(yc, (i + 1) % 3)
        scatter_wait(yp, yp_i)
        scatter_wait(yc, yc_i)
        gather_wait(xn, n)


def _moe(block_e, row_tok, row_dst, u, wg, wu, wd, layer, n_rows):
    n_blocks = block_e.shape[0]
    bm = row_tok.shape[-1]
    wspec = lambda shp: pl.BlockSpec((1, 1) + shp, lambda i, be: (layer, be[i], 0, 0))
    ispec = lambda f: pl.BlockSpec((1, 1, bm), f, memory_space=pltpu.SMEM)
    grid_spec = pltpu.PrefetchScalarGridSpec(
        num_scalar_prefetch=1,
        grid=(n_blocks,),
        in_specs=[ispec(lambda i, be: (i, 0, 0)),
                  ispec(lambda i, be: (jnp.minimum(i + 1, n_blocks - 1), 0, 0)),
                  ispec(lambda i, be: (i, 0, 0)),
                  ispec(lambda i, be: (i + 1, 0, 0)),
                  pl.BlockSpec(memory_space=pl.ANY),
                  wspec((D_MODEL, D_FF)), wspec((D_MODEL, D_FF)), wspec((D_FF, D_MODEL))],
        out_specs=pl.BlockSpec(memory_space=pl.ANY),
        scratch_shapes=[pltpu.VMEM((2, bm, D_MODEL), F32), pltpu.VMEM((3, bm, D_MODEL), F32),
                        pltpu.SemaphoreType.DMA((2,)), pltpu.SemaphoreType.DMA((3,))],
    )
    return pl.pallas_call(
        _moe_kernel,
        grid_spec=grid_spec,
        out_shape=jax.ShapeDtypeStruct((n_rows, D_MODEL), F32),
        compiler_params=_cparams(("arbitrary",)),
        name="moe_experts",
    )(block_e, row_tok, row_tok, row_dst, row_dst, u, wg, wu, wd)


def _combine_kernel(ya_ref, yb_ref, h_ref, route_ref, fg_ref, o_ref, *, final):
    w1 = route_ref[:, 2:3]
    w2 = route_ref[:, 3:4]
    hn = h_ref[...] + w1 * ya_ref[...] + w2 * yb_ref[...]
    if final:
        hn = hn * lax.rsqrt(jnp.mean(hn * hn, axis=-1, keepdims=True) + EPS) * fg_ref[...]
    o_ref[...] = hn


def _combine(y2, h, route, fg, tm, final):
    n = h.shape[0]
    return pl.pallas_call(
        functools.partial(_combine_kernel, final=final),
        grid=(n // tm,),
        in_specs=[pl.BlockSpec((tm, D_MODEL), lambda i: (i, 0)),
                  pl.BlockSpec((tm, D_MODEL), lambda i: (i + n // tm, 0)),
                  pl.BlockSpec((tm, D_MODEL), lambda i: (i, 0)),
                  pl.BlockSpec((tm, LANES), lambda i: (i, 0)),
                  _const_spec(fg.shape)],
        out_specs=pl.BlockSpec((tm, D_MODEL), lambda i: (i, 0)),
        out_shape=jax.ShapeDtypeStruct((n, D_MODEL), F32),
        compiler_params=_cparams(("parallel",)),
        name="moe_combine",
    )(y2, y2, h, route, fg)


def _rmsnorm_kernel(h_ref, g_ref, o_ref):
    x = h_ref[...]
    o_ref[...] = x * lax.rsqrt(jnp.mean(x * x, axis=-1, keepdims=True) + EPS) * g_ref[...]


def _final_norm(h, g, tm):
    n = h.shape[0]
    row = pl.BlockSpec((tm, D_MODEL), lambda i: (i, 0))
    return pl.pallas_call(
        _rmsnorm_kernel, grid=(n // tm,), in_specs=[row, _const_spec(g.shape)], out_specs=row,
        out_shape=jax.ShapeDtypeStruct((n, D_MODEL), F32),
        compiler_params=_cparams(("parallel",)), name="final_norm",
    )(h, g)


def _pad_cols(w, width):
    return jnp.pad(w, ((0, 0), (0, width - w.shape[1])))


def _pad_rows(w, height):
    return jnp.pad(w, ((0, height - w.shape[0]), (0, 0)))


def _inproj_weights(w):
    o = np.cumsum([0, 512, 256, 256, 8, 192, 128, 32, 128, 128, 256, 256, 16, 512])
    seg = lambda i: w[:, o[i]:o[i + 1]]
    gates = seg(3)
    wm = jnp.concatenate([seg(0), seg(1), seg(2), _pad_cols(gates[:, :4], LANES), _pad_cols(gates[:, 4:], LANES)], axis=1)
    wa = jnp.concatenate([_pad_cols(seg(4), 256), seg(5), _pad_cols(seg(6), LANES)], axis=1)
    wg = jnp.concatenate([seg(7), seg(8), seg(9), seg(10), _pad_cols(seg(11), LANES)], axis=1)
    wc = seg(12)
    return [x.astype(BF16) for x in (wm, wa, wg, wc)]


def _mla_weights(w_uq, w_ukv):
    half = MLA_ROPE // 2
    zq = jnp.zeros((MLA_Q_RANK, half), F32)
    qa, qb, ka, kb, wv = [], [], [], [], []
    eye = jnp.eye(LANES, dtype=F32)[:, :MLA_ROPE]
    zk = jnp.zeros((MLA_KV_RANK, MLA_HP), F32)
    for h in range(MLA_HEADS):
        wq = w_uq[:, h * 96:(h + 1) * 96]
        nope, r1, r2 = wq[:, :64], wq[:, 64:64 + half], wq[:, 64 + half:]
        qa.append(_pad_cols(jnp.concatenate([nope, r1, r2], axis=1), MLA_HP))
        qb.append(_pad_cols(jnp.concatenate([jnp.zeros_like(nope), -r2, r1], axis=1), MLA_HP))
        wkv = w_ukv[:, h * 128:(h + 1) * 128]
        kn, vh = wkv[:, :64], wkv[:, 64:]
        e1, e2 = eye[:, :half], eye[:, half:]
        z64 = jnp.zeros((LANES, 64), F32)
        ka.append(jnp.concatenate([_pad_cols(kn, MLA_HP), _pad_cols(jnp.concatenate([z64, e1, e2], axis=1), MLA_HP)], axis=0))
        kb.append(jnp.concatenate([zk, _pad_cols(jnp.concatenate([z64, -e2, e1], axis=1), MLA_HP)], axis=0))
        wv.append(vh.T)
    cat = lambda xs, rows: _pad_rows(jnp.concatenate(xs, axis=1), rows).astype(BF16)
    wvt = _pad_cols(jnp.concatenate(wv, axis=0), 256).astype(BF16)
    return cat(qa, 256), cat(qb, 256), cat(ka, 256), cat(kb, 256), wvt


def _rope_tables(positions):
    half = MLA_ROPE // 2
    inv_freq = 1.0 / (ROPE_BASE ** (jnp.arange(0, MLA_ROPE, 2, dtype=F32) / MLA_ROPE))
    ang = positions.astype(F32)[..., None] * inv_freq
    cos, sin = jnp.cos(ang), jnp.sin(ang)
    shp = cos.shape[:-1]
    rc = jnp.concatenate([jnp.ones(shp + (64,), F32), cos, cos, jnp.zeros(shp + (MLA_HP - 64 - 2 * half,), F32)], axis=-1)
    rs = jnp.concatenate([jnp.zeros(shp + (64,), F32), sin, sin, jnp.zeros(shp + (MLA_HP - 64 - 2 * half,), F32)], axis=-1)
    return rc, rs


def _invert_kernel(dest_ref, lo_ref, hi_ref, out_ref):
    i = pl.program_id(0)
    ch = dest_ref.shape[-1]

    @pl.when(i == 0)
    def _():
        for e in range(lo_ref.shape[0]):
            def init(p, c):
                out_ref[p] = -1
                return c
            lax.fori_loop(lo_ref[e], hi_ref[e], init, 0)

    def body(a, c):
        out_ref[dest_ref[0, 0, a]] = i * ch + a
        return c
    lax.fori_loop(0, ch, body, 0, unroll=8)


def _invert_assignment(dest, pad_lo, pad_hi, rows):
    a = dest.shape[0]
    ch = min(INVERT_CHUNK, a)
    smem = pl.BlockSpec(memory_space=pltpu.SMEM)
    return pl.pallas_call(
        _invert_kernel,
        grid=(a // ch,),
        in_specs=[pl.BlockSpec((1, 1, ch), lambda i: (i, 0, 0), memory_space=pltpu.SMEM), smem, smem],
        out_specs=smem,
        out_shape=jax.ShapeDtypeStruct((rows,), jnp.int32),
        compiler_params=_cparams(("arbitrary",)),
        name="moe_invert",
    )(dest.reshape(a // ch, 1, ch), pad_lo, pad_hi)


def _routing_tables(route, bm):
    n = route.shape[0]
    a = n * TOP_K
    flat_e = route[:, :TOP_K].astype(jnp.int32).reshape(-1)
    onehot = (flat_e[:, None] == jnp.arange(N_EXPERTS, dtype=jnp.int32)[None, :]).astype(jnp.int32)
    csum = jnp.cumsum(onehot, axis=0)
    rank = jnp.sum(onehot * csum, axis=1) - 1
    counts = csum[-1]
    padded = (counts + bm - 1) // bm * bm
    padded_end = jnp.cumsum(padded)
    padded_start = padded_end - padded
    dest = padded_start[flat_e] + rank
    n_blocks = -(-a // bm) + N_EXPERTS
    rows = n_blocks * bm
    pad_lo = jnp.concatenate([padded_start + counts, padded_end[-1:]]).astype(jnp.int32)
    pad_hi = jnp.concatenate([padded_end, jnp.full((1,), rows, jnp.int32)]).astype(jnp.int32)
    row_asg = _invert_assignment(dest, pad_lo, pad_hi, rows)
    row_tok = jnp.maximum(row_asg, 0) // TOP_K
    pad_rank = jnp.cumsum((row_asg < 0).astype(jnp.int32)) - 1
    slot_row = (row_asg % TOP_K) * n + row_asg // TOP_K
    row_dst = jnp.where(row_asg >= 0, slot_row, a + pad_rank)
    first = a + (rows - a) + jnp.arange(bm, dtype=jnp.int32)
    row_dst = jnp.concatenate([first, row_dst]).reshape(n_blocks + 1, 1, bm)
    block_start = jnp.arange(n_blocks, dtype=jnp.int32) * bm
    block_e = jnp.sum((padded_end[None, :] <= block_start[:, None]).astype(jnp.int32), axis=1)
    block_e = jnp.minimum(block_e, N_EXPERTS - 1).astype(jnp.int32)
    return block_e, row_tok.reshape(n_blocks, 1, bm), row_dst, rows + bm


def kernel(x, positions, norm1_g, w_in, mlstm_conv_w, mlstm_conv_b, mlstm_gate_b, mla_q_norm_g, mla_w_uq,
           mla_kv_norm_g, mla_w_ukv, gla_w_alpha, gla_b_alpha, conv_w, conv_b, conv_ln_g, conv_ln_b, fuse_g,
           w_out, norm2_g, ffn_w_gate, ffn_w_up, ffn_w_down, moe_router, moe_w_gate, moe_w_up, moe_w_down,
           final_norm_g):
    bsz, seq, _ = x.shape
    n = bsz * seq
    depth = w_in.shape[0]
    tm = min(512, n)
    tm2 = min(1024, n)
    tt = min(1024, seq)
    tq = min(512, seq)
    nct = seq // CHUNK
    row2 = lambda v: v.reshape(1, -1).astype(F32)

    rc, rs = _rope_tables(positions)
    h = x.reshape(n, D_MODEL)
    for l in range(depth):
        wm, wa, wg, wc = _inproj_weights(w_in[l])
        zm, za, zg, zc, zgate = _inproj(h, row2(norm1_g[l]), wm, wa, wg, wc, tm2)
        zm = zm.reshape(bsz, seq, ZM_W)

        gates = jnp.concatenate([zgate[:, 0:MLSTM_HEADS], zgate[:, LANES:LANES + MLSTM_HEADS]], axis=-1)
        grow = gates.reshape(bsz, nct, CHUNK, 2, MLSTM_HEADS).transpose(0, 3, 1, 4, 2).reshape(bsz, 2, nct, GROUP_W)
        gb = mlstm_gate_b[l].astype(F32)
        gbc = _pad_cols(gb.reshape(2, MLSTM_HEADS), LANES)
        gbr = jnp.repeat(gb.reshape(2, MLSTM_HEADS), CHUNK, axis=1)
        ym, yg = _recurrent_mixers(zm, grow, mlstm_conv_w[l], row2(mlstm_conv_b[l]), gbc, gbr,
                                   zg.reshape(bsz, seq, ZG_W), _pad_rows(gla_w_alpha[l], LANES),
                                   row2(gla_b_alpha[l]), tt)

        wqa, wqb, wka, wkb, wv = _mla_weights(mla_w_uq[l], mla_w_ukv[l])
        qh, kh, vh = _mla_prep(za.reshape(bsz, seq, ZA_W), rc, rs, _pad_cols(row2(mla_q_norm_g[l]), 256),
                               row2(mla_kv_norm_g[l]), wqa, wqb, wka, wkb, wv, tt)
        ya = _mla_attn(qh, kh, vh, tq)

        yc = _conformer(zc.reshape(bsz, seq, ZC_W), conv_w[l], row2(conv_b[l]), row2(conv_ln_g[l]),
                        row2(conv_ln_b[l]), tt)

        flat = lambda y: y.reshape(n, GROUP_W)
        j = l // 2
        if l % 2 == 0:
            ffn = (ffn_w_gate[j].astype(BF16), ffn_w_up[j].astype(BF16), ffn_w_down[j].astype(BF16))
            h = _outproj(flat(ym), flat(ya), flat(yg), flat(yc), h, row2(fuse_g[l]), w_out[l].astype(BF16),
                         row2(norm2_g[l]), tm, ffn=ffn)
            if l == depth - 1:
                h = _final_norm(h, row2(final_norm_g), tm2)
        else:
            h, u, route = _outproj(flat(ym), flat(ya), flat(yg), flat(yc), h, row2(fuse_g[l]),
                                   w_out[l].astype(BF16), row2(norm2_g[l]), tm2,
                                   router=_pad_cols(moe_router[j], LANES))
            block_e, row_tok, row_dst, n_rows = _routing_tables(route, MOE_BM)
            y2 = _moe(block_e, row_tok, row_dst, u, moe_w_gate.astype(BF16), moe_w_up.astype(BF16),
                      moe_w_down.astype(BF16), j, n_rows)
            h = _combine(y2, h, route, row2(final_norm_g), tm2, l == depth - 1)
    return h.reshape(bsz, seq, D_MODEL)
```

```python
import functools

import jax
import jax.numpy as jnp
import numpy as np
from jax import lax
from jax.experimental import pallas as pl
from jax.experimental.pallas import tpu as pltpu

F32 = jnp.float32
BF16 = jnp.bfloat16

D_MODEL = 1024
GROUP_W = 256
EPS = 1e-6
LOG2E = 1.4426950408889634
LANES = 128
CHUNK = 64
MLSTM_HEADS = 4
MLSTM_DH = 64
MLSTM_CONV = 4
MLA_HEADS = 4
MLA_NOPE = 64
MLA_ROPE = 32
MLA_V = 64
MLA_Q_RANK = 192
MLA_KV_RANK = 128
ROPE_BASE = 10000.0
GLA_HEADS = 4
GLA_DK = 32
GLA_DV = 64
GLA_GATE_RANK = 16
GLA_TAU = 16.0
GLA_SUB = 16
CONV_WIDTH = 31
D_FF = 2816
N_EXPERTS = 8
TOP_K = 2
MOE_BM = 256
INVERT_CHUNK = 8192
VMEM_LIMIT = 56 * 1024 * 1024

ZM_W = 1280
ZA_W = 512
ZG_W = 896
ZC_W = 512


def _cparams(sem):
    return pltpu.CompilerParams(dimension_semantics=sem, vmem_limit_bytes=VMEM_LIMIT)


def _const_spec(shape):
    nd = len(shape)
    return pl.BlockSpec(shape, lambda *_: (0,) * nd, pipeline_mode=pl.Buffered(1))


def _sigmoid(x):
    return 1.0 / (1.0 + jnp.exp(-x))


def _log_sigmoid(x):
    return jnp.minimum(x, 0.0) - jnp.log(1.0 + jnp.exp(-jnp.abs(x)))


def _iota(shape, dim):
    return lax.broadcasted_iota(jnp.int32, shape, dim)


def _tri(n):
    return (_iota((n, n), 0) >= _iota((n, n), 1)).astype(BF16)


def _split3(x):
    hi = x.astype(BF16)
    r1 = x - hi.astype(F32)
    mid = r1.astype(BF16)
    lo = (r1 - mid.astype(F32)).astype(BF16)
    return hi, mid, lo


def _sel_dot(sel, x):
    hi, mid, lo = _split3(x)
    d = lambda p: jnp.dot(sel, p, preferred_element_type=F32)
    return d(hi) + d(mid) + d(lo)


def _dot_sel(x, sel):
    hi, mid, lo = _split3(x)
    d = lambda p: jnp.dot(p, sel, preferred_element_type=F32)
    return d(hi) + d(mid) + d(lo)


def _inproj_kernel(*refs, after_moe):
    if after_moe:
        (ya_ref, yb_ref, route_ref, h_ref, g_ref, wm_ref, wa_ref, wg_ref, wc_ref,
         zm_ref, za_ref, zg_ref, zc_ref, gate_ref, hn_ref) = refs
        x = h_ref[...] + route_ref[:, 2:3] * ya_ref[...] + route_ref[:, 3:4] * yb_ref[...]
        hn_ref[...] = x
    else:
        h_ref, g_ref, wm_ref, wa_ref, wg_ref, wc_ref, zm_ref, za_ref, zg_ref, zc_ref, gate_ref = refs
        x = h_ref[...]
    ms = jnp.mean(x * x, axis=-1, keepdims=True)
    xn = (x * lax.rsqrt(ms + EPS) * g_ref[...]).astype(BF16)
    zm = jnp.dot(xn, wm_ref[...], preferred_element_type=F32)
    zm_ref[...] = zm
    gate_ref[...] = zm[:, ZM_W - 2 * LANES:]
    za_ref[...] = jnp.dot(xn, wa_ref[...], preferred_element_type=F32)
    zg_ref[...] = jnp.dot(xn, wg_ref[...], preferred_element_type=F32)
    zc_ref[...] = jnp.dot(xn, wc_ref[...], preferred_element_type=F32)


def _inproj(h, g, wm, wa, wg, wc, tm, moe=None):
    n = h.shape[0]
    row = lambda w: pl.BlockSpec((tm, w), lambda i: (i, 0))
    in_specs = [row(D_MODEL), _const_spec((1, D_MODEL)), _const_spec(wm.shape), _const_spec(wa.shape),
                _const_spec(wg.shape), _const_spec(wc.shape)]
    args = [h, g, wm, wa, wg, wc]
    widths = [ZM_W, ZA_W, ZG_W, ZC_W, 2 * LANES]
    if moe is not None:
        y2, route = moe
        in_specs = [row(D_MODEL), pl.BlockSpec((tm, D_MODEL), lambda i: (i + n // tm, 0)), row(LANES)] + in_specs
        args = [y2, y2, route] + args
        widths.append(D_MODEL)
    return pl.pallas_call(
        functools.partial(_inproj_kernel, after_moe=moe is not None),
        grid=(n // tm,),
        in_specs=in_specs,
        out_specs=[row(w) for w in widths],
        out_shape=[jax.ShapeDtypeStruct((n, w), F32) for w in widths],
        compiler_params=_cparams(("parallel",)),
        name="inproj_moe" if moe is not None else "inproj",
    )(*args)


def _expand_heads(x, width):
    r = x.shape[0]
    lane = _iota((r, 4 * width), 1)
    out = jnp.broadcast_to(x[:, 3:4], (r, 4 * width))
    for h in (2, 1, 0):
        out = jnp.where(lane < (h + 1) * width, jnp.broadcast_to(x[:, h:h + 1], (r, 4 * width)), out)
    return out


def _mlstm_setup(zm_ref, grow_ref, cw_ref, cb_ref, gbc_ref, gbr_ref, y_ref,
                 xpad_ref, q_ref, k_ref, grs_ref, cn_ref, m_ref, cnm_ref, trit_ref, b_ref, gc_ref, cm_ref, *, tt):
    L = CHUNK
    W = MLSTM_HEADS * MLSTM_DH

    @pl.when(pl.program_id(1) == 0)
    def _():
        xpad_ref[0:8, :] = jnp.zeros((8, 2 * W), F32)
        cn_ref[...] = jnp.zeros_like(cn_ref)
        m_ref[...] = jnp.zeros_like(m_ref)
        hsame = (_iota((W, W), 0) // L) == (_iota((W, W), 1) // L)
        ncol = (_iota((W, LANES), 0) // L) == _iota((W, LANES), 1)
        cnm_ref[...] = jnp.concatenate([hsame, ncol], axis=1).astype(F32)
        trit_ref[...] = (hsame & ((_iota((W, W), 0) % L) <= (_iota((W, W), 1) % L))).astype(BF16)

    xpad_ref[8:8 + tt, :] = zm_ref[0, :, 0:2 * W]
    for r in range(0, tt, L):
        acc = jnp.broadcast_to(cb_ref[...], (L, 2 * W))
        for j in range(MLSTM_CONV):
            acc = acc + cw_ref[j:j + 1, :] * xpad_ref[pl.ds(r + 8 - (MLSTM_CONV - 1) + j, L), :]
        qk = acc * _sigmoid(acc)
        q_ref[r:r + L, :] = (qk[:, :W] * (MLSTM_DH ** -0.5)).astype(BF16)
        k_ref[r:r + L, :] = qk[:, W:]
    xpad_ref[0:8, :] = xpad_ref[tt:tt + 8, :]

    i_row = grow_ref[0, 0] + gbr_ref[0:1, :]
    lf_row = _log_sigmoid(grow_ref[0, 1] + gbr_ref[1:2, :])
    grs_ref[...] = i_row - _dot_sel(lf_row, trit_ref[...])

    tri = _tri(L)
    lf_all = _log_sigmoid(zm_ref[0, :, 4 * W + LANES:4 * W + 2 * LANES] + gbc_ref[1:2, :])
    for r in range(0, tt, L):
        b_ref[r:r + L, :] = _sel_dot(tri, lf_all[r:r + L, :])
    g_all = zm_ref[0, :, 4 * W:4 * W + LANES] + gbc_ref[0:1, :] - b_ref[...]
    gc_ref[...] = g_all
    pos = _iota((tt, LANES), 0) % L
    cm_all = g_all
    s = 1
    while s < L:
        cm_all = jnp.maximum(cm_all, jnp.where(pos >= s, pltpu.roll(cm_all, s, 0), -jnp.inf))
        s *= 2
    cm_ref[...] = cm_all

    row_t = _iota((L, W), 0)
    lane_j = _iota((L, W), 1) % L
    causal = lane_j <= row_t
    ones_col = (_iota((L, LANES), 1) < MLSTM_HEADS).astype(F32)

    def chunk(c):
        r0 = pl.multiple_of(c * L, L)
        b = b_ref[pl.ds(r0, L), :]
        g = gc_ref[pl.ds(r0, L), :]
        cm = cm_ref[pl.ds(r0, L), :]
        m_prev = m_ref[...]
        mx = jnp.maximum(m_prev, cm)
        w_inter = jnp.exp(m_prev - mx)
        b_end = b[L - 1:L, :]
        mx_end = mx[L - 1:L, :]
        m_new = b_end + mx_end
        decay = jnp.exp(m_prev - mx_end)
        kw_col = jnp.exp(g - mx_end)

        qc = q_ref[pl.ds(r0, L), :]
        kc = k_ref[pl.ds(r0, L), :]
        vc = zm_ref[0, pl.ds(r0, L), 2 * W:3 * W]
        oc = zm_ref[0, pl.ds(r0, L), 3 * W:4 * W]

        kbd = (jnp.concatenate([kc] * 4, axis=0) * cnm_ref[:, 0:W]).astype(BF16)
        s_qk = lax.dot_general(qc, kbd, (((1,), (1,)), ((), ())), preferred_element_type=F32)
        inter = jnp.dot(qc, cn_ref[...].astype(BF16), preferred_element_type=F32)
        vaug = jnp.concatenate([vc, ones_col], axis=1)
        kw = (_expand_heads(kw_col, L) * kc).astype(BF16)
        upd = lax.dot_general(kw, vaug.astype(BF16), (((0,), (0,)), ((), ())), preferred_element_type=F32)
        decay_x = jnp.concatenate([_expand_heads(decay, L), decay], axis=1)
        cn_ref[...] = decay_x * cn_ref[...] + upd * cnm_ref[...]
        m_ref[...] = m_new

        def output():
            g_row = grs_ref[pl.ds(c, 1), :]
            dmat = jnp.where(causal, jnp.exp(jnp.where(causal, g_row - _expand_heads(mx, L), 0.0)), 0.0)
            s_w = (s_qk * dmat).astype(BF16)
            vbd = (jnp.concatenate([vaug] * 4, axis=0) * cnm_ref[...]).astype(BF16)
            intra = jnp.dot(s_w, vbd, preferred_element_type=F32)
            num = _expand_heads(w_inter, L) * inter[:, :W] + intra[:, :W]
            den = w_inter * inter[:, W:] + intra[:, W:]
            hden = jnp.maximum(jnp.abs(den), jnp.exp(-(b + mx)))
            hout = num * _expand_heads(1.0 / hden, L)
            y_ref[0, pl.ds(r0, L), :] = _sigmoid(oc) * hout

        return output

    return chunk


def _mlstm_scratch(tt):
    nct = tt // CHUNK
    return [pltpu.VMEM((tt + 8, 2 * GROUP_W), F32), pltpu.VMEM((tt, GROUP_W), BF16),
            pltpu.VMEM((tt, GROUP_W), F32), pltpu.VMEM((nct, GROUP_W), F32),
            pltpu.VMEM((GROUP_W, GROUP_W + LANES), F32), pltpu.VMEM((1, LANES), F32),
            pltpu.VMEM((GROUP_W, GROUP_W + LANES), F32), pltpu.VMEM((GROUP_W, GROUP_W), BF16),
            pltpu.VMEM((tt, LANES), F32), pltpu.VMEM((tt, LANES), F32), pltpu.VMEM((tt, LANES), F32)]


def _gla_setup(zg_ref, wa_ref, ba_ref, y_ref, q_ref, k_ref, g_ref, o_ref, st_ref, kmask_ref, vmask_ref, *, tt):
    L = CHUNK
    WK = GLA_HEADS * GLA_DK
    WV = GLA_HEADS * GLA_DV

    @pl.when(pl.program_id(1) == 0)
    def _():
        st_ref[...] = jnp.zeros_like(st_ref)
        kmask_ref[...] = ((_iota((WV, WK), 0) // L) == (_iota((WV, WK), 1) // GLA_DK)).astype(F32)
        vmask_ref[...] = ((_iota((WV, WV), 0) // L) == (_iota((WV, WV), 1) // GLA_DV)).astype(F32)

    q_ref[...] = zg_ref[0, :, 0:WK] * (GLA_DK ** -0.5)
    k_ref[...] = zg_ref[0, :, WK:2 * WK]
    a = zg_ref[0, :, 2 * WK + 2 * WV:2 * WK + 2 * WV + LANES]
    a_hi = a.astype(BF16)
    a_lo = (a - a_hi.astype(F32)).astype(BF16)
    wa = wa_ref[...]
    w_hi = wa.astype(BF16)
    w_lo = (wa - w_hi.astype(F32)).astype(BF16)
    t = jnp.dot(a_hi, jnp.concatenate([w_hi, w_lo], axis=1), preferred_element_type=F32)
    pre = t[:, :WK] + (jnp.dot(a_lo, w_hi, preferred_element_type=F32) + t[:, WK:]) + ba_ref[...]
    g_all = _log_sigmoid(pre) * (1.0 / GLA_TAU)
    tri = _tri(L)
    for r in range(0, tt, L):
        g_ref[r:r + L, :] = _sel_dot(tri, g_all[r:r + L, :])

    hs2 = ((_iota((WK, WV), 0) // GLA_DK) == (_iota((WK, WV), 1) // GLA_DV)).astype(BF16)

    def chunk(c):
        r0 = pl.multiple_of(c * L, L)
        qc = q_ref[pl.ds(r0, L), :]
        kc = k_ref[pl.ds(r0, L), :]
        vc = zg_ref[0, pl.ds(r0, L), 2 * WK:2 * WK + WV]
        b = g_ref[pl.ds(r0, L), :]
        st = st_ref[...]
        qd = (qc * jnp.exp(b)).astype(BF16)
        o = lax.dot_general(qd, st.astype(BF16), (((1,), (1,)), ((), ())), preferred_element_type=F32)
        b_end = b[L - 1:L, :]
        kd = (kc * jnp.exp(b_end - b)).astype(BF16)
        upd = lax.dot_general(vc.astype(BF16), kd, (((0,), (0,)), ((), ())), preferred_element_type=F32)
        st_ref[...] = st * jnp.exp(b_end) + upd * kmask_ref[...]

        def output(o=o):
            S = GLA_SUB
            key_pos = _iota((L, WK), 0)
            vbd = (jnp.concatenate([vc] * GLA_HEADS, axis=0) * vmask_ref[...]).astype(BF16)
            a_rows = []
            for lo in range(S, L, S):
                r = b[lo - 1:lo, :]
                qf = (qc[lo:lo + S, :] * jnp.exp(b[lo:lo + S, :] - r)).astype(BF16)
                kf = kc * jnp.exp(jnp.where(key_pos < lo, r - b, -jnp.inf))
                kbd = (jnp.concatenate([kf] * GLA_HEADS, axis=0) * kmask_ref[...]).astype(BF16)
                a_rows.append(lax.dot_general(qf, kbd, (((1,), (1,)), ((), ())), preferred_element_type=F32))
            o_off = jnp.dot(jnp.concatenate(a_rows, axis=0).astype(BF16), vbd, preferred_element_type=F32)

            tl = _iota((8, WK), 0)
            ps = []
            for lo in range(0, L, S):
                for jj in range(8):
                    j = lo + jj
                    d = b[lo:lo + S, :] - b[j:j + 1, :]
                    e = jnp.exp(jnp.concatenate([jnp.where(tl >= jj, d[0:8, :], -jnp.inf), d[8:, :]], axis=0))
                    ps.append((qc[lo:lo + S, :] * kc[j:j + 1, :] * e).astype(BF16))
                for jj in range(8):
                    j = lo + 8 + jj
                    d = b[lo + 8:lo + S, :] - b[j:j + 1, :]
                    e = jnp.exp(jnp.where(tl >= jj, d, -jnp.inf))
                    ps.append((qc[lo + 8:lo + S, :] * kc[j:j + 1, :] * e).astype(BF16))
            rexp = jnp.dot(jnp.concatenate(ps, axis=0), hs2, preferred_element_type=F32)
            bands = []
            row = 0
            for lo in range(0, L, S):
                acc_a = None
                for jj in range(8):
                    term = rexp[row:row + S, :] * vc[lo + jj:lo + jj + 1, :]
                    acc_a = term if acc_a is None else acc_a + term
                    row += S
                acc_b = None
                for jj in range(8):
                    term = rexp[row:row + 8, :] * vc[lo + 8 + jj:lo + 8 + jj + 1, :]
                    acc_b = term if acc_b is None else acc_b + term
                    row += 8
                band = jnp.concatenate([acc_a[0:8, :], acc_a[8:, :] + acc_b], axis=0)
                bands.append(band if lo == 0 else band + o_off[lo - S:lo, :])
            o_ref[pl.ds(r0, L), :] = o + jnp.concatenate(bands, axis=0)

        return output

    def finish():
        o = o_ref[...]
        hsame = ((_iota((WV, WV), 0) // GLA_DV) == (_iota((WV, WV), 1) // GLA_DV)).astype(BF16)
        ms = _dot_sel(o * o, hsame) * (1.0 / GLA_DV)
        r = zg_ref[0, :, 2 * WK + WV:2 * WK + 2 * WV]
        y_ref[0] = o * lax.rsqrt(ms + EPS) * (r * _sigmoid(r))

    return chunk, finish


def _gla_scratch(tt):
    return [pltpu.VMEM((tt, LANES), F32), pltpu.VMEM((tt, LANES), F32), pltpu.VMEM((tt, LANES), F32),
            pltpu.VMEM((tt, GROUP_W), F32), pltpu.VMEM((GROUP_W, LANES), F32),
            pltpu.VMEM((GROUP_W, LANES), F32), pltpu.VMEM((GROUP_W, GROUP_W), F32)]


def _recurrent_kernel(zm_ref, grow_ref, cw_ref, cb_ref, gbc_ref, gbr_ref, zg_ref, wa_ref, ba_ref,
                      ym_ref, yg_ref, *scratch, tt):
    n_m = len(_mlstm_scratch(tt))
    m_chunk = _mlstm_setup(zm_ref, grow_ref, cw_ref, cb_ref, gbc_ref, gbr_ref, ym_ref, *scratch[:n_m], tt=tt)
    g_chunk, g_finish = _gla_setup(zg_ref, wa_ref, ba_ref, yg_ref, *scratch[n_m:], tt=tt)

    def both(c, carry):
        m_output = m_chunk(c)
        g_output = g_chunk(c)
        m_output()
        g_output()
        return carry

    lax.fori_loop(0, tt // CHUNK, both, 0, unroll=2)
    g_finish()


def _recurrent_mixers(zm, grow, cw, cb, gbc, gbr, zg, wa, ba, tt):
    b, s, _ = zm.shape
    nct = tt // CHUNK
    tile = lambda w: pl.BlockSpec((1, tt, w), lambda i, t: (i, t, 0))
    yshape = jax.ShapeDtypeStruct((b, s, GROUP_W), F32)
    return pl.pallas_call(
        functools.partial(_recurrent_kernel, tt=tt),
        grid=(b, s // tt),
        in_specs=[tile(ZM_W), pl.BlockSpec((1, 2, nct, GROUP_W), lambda i, t: (i, 0, t, 0)),
                  _const_spec(cw.shape), _const_spec(cb.shape), _const_spec(gbc.shape), _const_spec(gbr.shape),
                  tile(ZG_W), _const_spec(wa.shape), _const_spec(ba.shape)],
        out_specs=[tile(GROUP_W), tile(GROUP_W)],
        out_shape=[yshape, yshape],
        scratch_shapes=_mlstm_scratch(tt) + _gla_scratch(tt),
        compiler_params=_cparams(("parallel", "arbitrary")),
        name="mlstm_gla",
    )(zm, grow, cw, cb, gbc, gbr, zg, wa, ba)


CONV_HIST = 32
CONV_SUB = 64


def _conv_kernel(zc_ref, cw_ref, cb_ref, lg_ref, lb_ref, y_ref, xpad_ref, xs_ref, *, tt):
    C = GROUP_W
    span = tt + CONV_HIST - 8

    @pl.when(pl.program_id(1) == 0)
    def _():
        xpad_ref[0:CONV_HIST, :] = jnp.zeros((CONV_HIST, C), F32)

    val = zc_ref[0, :, 0:C]
    gate = zc_ref[0, :, C:2 * C]
    xpad_ref[CONV_HIST:CONV_HIST + tt, :] = val * _sigmoid(gate)
    for sh in range(1, 8):
        xs_ref[sh - 1, 0:span, :] = xpad_ref[pl.ds(sh, span), :]
    sub = min(CONV_SUB, tt)
    for r in range(0, tt, sub):
        acc = jnp.broadcast_to(cb_ref[...], (sub, C))
        for j in range(CONV_WIDTH):
            off = CONV_HIST - (CONV_WIDTH - 1) + j
            sh, base = off % 8, r + off - off % 8
            src = xpad_ref[base:base + sub, :] if sh == 0 else xs_ref[sh - 1, base:base + sub, :]
            acc = acc + cw_ref[j:j + 1, :] * src
        mu = jnp.mean(acc, axis=-1, keepdims=True)
        d = acc - mu
        var = jnp.mean(d * d, axis=-1, keepdims=True)
        yn = d * lax.rsqrt(var + EPS) * lg_ref[...] + lb_ref[...]
        y_ref[0, r:r + sub, :] = yn * _sigmoid(yn)
    xpad_ref[0:CONV_HIST, :] = xpad_ref[tt:tt + CONV_HIST, :]


def _conformer(zc, cw, cb, lg, lb, tt):
    b, s, _ = zc.shape
    return pl.pallas_call(
        functools.partial(_conv_kernel, tt=tt),
        grid=(b, s // tt),
        in_specs=[pl.BlockSpec((1, tt, ZC_W), lambda i, t: (i, t, 0)), _const_spec(cw.shape), _const_spec(cb.shape),
                  _const_spec(lg.shape), _const_spec(lb.shape)],
        out_specs=pl.BlockSpec((1, tt, GROUP_W), lambda i, t: (i, t, 0)),
        out_shape=jax.ShapeDtypeStruct((b, s, GROUP_W), F32),
        scratch_shapes=[pltpu.VMEM((tt + CONV_HIST, GROUP_W), F32), pltpu.VMEM((7, tt + CONV_HIST, GROUP_W), F32)],
        compiler_params=_cparams(("parallel", "arbitrary")),
        name="conformer",
    )(zc, cw, cb, lg, lb)


MLA_HP = 128
MLA_VP = 80


def _mla_prep_kernel(za_ref, rc_ref, rs_ref, qg_ref, kg_ref, wqa_ref, wqb_ref, wka_ref, wkb_ref, wv_ref,
                     q_ref, k_ref, v_ref):
    cq = za_ref[0, :, 0:256]
    ckv = za_ref[0, :, 256:384]
    kr = za_ref[0, :, 384:512]
    qn = (cq * lax.rsqrt(jnp.sum(cq * cq, axis=-1, keepdims=True) * (1.0 / MLA_Q_RANK) + EPS) * qg_ref[...]).astype(BF16)
    kvn = ckv * lax.rsqrt(jnp.mean(ckv * ckv, axis=-1, keepdims=True) + EPS) * kg_ref[...]
    kin = jnp.concatenate([kvn, kr], axis=1).astype(BF16)
    cos = rc_ref[0]
    sin = rs_ref[0]
    qa = jnp.dot(qn, wqa_ref[...], preferred_element_type=F32)
    qb = jnp.dot(qn, wqb_ref[...], preferred_element_type=F32)
    ka = jnp.dot(kin, wka_ref[...], preferred_element_type=F32)
    kb = jnp.dot(kin, wkb_ref[...], preferred_element_type=F32)
    vt = lax.dot_general(wv_ref[...], kin, (((1,), (1,)), ((), ())), preferred_element_type=F32)
    scale = (MLA_NOPE + MLA_ROPE) ** -0.5 * LOG2E
    for h in range(MLA_HEADS):
        sl = slice(h * MLA_HP, (h + 1) * MLA_HP)
        q_ref[0, h] = ((qa[:, sl] * cos + qb[:, sl] * sin) * scale).astype(BF16)
        k_ref[0, h] = (ka[:, sl] * cos + kb[:, sl] * sin).astype(BF16)
        v_ref[0, h, 0:MLA_V, :] = vt[h * MLA_V:(h + 1) * MLA_V, :].astype(BF16)
        v_ref[0, h, MLA_V:MLA_VP, :] = jnp.ones((MLA_VP - MLA_V, vt.shape[1]), BF16)


def _mla_prep(za, rc, rs, qg, kg, wqa, wqb, wka, wkb, wv, tt):
    b, s, _ = za.shape
    hspec = pl.BlockSpec((1, MLA_HEADS, tt, MLA_HP), lambda i, t: (i, 0, t, 0))
    tspec = pl.BlockSpec((1, tt, MLA_HP), lambda i, t: (i, t, 0))
    hshape = jax.ShapeDtypeStruct((b, MLA_HEADS, s, MLA_HP), BF16)
    vspec = pl.BlockSpec((1, MLA_HEADS, MLA_VP, tt), lambda i, t: (i, 0, 0, t))
    vshape = jax.ShapeDtypeStruct((b, MLA_HEADS, MLA_VP, s), BF16)
    return pl.pallas_call(
        _mla_prep_kernel,
        grid=(b, s // tt),
        in_specs=[pl.BlockSpec((1, tt, ZA_W), lambda i, t: (i, t, 0)), tspec, tspec,
                  _const_spec(qg.shape), _const_spec(kg.shape), _const_spec(wqa.shape), _const_spec(wqb.shape),
                  _const_spec(wka.shape), _const_spec(wkb.shape), _const_spec(wv.shape)],
        out_specs=[hspec, hspec, vspec],
        out_shape=[hshape, hshape, vshape],
        compiler_params=_cparams(("parallel", "parallel")),
        name="mla_prep",
    )(za, rc, rs, qg, kg, wqa, wqb, wka, wkb, wv)


def _mla_attn_kernel(qi_ref, ki_ref, q_ref, k_ref, v_ref, y_ref, m_ref, acc_ref, *, tq):
    qi = qi_ref[pl.program_id(1)]
    ki = ki_ref[pl.program_id(1)]

    @pl.when(ki == 0)
    def _():
        m_ref[...] = jnp.full_like(m_ref, -jnp.inf)
        acc_ref[...] = jnp.zeros_like(acc_ref)

    def step(masked):
        def scores(h):
            s = lax.dot_general(k_ref[0, h], q_ref[0, h], (((1,), (1,)), ((), ())), preferred_element_type=F32)
            if masked:
                s = jnp.where(_iota((tq, tq), 0) <= _iota((tq, tq), 1), s, -jnp.inf)
            return s

        def softmax(h, s):
            m_prev = m_ref[h]
            m_new = jnp.maximum(m_prev, jnp.max(s, axis=0, keepdims=True))
            alpha = jnp.exp2(m_prev - m_new)
            p = jnp.exp2(s - m_new)
            m_ref[h] = m_new
            return p.astype(BF16), alpha

        def values(h, p, alpha):
            acc_ref[h] = alpha * acc_ref[h] + jnp.dot(v_ref[0, h], p, preferred_element_type=F32)

        s_next = scores(0)
        for h in range(MLA_HEADS):
            s_cur = s_next
            if h + 1 < MLA_HEADS:
                s_next = scores(h + 1)
            values(h, *softmax(h, s_cur))

    @pl.when(ki < qi)
    def _():
        step(False)

    @pl.when(ki == qi)
    def _():
        step(True)
        ot = jnp.concatenate([acc_ref[h, 0:MLA_V, :] * (1.0 / acc_ref[h, MLA_V:MLA_V + 1, :])
                              for h in range(MLA_HEADS)], axis=0)
        y_ref[0] = ot.T


def _mla_attn(q, k, v, tq):
    b, _, s, _ = q.shape
    nq = s // tq
    pairs = [(qi, ki) for qi in range(nq) for ki in range(qi + 1)]
    qi_tab = jnp.asarray([p[0] for p in pairs], jnp.int32)
    ki_tab = jnp.asarray([p[1] for p in pairs], jnp.int32)
    grid_spec = pltpu.PrefetchScalarGridSpec(
        num_scalar_prefetch=2,
        grid=(b, len(pairs)),
        in_specs=[pl.BlockSpec((1, MLA_HEADS, tq, MLA_HP), lambda i, p, qt, kt: (i, 0, qt[p], 0)),
                  pl.BlockSpec((1, MLA_HEADS, tq, MLA_HP), lambda i, p, qt, kt: (i, 0, kt[p], 0)),
                  pl.BlockSpec((1, MLA_HEADS, MLA_VP, tq), lambda i, p, qt, kt: (i, 0, 0, kt[p]))],
        out_specs=pl.BlockSpec((1, tq, GROUP_W), lambda i, p, qt, kt: (i, qt[p], 0)),
        scratch_shapes=[pltpu.VMEM((MLA_HEADS, 1, tq), F32), pltpu.VMEM((MLA_HEADS, MLA_VP, tq), F32)],
    )
    return pl.pallas_call(
        functools.partial(_mla_attn_kernel, tq=tq),
        grid_spec=grid_spec,
        out_shape=jax.ShapeDtypeStruct((b, s, GROUP_W), F32),
        compiler_params=_cparams(("parallel", "arbitrary")),
        name="mla_attn",
    )(qi_tab, ki_tab, q, k, v)


def _outproj_kernel(*refs, routed):
    if routed:
        ym, ya, yg, yc, h_ref, fg_ref, wo_ref, n2_ref, rt_ref, hn_ref, u_ref, route_ref = refs
    else:
        ym, ya, yg, yc, h_ref, fg_ref, wo_ref, n2_ref, wg_ref, wu_ref, wd_ref, hn_ref = refs
    parts = []
    for gi, y_ref in enumerate((ym, ya, yg, yc)):
        y = y_ref[...]
        yn = y * lax.rsqrt(jnp.mean(y * y, axis=-1, keepdims=True) + EPS)
        parts.append((yn * fg_ref[:, gi * GROUP_W:(gi + 1) * GROUP_W]).astype(BF16))
    ycat = jnp.concatenate(parts, axis=1)
    hn = h_ref[...] + jnp.dot(ycat, wo_ref[...], preferred_element_type=F32)
    u = hn * lax.rsqrt(jnp.mean(hn * hn, axis=-1, keepdims=True) + EPS) * n2_ref[...]
    if not routed:
        hn_ref[...] = hn + _swiglu(u.astype(BF16), wg_ref[...], wu_ref[...], wd_ref[...])
        return
    hn_ref[...] = hn
    u_ref[...] = u
    if routed:
        tm = u.shape[0]
        lane = _iota((tm, LANES), 1)
        u_hi = u.astype(BF16)
        u_lo = (u - u_hi.astype(F32)).astype(BF16)
        rt = rt_ref[...]
        r_hi = rt.astype(BF16)
        r_lo = (rt - r_hi.astype(F32)).astype(BF16)
        d = lambda a, b: jnp.dot(a, b, preferred_element_type=F32)
        t = d(u_hi, jnp.concatenate([r_hi, r_lo], axis=1))
        logits = t[:, :LANES] + (d(u_lo, r_hi) + t[:, LANES:])
        logits = jnp.where(lane < N_EXPERTS, logits, -jnp.inf)
        m1 = jnp.max(logits, axis=-1, keepdims=True)
        i1 = jnp.min(jnp.where(logits == m1, lane, LANES), axis=-1, keepdims=True)
        rest = jnp.where(lane == i1, -jnp.inf, logits)
        m2 = jnp.max(rest, axis=-1, keepdims=True)
        i2 = jnp.min(jnp.where(rest == m2, lane, LANES), axis=-1, keepdims=True)
        e2 = jnp.exp(m2 - m1)
        w1 = 1.0 / (1.0 + e2)
        w2 = e2 / (1.0 + e2)
        route_ref[...] = jnp.where(lane == 0, i1.astype(F32),
                                   jnp.where(lane == 1, i2.astype(F32),
                                             jnp.where(lane == 2, w1, jnp.where(lane == 3, w2, 0.0))))


def _outproj(ym, ya, yg, yc, h, fg, wo, n2, tm, router=None, ffn=None):
    n = h.shape[0]
    routed = router is not None
    row = lambda w: pl.BlockSpec((tm, w), lambda i: (i, 0))
    in_specs = [row(GROUP_W)] * 4 + [row(D_MODEL), _const_spec(fg.shape), _const_spec(wo.shape), _const_spec(n2.shape)]
    args = [ym, ya, yg, yc, h, fg, wo, n2]
    hshape = jax.ShapeDtypeStruct((n, D_MODEL), F32)
    if routed:
        in_specs.append(_const_spec(router.shape))
        args.append(router)
        out_specs = [row(D_MODEL), row(D_MODEL), row(LANES)]
        out_shape = [hshape, hshape, jax.ShapeDtypeStruct((n, LANES), F32)]
    else:
        in_specs += [_const_spec(w.shape) for w in ffn]
        args += list(ffn)
        out_specs, out_shape = row(D_MODEL), hshape
    return pl.pallas_call(
        functools.partial(_outproj_kernel, routed=routed),
        grid=(n // tm,),
        in_specs=in_specs, out_specs=out_specs, out_shape=out_shape,
        compiler_params=_cparams(("parallel",)),
        name="outproj_routed" if routed else "outproj_ffn",
    )(*args)


def _swiglu(x, wg, wu, wd):
    g = jnp.dot(x, wg, preferred_element_type=F32)
    up = jnp.dot(x, wu, preferred_element_type=F32)
    a = (g * _sigmoid(g) * up).astype(BF16)
    return jnp.dot(a, wd, preferred_element_type=F32)


def _moe_kernel(be_ref, tokc_ref, tokn_ref, dstp_ref, dstc_ref, u_hbm, wg_ref, wu_ref, wd_ref, y_hbm,
                xbuf, ybuf, gsem, ssem):
    i = pl.program_id(0)
    nb = pl.num_programs(0)
    bm = xbuf.shape[1]

    def gather_wait(x, s):
        pltpu.make_async_copy(u_hbm.at[pl.ds(0, bm), :], x, gsem.at[s]).wait()

    def scatter_wait(y, s):
        pltpu.make_async_copy(y, y_hbm.at[pl.ds(0, bm), :], ssem.at[s]).wait()

    def scatter_row(y, dst_ref, r, s):
        pltpu.make_async_copy(y.at[pl.ds(r, 1), :], y_hbm.at[pl.ds(dst_ref[0, 0, r], 1), :], ssem.at[s]).start()

    @pl.when(i == 0)
    def _():
        def body(r, c):
            pltpu.make_async_copy(u_hbm.at[pl.ds(tokc_ref[0, 0, r], 1), :], xbuf.at[0, pl.ds(r, 1), :],
                                  gsem.at[0]).start()
            return c
        lax.fori_loop(0, bm, body, 0, unroll=8)
        ybuf[2] = jnp.zeros((bm, D_MODEL), F32)

    c = i % 2
    n = 1 - c
    yc_i = i % 3
    yp_i = (i + 2) % 3
    xc, xn, yc, yp = xbuf.at[c], xbuf.at[n], ybuf.at[yc_i], ybuf.at[yp_i]
    gather_wait(xc, c)

    @pl.when(i >= 2)
    def _():
        scatter_wait(yc, yc_i)

    g = jnp.dot(xc[...].astype(BF16), wg_ref[0, 0], preferred_element_type=F32)
    for r in range(bm):
        pltpu.make_async_copy(u_hbm.at[pl.ds(tokn_ref[0, 0, r], 1), :], xn.at[pl.ds(r, 1), :], gsem.at[n]).start()
    up = jnp.dot(xc[...].astype(BF16), wu_ref[0, 0], preferred_element_type=F32)
    for r in range(bm):
        scatter_row(yp, dstp_ref, r, yp_i)
    a = (g * _sigmoid(g) * up).astype(BF16)
    yc[...] = jnp.dot(a, wd_ref[0, 0], preferred_element_type=F32)

    @pl.when(i == nb - 1)
    def _():
        def body(r, carry):
            scatter_row(yc, dstc_ref, r, yc_i)
            return carry
        lax.fori_loop(0, bm, body, 0, unroll=8)
        scatter_wait(yc, (i + 1) % 3)
        scatter_wait(yp, yp_i)
        scatter_wait(yc, yc_i)
        gather_wait(xn, n)


def _moe(block_e, row_tok, row_dst, u, wg, wu, wd, layer, n_rows):
    n_blocks = block_e.shape[0]
    bm = row_tok.shape[-1]
    wspec = lambda shp: pl.BlockSpec((1, 1) + shp, lambda i, be: (layer, be[i], 0, 0))
    ispec = lambda f: pl.BlockSpec((1, 1, bm), f, memory_space=pltpu.SMEM)
    grid_spec = pltpu.PrefetchScalarGridSpec(
        num_scalar_prefetch=1,
        grid=(n_blocks,),
        in_specs=[ispec(lambda i, be: (i, 0, 0)),
                  ispec(lambda i, be: (jnp.minimum(i + 1, n_blocks - 1), 0, 0)),
                  ispec(lambda i, be: (i, 0, 0)),
                  ispec(lambda i, be: (i + 1, 0, 0)),
                  pl.BlockSpec(memory_space=pl.ANY),
                  wspec((D_MODEL, D_FF)), wspec((D_MODEL, D_FF)), wspec((D_FF, D_MODEL))],
        out_specs=pl.BlockSpec(memory_space=pl.ANY),
        scratch_shapes=[pltpu.VMEM((2, bm, D_MODEL), F32), pltpu.VMEM((3, bm, D_MODEL), F32),
                        pltpu.SemaphoreType.DMA((2,)), pltpu.SemaphoreType.DMA((3,))],
    )
    return pl.pallas_call(
        _moe_kernel,
        grid_spec=grid_spec,
        out_shape=jax.ShapeDtypeStruct((n_rows, D_MODEL), F32),
        compiler_params=_cparams(("arbitrary",)),
        name="moe_experts",
    )(block_e, row_tok, row_tok, row_dst, row_dst, u, wg, wu, wd)


def _combine_kernel(ya_ref, yb_ref, h_ref, route_ref, fg_ref, o_ref, *, final):
    w1 = route_ref[:, 2:3]
    w2 = route_ref[:, 3:4]
    hn = h_ref[...] + w1 * ya_ref[...] + w2 * yb_ref[...]
    if final:
        hn = hn * lax.rsqrt(jnp.mean(hn * hn, axis=-1, keepdims=True) + EPS) * fg_ref[...]
    o_ref[...] = hn


def _combine(y2, h, route, fg, tm, final):
    n = h.shape[0]
    return pl.pallas_call(
        functools.partial(_combine_kernel, final=final),
        grid=(n // tm,),
        in_specs=[pl.BlockSpec((tm, D_MODEL), lambda i: (i, 0)),
                  pl.BlockSpec((tm, D_MODEL), lambda i: (i + n // tm, 0)),
                  pl.BlockSpec((tm, D_MODEL), lambda i: (i, 0)),
                  pl.BlockSpec((tm, LANES), lambda i: (i, 0)),
                  _const_spec(fg.shape)],
        out_specs=pl.BlockSpec((tm, D_MODEL), lambda i: (i, 0)),
        out_shape=jax.ShapeDtypeStruct((n, D_MODEL), F32),
        compiler_params=_cparams(("parallel",)),
        name="moe_combine",
    )(y2, y2, h, route, fg)


def _rmsnorm_kernel(h_ref, g_ref, o_ref):
    x = h_ref[...]
    o_ref[...] = x * lax.rsqrt(jnp.mean(x * x, axis=-1, keepdims=True) + EPS) * g_ref[...]


def _final_norm(h, g, tm):
    n = h.shape[0]
    row = pl.BlockSpec((tm, D_MODEL), lambda i: (i, 0))
    return pl.pallas_call(
        _rmsnorm_kernel, grid=(n // tm,), in_specs=[row, _const_spec(g.shape)], out_specs=row,
        out_shape=jax.ShapeDtypeStruct((n, D_MODEL), F32),
        compiler_params=_cparams(("parallel",)), name="final_norm",
    )(h, g)


def _pad_cols(w, width):
    return jnp.pad(w, ((0, 0), (0, width - w.shape[1])))


def _pad_rows(w, height):
    return jnp.pad(w, ((0, height - w.shape[0]), (0, 0)))


def _inproj_weights(w):
    o = np.cumsum([0, 512, 256, 256, 8, 192, 128, 32, 128, 128, 256, 256, 16, 512])
    seg = lambda i: w[:, o[i]:o[i + 1]]
    gates = seg(3)
    wm = jnp.concatenate([seg(0), seg(1), seg(2), _pad_cols(gates[:, :4], LANES), _pad_cols(gates[:, 4:], LANES)], axis=1)
    wa = jnp.concatenate([_pad_cols(seg(4), 256), seg(5), _pad_cols(seg(6), LANES)], axis=1)
    wg = jnp.concatenate([seg(7), seg(8), seg(9), seg(10), _pad_cols(seg(11), LANES)], axis=1)
    wc = seg(12)
    return [x.astype(BF16) for x in (wm, wa, wg, wc)]


def _mla_weights(w_uq, w_ukv):
    half = MLA_ROPE // 2
    zq = jnp.zeros((MLA_Q_RANK, half), F32)
    qa, qb, ka, kb, wv = [], [], [], [], []
    eye = jnp.eye(LANES, dtype=F32)[:, :MLA_ROPE]
    zk = jnp.zeros((MLA_KV_RANK, MLA_HP), F32)
    for h in range(MLA_HEADS):
        wq = w_uq[:, h * 96:(h + 1) * 96]
        nope, r1, r2 = wq[:, :64], wq[:, 64:64 + half], wq[:, 64 + half:]
        qa.append(_pad_cols(jnp.concatenate([nope, r1, r2], axis=1), MLA_HP))
        qb.append(_pad_cols(jnp.concatenate([jnp.zeros_like(nope), -r2, r1], axis=1), MLA_HP))
        wkv = w_ukv[:, h * 128:(h + 1) * 128]
        kn, vh = wkv[:, :64], wkv[:, 64:]
        e1, e2 = eye[:, :half], eye[:, half:]
        z64 = jnp.zeros((LANES, 64), F32)
        ka.append(jnp.concatenate([_pad_cols(kn, MLA_HP), _pad_cols(jnp.concatenate([z64, e1, e2], axis=1), MLA_HP)], axis=0))
        kb.append(jnp.concatenate([zk, _pad_cols(jnp.concatenate([z64, -e2, e1], axis=1), MLA_HP)], axis=0))
        wv.append(vh.T)
    cat = lambda xs, rows: _pad_rows(jnp.concatenate(xs, axis=1), rows).astype(BF16)
    wvt = _pad_cols(jnp.concatenate(wv, axis=0), 256).astype(BF16)
    return cat(qa, 256), cat(qb, 256), cat(ka, 256), cat(kb, 256), wvt


def _rope_tables(positions):
    half = MLA_ROPE // 2
    inv_freq = 1.0 / (ROPE_BASE ** (jnp.arange(0, MLA_ROPE, 2, dtype=F32) / MLA_ROPE))
    ang = positions.astype(F32)[..., None] * inv_freq
    cos, sin = jnp.cos(ang), jnp.sin(ang)
    shp = cos.shape[:-1]
    rc = jnp.concatenate([jnp.ones(shp + (64,), F32), cos, cos, jnp.zeros(shp + (MLA_HP - 64 - 2 * half,), F32)], axis=-1)
    rs = jnp.concatenate([jnp.zeros(shp + (64,), F32), sin, sin, jnp.zeros(shp + (MLA_HP - 64 - 2 * half,), F32)], axis=-1)
    return rc, rs


def _invert_kernel(dest_ref, lo_ref, hi_ref, out_ref):
    i = pl.program_id(0)
    ch = dest_ref.shape[-1]

    @pl.when(i == 0)
    def _():
        for e in range(lo_ref.shape[0]):
            def init(p, c):
                out_ref[p] = -1
                return c
            lax.fori_loop(lo_ref[e], hi_ref[e], init, 0)

    def body(a, c):
        out_ref[dest_ref[0, 0, a]] = i * ch + a
        return c
    lax.fori_loop(0, ch, body, 0, unroll=8)


def _invert_assignment(dest, pad_lo, pad_hi, rows):
    a = dest.shape[0]
    ch = min(INVERT_CHUNK, a)
    smem = pl.BlockSpec(memory_space=pltpu.SMEM)
    return pl.pallas_call(
        _invert_kernel,
        grid=(a // ch,),
        in_specs=[pl.BlockSpec((1, 1, ch), lambda i: (i, 0, 0), memory_space=pltpu.SMEM), smem, smem],
        out_specs=smem,
        out_shape=jax.ShapeDtypeStruct((rows,), jnp.int32),
        compiler_params=_cparams(("arbitrary",)),
        name="moe_invert",
    )(dest.reshape(a // ch, 1, ch), pad_lo, pad_hi)


def _routing_tables(route, bm):
    n = route.shape[0]
    a = n * TOP_K
    flat_e = route[:, :TOP_K].astype(jnp.int32).reshape(-1)
    onehot = (flat_e[:, None] == jnp.arange(N_EXPERTS, dtype=jnp.int32)[None, :]).astype(jnp.int32)
    csum = jnp.cumsum(onehot, axis=0)
    rank = jnp.sum(onehot * csum, axis=1) - 1
    counts = csum[-1]
    padded = (counts + bm - 1) // bm * bm
    padded_end = jnp.cumsum(padded)
    padded_start = padded_end - padded
    dest = padded_start[flat_e] + rank
    n_blocks = -(-a // bm) + N_EXPERTS
    rows = n_blocks * bm
    pad_lo = jnp.concatenate([padded_start + counts, padded_end[-1:]]).astype(jnp.int32)
    pad_hi = jnp.concatenate([padded_end, jnp.full((1,), rows, jnp.int32)]).astype(jnp.int32)
    row_asg = _invert_assignment(dest, pad_lo, pad_hi, rows)
    row_tok = jnp.maximum(row_asg, 0) // TOP_K
    pad_rank = jnp.cumsum((row_asg < 0).astype(jnp.int32)) - 1
    slot_row = (row_asg % TOP_K) * n + row_asg // TOP_K
    row_dst = jnp.where(row_asg >= 0, slot_row, a + pad_rank)
    first = a + (rows - a) + jnp.arange(bm, dtype=jnp.int32)
    row_dst = jnp.concatenate([first, row_dst]).reshape(n_blocks + 1, 1, bm)
    block_start = jnp.arange(n_blocks, dtype=jnp.int32) * bm
    block_e = jnp.sum((padded_end[None, :] <= block_start[:, None]).astype(jnp.int32), axis=1)
    block_e = jnp.minimum(block_e, N_EXPERTS - 1).astype(jnp.int32)
    return block_e, row_tok.reshape(n_blocks, 1, bm), row_dst, rows + bm


def kernel(x, positions, norm1_g, w_in, mlstm_conv_w, mlstm_conv_b, mlstm_gate_b, mla_q_norm_g, mla_w_uq,
           mla_kv_norm_g, mla_w_ukv, gla_w_alpha, gla_b_alpha, conv_w, conv_b, conv_ln_g, conv_ln_b, fuse_g,
           w_out, norm2_g, ffn_w_gate, ffn_w_up, ffn_w_down, moe_router, moe_w_gate, moe_w_up, moe_w_down,
           final_norm_g):
    bsz, seq, _ = x.shape
    n = bsz * seq
    depth = w_in.shape[0]
    tm = min(512, n)
    tm2 = min(1024, n)
    tt = min(1024, seq)
    tq = min(512, seq)
    nct = seq // CHUNK
    row2 = lambda v: v.reshape(1, -1).astype(F32)

    rc, rs = _rope_tables(positions)
    h = x.reshape(n, D_MODEL)
    pending_moe = None
    for l in range(depth):
        wm, wa, wg, wc = _inproj_weights(w_in[l])
        if pending_moe is None:
            zm, za, zg, zc, zgate = _inproj(h, row2(norm1_g[l]), wm, wa, wg, wc, tm2)
        else:
            zm, za, zg, zc, zgate, h = _inproj(h, row2(norm1_g[l]), wm, wa, wg, wc, tm, moe=pending_moe)
            pending_moe = None
        zm = zm.reshape(bsz, seq, ZM_W)

        gates = jnp.concatenate([zgate[:, 0:MLSTM_HEADS], zgate[:, LANES:LANES + MLSTM_HEADS]], axis=-1)
        grow = gates.reshape(bsz, nct, CHUNK, 2, MLSTM_HEADS).transpose(0, 3, 1, 4, 2).reshape(bsz, 2, nct, GROUP_W)
        gb = mlstm_gate_b[l].astype(F32)
        gbc = _pad_cols(gb.reshape(2, MLSTM_HEADS), LANES)
        gbr = jnp.repeat(gb.reshape(2, MLSTM_HEADS), CHUNK, axis=1)
        ym, yg = _recurrent_mixers(zm, grow, mlstm_conv_w[l], row2(mlstm_conv_b[l]), gbc, gbr,
                                   zg.reshape(bsz, seq, ZG_W), _pad_rows(gla_w_alpha[l], LANES),
                                   row2(gla_b_alpha[l]), tt)

        wqa, wqb, wka, wkb, wv = _mla_weights(mla_w_uq[l], mla_w_ukv[l])
        qh, kh, vh = _mla_prep(za.reshape(bsz, seq, ZA_W), rc, rs, _pad_cols(row2(mla_q_norm_g[l]), 256),
                               row2(mla_kv_norm_g[l]), wqa, wqb, wka, wkb, wv, tt)
        ya = _mla_attn(qh, kh, vh, tq)

        yc = _conformer(zc.reshape(bsz, seq, ZC_W), conv_w[l], row2(conv_b[l]), row2(conv_ln_g[l]),
                        row2(conv_ln_b[l]), tt)

        flat = lambda y: y.reshape(n, GROUP_W)
        j = l // 2
        if l % 2 == 0:
            ffn = (ffn_w_gate[j].astype(BF16), ffn_w_up[j].astype(BF16), ffn_w_down[j].astype(BF16))
            h = _outproj(flat(ym), flat(ya), flat(yg), flat(yc), h, row2(fuse_g[l]), w_out[l].astype(BF16),
                         row2(norm2_g[l]), tm, ffn=ffn)
            if l == depth - 1:
                h = _final_norm(h, row2(final_norm_g), tm2)
        else:
            h, u, route = _outproj(flat(ym), flat(ya), flat(yg), flat(yc), h, row2(fuse_g[l]),
                                   w_out[l].astype(BF16), row2(norm2_g[l]), tm2,
                                   router=_pad_cols(moe_router[j], LANES))
            block_e, row_tok, row_dst, n_rows = _routing_tables(route, MOE_BM)
            y2 = _moe(block_e, row_tok, row_dst, u, moe_w_gate.astype(BF16), moe_w_up.astype(BF16),
                      moe_w_down.astype(BF16), j, n_rows)
            if l == depth - 1:
                h = _combine(y2, h, route, row2(final_norm_g), tm2, True)
            else:
                pending_moe = (y2, route)
    return h.reshape(bsz, seq, D_MODEL)
```

```python
import functools

import jax
import jax.numpy as jnp
import numpy as np
from jax import lax
from jax.experimental import pallas as pl
from jax.experimental.pallas import tpu as pltpu

F32 = jnp.float32
BF16 = jnp.bfloat16

D_MODEL = 1024
GROUP_W = 256
EPS = 1e-6
LOG2E = 1.4426950408889634
LANES = 128
CHUNK = 64
MLSTM_HEADS = 4
MLSTM_DH = 64
MLSTM_CONV = 4
MLA_HEADS = 4
MLA_NOPE = 64
MLA_ROPE = 32
MLA_V = 64
MLA_Q_RANK = 192
MLA_KV_RANK = 128
ROPE_BASE = 10000.0
GLA_HEADS = 4
GLA_DK = 32
GLA_DV = 64
GLA_GATE_RANK = 16
GLA_TAU = 16.0
GLA_SUB = 16
CONV_WIDTH = 31
D_FF = 2816
N_EXPERTS = 8
TOP_K = 2
MOE_BM = 256
INVERT_CHUNK = 8192
VMEM_LIMIT = 56 * 1024 * 1024

ZM_W = 1280
ZA_W = 512
ZG_W = 896
ZC_W = 512


def _cparams(sem):
    return pltpu.CompilerParams(dimension_semantics=sem, vmem_limit_bytes=VMEM_LIMIT)


def _const_spec(shape):
    nd = len(shape)
    return pl.BlockSpec(shape, lambda *_: (0,) * nd, pipeline_mode=pl.Buffered(1))


def _sigmoid(x):
    return 1.0 / (1.0 + jnp.exp(-x))


def _log_sigmoid(x):
    return jnp.minimum(x, 0.0) - jnp.log(1.0 + jnp.exp(-jnp.abs(x)))


def _iota(shape, dim):
    return lax.broadcasted_iota(jnp.int32, shape, dim)


def _tri(n):
    return (_iota((n, n), 0) >= _iota((n, n), 1)).astype(BF16)


def _split3(x):
    hi = x.astype(BF16)
    r1 = x - hi.astype(F32)
    mid = r1.astype(BF16)
    lo = (r1 - mid.astype(F32)).astype(BF16)
    return hi, mid, lo


def _sel_dot(sel, x):
    hi, mid, lo = _split3(x)
    d = lambda p: jnp.dot(sel, p, preferred_element_type=F32)
    return d(hi) + d(mid) + d(lo)


def _dot_sel(x, sel):
    hi, mid, lo = _split3(x)
    d = lambda p: jnp.dot(p, sel, preferred_element_type=F32)
    return d(hi) + d(mid) + d(lo)


def _inproj_kernel(h_ref, g_ref, wm_ref, wa_ref, wg_ref, wc_ref, zm_ref, za_ref, zg_ref, zc_ref, gate_ref):
    x = h_ref[...]
    ms = jnp.mean(x * x, axis=-1, keepdims=True)
    xn = (x * lax.rsqrt(ms + EPS) * g_ref[...]).astype(BF16)
    zm = jnp.dot(xn, wm_ref[...], preferred_element_type=F32)
    zm_ref[...] = zm
    gate_ref[...] = zm[:, ZM_W - 2 * LANES:]
    za_ref[...] = jnp.dot(xn, wa_ref[...], preferred_element_type=F32)
    zg_ref[...] = jnp.dot(xn, wg_ref[...], preferred_element_type=F32)
    zc_ref[...] = jnp.dot(xn, wc_ref[...], preferred_element_type=F32)


def _inproj(h, g, wm, wa, wg, wc, tm):
    n = h.shape[0]
    row = lambda w: pl.BlockSpec((tm, w), lambda i: (i, 0))
    widths = (ZM_W, ZA_W, ZG_W, ZC_W, 2 * LANES)
    return pl.pallas_call(
        _inproj_kernel,
        grid=(n // tm,),
        in_specs=[row(D_MODEL), _const_spec((1, D_MODEL)), _const_spec(wm.shape), _const_spec(wa.shape),
                  _const_spec(wg.shape), _const_spec(wc.shape)],
        out_specs=[row(w) for w in widths],
        out_shape=[jax.ShapeDtypeStruct((n, w), F32) for w in widths],
        compiler_params=_cparams(("parallel",)),
        name="inproj",
    )(h, g, wm, wa, wg, wc)


def _expand_heads(x, width):
    r = x.shape[0]
    lane = _iota((r, 4 * width), 1)
    out = jnp.broadcast_to(x[:, 3:4], (r, 4 * width))
    for h in (2, 1, 0):
        out = jnp.where(lane < (h + 1) * width, jnp.broadcast_to(x[:, h:h + 1], (r, 4 * width)), out)
    return out


def _mlstm_setup(zm_ref, grow_ref, cw_ref, cb_ref, gbc_ref, gbr_ref, y_ref,
                 xpad_ref, q_ref, k_ref, grs_ref, cn_ref, m_ref, cnm_ref, trit_ref, b_ref, gc_ref, cm_ref, *, tt):
    L = CHUNK
    W = MLSTM_HEADS * MLSTM_DH

    @pl.when(pl.program_id(1) == 0)
    def _():
        xpad_ref[0:8, :] = jnp.zeros((8, 2 * W), F32)
        cn_ref[...] = jnp.zeros_like(cn_ref)
        m_ref[...] = jnp.zeros_like(m_ref)
        hsame = (_iota((W, W), 0) // L) == (_iota((W, W), 1) // L)
        ncol = (_iota((W, LANES), 0) // L) == _iota((W, LANES), 1)
        cnm_ref[...] = jnp.concatenate([hsame, ncol], axis=1).astype(F32)
        trit_ref[...] = (hsame & ((_iota((W, W), 0) % L) <= (_iota((W, W), 1) % L))).astype(BF16)

    xpad_ref[8:8 + tt, :] = zm_ref[0, :, 0:2 * W]
    for r in range(0, tt, L):
        acc = jnp.broadcast_to(cb_ref[...], (L, 2 * W))
        for j in range(MLSTM_CONV):
            acc = acc + cw_ref[j:j + 1, :] * xpad_ref[pl.ds(r + 8 - (MLSTM_CONV - 1) + j, L), :]
        qk = acc * _sigmoid(acc)
        q_ref[r:r + L, :] = (qk[:, :W] * (MLSTM_DH ** -0.5)).astype(BF16)
        k_ref[r:r + L, :] = qk[:, W:]
    xpad_ref[0:8, :] = xpad_ref[tt:tt + 8, :]

    i_row = grow_ref[0, 0] + gbr_ref[0:1, :]
    lf_row = _log_sigmoid(grow_ref[0, 1] + gbr_ref[1:2, :])
    grs_ref[...] = i_row - _dot_sel(lf_row, trit_ref[...])

    tri = _tri(L)
    lf_all = _log_sigmoid(zm_ref[0, :, 4 * W + LANES:4 * W + 2 * LANES] + gbc_ref[1:2, :])
    for r in range(0, tt, L):
        b_ref[r:r + L, :] = _sel_dot(tri, lf_all[r:r + L, :])
    g_all = zm_ref[0, :, 4 * W:4 * W + LANES] + gbc_ref[0:1, :] - b_ref[...]
    gc_ref[...] = g_all
    pos = _iota((tt, LANES), 0) % L
    cm_all = g_all
    s = 1
    while s < L:
        cm_all = jnp.maximum(cm_all, jnp.where(pos >= s, pltpu.roll(cm_all, s, 0), -jnp.inf))
        s *= 2
    cm_ref[...] = cm_all

    row_t = _iota((L, W), 0)
    lane_j = _iota((L, W), 1) % L
    causal = lane_j <= row_t
    ones_col = (_iota((L, LANES), 1) < MLSTM_HEADS).astype(F32)

    def chunk(c):
        r0 = pl.multiple_of(c * L, L)
        b = b_ref[pl.ds(r0, L), :]
        g = gc_ref[pl.ds(r0, L), :]
        cm = cm_ref[pl.ds(r0, L), :]
        m_prev = m_ref[...]
        mx = jnp.maximum(m_prev, cm)
        w_inter = jnp.exp(m_prev - mx)
        b_end = b[L - 1:L, :]
        mx_end = mx[L - 1:L, :]
        m_new = b_end + mx_end
        decay = jnp.exp(m_prev - mx_end)
        kw_col = jnp.exp(g - mx_end)

        qc = q_ref[pl.ds(r0, L), :]
        kc = k_ref[pl.ds(r0, L), :]
        vc = zm_ref[0, pl.ds(r0, L), 2 * W:3 * W]
        oc = zm_ref[0, pl.ds(r0, L), 3 * W:4 * W]

        kbd = (jnp.concatenate([kc] * 4, axis=0) * cnm_ref[:, 0:W]).astype(BF16)
        s_qk = lax.dot_general(qc, kbd, (((1,), (1,)), ((), ())), preferred_element_type=F32)
        inter = jnp.dot(qc, cn_ref[...].astype(BF16), preferred_element_type=F32)
        vaug = jnp.concatenate([vc, ones_col], axis=1)
        kw = (_expand_heads(kw_col, L) * kc).astype(BF16)
        upd = lax.dot_general(kw, vaug.astype(BF16), (((0,), (0,)), ((), ())), preferred_element_type=F32)
        decay_x = jnp.concatenate([_expand_heads(decay, L), decay], axis=1)
        cn_ref[...] = decay_x * cn_ref[...] + upd * cnm_ref[...]
        m_ref[...] = m_new

        def output():
            g_row = grs_ref[pl.ds(c, 1), :]
            dmat = jnp.where(causal, jnp.exp(jnp.where(causal, g_row - _expand_heads(mx, L), 0.0)), 0.0)
            s_w = (s_qk * dmat).astype(BF16)
            vbd = (jnp.concatenate([vaug] * 4, axis=0) * cnm_ref[...]).astype(BF16)
            intra = jnp.dot(s_w, vbd, preferred_element_type=F32)
            num = _expand_heads(w_inter, L) * inter[:, :W] + intra[:, :W]
            den = w_inter * inter[:, W:] + intra[:, W:]
            hden = jnp.maximum(jnp.abs(den), jnp.exp(-(b + mx)))
            hout = num * _expand_heads(1.0 / hden, L)
            y_ref[0, pl.ds(r0, L), :] = _sigmoid(oc) * hout

        return output

    return chunk


def _mlstm_scratch(tt):
    nct = tt // CHUNK
    return [pltpu.VMEM((tt + 8, 2 * GROUP_W), F32), pltpu.VMEM((tt, GROUP_W), BF16),
            pltpu.VMEM((tt, GROUP_W), F32), pltpu.VMEM((nct, GROUP_W), F32),
            pltpu.VMEM((GROUP_W, GROUP_W + LANES), F32), pltpu.VMEM((1, LANES), F32),
            pltpu.VMEM((GROUP_W, GROUP_W + LANES), F32), pltpu.VMEM((GROUP_W, GROUP_W), BF16),
            pltpu.VMEM((tt, LANES), F32), pltpu.VMEM((tt, LANES), F32), pltpu.VMEM((tt, LANES), F32)]


def _gla_setup(zg_ref, wa_ref, ba_ref, y_ref, q_ref, k_ref, g_ref, o_ref, st_ref, kmask_ref, vmask_ref, *, tt):
    L = CHUNK
    WK = GLA_HEADS * GLA_DK
    WV = GLA_HEADS * GLA_DV

    @pl.when(pl.program_id(1) == 0)
    def _():
        st_ref[...] = jnp.zeros_like(st_ref)
        kmask_ref[...] = ((_iota((WV, WK), 0) // L) == (_iota((WV, WK), 1) // GLA_DK)).astype(F32)
        vmask_ref[...] = ((_iota((WV, WV), 0) // L) == (_iota((WV, WV), 1) // GLA_DV)).astype(F32)

    q_ref[...] = zg_ref[0, :, 0:WK] * (GLA_DK ** -0.5)
    k_ref[...] = zg_ref[0, :, WK:2 * WK]
    a = zg_ref[0, :, 2 * WK + 2 * WV:2 * WK + 2 * WV + LANES]
    a_hi = a.astype(BF16)
    a_lo = (a - a_hi.astype(F32)).astype(BF16)
    wa = wa_ref[...]
    w_hi = wa.astype(BF16)
    w_lo = (wa - w_hi.astype(F32)).astype(BF16)
    t = jnp.dot(a_hi, jnp.concatenate([w_hi, w_lo], axis=1), preferred_element_type=F32)
    pre = t[:, :WK] + (jnp.dot(a_lo, w_hi, preferred_element_type=F32) + t[:, WK:]) + ba_ref[...]
    g_all = _log_sigmoid(pre) * (1.0 / GLA_TAU)
    tri = _tri(L)
    for r in range(0, tt, L):
        g_ref[r:r + L, :] = _sel_dot(tri, g_all[r:r + L, :])

    hs2 = ((_iota((WK, WV), 0) // GLA_DK) == (_iota((WK, WV), 1) // GLA_DV)).astype(BF16)

    def chunk(c):
        r0 = pl.multiple_of(c * L, L)
        qc = q_ref[pl.ds(r0, L), :]
        kc = k_ref[pl.ds(r0, L), :]
        vc = zg_ref[0, pl.ds(r0, L), 2 * WK:2 * WK + WV]
        b = g_ref[pl.ds(r0, L), :]
        st = st_ref[...]
        qd = (qc * jnp.exp(b)).astype(BF16)
        o = lax.dot_general(qd, st.astype(BF16), (((1,), (1,)), ((), ())), preferred_element_type=F32)
        b_end = b[L - 1:L, :]
        kd = (kc * jnp.exp(b_end - b)).astype(BF16)
        upd = lax.dot_general(vc.astype(BF16), kd, (((0,), (0,)), ((), ())), preferred_element_type=F32)
        st_ref[...] = st * jnp.exp(b_end) + upd * kmask_ref[...]

        def output(o=o):
            S = GLA_SUB
            key_pos = _iota((L, WK), 0)
            vbd = (jnp.concatenate([vc] * GLA_HEADS, axis=0) * vmask_ref[...]).astype(BF16)
            a_rows = []
            for lo in range(S, L, S):
                r = b[lo - 1:lo, :]
                qf = (qc[lo:lo + S, :] * jnp.exp(b[lo:lo + S, :] - r)).astype(BF16)
                kf = kc * jnp.exp(jnp.where(key_pos < lo, r - b, -jnp.inf))
                kbd = (jnp.concatenate([kf] * GLA_HEADS, axis=0) * kmask_ref[...]).astype(BF16)
                a_rows.append(lax.dot_general(qf, kbd, (((1,), (1,)), ((), ())), preferred_element_type=F32))
            o_off = jnp.dot(jnp.concatenate(a_rows, axis=0).astype(BF16), vbd, preferred_element_type=F32)

            tl = _iota((8, WK), 0)
            ps = []
            for lo in range(0, L, S):
                for jj in range(8):
                    j = lo + jj
                    d = b[lo:lo + S, :] - b[j:j + 1, :]
                    e = jnp.exp(jnp.concatenate([jnp.where(tl >= jj, d[0:8, :], -jnp.inf), d[8:, :]], axis=0))
                    ps.append((qc[lo:lo + S, :] * kc[j:j + 1, :] * e).astype(BF16))
                for jj in range(8):
                    j = lo + 8 + jj
                    d = b[lo + 8:lo + S, :] - b[j:j + 1, :]
                    e = jnp.exp(jnp.where(tl >= jj, d, -jnp.inf))
                    ps.append((qc[lo + 8:lo + S, :] * kc[j:j + 1, :] * e).astype(BF16))
            rexp = jnp.dot(jnp.concatenate(ps, axis=0), hs2, preferred_element_type=F32)
            bands = []
            row = 0
            for lo in range(0, L, S):
                acc_a = None
                for jj in range(8):
                    term = rexp[row:row + S, :] * vc[lo + jj:lo + jj + 1, :]
                    acc_a = term if acc_a is None else acc_a + term
                    row += S
                acc_b = None
                for jj in range(8):
                    term = rexp[row:row + 8, :] * vc[lo + 8 + jj:lo + 8 + jj + 1, :]
                    acc_b = term if acc_b is None else acc_b + term
                    row += 8
                band = jnp.concatenate([acc_a[0:8, :], acc_a[8:, :] + acc_b], axis=0)
                bands.append(band if lo == 0 else band + o_off[lo - S:lo, :])
            o_ref[pl.ds(r0, L), :] = o + jnp.concatenate(bands, axis=0)

        return output

    def finish():
        o = o_ref[...]
        hsame = ((_iota((WV, WV), 0) // GLA_DV) == (_iota((WV, WV), 1) // GLA_DV)).astype(BF16)
        ms = _dot_sel(o * o, hsame) * (1.0 / GLA_DV)
        r = zg_ref[0, :, 2 * WK + WV:2 * WK + 2 * WV]
        y_ref[0] = o * lax.rsqrt(ms + EPS) * (r * _sigmoid(r))

    return chunk, finish


def _gla_scratch(tt):
    return [pltpu.VMEM((tt, LANES), F32), pltpu.VMEM((tt, LANES), F32), pltpu.VMEM((tt, LANES), F32),
            pltpu.VMEM((tt, GROUP_W), F32), pltpu.VMEM((GROUP_W, LANES), F32),
            pltpu.VMEM((GROUP_W, LANES), F32), pltpu.VMEM((GROUP_W, GROUP_W), F32)]


def _recurrent_kernel(zm_ref, grow_ref, cw_ref, cb_ref, gbc_ref, gbr_ref, zg_ref, wa_ref, ba_ref,
                      ym_ref, yg_ref, *scratch, tt):
    n_m = len(_mlstm_scratch(tt))
    m_chunk = _mlstm_setup(zm_ref, grow_ref, cw_ref, cb_ref, gbc_ref, gbr_ref, ym_ref, *scratch[:n_m], tt=tt)
    g_chunk, g_finish = _gla_setup(zg_ref, wa_ref, ba_ref, yg_ref, *scratch[n_m:], tt=tt)

    def both(c, carry):
        m_output = m_chunk(c)
        g_output = g_chunk(c)
        m_output()
        g_output()
        return carry

    lax.fori_loop(0, tt // CHUNK, both, 0, unroll=2)
    g_finish()


def _recurrent_mixers(zm, grow, cw, cb, gbc, gbr, zg, wa, ba, tt):
    b, s, _ = zm.shape
    nct = tt // CHUNK
    tile = lambda w: pl.BlockSpec((1, tt, w), lambda i, t: (i, t, 0))
    yshape = jax.ShapeDtypeStruct((b, s, GROUP_W), F32)
    return pl.pallas_call(
        functools.partial(_recurrent_kernel, tt=tt),
        grid=(b, s // tt),
        in_specs=[tile(ZM_W), pl.BlockSpec((1, 2, nct, GROUP_W), lambda i, t: (i, 0, t, 0)),
                  _const_spec(cw.shape), _const_spec(cb.shape), _const_spec(gbc.shape), _const_spec(gbr.shape),
                  tile(ZG_W), _const_spec(wa.shape), _const_spec(ba.shape)],
        out_specs=[tile(GROUP_W), tile(GROUP_W)],
        out_shape=[yshape, yshape],
        scratch_shapes=_mlstm_scratch(tt) + _gla_scratch(tt),
        compiler_params=_cparams(("parallel", "arbitrary")),
        name="mlstm_gla",
    )(zm, grow, cw, cb, gbc, gbr, zg, wa, ba)


CONV_HIST = 32
CONV_SUB = 64


def _conv_kernel(zc_ref, cw_ref, cb_ref, lg_ref, lb_ref, y_ref, xpad_ref, xs_ref, *, tt):
    C = GROUP_W
    span = tt + CONV_HIST - 8

    @pl.when(pl.program_id(1) == 0)
    def _():
        xpad_ref[0:CONV_HIST, :] = jnp.zeros((CONV_HIST, C), F32)

    val = zc_ref[0, :, 0:C]
    gate = zc_ref[0, :, C:2 * C]
    xpad_ref[CONV_HIST:CONV_HIST + tt, :] = val * _sigmoid(gate)
    for sh in range(1, 8):
        xs_ref[sh - 1, 0:span, :] = xpad_ref[pl.ds(sh, span), :]
    sub = min(CONV_SUB, tt)
    for r in range(0, tt, sub):
        acc = jnp.broadcast_to(cb_ref[...], (sub, C))
        for j in range(CONV_WIDTH):
            off = CONV_HIST - (CONV_WIDTH - 1) + j
            sh, base = off % 8, r + off - off % 8
            src = xpad_ref[base:base + sub, :] if sh == 0 else xs_ref[sh - 1, base:base + sub, :]
            acc = acc + cw_ref[j:j + 1, :] * src
        mu = jnp.mean(acc, axis=-1, keepdims=True)
        d = acc - mu
        var = jnp.mean(d * d, axis=-1, keepdims=True)
        yn = d * lax.rsqrt(var + EPS) * lg_ref[...] + lb_ref[...]
        y_ref[0, r:r + sub, :] = yn * _sigmoid(yn)
    xpad_ref[0:CONV_HIST, :] = xpad_ref[tt:tt + CONV_HIST, :]


def _conformer(zc, cw, cb, lg, lb, tt):
    b, s, _ = zc.shape
    return pl.pallas_call(
        functools.partial(_conv_kernel, tt=tt),
        grid=(b, s // tt),
        in_specs=[pl.BlockSpec((1, tt, ZC_W), lambda i, t: (i, t, 0)), _const_spec(cw.shape), _const_spec(cb.shape),
                  _const_spec(lg.shape), _const_spec(lb.shape)],
        out_specs=pl.BlockSpec((1, tt, GROUP_W), lambda i, t: (i, t, 0)),
        out_shape=jax.ShapeDtypeStruct((b, s, GROUP_W), F32),
        scratch_shapes=[pltpu.VMEM((tt + CONV_HIST, GROUP_W), F32), pltpu.VMEM((7, tt + CONV_HIST, GROUP_W), F32)],
        compiler_params=_cparams(("parallel", "arbitrary")),
        name="conformer",
    )(zc, cw, cb, lg, lb)


MLA_HP = 128
MLA_VP = 80


def _mla_prep_kernel(za_ref, rc_ref, rs_ref, qg_ref, kg_ref, wqa_ref, wqb_ref, wka_ref, wkb_ref, wv_ref,
                     q_ref, k_ref, v_ref):
    cq = za_ref[0, :, 0:256]
    ckv = za_ref[0, :, 256:384]
    kr = za_ref[0, :, 384:512]
    qn = (cq * lax.rsqrt(jnp.sum(cq * cq, axis=-1, keepdims=True) * (1.0 / MLA_Q_RANK) + EPS) * qg_ref[...]).astype(BF16)
    kvn = ckv * lax.rsqrt(jnp.mean(ckv * ckv, axis=-1, keepdims=True) + EPS) * kg_ref[...]
    kin = jnp.concatenate([kvn, kr], axis=1).astype(BF16)
    cos = rc_ref[0]
    sin = rs_ref[0]
    qa = jnp.dot(qn, wqa_ref[...], preferred_element_type=F32)
    qb = jnp.dot(qn, wqb_ref[...], preferred_element_type=F32)
    ka = jnp.dot(kin, wka_ref[...], preferred_element_type=F32)
    kb = jnp.dot(kin, wkb_ref[...], preferred_element_type=F32)
    vt = lax.dot_general(wv_ref[...], kin, (((1,), (1,)), ((), ())), preferred_element_type=F32)
    scale = (MLA_NOPE + MLA_ROPE) ** -0.5 * LOG2E
    for h in range(MLA_HEADS):
        sl = slice(h * MLA_HP, (h + 1) * MLA_HP)
        q_ref[0, h] = ((qa[:, sl] * cos + qb[:, sl] * sin) * scale).astype(BF16)
        k_ref[0, h] = (ka[:, sl] * cos + kb[:, sl] * sin).astype(BF16)
        v_ref[0, h, 0:MLA_V, :] = vt[h * MLA_V:(h + 1) * MLA_V, :].astype(BF16)
        v_ref[0, h, MLA_V:MLA_VP, :] = jnp.ones((MLA_VP - MLA_V, vt.shape[1]), BF16)


def _mla_prep(za, rc, rs, qg, kg, wqa, wqb, wka, wkb, wv, tt):
    b, s, _ = za.shape
    hspec = pl.BlockSpec((1, MLA_HEADS, tt, MLA_HP), lambda i, t: (i, 0, t, 0))
    tspec = pl.BlockSpec((1, tt, MLA_HP), lambda i, t: (i, t, 0))
    hshape = jax.ShapeDtypeStruct((b, MLA_HEADS, s, MLA_HP), BF16)
    vspec = pl.BlockSpec((1, MLA_HEADS, MLA_VP, tt), lambda i, t: (i, 0, 0, t))
    vshape = jax.ShapeDtypeStruct((b, MLA_HEADS, MLA_VP, s), BF16)
    return pl.pallas_call(
        _mla_prep_kernel,
        grid=(b, s // tt),
        in_specs=[pl.BlockSpec((1, tt, ZA_W), lambda i, t: (i, t, 0)), tspec, tspec,
                  _const_spec(qg.shape), _const_spec(kg.shape), _const_spec(wqa.shape), _const_spec(wqb.shape),
                  _const_spec(wka.shape), _const_spec(wkb.shape), _const_spec(wv.shape)],
        out_specs=[hspec, hspec, vspec],
        out_shape=[hshape, hshape, vshape],
        compiler_params=_cparams(("parallel", "parallel")),
        name="mla_prep",
    )(za, rc, rs, qg, kg, wqa, wqb, wka, wkb, wv)


def _mla_attn_kernel(qi_ref, ki_ref, q_ref, k_ref, v_ref, y_ref, m_ref, acc_ref, *, tq):
    qi = qi_ref[pl.program_id(1)]
    ki = ki_ref[pl.program_id(1)]

    @pl.when(ki == 0)
    def _():
        m_ref[...] = jnp.full_like(m_ref, -jnp.inf)
        acc_ref[...] = jnp.zeros_like(acc_ref)

    def step(masked):
        def scores(h):
            s = lax.dot_general(k_ref[0, h], q_ref[0, h], (((1,), (1,)), ((), ())), preferred_element_type=F32)
            if masked:
                s = jnp.where(_iota((tq, tq), 0) <= _iota((tq, tq), 1), s, -jnp.inf)
            return s

        def softmax(h, s):
            m_prev = m_ref[h]
            m_new = jnp.maximum(m_prev, jnp.max(s, axis=0, keepdims=True))
            alpha = jnp.exp2(m_prev - m_new)
            p = jnp.exp2(s - m_new)
            m_ref[h] = m_new
            return p.astype(BF16), alpha

        def values(h, p, alpha):
            acc_ref[h] = alpha * acc_ref[h] + jnp.dot(v_ref[0, h], p, preferred_element_type=F32)

        s_next = scores(0)
        for h in range(MLA_HEADS):
            s_cur = s_next
            if h + 1 < MLA_HEADS:
                s_next = scores(h + 1)
            values(h, *softmax(h, s_cur))

    @pl.when(ki < qi)
    def _():
        step(False)

    @pl.when(ki == qi)
    def _():
        step(True)
        ot = jnp.concatenate([acc_ref[h, 0:MLA_V, :] * (1.0 / acc_ref[h, MLA_V:MLA_V + 1, :])
                              for h in range(MLA_HEADS)], axis=0)
        y_ref[0] = ot.T


def _mla_attn(q, k, v, tq):
    b, _, s, _ = q.shape
    nq = s // tq
    pairs = [(qi, ki) for qi in range(nq) for ki in range(qi + 1)]
    qi_tab = jnp.asarray([p[0] for p in pairs], jnp.int32)
    ki_tab = jnp.asarray([p[1] for p in pairs], jnp.int32)
    grid_spec = pltpu.PrefetchScalarGridSpec(
        num_scalar_prefetch=2,
        grid=(b, len(pairs)),
        in_specs=[pl.BlockSpec((1, MLA_HEADS, tq, MLA_HP), lambda i, p, qt, kt: (i, 0, qt[p], 0)),
                  pl.BlockSpec((1, MLA_HEADS, tq, MLA_HP), lambda i, p, qt, kt: (i, 0, kt[p], 0)),
                  pl.BlockSpec((1, MLA_HEADS, MLA_VP, tq), lambda i, p, qt, kt: (i, 0, 0, kt[p]))],
        out_specs=pl.BlockSpec((1, tq, GROUP_W), lambda i, p, qt, kt: (i, qt[p], 0)),
        scratch_shapes=[pltpu.VMEM((MLA_HEADS, 1, tq), F32), pltpu.VMEM((MLA_HEADS, MLA_VP, tq), F32)],
    )
    return pl.pallas_call(
        functools.partial(_mla_attn_kernel, tq=tq),
        grid_spec=grid_spec,
        out_shape=jax.ShapeDtypeStruct((b, s, GROUP_W), F32),
        compiler_params=_cparams(("parallel", "arbitrary")),
        name="mla_attn",
    )(qi_tab, ki_tab, q, k, v)


def _outproj_kernel(*refs, routed):
    if routed:
        ym, ya, yg, yc, h_ref, fg_ref, wo_ref, n2_ref, rt_ref, hn_ref, u_ref, route_ref = refs
    else:
        ym, ya, yg, yc, h_ref, fg_ref, wo_ref, n2_ref, wg_ref, wu_ref, wd_ref, hn_ref = refs
    parts = []
    for gi, y_ref in enumerate((ym, ya, yg, yc)):
        y = y_ref[...]
        yn = y * lax.rsqrt(jnp.mean(y * y, axis=-1, keepdims=True) + EPS)
        parts.append((yn * fg_ref[:, gi * GROUP_W:(gi + 1) * GROUP_W]).astype(BF16))
    ycat = jnp.concatenate(parts, axis=1)
    hn = h_ref[...] + jnp.dot(ycat, wo_ref[...], preferred_element_type=F32)
    u = hn * lax.rsqrt(jnp.mean(hn * hn, axis=-1, keepdims=True) + EPS) * n2_ref[...]
    if not routed:
        hn_ref[...] = hn + _swiglu(u.astype(BF16), wg_ref[...], wu_ref[...], wd_ref[...])
        return
    hn_ref[...] = hn
    u_ref[...] = u
    if routed:
        tm = u.shape[0]
        lane = _iota((tm, LANES), 1)
        u_hi = u.astype(BF16)
        u_lo = (u - u_hi.astype(F32)).astype(BF16)
        rt = rt_ref[...]
        r_hi = rt.astype(BF16)
        r_lo = (rt - r_hi.astype(F32)).astype(BF16)
        d = lambda a, b: jnp.dot(a, b, preferred_element_type=F32)
        t = d(u_hi, jnp.concatenate([r_hi, r_lo], axis=1))
        logits = t[:, :LANES] + (d(u_lo, r_hi) + t[:, LANES:])
        logits = jnp.where(lane < N_EXPERTS, logits, -jnp.inf)
        m1 = jnp.max(logits, axis=-1, keepdims=True)
        i1 = jnp.min(jnp.where(logits == m1, lane, LANES), axis=-1, keepdims=True)
        rest = jnp.where(lane == i1, -jnp.inf, logits)
        m2 = jnp.max(rest, axis=-1, keepdims=True)
        i2 = jnp.min(jnp.where(rest == m2, lane, LANES), axis=-1, keepdims=True)
        e2 = jnp.exp(m2 - m1)
        w1 = 1.0 / (1.0 + e2)
        w2 = e2 / (1.0 + e2)
        route_ref[...] = jnp.where(lane == 0, i1.astype(F32),
                                   jnp.where(lane == 1, i2.astype(F32),
                                             jnp.where(lane == 2, w1, jnp.where(lane == 3, w2, 0.0))))


def _outproj(ym, ya, yg, yc, h, fg, wo, n2, tm, router=None, ffn=None):
    n = h.shape[0]
    routed = router is not None
    row = lambda w: pl.BlockSpec((tm, w), lambda i: (i, 0))
    in_specs = [row(GROUP_W)] * 4 + [row(D_MODEL), _const_spec(fg.shape), _const_spec(wo.shape), _const_spec(n2.shape)]
    args = [ym, ya, yg, yc, h, fg, wo, n2]
    hshape = jax.ShapeDtypeStruct((n, D_MODEL), F32)
    if routed:
        in_specs.append(_const_spec(router.shape))
        args.append(router)
        out_specs = [row(D_MODEL), row(D_MODEL), row(LANES)]
        out_shape = [hshape, hshape, jax.ShapeDtypeStruct((n, LANES), F32)]
    else:
        in_specs += [_const_spec(w.shape) for w in ffn]
        args += list(ffn)
        out_specs, out_shape = row(D_MODEL), hshape
    return pl.pallas_call(
        functools.partial(_outproj_kernel, routed=routed),
        grid=(n // tm,),
        in_specs=in_specs, out_specs=out_specs, out_shape=out_shape,
        compiler_params=_cparams(("parallel",)),
        name="outproj_routed" if routed else "outproj_ffn",
    )(*args)


def _swiglu(x, wg, wu, wd):
    g = jnp.dot(x, wg, preferred_element_type=F32)
    up = jnp.dot(x, wu, preferred_element_type=F32)
    a = (g * _sigmoid(g) * up).astype(BF16)
    return jnp.dot(a, wd, preferred_element_type=F32)


def _moe_kernel(be_ref, tokc_ref, tokn_ref, dstp_ref, dstc_ref, u_hbm, wg_ref, wu_ref, wd_ref, y_hbm,
                xbuf, ybuf, gsem, ssem):
    i = pl.program_id(0)
    nb = pl.num_programs(0)
    bm = xbuf.shape[1]

    def gather_wait(x, s):
        pltpu.make_async_copy(u_hbm.at[pl.ds(0, bm), :], x, gsem.at[s]).wait()

    def scatter_wait(y, s):
        pltpu.make_async_copy(y, y_hbm.at[pl.ds(0, bm), :], ssem.at[s]).wait()

    def scatter_row(y, dst_ref, r, s):
        pltpu.make_async_copy(y.at[pl.ds(r, 1), :], y_hbm.at[pl.ds(dst_ref[0, 0, r], 1), :], ssem.at[s]).start()

    @pl.when(i == 0)
    def _():
        def body(r, c):
            pltpu.make_async_copy(u_hbm.at[pl.ds(tokc_ref[0, 0, r], 1), :], xbuf.at[0, pl.ds(r, 1), :],
                                  gsem.at[0]).start()
            return c
        lax.fori_loop(0, bm, body, 0, unroll=8)
        ybuf[2] = jnp.zeros((bm, D_MODEL), F32)

    c = i % 2
    n = 1 - c
    yc_i = i % 3
    yp_i = (i + 2) % 3
    xc, xn, yc, yp = xbuf.at[c], xbuf.at[n], ybuf.at[yc_i], ybuf.at[yp_i]
    gather_wait(xc, c)

    @pl.when(i >= 2)
    def _():
        scatter_wait(yc, yc_i)

    g = jnp.dot(xc[...].astype(BF16), wg_ref[0, 0], preferred_element_type=F32)
    for r in range(bm):
        pltpu.make_async_copy(u_hbm.at[pl.ds(tokn_ref[0, 0, r], 1), :], xn.at[pl.ds(r, 1), :], gsem.at[n]).start()
    up = jnp.dot(xc[...].astype(BF16), wu_ref[0, 0], preferred_element_type=F32)
    for r in range(bm):
        scatter_row(yp, dstp_ref, r, yp_i)
    a = (g * _sigmoid(g) * up).astype(BF16)
    yc[...] = jnp.dot(a, wd_ref[0, 0], preferred_element_type=F32)

    @pl.when(i == nb - 1)
    def _():
        def body(r, carry):
            scatter_row(yc, dstc_ref, r, yc_i)
            return carry
        lax.fori_loop(0, bm, body, 0, unroll=8)
        scatter_wait(yc, (i + 1) % 3)
        scatter_wait(yp, yp_i)
        scatter_wait(yc, yc_i)
        gather_wait(xn, n)


def _moe(block_e, row_tok, row_dst, u, wg, wu, wd, layer, n_rows):
    n_blocks = block_e.shape[0]
    bm = row_tok.shape[-1]
    wspec = lambda shp: pl.BlockSpec((1, 1) + shp, lambda i, be: (layer, be[i], 0, 0))
    ispec = lambda f: pl.BlockSpec((1, 1, bm), f, memory_space=pltpu.SMEM)
    grid_spec = pltpu.PrefetchScalarGridSpec(
        num_scalar_prefetch=1,
        grid=(n_blocks,),
        in_specs=[ispec(lambda i, be: (i, 0, 0)),
                  ispec(lambda i, be: (jnp.minimum(i + 1, n_blocks - 1), 0, 0)),
                  ispec(lambda i, be: (i, 0, 0)),
                  ispec(lambda i, be: (i + 1, 0, 0)),
                  pl.BlockSpec(memory_space=pl.ANY),
                  wspec((D_MODEL, D_FF)), wspec((D_MODEL, D_FF)), wspec((D_FF, D_MODEL))],
        out_specs=pl.BlockSpec(memory_space=pl.ANY),
        scratch_shapes=[pltpu.VMEM((2, bm, D_MODEL), F32), pltpu.VMEM((3, bm, D_MODEL), F32),
                        pltpu.SemaphoreType.DMA((2,)), pltpu.SemaphoreType.DMA((3,))],
    )
    return pl.pallas_call(
        _moe_kernel,
        grid_spec=grid_spec,
        out_shape=jax.ShapeDtypeStruct((n_rows, D_MODEL), F32),
        compiler_params=_cparams(("arbitrary",)),
        name="moe_experts",
    )(block_e, row_tok, row_tok, row_dst, row_dst, u, wg, wu, wd)


def _combine_kernel(ya_ref, yb_ref, h_ref, route_ref, fg_ref, o_ref, *, final):
    w1 = route_ref[:, 2:3]
    w2 = route_ref[:, 3:4]
    hn = h_ref[...] + w1 * ya_ref[...] + w2 * yb_ref[...]
    if final:
        hn = hn * lax.rsqrt(jnp.mean(hn * hn, axis=-1, keepdims=True) + EPS) * fg_ref[...]
    o_ref[...] = hn


def _combine(y2, h, route, fg, tm, final):
    n = h.shape[0]
    return pl.pallas_call(
        functools.partial(_combine_kernel, final=final),
        grid=(n // tm,),
        in_specs=[pl.BlockSpec((tm, D_MODEL), lambda i: (i, 0)),
                  pl.BlockSpec((tm, D_MODEL), lambda i: (i + n // tm, 0)),
                  pl.BlockSpec((tm, D_MODEL), lambda i: (i, 0)),
                  pl.BlockSpec((tm, LANES), lambda i: (i, 0)),
                  _const_spec(fg.shape)],
        out_specs=pl.BlockSpec((tm, D_MODEL), lambda i: (i, 0)),
        out_shape=jax.ShapeDtypeStruct((n, D_MODEL), F32),
        compiler_params=_cparams(("parallel",)),
        name="moe_combine",
    )(y2, y2, h, route, fg)


def _rmsnorm_kernel(h_ref, g_ref, o_ref):
    x = h_ref[...]
    o_ref[...] = x * lax.rsqrt(jnp.mean(x * x, axis=-1, keepdims=True) + EPS) * g_ref[...]


def _final_norm(h, g, tm):
    n = h.shape[0]
    row = pl.BlockSpec((tm, D_MODEL), lambda i: (i, 0))
    return pl.pallas_call(
        _rmsnorm_kernel, grid=(n // tm,), in_specs=[row, _const_spec(g.shape)], out_specs=row,
        out_shape=jax.ShapeDtypeStruct((n, D_MODEL), F32),
        compiler_params=_cparams(("parallel",)), name="final_norm",
    )(h, g)


def _pad_cols(w, width):
    return jnp.pad(w, ((0, 0), (0, width - w.shape[1])))


def _pad_rows(w, height):
    return jnp.pad(w, ((0, height - w.shape[0]), (0, 0)))


def _inproj_weights(w):
    o = np.cumsum([0, 512, 256, 256, 8, 192, 128, 32, 128, 128, 256, 256, 16, 512])
    seg = lambda i: w[:, o[i]:o[i + 1]]
    gates = seg(3)
    wm = jnp.concatenate([seg(0), seg(1), seg(2), _pad_cols(gates[:, :4], LANES), _pad_cols(gates[:, 4:], LANES)], axis=1)
    wa = jnp.concatenate([_pad_cols(seg(4), 256), seg(5), _pad_cols(seg(6), LANES)], axis=1)
    wg = jnp.concatenate([seg(7), seg(8), seg(9), seg(10), _pad_cols(seg(11), LANES)], axis=1)
    wc = seg(12)
    return [x.astype(BF16) for x in (wm, wa, wg, wc)]


def _mla_weights(w_uq, w_ukv):
    half = MLA_ROPE // 2
    zq = jnp.zeros((MLA_Q_RANK, half), F32)
    qa, qb, ka, kb, wv = [], [], [], [], []
    eye = jnp.eye(LANES, dtype=F32)[:, :MLA_ROPE]
    zk = jnp.zeros((MLA_KV_RANK, MLA_HP), F32)
    for h in range(MLA_HEADS):
        wq = w_uq[:, h * 96:(h + 1) * 96]
        nope, r1, r2 = wq[:, :64], wq[:, 64:64 + half], wq[:, 64 + half:]
        qa.append(_pad_cols(jnp.concatenate([nope, r1, r2], axis=1), MLA_HP))
        qb.append(_pad_cols(jnp.concatenate([jnp.zeros_like(nope), -r2, r1], axis=1), MLA_HP))
        wkv = w_ukv[:, h * 128:(h + 1) * 128]
        kn, vh = wkv[:, :64], wkv[:, 64:]
        e1, e2 = eye[:, :half], eye[:, half:]
        z64 = jnp.zeros((LANES, 64), F32)
        ka.append(jnp.concatenate([_pad_cols(kn, MLA_HP), _pad_cols(jnp.concatenate([z64, e1, e2], axis=1), MLA_HP)], axis=0))
        kb.append(jnp.concatenate([zk, _pad_cols(jnp.concatenate([z64, -e2, e1], axis=1), MLA_HP)], axis=0))
        wv.append(vh.T)
    cat = lambda xs, rows: _pad_rows(jnp.concatenate(xs, axis=1), rows).astype(BF16)
    wvt = _pad_cols(jnp.concatenate(wv, axis=0), 256).astype(BF16)
    return cat(qa, 256), cat(qb, 256), cat(ka, 256), cat(kb, 256), wvt


def _rope_tables(positions):
    half = MLA_ROPE // 2
    inv_freq = 1.0 / (ROPE_BASE ** (jnp.arange(0, MLA_ROPE, 2, dtype=F32) / MLA_ROPE))
    ang = positions.astype(F32)[..., None] * inv_freq
    cos, sin = jnp.cos(ang), jnp.sin(ang)
    shp = cos.shape[:-1]
    rc = jnp.concatenate([jnp.ones(shp + (64,), F32), cos, cos, jnp.zeros(shp + (MLA_HP - 64 - 2 * half,), F32)], axis=-1)
    rs = jnp.concatenate([jnp.zeros(shp + (64,), F32), sin, sin, jnp.zeros(shp + (MLA_HP - 64 - 2 * half,), F32)], axis=-1)
    return rc, rs


def _invert_kernel(dest_ref, lo_ref, hi_ref, out_ref):
    i = pl.program_id(0)
    ch = dest_ref.shape[-1]

    @pl.when(i == 0)
    def _():
        for e in range(lo_ref.shape[0]):
            def init(p, c):
                out_ref[p] = -1
                return c
            lax.fori_loop(lo_ref[e], hi_ref[e], init, 0)

    def body(a, c):
        out_ref[dest_ref[0, 0, a]] = i * ch + a
        return c
    lax.fori_loop(0, ch, body, 0, unroll=8)


def _invert_assignment(dest, pad_lo, pad_hi, rows):
    a = dest.shape[0]
    ch = min(INVERT_CHUNK, a)
    smem = pl.BlockSpec(memory_space=pltpu.SMEM)
    return pl.pallas_call(
        _invert_kernel,
        grid=(a // ch,),
        in_specs=[pl.BlockSpec((1, 1, ch), lambda i: (i, 0, 0), memory_space=pltpu.SMEM), smem, smem],
        out_specs=smem,
        out_shape=jax.ShapeDtypeStruct((rows,), jnp.int32),
        compiler_params=_cparams(("arbitrary",)),
        name="moe_invert",
    )(dest.reshape(a // ch, 1, ch), pad_lo, pad_hi)


def _routing_tables(route, bm):
    n = route.shape[0]
    a = n * TOP_K
    flat_e = route[:, :TOP_K].astype(jnp.int32).reshape(-1)
    onehot = (flat_e[:, None] == jnp.arange(N_EXPERTS, dtype=jnp.int32)[None, :]).astype(jnp.int32)
    csum = jnp.cumsum(onehot, axis=0)
    rank = jnp.sum(onehot * csum, axis=1) - 1
    counts = csum[-1]
    padded = (counts + bm - 1) // bm * bm
    padded_end = jnp.cumsum(padded)
    padded_start = padded_end - padded
    dest = padded_start[flat_e] + rank
    n_blocks = -(-a // bm) + N_EXPERTS
    rows = n_blocks * bm
    pad_lo = jnp.concatenate([padded_start + counts, padded_end[-1:]]).astype(jnp.int32)
    pad_hi = jnp.concatenate([padded_end, jnp.full((1,), rows, jnp.int32)]).astype(jnp.int32)
    row_asg = _invert_assignment(dest, pad_lo, pad_hi, rows)
    row_tok = jnp.maximum(row_asg, 0) // TOP_K
    pad_rank = jnp.cumsum((row_asg < 0).astype(jnp.int32)) - 1
    slot_row = (row_asg % TOP_K) * n + row_asg // TOP_K
    row_dst = jnp.where(row_asg >= 0, slot_row, a + pad_rank)
    first = a + (rows - a) + jnp.arange(bm, dtype=jnp.int32)
    row_dst = jnp.concatenate([first, row_dst]).reshape(n_blocks + 1, 1, bm)
    block_start = jnp.arange(n_blocks, dtype=jnp.int32) * bm
    block_e = jnp.sum((padded_end[None, :] <= block_start[:, None]).astype(jnp.int32), axis=1)
    block_e = jnp.minimum(block_e, N_EXPERTS - 1).astype(jnp.int32)
    return block_e, row_tok.reshape(n_blocks, 1, bm), row_dst, rows + bm


def kernel(x, positions, norm1_g, w_in, mlstm_conv_w, mlstm_conv_b, mlstm_gate_b, mla_q_norm_g, mla_w_uq,
           mla_kv_norm_g, mla_w_ukv, gla_w_alpha, gla_b_alpha, conv_w, conv_b, conv_ln_g, conv_ln_b, fuse_g,
           w_out, norm2_g, ffn_w_gate, ffn_w_up, ffn_w_down, moe_router, moe_w_gate, moe_w_up, moe_w_down,
           final_norm_g):
    bsz, seq, _ = x.shape
    n = bsz * seq
    depth = w_in.shape[0]
    tm = min(512, n)
    tm2 = min(1024, n)
    tt = min(1024, seq)
    tq = min(512, seq)
    nct = seq // CHUNK
    row2 = lambda v: v.reshape(1, -1).astype(F32)

    rc, rs = _rope_tables(positions)
    h = x.reshape(n, D_MODEL)
    for l in range(depth):
        wm, wa, wg, wc = _inproj_weights(w_in[l])
        zm, za, zg, zc, zgate = _inproj(h, row2(norm1_g[l]), wm, wa, wg, wc, tm2)
        zm = zm.reshape(bsz, seq, ZM_W)

        gates = jnp.concatenate([zgate[:, 0:MLSTM_HEADS], zgate[:, LANES:LANES + MLSTM_HEADS]], axis=-1)
        grow = gates.reshape(bsz, nct, CHUNK, 2, MLSTM_HEADS).transpose(0, 3, 1, 4, 2).reshape(bsz, 2, nct, GROUP_W)
        gb = mlstm_gate_b[l].astype(F32)
        gbc = _pad_cols(gb.reshape(2, MLSTM_HEADS), LANES)
        gbr = jnp.repeat(gb.reshape(2, MLSTM_HEADS), CHUNK, axis=1)
        ym, yg = _recurrent_mixers(zm, grow, mlstm_conv_w[l], row2(mlstm_conv_b[l]), gbc, gbr,
                                   zg.reshape(bsz, seq, ZG_W), _pad_rows(gla_w_alpha[l], LANES),
                                   row2(gla_b_alpha[l]), tt)

        wqa, wqb, wka, wkb, wv = _mla_weights(mla_w_uq[l], mla_w_ukv[l])
        qh, kh, vh = _mla_prep(za.reshape(bsz, seq, ZA_W), rc, rs, _pad_cols(row2(mla_q_norm_g[l]), 256),
                               row2(mla_kv_norm_g[l]), wqa, wqb, wka, wkb, wv, tt)
        ya = _mla_attn(qh, kh, vh, tq)

        yc = _conformer(zc.reshape(bsz, seq, ZC_W), conv_w[l], row2(conv_b[l]), row2(conv_ln_g[l]),
                        row2(conv_ln_b[l]), tt)

        flat = lambda y: y.reshape(n, GROUP_W)
        j = l // 2
        if l % 2 == 0:
            ffn = (ffn_w_gate[j].astype(BF16), ffn_w_up[j].astype(BF16), ffn_w_down[j].astype(BF16))
            h = _outproj(flat(ym), flat(ya), flat(yg), flat(yc), h, row2(fuse_g[l]), w_out[l].astype(BF16),
                         row2(norm2_g[l]), tm, ffn=ffn)
            if l == depth - 1:
                h = _final_norm(h, row2(final_norm_g), tm2)
        else:
            h, u, route = _outproj(flat(ym), flat(ya), flat(yg), flat(yc), h, row2(fuse_g[l]),
                                   w_out[l].astype(BF16), row2(norm2_g[l]), tm2,
                                   router=_pad_cols(moe_router[j], LANES))
            block_e, row_tok, row_dst, n_rows = _routing_tables(route, MOE_BM)
            y2 = _moe(block_e, row_tok, row_dst, u, moe_w_gate.astype(BF16), moe_w_up.astype(BF16),
                      moe_w_down.astype(BF16), j, n_rows)
            h = _combine(y2, h, route, row2(final_norm_g), tm2, l == depth - 1)
    return h.reshape(bsz, seq, D_MODEL)
```

```python
import functools

import jax
import jax.numpy as jnp
import numpy as np
from jax import lax
from jax.experimental import pallas as pl
from jax.experimental.pallas import tpu as pltpu

F32 = jnp.float32
BF16 = jnp.bfloat16

D_MODEL = 1024
GROUP_W = 256
EPS = 1e-6
LOG2E = 1.4426950408889634
LANES = 128
CHUNK = 64
MLSTM_HEADS = 4
MLSTM_DH = 64
MLSTM_CONV = 4
MLA_HEADS = 4
MLA_NOPE = 64
MLA_ROPE = 32
MLA_V = 64
MLA_Q_RANK = 192
MLA_KV_RANK = 128
ROPE_BASE = 10000.0
GLA_HEADS = 4
GLA_DK = 32
GLA_DV = 64
GLA_GATE_RANK = 16
GLA_TAU = 16.0
GLA_SUB = 16
CONV_WIDTH = 31
D_FF = 2816
N_EXPERTS = 8
TOP_K = 2
MOE_BM = 256
INVERT_CHUNK = 8192
VMEM_LIMIT = 56 * 1024 * 1024

ZM_W = 1280
ZA_W = 512
ZG_W = 896
ZC_W = 512


def _cparams(sem):
    return pltpu.CompilerParams(dimension_semantics=sem, vmem_limit_bytes=VMEM_LIMIT)


def _const_spec(shape):
    nd = len(shape)
    return pl.BlockSpec(shape, lambda *_: (0,) * nd, pipeline_mode=pl.Buffered(1))


def _sigmoid(x):
    return 1.0 / (1.0 + jnp.exp(-x))


def _log_sigmoid(x):
    return jnp.minimum(x, 0.0) - jnp.log(1.0 + jnp.exp(-jnp.abs(x)))


def _iota(shape, dim):
    return lax.broadcasted_iota(jnp.int32, shape, dim)


def _tri(n):
    return (_iota((n, n), 0) >= _iota((n, n), 1)).astype(BF16)


def _split3(x):
    hi = x.astype(BF16)
    r1 = x - hi.astype(F32)
    mid = r1.astype(BF16)
    lo = (r1 - mid.astype(F32)).astype(BF16)
    return hi, mid, lo


def _sel_dot(sel, x):
    hi, mid, lo = _split3(x)
    d = lambda p: jnp.dot(sel, p, preferred_element_type=F32)
    return d(hi) + d(mid) + d(lo)


def _dot_sel(x, sel):
    hi, mid, lo = _split3(x)
    d = lambda p: jnp.dot(p, sel, preferred_element_type=F32)
    return d(hi) + d(mid) + d(lo)


def _inproj_kernel(h_ref, g_ref, wm_ref, wa_ref, wg_ref, wc_ref, zm_ref, za_ref, zg_ref, zc_ref, gate_ref):
    x = h_ref[...]
    ms = jnp.mean(x * x, axis=-1, keepdims=True)
    xn = (x * lax.rsqrt(ms + EPS) * g_ref[...]).astype(BF16)
    zm = jnp.dot(xn, wm_ref[...], preferred_element_type=F32)
    zm_ref[...] = zm
    gate_ref[...] = zm[:, ZM_W - 2 * LANES:]
    za_ref[...] = jnp.dot(xn, wa_ref[...], preferred_element_type=F32)
    zg_ref[...] = jnp.dot(xn, wg_ref[...], preferred_element_type=F32)
    zc_ref[...] = jnp.dot(xn, wc_ref[...], preferred_element_type=F32)


def _inproj(h, g, wm, wa, wg, wc, tm):
    n = h.shape[0]
    row = lambda w: pl.BlockSpec((tm, w), lambda i: (i, 0))
    widths = (ZM_W, ZA_W, ZG_W, ZC_W, 2 * LANES)
    return pl.pallas_call(
        _inproj_kernel,
        grid=(n // tm,),
        in_specs=[row(D_MODEL), _const_spec((1, D_MODEL)), _const_spec(wm.shape), _const_spec(wa.shape),
                  _const_spec(wg.shape), _const_spec(wc.shape)],
        out_specs=[row(w) for w in widths],
        out_shape=[jax.ShapeDtypeStruct((n, w), F32) for w in widths],
        compiler_params=_cparams(("parallel",)),
        name="inproj",
    )(h, g, wm, wa, wg, wc)


def _expand_heads(x, width):
    r = x.shape[0]
    lane = _iota((r, 4 * width), 1)
    out = jnp.broadcast_to(x[:, 3:4], (r, 4 * width))
    for h in (2, 1, 0):
        out = jnp.where(lane < (h + 1) * width, jnp.broadcast_to(x[:, h:h + 1], (r, 4 * width)), out)
    return out


def _mlstm_setup(zm_ref, grow_ref, cw_ref, cb_ref, gbc_ref, gbr_ref, y_ref,
                 xpad_ref, q_ref, k_ref, grs_ref, cn_ref, m_ref, cnm_ref, trit_ref, b_ref, gc_ref, cm_ref, *, tt):
    L = CHUNK
    W = MLSTM_HEADS * MLSTM_DH

    @pl.when(pl.program_id(1) == 0)
    def _():
        xpad_ref[0:8, :] = jnp.zeros((8, 2 * W), F32)
        cn_ref[...] = jnp.zeros_like(cn_ref)
        m_ref[...] = jnp.zeros_like(m_ref)
        hsame = (_iota((W, W), 0) // L) == (_iota((W, W), 1) // L)
        ncol = (_iota((W, LANES), 0) // L) == _iota((W, LANES), 1)
        cnm_ref[...] = jnp.concatenate([hsame, ncol], axis=1).astype(F32)
        trit_ref[...] = (hsame & ((_iota((W, W), 0) % L) <= (_iota((W, W), 1) % L))).astype(BF16)

    xpad_ref[8:8 + tt, :] = zm_ref[0, :, 0:2 * W]
    for r in range(0, tt, L):
        acc = jnp.broadcast_to(cb_ref[...], (L, 2 * W))
        for j in range(MLSTM_CONV):
            acc = acc + cw_ref[j:j + 1, :] * xpad_ref[pl.ds(r + 8 - (MLSTM_CONV - 1) + j, L), :]
        qk = acc * _sigmoid(acc)
        q_ref[r:r + L, :] = (qk[:, :W] * (MLSTM_DH ** -0.5)).astype(BF16)
        k_ref[r:r + L, :] = qk[:, W:]
    xpad_ref[0:8, :] = xpad_ref[tt:tt + 8, :]

    i_row = grow_ref[0, 0] + gbr_ref[0:1, :]
    lf_row = _log_sigmoid(grow_ref[0, 1] + gbr_ref[1:2, :])
    grs_ref[...] = i_row - _dot_sel(lf_row, trit_ref[...])

    tri = _tri(L)
    lf_all = _log_sigmoid(zm_ref[0, :, 4 * W + LANES:4 * W + 2 * LANES] + gbc_ref[1:2, :])
    for r in range(0, tt, L):
        b_ref[r:r + L, :] = _sel_dot(tri, lf_all[r:r + L, :])
    g_all = zm_ref[0, :, 4 * W:4 * W + LANES] + gbc_ref[0:1, :] - b_ref[...]
    gc_ref[...] = g_all
    pos = _iota((tt, LANES), 0) % L
    cm_all = g_all
    s = 1
    while s < L:
        cm_all = jnp.maximum(cm_all, jnp.where(pos >= s, pltpu.roll(cm_all, s, 0), -jnp.inf))
        s *= 2
    cm_ref[...] = cm_all

    row_t = _iota((L, W), 0)
    lane_j = _iota((L, W), 1) % L
    causal = lane_j <= row_t
    ones_col = (_iota((L, LANES), 1) < MLSTM_HEADS).astype(F32)

    def chunk(c):
        r0 = pl.multiple_of(c * L, L)
        b = b_ref[pl.ds(r0, L), :]
        g = gc_ref[pl.ds(r0, L), :]
        cm = cm_ref[pl.ds(r0, L), :]
        m_prev = m_ref[...]
        mx = jnp.maximum(m_prev, cm)
        w_inter = jnp.exp(m_prev - mx)
        b_end = b[L - 1:L, :]
        mx_end = mx[L - 1:L, :]
        m_new = b_end + mx_end
        decay = jnp.exp(m_prev - mx_end)
        kw_col = jnp.exp(g - mx_end)

        qc = q_ref[pl.ds(r0, L), :]
        kc = k_ref[pl.ds(r0, L), :]
        vc = zm_ref[0, pl.ds(r0, L), 2 * W:3 * W]
        oc = zm_ref[0, pl.ds(r0, L), 3 * W:4 * W]

        kbd = (jnp.concatenate([kc] * 4, axis=0) * cnm_ref[:, 0:W]).astype(BF16)
        s_qk = lax.dot_general(qc, kbd, (((1,), (1,)), ((), ())), preferred_element_type=F32)
        inter = jnp.dot(qc, cn_ref[...].astype(BF16), preferred_element_type=F32)
        vaug = jnp.concatenate([vc, ones_col], axis=1)
        kw = (_expand_heads(kw_col, L) * kc).astype(BF16)
        upd = lax.dot_general(kw, vaug.astype(BF16), (((0,), (0,)), ((), ())), preferred_element_type=F32)
        decay_x = jnp.concatenate([_expand_heads(decay, L), decay], axis=1)
        cn_ref[...] = decay_x * cn_ref[...] + upd * cnm_ref[...]
        m_ref[...] = m_new

        def output():
            g_row = grs_ref[pl.ds(c, 1), :]
            dmat = jnp.where(causal, jnp.exp(jnp.where(causal, g_row - _expand_heads(mx, L), 0.0)), 0.0)
            s_w = (s_qk * dmat).astype(BF16)
            vbd = (jnp.concatenate([vaug] * 4, axis=0) * cnm_ref[...]).astype(BF16)
            intra = jnp.dot(s_w, vbd, preferred_element_type=F32)
            num = _expand_heads(w_inter, L) * inter[:, :W] + intra[:, :W]
            den = w_inter * inter[:, W:] + intra[:, W:]
            hden = jnp.maximum(jnp.abs(den), jnp.exp(-(b + mx)))
            hout = num * _expand_heads(1.0 / hden, L)
            y_ref[0, pl.ds(r0, L), :] = _sigmoid(oc) * hout

        return output

    return chunk


def _mlstm_scratch(tt):
    nct = tt // CHUNK
    return [pltpu.VMEM((tt + 8, 2 * GROUP_W), F32), pltpu.VMEM((tt, GROUP_W), BF16),
            pltpu.VMEM((tt, GROUP_W), F32), pltpu.VMEM((nct, GROUP_W), F32),
            pltpu.VMEM((GROUP_W, GROUP_W + LANES), F32), pltpu.VMEM((1, LANES), F32),
            pltpu.VMEM((GROUP_W, GROUP_W + LANES), F32), pltpu.VMEM((GROUP_W, GROUP_W), BF16),
            pltpu.VMEM((tt, LANES), F32), pltpu.VMEM((tt, LANES), F32), pltpu.VMEM((tt, LANES), F32)]


def _gla_setup(zg_ref, wa_ref, ba_ref, y_ref, q_ref, k_ref, g_ref, o_ref, st_ref, kmask_ref, vmask_ref, *, tt):
    L = CHUNK
    WK = GLA_HEADS * GLA_DK
    WV = GLA_HEADS * GLA_DV

    @pl.when(pl.program_id(1) == 0)
    def _():
        st_ref[...] = jnp.zeros_like(st_ref)
        kmask_ref[...] = ((_iota((WV, WK), 0) // L) == (_iota((WV, WK), 1) // GLA_DK)).astype(F32)
        vmask_ref[...] = ((_iota((WV, WV), 0) // L) == (_iota((WV, WV), 1) // GLA_DV)).astype(F32)

    q_ref[...] = zg_ref[0, :, 0:WK] * (GLA_DK ** -0.5)
    k_ref[...] = zg_ref[0, :, WK:2 * WK]
    a = zg_ref[0, :, 2 * WK + 2 * WV:2 * WK + 2 * WV + LANES]
    a_hi = a.astype(BF16)
    a_lo = (a - a_hi.astype(F32)).astype(BF16)
    wa = wa_ref[...]
    w_hi = wa.astype(BF16)
    w_lo = (wa - w_hi.astype(F32)).astype(BF16)
    t = jnp.dot(a_hi, jnp.concatenate([w_hi, w_lo], axis=1), preferred_element_type=F32)
    pre = t[:, :WK] + (jnp.dot(a_lo, w_hi, preferred_element_type=F32) + t[:, WK:]) + ba_ref[...]
    g_all = _log_sigmoid(pre) * (1.0 / GLA_TAU)
    tri = _tri(L)
    for r in range(0, tt, L):
        g_ref[r:r + L, :] = _sel_dot(tri, g_all[r:r + L, :])

    hs2 = ((_iota((WK, WV), 0) // GLA_DK) == (_iota((WK, WV), 1) // GLA_DV)).astype(BF16)

    def chunk(c):
        r0 = pl.multiple_of(c * L, L)
        qc = q_ref[pl.ds(r0, L), :]
        kc = k_ref[pl.ds(r0, L), :]
        vc = zg_ref[0, pl.ds(r0, L), 2 * WK:2 * WK + WV]
        b = g_ref[pl.ds(r0, L), :]
        st = st_ref[...]
        qd = (qc * jnp.exp(b)).astype(BF16)
        o = lax.dot_general(qd, st.astype(BF16), (((1,), (1,)), ((), ())), preferred_element_type=F32)
        b_end = b[L - 1:L, :]
        kd = (kc * jnp.exp(b_end - b)).astype(BF16)
        upd = lax.dot_general(vc.astype(BF16), kd, (((0,), (0,)), ((), ())), preferred_element_type=F32)
        st_ref[...] = st * jnp.exp(b_end) + upd * kmask_ref[...]

        def output(o=o):
            S = GLA_SUB
            key_pos = _iota((L, WK), 0)
            vbd = (jnp.concatenate([vc] * GLA_HEADS, axis=0) * vmask_ref[...]).astype(BF16)
            a_rows = []
            for lo in range(S, L, S):
                r = b[lo - 1:lo, :]
                qf = (qc[lo:lo + S, :] * jnp.exp(b[lo:lo + S, :] - r)).astype(BF16)
                kf = kc * jnp.exp(jnp.where(key_pos < lo, r - b, -jnp.inf))
                kbd = (jnp.concatenate([kf] * GLA_HEADS, axis=0) * kmask_ref[...]).astype(BF16)
                a_rows.append(lax.dot_general(qf, kbd, (((1,), (1,)), ((), ())), preferred_element_type=F32))
            o_off = jnp.dot(jnp.concatenate(a_rows, axis=0).astype(BF16), vbd, preferred_element_type=F32)

            tl = _iota((8, WK), 0)
            ps = []
            for lo in range(0, L, S):
                for jj in range(8):
                    j = lo + jj
                    d = b[lo:lo + S, :] - b[j:j + 1, :]
                    e = jnp.exp(jnp.concatenate([jnp.where(tl >= jj, d[0:8, :], -jnp.inf), d[8:, :]], axis=0))
                    ps.append((qc[lo:lo + S, :] * kc[j:j + 1, :] * e).astype(BF16))
                for jj in range(8):
                    j = lo + 8 + jj
                    d = b[lo + 8:lo + S, :] - b[j:j + 1, :]
                    e = jnp.exp(jnp.where(tl >= jj, d, -jnp.inf))
                    ps.append((qc[lo + 8:lo + S, :] * kc[j:j + 1, :] * e).astype(BF16))
            rexp = jnp.dot(jnp.concatenate(ps, axis=0), hs2, preferred_element_type=F32)
            bands = []
            row = 0
            for lo in range(0, L, S):
                acc_a = None
                for jj in range(8):
                    term = rexp[row:row + S, :] * vc[lo + jj:lo + jj + 1, :]
                    acc_a = term if acc_a is None else acc_a + term
                    row += S
                acc_b = None
                for jj in range(8):
                    term = rexp[row:row + 8, :] * vc[lo + 8 + jj:lo + 8 + jj + 1, :]
                    acc_b = term if acc_b is None else acc_b + term
                    row += 8
                band = jnp.concatenate([acc_a[0:8, :], acc_a[8:, :] + acc_b], axis=0)
                bands.append(band if lo == 0 else band + o_off[lo - S:lo, :])
            o_ref[pl.ds(r0, L), :] = o + jnp.concatenate(bands, axis=0)

        return output

    def finish():
        o = o_ref[...]
        hsame = ((_iota((WV, WV), 0) // GLA_DV) == (_iota((WV, WV), 1) // GLA_DV)).astype(BF16)
        ms = _dot_sel(o * o, hsame) * (1.0 / GLA_DV)
        r = zg_ref[0, :, 2 * WK + WV:2 * WK + 2 * WV]
        y_ref[0] = o * lax.rsqrt(ms + EPS) * (r * _sigmoid(r))

    return chunk, finish


def _gla_scratch(tt):
    return [pltpu.VMEM((tt, LANES), F32), pltpu.VMEM((tt, LANES), F32), pltpu.VMEM((tt, LANES), F32),
            pltpu.VMEM((tt, GROUP_W), F32), pltpu.VMEM((GROUP_W, LANES), F32),
            pltpu.VMEM((GROUP_W, LANES), F32), pltpu.VMEM((GROUP_W, GROUP_W), F32)]


def _recurrent_kernel(zm_ref, grow_ref, cw_ref, cb_ref, gbc_ref, gbr_ref, zg_ref, wa_ref, ba_ref,
                      ym_ref, yg_ref, *scratch, tt):
    n_m = len(_mlstm_scratch(tt))
    m_chunk = _mlstm_setup(zm_ref, grow_ref, cw_ref, cb_ref, gbc_ref, gbr_ref, ym_ref, *scratch[:n_m], tt=tt)
    g_chunk, g_finish = _gla_setup(zg_ref, wa_ref, ba_ref, yg_ref, *scratch[n_m:], tt=tt)

    def both(c, carry):
        m_output = m_chunk(c)
        g_output = g_chunk(c)
        m_output()
        g_output()
        return carry

    lax.fori_loop(0, tt // CHUNK, both, 0, unroll=4)
    g_finish()


def _recurrent_mixers(zm, grow, cw, cb, gbc, gbr, zg, wa, ba, tt):
    b, s, _ = zm.shape
    nct = tt // CHUNK
    tile = lambda w: pl.BlockSpec((1, tt, w), lambda i, t: (i, t, 0))
    yshape = jax.ShapeDtypeStruct((b, s, GROUP_W), F32)
    return pl.pallas_call(
        functools.partial(_recurrent_kernel, tt=tt),
        grid=(b, s // tt),
        in_specs=[tile(ZM_W), pl.BlockSpec((1, 2, nct, GROUP_W), lambda i, t: (i, 0, t, 0)),
                  _const_spec(cw.shape), _const_spec(cb.shape), _const_spec(gbc.shape), _const_spec(gbr.shape),
                  tile(ZG_W), _const_spec(wa.shape), _const_spec(ba.shape)],
        out_specs=[tile(GROUP_W), tile(GROUP_W)],
        out_shape=[yshape, yshape],
        scratch_shapes=_mlstm_scratch(tt) + _gla_scratch(tt),
        compiler_params=_cparams(("parallel", "arbitrary")),
        name="mlstm_gla",
    )(zm, grow, cw, cb, gbc, gbr, zg, wa, ba)


CONV_HIST = 32
CONV_SUB = 64


def _conv_kernel(zc_ref, cw_ref, cb_ref, lg_ref, lb_ref, y_ref, xpad_ref, xs_ref, *, tt):
    C = GROUP_W
    span = tt + CONV_HIST - 8

    @pl.when(pl.program_id(1) == 0)
    def _():
        xpad_ref[0:CONV_HIST, :] = jnp.zeros((CONV_HIST, C), F32)

    val = zc_ref[0, :, 0:C]
    gate = zc_ref[0, :, C:2 * C]
    xpad_ref[CONV_HIST:CONV_HIST + tt, :] = val * _sigmoid(gate)
    for sh in range(1, 8):
        xs_ref[sh - 1, 0:span, :] = xpad_ref[pl.ds(sh, span), :]
    sub = min(CONV_SUB, tt)
    for r in range(0, tt, sub):
        acc = jnp.broadcast_to(cb_ref[...], (sub, C))
        for j in range(CONV_WIDTH):
            off = CONV_HIST - (CONV_WIDTH - 1) + j
            sh, base = off % 8, r + off - off % 8
            src = xpad_ref[base:base + sub, :] if sh == 0 else xs_ref[sh - 1, base:base + sub, :]
            acc = acc + cw_ref[j:j + 1, :] * src
        mu = jnp.mean(acc, axis=-1, keepdims=True)
        d = acc - mu
        var = jnp.mean(d * d, axis=-1, keepdims=True)
        yn = d * lax.rsqrt(var + EPS) * lg_ref[...] + lb_ref[...]
        y_ref[0, r:r + sub, :] = yn * _sigmoid(yn)
    xpad_ref[0:CONV_HIST, :] = xpad_ref[tt:tt + CONV_HIST, :]


def _conformer(zc, cw, cb, lg, lb, tt):
    b, s, _ = zc.shape
    return pl.pallas_call(
        functools.partial(_conv_kernel, tt=tt),
        grid=(b, s // tt),
        in_specs=[pl.BlockSpec((1, tt, ZC_W), lambda i, t: (i, t, 0)), _const_spec(cw.shape), _const_spec(cb.shape),
                  _const_spec(lg.shape), _const_spec(lb.shape)],
        out_specs=pl.BlockSpec((1, tt, GROUP_W), lambda i, t: (i, t, 0)),
        out_shape=jax.ShapeDtypeStruct((b, s, GROUP_W), F32),
        scratch_shapes=[pltpu.VMEM((tt + CONV_HIST, GROUP_W), F32), pltpu.VMEM((7, tt + CONV_HIST, GROUP_W), F32)],
        compiler_params=_cparams(("parallel", "arbitrary")),
        name="conformer",
    )(zc, cw, cb, lg, lb)


MLA_HP = 128
MLA_VP = 80


def _mla_prep_kernel(za_ref, rc_ref, rs_ref, qg_ref, kg_ref, wqa_ref, wqb_ref, wka_ref, wkb_ref, wv_ref,
                     q_ref, k_ref, v_ref):
    cq = za_ref[0, :, 0:256]
    ckv = za_ref[0, :, 256:384]
    kr = za_ref[0, :, 384:512]
    qn = (cq * lax.rsqrt(jnp.sum(cq * cq, axis=-1, keepdims=True) * (1.0 / MLA_Q_RANK) + EPS) * qg_ref[...]).astype(BF16)
    kvn = ckv * lax.rsqrt(jnp.mean(ckv * ckv, axis=-1, keepdims=True) + EPS) * kg_ref[...]
    kin = jnp.concatenate([kvn, kr], axis=1).astype(BF16)
    cos = rc_ref[0]
    sin = rs_ref[0]
    qa = jnp.dot(qn, wqa_ref[...], preferred_element_type=F32)
    qb = jnp.dot(qn, wqb_ref[...], preferred_element_type=F32)
    ka = jnp.dot(kin, wka_ref[...], preferred_element_type=F32)
    kb = jnp.dot(kin, wkb_ref[...], preferred_element_type=F32)
    vt = lax.dot_general(wv_ref[...], kin, (((1,), (1,)), ((), ())), preferred_element_type=F32)
    scale = (MLA_NOPE + MLA_ROPE) ** -0.5 * LOG2E
    for h in range(MLA_HEADS):
        sl = slice(h * MLA_HP, (h + 1) * MLA_HP)
        q_ref[0, h] = ((qa[:, sl] * cos + qb[:, sl] * sin) * scale).astype(BF16)
        k_ref[0, h] = (ka[:, sl] * cos + kb[:, sl] * sin).astype(BF16)
        v_ref[0, h, 0:MLA_V, :] = vt[h * MLA_V:(h + 1) * MLA_V, :].astype(BF16)
        v_ref[0, h, MLA_V:MLA_VP, :] = jnp.ones((MLA_VP - MLA_V, vt.shape[1]), BF16)


def _mla_prep(za, rc, rs, qg, kg, wqa, wqb, wka, wkb, wv, tt):
    b, s, _ = za.shape
    hspec = pl.BlockSpec((1, MLA_HEADS, tt, MLA_HP), lambda i, t: (i, 0, t, 0))
    tspec = pl.BlockSpec((1, tt, MLA_HP), lambda i, t: (i, t, 0))
    hshape = jax.ShapeDtypeStruct((b, MLA_HEADS, s, MLA_HP), BF16)
    vspec = pl.BlockSpec((1, MLA_HEADS, MLA_VP, tt), lambda i, t: (i, 0, 0, t))
    vshape = jax.ShapeDtypeStruct((b, MLA_HEADS, MLA_VP, s), BF16)
    return pl.pallas_call(
        _mla_prep_kernel,
        grid=(b, s // tt),
        in_specs=[pl.BlockSpec((1, tt, ZA_W), lambda i, t: (i, t, 0)), tspec, tspec,
                  _const_spec(qg.shape), _const_spec(kg.shape), _const_spec(wqa.shape), _const_spec(wqb.shape),
                  _const_spec(wka.shape), _const_spec(wkb.shape), _const_spec(wv.shape)],
        out_specs=[hspec, hspec, vspec],
        out_shape=[hshape, hshape, vshape],
        compiler_params=_cparams(("parallel", "parallel")),
        name="mla_prep",
    )(za, rc, rs, qg, kg, wqa, wqb, wka, wkb, wv)


def _mla_attn_kernel(qi_ref, ki_ref, q_ref, k_ref, v_ref, y_ref, m_ref, acc_ref, *, tq):
    qi = qi_ref[pl.program_id(1)]
    ki = ki_ref[pl.program_id(1)]

    @pl.when(ki == 0)
    def _():
        m_ref[...] = jnp.full_like(m_ref, -jnp.inf)
        acc_ref[...] = jnp.zeros_like(acc_ref)

    def step(masked):
        def scores(h):
            s = lax.dot_general(k_ref[0, h], q_ref[0, h], (((1,), (1,)), ((), ())), preferred_element_type=F32)
            if masked:
                s = jnp.where(_iota((tq, tq), 0) <= _iota((tq, tq), 1), s, -jnp.inf)
            return s

        def softmax(h, s):
            m_prev = m_ref[h]
            m_new = jnp.maximum(m_prev, jnp.max(s, axis=0, keepdims=True))
            alpha = jnp.exp2(m_prev - m_new)
            p = jnp.exp2(s - m_new)
            m_ref[h] = m_new
            return p.astype(BF16), alpha

        def values(h, p, alpha):
            acc_ref[h] = alpha * acc_ref[h] + jnp.dot(v_ref[0, h], p, preferred_element_type=F32)

        s_next = scores(0)
        for h in range(MLA_HEADS):
            s_cur = s_next
            if h + 1 < MLA_HEADS:
                s_next = scores(h + 1)
            values(h, *softmax(h, s_cur))

    @pl.when(ki < qi)
    def _():
        step(False)

    @pl.when(ki == qi)
    def _():
        step(True)
        ot = jnp.concatenate([acc_ref[h, 0:MLA_V, :] * (1.0 / acc_ref[h, MLA_V:MLA_V + 1, :])
                              for h in range(MLA_HEADS)], axis=0)
        y_ref[0] = ot.T


def _mla_attn(q, k, v, tq):
    b, _, s, _ = q.shape
    nq = s // tq
    pairs = [(qi, ki) for qi in range(nq) for ki in range(qi + 1)]
    qi_tab = jnp.asarray([p[0] for p in pairs], jnp.int32)
    ki_tab = jnp.asarray([p[1] for p in pairs], jnp.int32)
    grid_spec = pltpu.PrefetchScalarGridSpec(
        num_scalar_prefetch=2,
        grid=(b, len(pairs)),
        in_specs=[pl.BlockSpec((1, MLA_HEADS, tq, MLA_HP), lambda i, p, qt, kt: (i, 0, qt[p], 0)),
                  pl.BlockSpec((1, MLA_HEADS, tq, MLA_HP), lambda i, p, qt, kt: (i, 0, kt[p], 0)),
                  pl.BlockSpec((1, MLA_HEADS, MLA_VP, tq), lambda i, p, qt, kt: (i, 0, 0, kt[p]))],
        out_specs=pl.BlockSpec((1, tq, GROUP_W), lambda i, p, qt, kt: (i, qt[p], 0)),
        scratch_shapes=[pltpu.VMEM((MLA_HEADS, 1, tq), F32), pltpu.VMEM((MLA_HEADS, MLA_VP, tq), F32)],
    )
    return pl.pallas_call(
        functools.partial(_mla_attn_kernel, tq=tq),
        grid_spec=grid_spec,
        out_shape=jax.ShapeDtypeStruct((b, s, GROUP_W), F32),
        compiler_params=_cparams(("parallel", "arbitrary")),
        name="mla_attn",
    )(qi_tab, ki_tab, q, k, v)


def _outproj_kernel(*refs, routed):
    if routed:
        ym, ya, yg, yc, h_ref, fg_ref, wo_ref, n2_ref, rt_ref, hn_ref, u_ref, route_ref = refs
    else:
        ym, ya, yg, yc, h_ref, fg_ref, wo_ref, n2_ref, wg_ref, wu_ref, wd_ref, hn_ref = refs
    parts = []
    for gi, y_ref in enumerate((ym, ya, yg, yc)):
        y = y_ref[...]
        yn = y * lax.rsqrt(jnp.mean(y * y, axis=-1, keepdims=True) + EPS)
        parts.append((yn * fg_ref[:, gi * GROUP_W:(gi + 1) * GROUP_W]).astype(BF16))
    ycat = jnp.concatenate(parts, axis=1)
    hn = h_ref[...] + jnp.dot(ycat, wo_ref[...], preferred_element_type=F32)
    u = hn * lax.rsqrt(jnp.mean(hn * hn, axis=-1, keepdims=True) + EPS) * n2_ref[...]
    if not routed:
        hn_ref[...] = hn + _swiglu(u.astype(BF16), wg_ref[...], wu_ref[...], wd_ref[...])
        return
    hn_ref[...] = hn
    u_ref[...] = u
    if routed:
        tm = u.shape[0]
        lane = _iota((tm, LANES), 1)
        u_hi = u.astype(BF16)
        u_lo = (u - u_hi.astype(F32)).astype(BF16)
        rt = rt_ref[...]
        r_hi = rt.astype(BF16)
        r_lo = (rt - r_hi.astype(F32)).astype(BF16)
        d = lambda a, b: jnp.dot(a, b, preferred_element_type=F32)
        t = d(u_hi, jnp.concatenate([r_hi, r_lo], axis=1))
        logits = t[:, :LANES] + (d(u_lo, r_hi) + t[:, LANES:])
        logits = jnp.where(lane < N_EXPERTS, logits, -jnp.inf)
        m1 = jnp.max(logits, axis=-1, keepdims=True)
        i1 = jnp.min(jnp.where(logits == m1, lane, LANES), axis=-1, keepdims=True)
        rest = jnp.where(lane == i1, -jnp.inf, logits)
        m2 = jnp.max(rest, axis=-1, keepdims=True)
        i2 = jnp.min(jnp.where(rest == m2, lane, LANES), axis=-1, keepdims=True)
        e2 = jnp.exp(m2 - m1)
        w1 = 1.0 / (1.0 + e2)
        w2 = e2 / (1.0 + e2)
        route_ref[...] = jnp.where(lane == 0, i1.astype(F32),
                                   jnp.where(lane == 1, i2.astype(F32),
                                             jnp.where(lane == 2, w1, jnp.where(lane == 3, w2, 0.0))))


def _outproj(ym, ya, yg, yc, h, fg, wo, n2, tm, router=None, ffn=None):
    n = h.shape[0]
    routed = router is not None
    row = lambda w: pl.BlockSpec((tm, w), lambda i: (i, 0))
    in_specs = [row(GROUP_W)] * 4 + [row(D_MODEL), _const_spec(fg.shape), _const_spec(wo.shape), _const_spec(n2.shape)]
    args = [ym, ya, yg, yc, h, fg, wo, n2]
    hshape = jax.ShapeDtypeStruct((n, D_MODEL), F32)
    if routed:
        in_specs.append(_const_spec(router.shape))
        args.append(router)
        out_specs = [row(D_MODEL), row(D_MODEL), row(LANES)]
        out_shape = [hshape, hshape, jax.ShapeDtypeStruct((n, LANES), F32)]
    else:
        in_specs += [_const_spec(w.shape) for w in ffn]
        args += list(ffn)
        out_specs, out_shape = row(D_MODEL), hshape
    return pl.pallas_call(
        functools.partial(_outproj_kernel, routed=routed),
        grid=(n // tm,),
        in_specs=in_specs, out_specs=out_specs, out_shape=out_shape,
        compiler_params=_cparams(("parallel",)),
        name="outproj_routed" if routed else "outproj_ffn",
    )(*args)


def _swiglu(x, wg, wu, wd):
    g = jnp.dot(x, wg, preferred_element_type=F32)
    up = jnp.dot(x, wu, preferred_element_type=F32)
    a = (g * _sigmoid(g) * up).astype(BF16)
    return jnp.dot(a, wd, preferred_element_type=F32)


def _moe_kernel(be_ref, tokc_ref, tokn_ref, dstp_ref, dstc_ref, u_hbm, wg_ref, wu_ref, wd_ref, y_hbm,
                xbuf, ybuf, gsem, ssem):
    i = pl.program_id(0)
    nb = pl.num_programs(0)
    bm = xbuf.shape[1]

    def gather_wait(x, s):
        pltpu.make_async_copy(u_hbm.at[pl.ds(0, bm), :], x, gsem.at[s]).wait()

    def scatter_wait(y, s):
        pltpu.make_async_copy(y, y_hbm.at[pl.ds(0, bm), :], ssem.at[s]).wait()

    def scatter_row(y, dst_ref, r, s):
        pltpu.make_async_copy(y.at[pl.ds(r, 1), :], y_hbm.at[pl.ds(dst_ref[0, 0, r], 1), :], ssem.at[s]).start()

    @pl.when(i == 0)
    def _():
        def body(r, c):
            pltpu.make_async_copy(u_hbm.at[pl.ds(tokc_ref[0, 0, r], 1), :], xbuf.at[0, pl.ds(r, 1), :],
                                  gsem.at[0]).start()
            return c
        lax.fori_loop(0, bm, body, 0, unroll=8)
        ybuf[2] = jnp.zeros((bm, D_MODEL), F32)

    c = i % 2
    n = 1 - c
    yc_i = i % 3
    yp_i = (i + 2) % 3
    xc, xn, yc, yp = xbuf.at[c], xbuf.at[n], ybuf.at[yc_i], ybuf.at[yp_i]
    gather_wait(xc, c)

    @pl.when(i >= 2)
    def _():
        scatter_wait(yc, yc_i)

    g = jnp.dot(xc[...].astype(BF16), wg_ref[0, 0], preferred_element_type=F32)
    for r in range(bm):
        pltpu.make_async_copy(u_hbm.at[pl.ds(tokn_ref[0, 0, r], 1), :], xn.at[pl.ds(r, 1), :], gsem.at[n]).start()
    up = jnp.dot(xc[...].astype(BF16), wu_ref[0, 0], preferred_element_type=F32)
    for r in range(bm):
        scatter_row(yp, dstp_ref, r, yp_i)
    a = (g * _sigmoid(g) * up).astype(BF16)
    yc[...] = jnp.dot(a, wd_ref[0, 0], preferred_element_type=F32)

    @pl.when(i == nb - 1)
    def _():
        def body(r, carry):
            scatter_row(yc, dstc_ref, r, yc_i)
            return carry
        lax.fori_loop(0, bm, body, 0, unroll=8)
        scatter_wait(yc, (i + 1) % 3)
        scatter_wait(yp, yp_i)
        scatter_wait(yc, yc_i)
        gather_wait(xn, n)


def _moe(block_e, row_tok, row_dst, u, wg, wu, wd, layer, n_rows):
    n_blocks = block_e.shape[0]
    bm = row_tok.shape[-1]
    wspec = lambda shp: pl.BlockSpec((1, 1) + shp, lambda i, be: (layer, be[i], 0, 0))
    ispec = lambda f: pl.BlockSpec((1, 1, bm), f, memory_space=pltpu.SMEM)
    grid_spec = pltpu.PrefetchScalarGridSpec(
        num_scalar_prefetch=1,
        grid=(n_blocks,),
        in_specs=[ispec(lambda i, be: (i, 0, 0)),
                  ispec(lambda i, be: (jnp.minimum(i + 1, n_blocks - 1), 0, 0)),
                  ispec(lambda i, be: (i, 0, 0)),
                  ispec(lambda i, be: (i + 1, 0, 0)),
                  pl.BlockSpec(memory_space=pl.ANY),
                  wspec((D_MODEL, D_FF)), wspec((D_MODEL, D_FF)), wspec((D_FF, D_MODEL))],
        out_specs=pl.BlockSpec(memory_space=pl.ANY),
        scratch_shapes=[pltpu.VMEM((2, bm, D_MODEL), F32), pltpu.VMEM((3, bm, D_MODEL), F32),
                        pltpu.SemaphoreType.DMA((2,)), pltpu.SemaphoreType.DMA((3,))],
    )
    return pl.pallas_call(
        _moe_kernel,
        grid_spec=grid_spec,
        out_shape=jax.ShapeDtypeStruct((n_rows, D_MODEL), F32),
        compiler_params=_cparams(("arbitrary",)),
        name="moe_experts",
    )(block_e, row_tok, row_tok, row_dst, row_dst, u, wg, wu, wd)


def _combine_kernel(ya_ref, yb_ref, h_ref, route_ref, fg_ref, o_ref, *, final):
    w1 = route_ref[:, 2:3]
    w2 = route_ref[:, 3:4]
    hn = h_ref[...] + w1 * ya_ref[...] + w2 * yb_ref[...]
    if final:
        hn = hn * lax.rsqrt(jnp.mean(hn * hn, axis=-1, keepdims=True) + EPS) * fg_ref[...]
    o_ref[...] = hn


def _combine(y2, h, route, fg, tm, final):
    n = h.shape[0]
    return pl.pallas_call(
        functools.partial(_combine_kernel, final=final),
        grid=(n // tm,),
        in_specs=[pl.BlockSpec((tm, D_MODEL), lambda i: (i, 0)),
                  pl.BlockSpec((tm, D_MODEL), lambda i: (i + n // tm, 0)),
                  pl.BlockSpec((tm, D_MODEL), lambda i: (i, 0)),
                  pl.BlockSpec((tm, LANES), lambda i: (i, 0)),
                  _const_spec(fg.shape)],
        out_specs=pl.BlockSpec((tm, D_MODEL), lambda i: (i, 0)),
        out_shape=jax.ShapeDtypeStruct((n, D_MODEL), F32),
        compiler_params=_cparams(("parallel",)),
        name="moe_combine",
    )(y2, y2, h, route, fg)


def _rmsnorm_kernel(h_ref, g_ref, o_ref):
    x = h_ref[...]
    o_ref[...] = x * lax.rsqrt(jnp.mean(x * x, axis=-1, keepdims=True) + EPS) * g_ref[...]


def _final_norm(h, g, tm):
    n = h.shape[0]
    row = pl.BlockSpec((tm, D_MODEL), lambda i: (i, 0))
    return pl.pallas_call(
        _rmsnorm_kernel, grid=(n // tm,), in_specs=[row, _const_spec(g.shape)], out_specs=row,
        out_shape=jax.ShapeDtypeStruct((n, D_MODEL), F32),
        compiler_params=_cparams(("parallel",)), name="final_norm",
    )(h, g)


def _pad_cols(w, width):
    return jnp.pad(w, ((0, 0), (0, width - w.shape[1])))


def _pad_rows(w, height):
    return jnp.pad(w, ((0, height - w.shape[0]), (0, 0)))


def _inproj_weights(w):
    o = np.cumsum([0, 512, 256, 256, 8, 192, 128, 32, 128, 128, 256, 256, 16, 512])
    seg = lambda i: w[:, o[i]:o[i + 1]]
    gates = seg(3)
    wm = jnp.concatenate([seg(0), seg(1), seg(2), _pad_cols(gates[:, :4], LANES), _pad_cols(gates[:, 4:], LANES)], axis=1)
    wa = jnp.concatenate([_pad_cols(seg(4), 256), seg(5), _pad_cols(seg(6), LANES)], axis=1)
    wg = jnp.concatenate([seg(7), seg(8), seg(9), seg(10), _pad_cols(seg(11), LANES)], axis=1)
    wc = seg(12)
    return [x.astype(BF16) for x in (wm, wa, wg, wc)]


def _mla_weights(w_uq, w_ukv):
    half = MLA_ROPE // 2
    zq = jnp.zeros((MLA_Q_RANK, half), F32)
    qa, qb, ka, kb, wv = [], [], [], [], []
    eye = jnp.eye(LANES, dtype=F32)[:, :MLA_ROPE]
    zk = jnp.zeros((MLA_KV_RANK, MLA_HP), F32)
    for h in range(MLA_HEADS):
        wq = w_uq[:, h * 96:(h + 1) * 96]
        nope, r1, r2 = wq[:, :64], wq[:, 64:64 + half], wq[:, 64 + half:]
        qa.append(_pad_cols(jnp.concatenate([nope, r1, r2], axis=1), MLA_HP))
        qb.append(_pad_cols(jnp.concatenate([jnp.zeros_like(nope), -r2, r1], axis=1), MLA_HP))
        wkv = w_ukv[:, h * 128:(h + 1) * 128]
        kn, vh = wkv[:, :64], wkv[:, 64:]
        e1, e2 = eye[:, :half], eye[:, half:]
        z64 = jnp.zeros((LANES, 64), F32)
        ka.append(jnp.concatenate([_pad_cols(kn, MLA_HP), _pad_cols(jnp.concatenate([z64, e1, e2], axis=1), MLA_HP)], axis=0))
        kb.append(jnp.concatenate([zk, _pad_cols(jnp.concatenate([z64, -e2, e1], axis=1), MLA_HP)], axis=0))
        wv.append(vh.T)
    cat = lambda xs, rows: _pad_rows(jnp.concatenate(xs, axis=1), rows).astype(BF16)
    wvt = _pad_cols(jnp.concatenate(wv, axis=0), 256).astype(BF16)
    return cat(qa, 256), cat(qb, 256), cat(ka, 256), cat(kb, 256), wvt


def _rope_tables(positions):
    half = MLA_ROPE // 2
    inv_freq = 1.0 / (ROPE_BASE ** (jnp.arange(0, MLA_ROPE, 2, dtype=F32) / MLA_ROPE))
    ang = positions.astype(F32)[..., None] * inv_freq
    cos, sin = jnp.cos(ang), jnp.sin(ang)
    shp = cos.shape[:-1]
    rc = jnp.concatenate([jnp.ones(shp + (64,), F32), cos, cos, jnp.zeros(shp + (MLA_HP - 64 - 2 * half,), F32)], axis=-1)
    rs = jnp.concatenate([jnp.zeros(shp + (64,), F32), sin, sin, jnp.zeros(shp + (MLA_HP - 64 - 2 * half,), F32)], axis=-1)
    return rc, rs


def _invert_kernel(dest_ref, lo_ref, hi_ref, out_ref):
    i = pl.program_id(0)
    ch = dest_ref.shape[-1]

    @pl.when(i == 0)
    def _():
        for e in range(lo_ref.shape[0]):
            def init(p, c):
                out_ref[p] = -1
                return c
            lax.fori_loop(lo_ref[e], hi_ref[e], init, 0)

    def body(a, c):
        out_ref[dest_ref[0, 0, a]] = i * ch + a
        return c
    lax.fori_loop(0, ch, body, 0, unroll=8)


def _invert_assignment(dest, pad_lo, pad_hi, rows):
    a = dest.shape[0]
    ch = min(INVERT_CHUNK, a)
    smem = pl.BlockSpec(memory_space=pltpu.SMEM)
    return pl.pallas_call(
        _invert_kernel,
        grid=(a // ch,),
        in_specs=[pl.BlockSpec((1, 1, ch), lambda i: (i, 0, 0), memory_space=pltpu.SMEM), smem, smem],
        out_specs=smem,
        out_shape=jax.ShapeDtypeStruct((rows,), jnp.int32),
        compiler_params=_cparams(("arbitrary",)),
        name="moe_invert",
    )(dest.reshape(a // ch, 1, ch), pad_lo, pad_hi)


def _routing_tables(route, bm):
    n = route.shape[0]
    a = n * TOP_K
    flat_e = route[:, :TOP_K].astype(jnp.int32).reshape(-1)
    onehot = (flat_e[:, None] == jnp.arange(N_EXPERTS, dtype=jnp.int32)[None, :]).astype(jnp.int32)
    csum = jnp.cumsum(onehot, axis=0)
    rank = jnp.sum(onehot * csum, axis=1) - 1
    counts = csum[-1]
    padded = (counts + bm - 1) // bm * bm
    padded_end = jnp.cumsum(padded)
    padded_start = padded_end - padded
    dest = padded_start[flat_e] + rank
    n_blocks = -(-a // bm) + N_EXPERTS
    rows = n_blocks * bm
    pad_lo = jnp.concatenate([padded_start + counts, padded_end[-1:]]).astype(jnp.int32)
    pad_hi = jnp.concatenate([padded_end, jnp.full((1,), rows, jnp.int32)]).astype(jnp.int32)
    row_asg = _invert_assignment(dest, pad_lo, pad_hi, rows)
    row_tok = jnp.maximum(row_asg, 0) // TOP_K
    pad_rank = jnp.cumsum((row_asg < 0).astype(jnp.int32)) - 1
    slot_row = (row_asg % TOP_K) * n + row_asg // TOP_K
    row_dst = jnp.where(row_asg >= 0, slot_row, a + pad_rank)
    first = a + (rows - a) + jnp.arange(bm, dtype=jnp.int32)
    row_dst = jnp.concatenate([first, row_dst]).reshape(n_blocks + 1, 1, bm)
    block_start = jnp.arange(n_blocks, dtype=jnp.int32) * bm
    block_e = jnp.sum((padded_end[None, :] <= block_start[:, None]).astype(jnp.int32), axis=1)
    block_e = jnp.minimum(block_e, N_EXPERTS - 1).astype(jnp.int32)
    return block_e, row_tok.reshape(n_blocks, 1, bm), row_dst, rows + bm


def kernel(x, positions, norm1_g, w_in, mlstm_conv_w, mlstm_conv_b, mlstm_gate_b, mla_q_norm_g, mla_w_uq,
           mla_kv_norm_g, mla_w_ukv, gla_w_alpha, gla_b_alpha, conv_w, conv_b, conv_ln_g, conv_ln_b, fuse_g,
           w_out, norm2_g, ffn_w_gate, ffn_w_up, ffn_w_down, moe_router, moe_w_gate, moe_w_up, moe_w_down,
           final_norm_g):
    bsz, seq, _ = x.shape
    n = bsz * seq
    depth = w_in.shape[0]
    tm = min(512, n)
    tm2 = min(1024, n)
    tt = min(1024, seq)
    tq = min(512, seq)
    nct = seq // CHUNK
    row2 = lambda v: v.reshape(1, -1).astype(F32)

    rc, rs = _rope_tables(positions)
    h = x.reshape(n, D_MODEL)
    for l in range(depth):
        wm, wa, wg, wc = _inproj_weights(w_in[l])
        zm, za, zg, zc, zgate = _inproj(h, row2(norm1_g[l]), wm, wa, wg, wc, tm2)
        zm = zm.reshape(bsz, seq, ZM_W)

        gates = jnp.concatenate([zgate[:, 0:MLSTM_HEADS], zgate[:, LANES:LANES + MLSTM_HEADS]], axis=-1)
        grow = gates.reshape(bsz, nct, CHUNK, 2, MLSTM_HEADS).transpose(0, 3, 1, 4, 2).reshape(bsz, 2, nct, GROUP_W)
        gb = mlstm_gate_b[l].astype(F32)
        gbc = _pad_cols(gb.reshape(2, MLSTM_HEADS), LANES)
        gbr = jnp.repeat(gb.reshape(2, MLSTM_HEADS), CHUNK, axis=1)
        ym, yg = _recurrent_mixers(zm, grow, mlstm_conv_w[l], row2(mlstm_conv_b[l]), gbc, gbr,
                                   zg.reshape(bsz, seq, ZG_W), _pad_rows(gla_w_alpha[l], LANES),
                                   row2(gla_b_alpha[l]), tt)

        wqa, wqb, wka, wkb, wv = _mla_weights(mla_w_uq[l], mla_w_ukv[l])
        qh, kh, vh = _mla_prep(za.reshape(bsz, seq, ZA_W), rc, rs, _pad_cols(row2(mla_q_norm_g[l]), 256),
                               row2(mla_kv_norm_g[l]), wqa, wqb, wka, wkb, wv, tt)
        ya = _mla_attn(qh, kh, vh, tq)

        yc = _conformer(zc.reshape(bsz, seq, ZC_W), conv_w[l], row2(conv_b[l]), row2(conv_ln_g[l]),
                        row2(conv_ln_b[l]), tt)

        flat = lambda y: y.reshape(n, GROUP_W)
        j = l // 2
        if l % 2 == 0:
            ffn = (ffn_w_gate[j].astype(BF16), ffn_w_up[j].astype(BF16), ffn_w_down[j].astype(BF16))
            h = _outproj(flat(ym), flat(ya), flat(yg), flat(yc), h, row2(fuse_g[l]), w_out[l].astype(BF16),
                         row2(norm2_g[l]), tm, ffn=ffn)
            if l == depth - 1:
                h = _final_norm(h, row2(final_norm_g), tm2)
        else:
            h, u, route = _outproj(flat(ym), flat(ya), flat(yg), flat(yc), h, row2(fuse_g[l]),
                                   w_out[l].astype(BF16), row2(norm2_g[l]), tm2,
                                   router=_pad_cols(moe_router[j], LANES))
            block_e, row_tok, row_dst, n_rows = _routing_tables(route, MOE_BM)
            y2 = _moe(block_e, row_tok, row_dst, u, moe_w_gate.astype(BF16), moe_w_up.astype(BF16),
                      moe_w_down.astype(BF16), j, n_rows)
            h = _combine(y2, h, route, row2(final_norm_g), tm2, l == depth - 1)
    return h.reshape(bsz, seq, D_MODEL)
```
